```python
import jax, jax.numpy as jnp
from jax import lax
import numpy as np

D_MODEL = 1024
BATCH = 8
SEQ = 2048
DEPTH = 1
DEC_BATCH = 128
DEC_SEQ = 8
PAST_LEN = 16384
PAGE_SIZE = 128

M_HEADS = 4
M_DH = 128
M_WIDTH = M_HEADS * M_DH
R_HEADS = 4
R_DH = 128
R_WIDTH = R_HEADS * R_DH
MIX_WIDTH = M_WIDTH + R_WIDTH
CONV_W = 4
CHUNK = 128
ROPE_BASE = 10000.0
IN_SPLITS = (2 * M_WIDTH, M_WIDTH, M_WIDTH, 2 * M_HEADS, R_WIDTH, R_WIDTH, R_WIDTH, R_WIDTH)
D_IN = 4 * M_WIDTH + 2 * M_HEADS + 4 * R_WIDTH
N_EXPERTS = 64
TOP_K = 8
N_GROUPS = 8
TOPK_GROUPS = 4
D_EXPERT = 256
D_SHARED = 256
ROUTED_SCALE = 2.5
MOE_BLOCK = 128
NORM_EPS = 1e-6

kernel_name = 'hymba_mlstm_retention_moe_step'


def rms_norm(x, g):
    x32 = x.astype(jnp.float32)
    y = x32 * lax.rsqrt(jnp.mean(x32 * x32, axis=-1, keepdims=True) + NORM_EPS)
    return (y * g.astype(jnp.float32)).astype(x.dtype)


def head_norm(y, g):
    mu = jnp.mean(y, axis=-1, keepdims=True)
    var = jnp.mean(jnp.square(y - mu), axis=-1, keepdims=True)
    yn = (y - mu) * lax.rsqrt(var + NORM_EPS)
    return yn.reshape(y.shape[:2] + (-1,)) * g.astype(jnp.float32)


def rotary(x, pos):
    half = x.shape[-1] // 2
    freqs = ROPE_BASE ** (-jnp.arange(half, dtype=jnp.float32) / half)
    ang = pos.astype(jnp.float32)[:, None] * freqs[None, :]
    cos = jnp.cos(ang)[None, :, None, :]
    sin = jnp.sin(ang)[None, :, None, :]
    x1, x2 = x[..., :half], x[..., half:]
    return jnp.concatenate([x1 * cos - x2 * sin, x1 * sin + x2 * cos], axis=-1)


def chunked_scan(step, state, seqs):
    L = seqs[0].shape[1]
    cl = CHUNK if L % CHUNK == 0 else L
    nc = L // cl
    xs = tuple(jnp.swapaxes(s.reshape((s.shape[0], nc, cl) + s.shape[2:]), 0, 1) for s in seqs)
    state, ys = lax.scan(step, state, xs)
    ys = jnp.swapaxes(ys, 0, 1)
    return ys.reshape((ys.shape[0], L) + ys.shape[3:]), state


def mlstm_chunk(state, inp):
    C, n, m = state
    q, k, v, logi, logf = inp
    L = q.shape[1]
    b = jnp.cumsum(logf, axis=1)
    a = b + m[:, None, :]
    causal = jnp.tril(jnp.ones((L, L), dtype=bool))[None, :, :, None]
    dmat = jnp.where(causal, b[:, :, None, :] - b[:, None, :, :] + logi[:, None, :, :], -jnp.inf)
    m_t = jnp.maximum(a, jnp.max(dmat, axis=2))
    w_intra = jnp.exp(dmat - m_t[:, :, None, :])
    w_inter = jnp.exp(a - m_t)
    s = jnp.einsum('bihd,bjhd->bijh', q, k) * w_intra
    num = jnp.einsum('bijh,bjhe->bihe', s, v) + w_inter[..., None] * jnp.einsum('bihd,bhde->bihe', q, C)
    den = jnp.sum(s, axis=2) + w_inter * jnp.einsum('bihd,bhd->bih', q, n)
    h = num / jnp.maximum(jnp.abs(den), jnp.exp(-m_t))[..., None]
    m_new = m_t[:, -1]
    w_old = jnp.exp(b[:, -1] + m - m_new)
    w_k = jnp.exp(b[:, -1:] - b + logi - m_new[:, None])
    C_new = w_old[..., None, None] * C + jnp.einsum('bjh,bjhd,bjhe->bhde', w_k, k, v)
    n_new = w_old[..., None] * n + jnp.einsum('bjh,bjhd->bhd', w_k, k)
    return (C_new, n_new, m_new), h


def retention_chunk(S, inp):
    q, k, v = inp
    L = q.shape[1]
    log_g = jnp.log1p(-jnp.exp2(-5.0 - jnp.arange(R_HEADS, dtype=jnp.float32)))
    idx = jnp.arange(L, dtype=jnp.float32)
    rel = idx[:, None] - idx[None, :]
    decay = jnp.where(rel[None] >= 0, jnp.exp(jnp.maximum(rel, 0.0)[None] * log_g[:, None, None]), 0.0)
    scores = jnp.einsum('bihd,bjhd->bhij', q, k) * decay[None]
    y = jnp.einsum('bhij,bjhe->bihe', scores, v)
    inter = jnp.exp((idx[:, None] + 1.0) * log_g[None, :])
    y = y + inter[None, :, :, None] * jnp.einsum('bihd,bhde->bihe', q, S)
    k_w = jnp.exp((L - 1.0 - idx)[:, None] * log_g[None, :])
    S_new = jnp.exp(L * log_g)[None, :, None, None] * S + jnp.einsum('jh,bjhd,bjhe->bhde', k_w, k, v)
    return S_new, y


def mixer(h, pos, state, w_in, b_gates, conv_w, conv_b, m_gn, r_gn, w_out):
    C, n, m, conv_buf, S = [s.astype(jnp.float32) for s in state]
    B, L, _ = h.shape
    z = jnp.einsum('bld,de->ble', h, w_in).astype(jnp.float32)
    offs = np.cumsum(IN_SPLITS)[:-1].tolist()
    qk_pre, mv, mo, gates, rq, rk, rv, rg = jnp.split(z, offs, axis=-1)
    xcat = jnp.concatenate([conv_buf, qk_pre], axis=1)
    cw = conv_w.astype(jnp.float32)
    qk = conv_b.astype(jnp.float32)
    for t in range(CONV_W):
        qk = qk + xcat[:, t:t + L] * cw[t]
    qk = jax.nn.silu(qk)
    new_conv = xcat[:, L:]
    mq, mk = jnp.split(qk, 2, axis=-1)
    mq = mq.reshape(B, L, M_HEADS, M_DH)
    mk = mk.reshape(B, L, M_HEADS, M_DH) * (M_DH ** -0.5)
    mvh = mv.reshape(B, L, M_HEADS, M_DH)
    gates = gates + b_gates.astype(jnp.float32)
    logi = gates[..., :M_HEADS]
    logf = jax.nn.log_sigmoid(gates[..., M_HEADS:])
    h_m, (C, n, m) = chunked_scan(mlstm_chunk, (C, n, m), (mq, mk, mvh, logi, logf))
    out_m = head_norm(h_m, m_gn) * jax.nn.sigmoid(mo)
    rqh = rotary(rq.reshape(B, L, R_HEADS, R_DH), pos)
    rkh = rotary(rk.reshape(B, L, R_HEADS, R_DH), pos) * (R_DH ** -0.5)
    y_r, S = chunked_scan(retention_chunk, S, (rqh, rkh, rv.reshape(B, L, R_HEADS, R_DH)))
    out_r = head_norm(y_r, r_gn) * jax.nn.silu(rg)
    out = jnp.concatenate([out_m, out_r], axis=-1).astype(h.dtype)
    return out @ w_out, (C, n, m, new_conv, S)


def moe(h, w_router, b_router, we_g, we_u, we_d, ws_g, ws_u, ws_d):
    T, D = h.shape
    s = jax.nn.sigmoid(jnp.dot(h, w_router).astype(jnp.float32))
    sb = s + b_router.astype(jnp.float32)
    gsz = N_EXPERTS // N_GROUPS
    gscore = jnp.sum(lax.top_k(sb.reshape(T, N_GROUPS, gsz), 2)[0], axis=-1)
    _, gidx = lax.top_k(gscore, TOPK_GROUPS)
    gmask = jnp.sum(jax.nn.one_hot(gidx, N_GROUPS, dtype=jnp.float32), axis=1)
    emask = jnp.repeat(gmask, gsz, axis=1) > 0
    _, eidx = lax.top_k(jnp.where(emask, sb, -jnp.inf), TOP_K)
    wts = jnp.take_along_axis(s, eidx, axis=1)
    wts = wts / jnp.sum(wts, axis=-1, keepdims=True) * ROUTED_SCALE
    A = T * TOP_K
    flat_e = eidx.reshape(A)
    flat_t = jnp.repeat(jnp.arange(T, dtype=jnp.int32), TOP_K)
    flat_w = wts.reshape(A)
    order = jnp.argsort(flat_e)
    se = flat_e[order]
    counts = jnp.bincount(flat_e, length=N_EXPERTS)
    padded = (counts + MOE_BLOCK - 1) // MOE_BLOCK * MOE_BLOCK
    pend = jnp.cumsum(padded)
    dest = (pend - padded)[se] + jnp.arange(A) - (jnp.cumsum(counts) - counts)[se]
    nb = -(-A // MOE_BLOCK) + N_EXPERTS
    buf_t = jnp.zeros((nb * MOE_BLOCK,), jnp.int32).at[dest].set(flat_t[order])
    buf_w = jnp.zeros((nb * MOE_BLOCK,), jnp.float32).at[dest].set(flat_w[order])
    block_e = jnp.minimum(jnp.searchsorted(pend, jnp.arange(nb) * MOE_BLOCK, side='right'), N_EXPERTS - 1)

    def step(acc, blk):
        e, tok, wt = blk
        xb = h[tok]
        hb = jax.nn.silu(xb @ we_g[e]) * (xb @ we_u[e])
        return acc.at[tok].add(wt[:, None] * (hb @ we_d[e]).astype(jnp.float32)), None

    routed, _ = lax.scan(step, jnp.zeros((T, D), jnp.float32),
                         (block_e, buf_t.reshape(nb, MOE_BLOCK), buf_w.reshape(nb, MOE_BLOCK)))
    shared = (jax.nn.silu(h @ ws_g) * (h @ ws_u)) @ ws_d
    return (routed + shared.astype(jnp.float32)).astype(h.dtype)


def block(x, c, pos, state, lp):
    (w_ada, b_ada, g_mix_pre, g_mix_post, g_ffn_pre, g_ffn_post, w_in, b_gates,
     conv_w, conv_b, m_gn, r_gn, w_out, w_router, b_router,
     w_exp_gate, w_exp_up, w_exp_down, w_sh_gate, w_sh_up, w_sh_down) = lp
    mod = (jax.nn.silu(c) @ w_ada + b_ada)[:, None, :]
    sh1, sc1, gt1, sh2, sc2, gt2 = jnp.split(mod, 6, axis=-1)
    hm = rms_norm(x, g_mix_pre) * (1.0 + sc1) + sh1
    ym, new_state = mixer(hm, pos, state, w_in, b_gates, conv_w, conv_b, m_gn, r_gn, w_out)
    x = x + gt1 * rms_norm(ym, g_mix_post)
    hf = rms_norm(x, g_ffn_pre) * (1.0 + sc2) + sh2
    B, L, D = hf.shape
    yf = moe(hf.reshape(B * L, D), w_router, b_router, w_exp_gate, w_exp_up, w_exp_down,
             w_sh_gate, w_sh_up, w_sh_down).reshape(B, L, D)
    x = x + gt2 * rms_norm(yf, g_ffn_post)
    return x, new_state


def setup_inputs(seed: int = 0) -> dict:
    key = jax.random.key(seed)
    k = jax.random.split(key, 32)
    f32 = jnp.float32

    def nrm(kk, shape, scale):
        return scale * jax.random.normal(kk, shape, f32)

    def gain(kk, shape):
        return 1.0 + nrm(kk, shape, 0.05)

    E, F, FS, D = N_EXPERTS, D_EXPERT, D_SHARED, D_MODEL
    b_i = nrm(k[29], (DEPTH, M_HEADS), 0.1)
    b_f = jnp.linspace(3.0, 6.0, M_HEADS, dtype=f32)[None, :] + nrm(k[30], (DEPTH, M_HEADS), 0.1)
    return {
        'x_prompt': nrm(k[0], (BATCH, SEQ, D), 1.0),
        'x_sample': nrm(k[1], (DEC_BATCH, DEC_SEQ, D), 1.0),
        'c_prompt': nrm(k[2], (BATCH, D), 1.0),
        'c_sample': nrm(k[3], (DEC_BATCH, D), 1.0),
        'state_mlstm_C': nrm(k[4], (DEPTH, DEC_BATCH, M_HEADS, M_DH, M_DH), 0.2),
        'state_mlstm_n': nrm(k[5], (DEPTH, DEC_BATCH, M_HEADS, M_DH), 0.2),
        'state_mlstm_m': nrm(k[6], (DEPTH, DEC_BATCH, M_HEADS), 1.0),
        'state_mlstm_conv': nrm(k[7], (DEPTH, DEC_BATCH, CONV_W - 1, 2 * M_WIDTH), 1.0),
        'state_ret_S': nrm(k[8], (DEPTH, DEC_BATCH, R_HEADS, R_DH, R_DH), 0.2),
        'w_ada': nrm(k[9], (DEPTH, D, 6 * D), 0.5 * D ** -0.5),
        'b_ada': nrm(k[10], (DEPTH, 6 * D), 0.02),
        'g_mix_pre': gain(k[11], (DEPTH, D)),
        'g_mix_post': gain(k[12], (DEPTH, D)),
        'g_ffn_pre': gain(k[13], (DEPTH, D)),
        'g_ffn_post': gain(k[14], (DEPTH, D)),
        'w_in': nrm(k[15], (DEPTH, D, D_IN), D ** -0.5),
        'b_gates': jnp.concatenate([b_i, b_f], axis=-1),
        'conv_w': nrm(k[16], (DEPTH, CONV_W, 2 * M_WIDTH), CONV_W ** -0.5),
        'conv_b': nrm(k[17], (DEPTH, 2 * M_WIDTH), 0.02),
        'm_gn': gain(k[18], (DEPTH, M_WIDTH)),
        'r_gn': gain(k[19], (DEPTH, R_WIDTH)),
        'w_out': nrm(k[20], (DEPTH, MIX_WIDTH, D), MIX_WIDTH ** -0.5),
        'w_router': nrm(k[21], (DEPTH, D, E), D ** -0.5),
        'b_router': nrm(k[22], (DEPTH, E), 0.01),
        'w_exp_gate': nrm(k[23], (DEPTH, E, D, F), D ** -0.5),
        'w_exp_up': nrm(k[24], (DEPTH, E, D, F), D ** -0.5),
        'w_exp_down': nrm(k[25], (DEPTH, E, F, D), F ** -0.5),
        'w_sh_gate': nrm(k[26], (DEPTH, D, FS), D ** -0.5),
        'w_sh_up': nrm(k[27], (DEPTH, D, FS), D ** -0.5),
        'w_sh_down': nrm(k[28], (DEPTH, FS, D), FS ** -0.5),
    }


def reference(x_prompt, x_sample, c_prompt, c_sample, state_mlstm_C, state_mlstm_n, state_mlstm_m,
              state_mlstm_conv, state_ret_S, w_ada, b_ada, g_mix_pre, g_mix_post, g_ffn_pre, g_ffn_post,
              w_in, b_gates, conv_w, conv_b, m_gn, r_gn, w_out, w_router, b_router,
              w_exp_gate, w_exp_up, w_exp_down, w_sh_gate, w_sh_up, w_sh_down):
    f32 = jnp.float32
    pos_p = jnp.arange(x_prompt.shape[1], dtype=jnp.int32)
    pos_s = PAST_LEN + jnp.arange(x_sample.shape[1], dtype=jnp.int32)
    hp, hs = x_prompt, x_sample
    bp = x_prompt.shape[0]
    new_p, new_s = [], []
    for l in range(DEPTH):
        lp = (w_ada[l], b_ada[l], g_mix_pre[l], g_mix_post[l], g_ffn_pre[l], g_ffn_post[l], w_in[l], b_gates[l],
              conv_w[l], conv_b[l], m_gn[l], r_gn[l], w_out[l], w_router[l], b_router[l],
              w_exp_gate[l], w_exp_up[l], w_exp_down[l], w_sh_gate[l], w_sh_up[l], w_sh_down[l])
        init = (jnp.zeros((bp, M_HEADS, M_DH, M_DH), f32), jnp.zeros((bp, M_HEADS, M_DH), f32),
                jnp.zeros((bp, M_HEADS), f32), jnp.zeros((bp, CONV_W - 1, 2 * M_WIDTH), f32),
                jnp.zeros((bp, R_HEADS, R_DH, R_DH), f32))
        hp, sp = block(hp, c_prompt, pos_p, init, lp)
        past = (state_mlstm_C[l], state_mlstm_n[l], state_mlstm_m[l], state_mlstm_conv[l], state_ret_S[l])
        hs, ss = block(hs, c_sample, pos_s, past, lp)
        new_p.append(sp)
        new_s.append(ss)
    p_C, p_n, p_m, p_conv, p_S = [jnp.stack([st[i] for st in new_p]) for i in range(5)]
    s_C, s_n, s_m, s_conv, s_S = [jnp.stack([st[i] for st in new_s]) for i in range(5)]
    return (hp, hs, p_C, p_n, p_m, p_conv, p_S, s_C, s_n, s_m, s_conv, s_S)
```

```python
import functools
import math

import jax
import jax.numpy as jnp
from jax import lax
from jax.experimental import pallas as pl
from jax.experimental.pallas import tpu as pltpu

F32 = jnp.float32
MXU_DTYPE = jnp.bfloat16

NORM_EPS = 1e-6
ROPE_BASE = 10000.0
PAST_LEN = 16384
CHUNK = 128
CONV_W = 4
DH = 128
TOP_K = 8
N_GROUPS = 8
TOPK_GROUPS = 4
ROUTED_SCALE = 2.5
GATE_LANES = 128

V7X_VMEM_BYTES = 64 * 1024 * 1024
VMEM_LIMIT = 48 * 1024 * 1024

TOK_TILE = 256
MOE_TILE = 256
COMB_TILE = 128


def _cparams(sem):
    return pltpu.CompilerParams(dimension_semantics=sem, vmem_limit_bytes=VMEM_LIMIT)


def _silu(x):
    return x * jax.nn.sigmoid(x)


def _rms(x, g):
    return x * lax.rsqrt(jnp.mean(x * x, axis=-1, keepdims=True) + NORM_EPS) * g


def _mm(a, b):
    return jnp.dot(a.astype(MXU_DTYPE), b.astype(MXU_DTYPE), preferred_element_type=F32)


def _mm_nt(a, b):
    return lax.dot_general(a.astype(MXU_DTYPE), b.astype(MXU_DTYPE), (((1,), (1,)), ((), ())),
                           preferred_element_type=F32)


def _mm_tn(a, b):
    return lax.dot_general(a.astype(MXU_DTYPE), b.astype(MXU_DTYPE), (((0,), (0,)), ((), ())),
                           preferred_element_type=F32)


def _ada_kernel(c_ref, w_ref, b_ref, o_ref):
    o_ref[...] = _mm(_silu(c_ref[...]), w_ref[...]) + b_ref[...]


def _ada(c, w_ada, b_ada):
    n, d = c.shape
    dout = w_ada.shape[1]
    tn = 1024
    return pl.pallas_call(
        _ada_kernel,
        grid=(dout // tn,),
        in_specs=[pl.BlockSpec((n, d), lambda j: (0, 0)),
                  pl.BlockSpec((d, tn), lambda j: (0, j)),
                  pl.BlockSpec((1, tn), lambda j: (0, j))],
        out_specs=pl.BlockSpec((n, tn), lambda j: (0, j)),
        out_shape=jax.ShapeDtypeStruct((n, dout), F32),
        compiler_params=_cparams(("arbitrary",)),
        name="ada",
    )(c, w_ada, b_ada.reshape(1, dout))


def _inproj_kernel(x_ref, sc_ref, sh_ref, g_ref, w_ref, z_ref):
    x = x_ref[...]
    h = _rms(x, g_ref[...]) * (1.0 + sc_ref[...]) + sh_ref[...]
    bb, ll, d = x.shape
    z_ref[...] = _mm(h.reshape(bb * ll, d), w_ref[...])


def _inproj(x, sc, sh, g, w):
    b, l, d = x.shape
    n = w.shape[1]
    ll = min(l, TOK_TILE)
    bb = TOK_TILE // ll
    mod_spec = pl.BlockSpec((bb, 1, d), lambda i, j: (i, 0, 0))
    return pl.pallas_call(
        _inproj_kernel,
        grid=(b // bb, l // ll),
        in_specs=[pl.BlockSpec((bb, ll, d), lambda i, j: (i, j, 0)), mod_spec, mod_spec,
                  pl.BlockSpec((1, d), lambda i, j: (0, 0)),
                  pl.BlockSpec((d, n), lambda i, j: (0, 0))],
        out_specs=pl.BlockSpec((bb * ll, n), lambda i, j: (i * (l // ll) + j, 0)),
        out_shape=jax.ShapeDtypeStruct((b * l, n), F32),
        compiler_params=_cparams(("arbitrary", "arbitrary")),
        name="inproj",
    )(x, sc, sh, g, w)


def _cumsum_rows(x):
    n = x.shape[0]
    row = lax.broadcasted_iota(jnp.int32, x.shape, 0)
    s = 1
    while s < n:
        x = x + jnp.where(row >= s, pltpu.roll(x, s, axis=0), 0.0)
        s *= 2
    return x


def _head_norm(y, g):
    mu = jnp.mean(y, axis=-1, keepdims=True)
    yc = y - mu
    var = jnp.mean(yc * yc, axis=-1, keepdims=True)
    return yc * lax.rsqrt(var + NORM_EPS) * g


def _log_sigmoid(x):
    return jnp.minimum(x, 0.0) - jnp.log1p(jnp.exp(-jnp.abs(x)))


def _mixcore_kernel(z_ref, cos_ref, sin_ref, c0_ref, n0_ref, m0_ref, conv0_ref, s0_ref,
                    convw_ref, convb_ref, bg_ref, mgn_ref, rgn_ref,
                    out_ref, c_ref, n_ref, m_ref, conv_ref, s_ref,
                    xcat_ref, *, cl, lv, mh, rh):
    @pl.when(pl.program_id(1) == 0)
    def _():
        c_ref[...] = c0_ref[...]
        n_ref[...] = n0_ref[...]
        m_ref[...] = m0_ref[...]
        s_ref[...] = s0_ref[...]
        xcat_ref[8 - (CONV_W - 1):8, :] = conv0_ref[0]

    mw = mh * DH
    rw = rh * DH

    def rows(lo, hi):
        x = z_ref[0, :, lo:hi]
        if lv == cl:
            return x
        return jnp.concatenate([x, jnp.zeros((cl - lv, hi - lo), F32)], axis=0)

    o_qk, o_mv, o_mo = 0, 2 * mw, 3 * mw
    o_rq = 4 * mw
    o_rk, o_rv, o_rg, o_gt = o_rq + rw, o_rq + 2 * rw, o_rq + 3 * rw, o_rq + 4 * rw

    xcat_ref[8:8 + cl, :] = rows(o_qk, o_qk + 2 * mw)
    qk = convb_ref[...]
    for t in range(CONV_W):
        qk = qk + xcat_ref[8 - (CONV_W - 1) + t:8 - (CONV_W - 1) + t + cl, :] * convw_ref[t:t + 1, :]
    qk = _silu(qk)
    new_conv = xcat_ref[8 + lv - (CONV_W - 1):8 + lv, :]
    conv_ref[0] = new_conv
    xcat_ref[8 - (CONV_W - 1):8, :] = new_conv

    row_c = lax.broadcasted_iota(jnp.int32, (cl, 1), 0)
    ii = lax.broadcasted_iota(jnp.int32, (cl, cl), 0)
    jj = lax.broadcasted_iota(jnp.int32, (cl, cl), 1)
    causal = jj <= ii
    eye = jj == ii
    valid_c = row_c < lv

    def to_row(col):
        return jnp.sum(jnp.where(eye, col, 0.0), axis=0, keepdims=True)

    g = rows(o_gt, o_gt + GATE_LANES) + bg_ref[...]
    logf = jnp.where(valid_c, _log_sigmoid(g), 0.0)
    bcum = _cumsum_rows(logf)
    m_prev = m_ref[0]
    a_all = bcum + m_prev
    lane = lax.broadcasted_iota(jnp.int32, (1, GATE_LANES), 1)
    m_new_row = m_prev

    for h in range(mh):
        q = qk[:, h * DH:(h + 1) * DH]
        k = qk[:, mw + h * DH:mw + (h + 1) * DH] * (DH ** -0.5)
        v = rows(o_mv + h * DH, o_mv + (h + 1) * DH)
        b_col = bcum[:, mh + h:mh + h + 1]
        logi_col = jnp.where(valid_c, g[:, h:h + 1], -jnp.inf)
        a_col = a_all[:, mh + h:mh + h + 1]
        dmat = jnp.where(causal, b_col + to_row(logi_col - b_col), -jnp.inf)
        m_t = jnp.maximum(a_col, jnp.max(dmat, axis=1, keepdims=True))
        w_intra = jnp.exp(dmat - m_t)
        w_inter = jnp.exp(a_col - m_t)
        c_old = c_ref[0, h]
        n_old = n_ref[0, h:h + 1, :]
        s = _mm_nt(q, k) * w_intra
        num = _mm(s, v) + w_inter * _mm(q, c_old)
        den = jnp.sum(s, axis=1, keepdims=True) + w_inter * jnp.sum(q * n_old, axis=1, keepdims=True)
        hh = num / jnp.maximum(jnp.abs(den), jnp.exp(-m_t))
        m_new = m_t[cl - 1:cl, :]
        b_last = b_col[cl - 1:cl, :]
        w_old = jnp.exp(b_last + m_prev[:, mh + h:mh + h + 1] - m_new)
        w_k = jnp.exp(b_last - b_col + logi_col - m_new)
        c_ref[0, h] = w_old * c_old + _mm_tn(k, w_k * v)
        n_ref[0, h:h + 1, :] = w_old * n_old + jnp.sum(w_k * k, axis=0, keepdims=True)
        m_new_row = jnp.where(lane == mh + h, m_new, m_new_row)
        om = _head_norm(hh, mgn_ref[:, h * DH:(h + 1) * DH]) * jax.nn.sigmoid(rows(o_mo + h * DH, o_mo + (h + 1) * DH))
        out_ref[0, :, h * DH:(h + 1) * DH] = om[:lv]
    m_ref[0] = m_new_row

    cos2 = cos_ref[...]
    sin2 = sin_ref[...]
    rel = (ii - jj).astype(F32)
    row_f = row_c.astype(F32)

    def rot(x):
        return x * cos2 + pltpu.roll(x, DH // 2, axis=1) * sin2

    for h in range(rh):
        log_g = math.log1p(-2.0 ** (-5.0 - h))
        q = rot(rows(o_rq + h * DH, o_rq + (h + 1) * DH))
        k = rot(rows(o_rk + h * DH, o_rk + (h + 1) * DH)) * (DH ** -0.5)
        v = rows(o_rv + h * DH, o_rv + (h + 1) * DH)
        decay = jnp.where(causal, jnp.exp(jnp.maximum(rel, 0.0) * log_g), 0.0)
        s_old = s_ref[0, h]
        y = _mm(_mm_nt(q, k) * decay, v) + jnp.exp((row_f + 1.0) * log_g) * _mm(q, s_old)
        k_w = jnp.where(valid_c, jnp.exp((lv - 1.0 - row_f) * log_g), 0.0)
        s_ref[0, h] = math.exp(lv * log_g) * s_old + _mm_tn(k, k_w * v)
        orr = _head_norm(y, rgn_ref[:, h * DH:(h + 1) * DH]) * _silu(rows(o_rg + h * DH, o_rg + (h + 1) * DH))
        out_ref[0, :, mw + h * DH:mw + (h + 1) * DH] = orr[:lv]


def _mixcore(z, cos2, sin2, c0, n0, m0, conv0, s0, convw, convb, bg, mgn, rgn):
    b, l, nz = z.shape
    mh, rh = c0.shape[1], s0.shape[1]
    mw, rw = mh * DH, rh * DH
    cl = CHUNK
    lv = cl if l % cl == 0 else l
    nc = l // lv
    bmap4 = lambda i, c: (i, 0, 0, 0)
    bmap3 = lambda i, c: (i, 0, 0)
    wmap = lambda i, c: (0, 0)
    state_specs = [pl.BlockSpec((1, mh, DH, DH), bmap4), pl.BlockSpec((1, mh, DH), bmap3),
                   pl.BlockSpec((1, 1, GATE_LANES), bmap3), pl.BlockSpec((1, CONV_W - 1, 2 * mw), bmap3),
                   pl.BlockSpec((1, rh, DH, DH), bmap4)]
    state_shapes = [jax.ShapeDtypeStruct(c0.shape, F32), jax.ShapeDtypeStruct(n0.shape, F32),
                    jax.ShapeDtypeStruct(m0.shape, F32), jax.ShapeDtypeStruct(conv0.shape, F32),
                    jax.ShapeDtypeStruct(s0.shape, F32)]
    return pl.pallas_call(
        functools.partial(_mixcore_kernel, cl=cl, lv=lv, mh=mh, rh=rh),
        grid=(b, nc),
        in_specs=[pl.BlockSpec((1, lv, nz), lambda i, c: (i, c, 0)),
                  pl.BlockSpec((cl, DH), lambda i, c: (c, 0)), pl.BlockSpec((cl, DH), lambda i, c: (c, 0))]
                 + state_specs
                 + [pl.BlockSpec(convw.shape, wmap), pl.BlockSpec(convb.shape, wmap), pl.BlockSpec(bg.shape, wmap),
                    pl.BlockSpec(mgn.shape, wmap), pl.BlockSpec(rgn.shape, wmap)],
        out_specs=[pl.BlockSpec((1, lv, mw + rw), lambda i, c: (i, c, 0))] + state_specs,
        out_shape=[jax.ShapeDtypeStruct((b, l, mw + rw), F32)] + state_shapes,
        scratch_shapes=[pltpu.VMEM((cl + 8, 2 * mw), F32)],
        compiler_params=_cparams(("arbitrary", "arbitrary")),
        name="mixcore",
    )(z, cos2, sin2, c0, n0, m0, conv0, s0, convw, convb, bg, mgn, rgn)


def _split_hi_lo(x):
    hi = x.astype(MXU_DTYPE)
    lo = (x - hi.astype(F32)).astype(MXU_DTYPE)
    return hi, lo


def _mixout_kernel(o_ref, x_ref, gt1_ref, sc2_ref, sh2_ref, gpost_ref, gpre_ref, wout_ref, wr_hi_ref, wr_lo_ref,
                   x1_ref, hf_ref, lg_ref):
    bb, ll, d = x_ref.shape
    o = o_ref[...]
    ym = _mm(o.reshape(bb * ll, o.shape[-1]), wout_ref[...]).reshape(bb, ll, d)
    x1 = x_ref[...] + gt1_ref[...] * _rms(ym, gpost_ref[...])
    x1_ref[...] = x1
    hf = (_rms(x1, gpre_ref[...]) * (1.0 + sc2_ref[...]) + sh2_ref[...]).reshape(bb * ll, d)
    hf_ref[...] = hf
    hi, lo = _split_hi_lo(hf)
    lg_ref[...] = _mm(hi, wr_hi_ref[...]) + (_mm(lo, wr_hi_ref[...]) + _mm(hi, wr_lo_ref[...]))


def _mixout(o, x, gt1, sc2, sh2, gpost, gpre, wout, wr_hi, wr_lo):
    b, l, d = x.shape
    w = o.shape[-1]
    ne = wr_hi.shape[1]
    ll = min(l, TOK_TILE)
    bb = TOK_TILE // ll
    tmap = lambda i, j: (i, j, 0)
    fmap = lambda i, j: (i * (l // ll) + j, 0)
    mod_spec = pl.BlockSpec((bb, 1, d), lambda i, j: (i, 0, 0))
    wmap = lambda i, j: (0, 0)
    return pl.pallas_call(
        _mixout_kernel,
        grid=(b // bb, l // ll),
        in_specs=[pl.BlockSpec((bb, ll, w), tmap), pl.BlockSpec((bb, ll, d), tmap), mod_spec, mod_spec, mod_spec,
                  pl.BlockSpec((1, d), wmap), pl.BlockSpec((1, d), wmap), pl.BlockSpec((w, d), wmap),
                  pl.BlockSpec((d, ne), wmap), pl.BlockSpec((d, ne), wmap)],
        out_specs=[pl.BlockSpec((bb, ll, d), tmap), pl.BlockSpec((bb * ll, d), fmap), pl.BlockSpec((bb * ll, ne), fmap)],
        out_shape=[jax.ShapeDtypeStruct((b, l, d), F32), jax.ShapeDtypeStruct((b * l, d), F32),
                   jax.ShapeDtypeStruct((b * l, ne), F32)],
        compiler_params=_cparams(("arbitrary", "arbitrary")),
        name="mixout",
    )(o, x, gt1, sc2, sh2, gpost, gpre, wout, wr_hi, wr_lo)


def _router_kernel(lg_ref, br_ref, gate_ref, sel_ref, *, ne):
    tm = lg_ref.shape[0]
    gsz = ne // N_GROUPS
    s = jax.nn.sigmoid(lg_ref[...].T[:ne, :])
    sb = s + br_ref[...]
    sb3 = sb.reshape(N_GROUPS, gsz, tm)
    e3 = lax.broadcasted_iota(jnp.int32, sb3.shape, 1)
    m1 = jnp.max(sb3, axis=1, keepdims=True)
    first = jnp.min(jnp.where(sb3 == m1, e3, gsz), axis=1, keepdims=True)
    m2 = jnp.max(jnp.where(e3 == first, -jnp.inf, sb3), axis=1, keepdims=True)
    gs = (m1 + m2).reshape(N_GROUPS, tm)
    gi = lax.broadcasted_iota(jnp.int32, gs.shape, 0)
    grank = jnp.zeros(gs.shape, jnp.int32)
    for g in range(N_GROUPS):
        r = gs[g:g + 1, :]
        grank = grank + jnp.where((r > gs) | ((r == gs) & (g < gi)), 1, 0)
    gsel = jnp.where(grank < TOPK_GROUPS, 1.0, 0.0)
    emask = jnp.broadcast_to(gsel.reshape(N_GROUPS, 1, tm), sb3.shape).reshape(ne, tm) > 0.0
    masked = jnp.where(emask, sb, -jnp.inf)
    ei = lax.broadcasted_iota(jnp.int32, masked.shape, 0)
    rank = jnp.zeros(masked.shape, jnp.int32)
    for j in range(ne):
        r = masked[j:j + 1, :]
        rank = rank + jnp.where((r > masked) | ((r == masked) & (j < ei)), 1, 0)
    sel = rank < TOP_K
    w = jnp.where(sel, s, 0.0)
    w = w / jnp.sum(w, axis=0, keepdims=True) * ROUTED_SCALE
    pad = jnp.zeros((gate_ref.shape[1] - ne, tm), F32)
    gate_ref[...] = jnp.concatenate([w, pad], axis=0).T
    sel_ref[...] = jnp.concatenate([jnp.where(sel, 1.0, 0.0), pad], axis=0).T


def _router(logits, b_router, ne):
    t, lanes = logits.shape
    tm = TOK_TILE
    spec = pl.BlockSpec((tm, lanes), lambda i: (i, 0))
    return pl.pallas_call(
        functools.partial(_router_kernel, ne=ne),
        grid=(t // tm,),
        in_specs=[spec, pl.BlockSpec((ne, 1), lambda i: (0, 0))],
        out_specs=[spec, spec],
        out_shape=[jax.ShapeDtypeStruct((t, lanes), F32)] * 2,
        compiler_params=_cparams(("arbitrary",)),
        name="router",
    )(logits, b_router.reshape(ne, 1))


def _scatter_kernel(dest_ref, hf_ref, xs_in_ref, xs_ref, sem):
    del xs_in_ref
    tt = hf_ref.shape[0]

    def body(i, carry):
        for k in range(TOP_K):
            d = dest_ref[i * TOP_K + k]
            pltpu.make_async_copy(hf_ref.at[pl.ds(i, 1)], xs_ref.at[pl.ds(d, 1)], sem).start()
        return carry

    lax.fori_loop(0, tt, body, 0)
    for k in range(TOP_K):
        pltpu.make_async_copy(hf_ref, xs_ref.at[pl.ds(0, tt)], sem).wait()


def _scatter(dest_flat, hf, xs):
    t, d = hf.shape
    tt = TOK_TILE
    return pl.pallas_call(
        _scatter_kernel,
        grid=(t // tt,),
        in_specs=[pl.BlockSpec((tt * TOP_K,), lambda i: (i,), memory_space=pltpu.SMEM),
                  pl.BlockSpec((tt, d), lambda i: (i, 0)),
                  pl.BlockSpec(memory_space=pl.ANY)],
        out_specs=pl.BlockSpec(memory_space=pl.ANY),
        out_shape=jax.ShapeDtypeStruct(xs.shape, xs.dtype),
        scratch_shapes=[pltpu.SemaphoreType.DMA(())],
        input_output_aliases={2: 0},
        compiler_params=_cparams(("arbitrary",)),
        name="scatter",
    )(dest_flat, hf, xs)


def _ffn_kernel(be_ref, nbu_ref, xs_ref, wg_ref, wu_ref, wd_ref, ys_ref):
    del be_ref

    @pl.when(pl.program_id(0) < nbu_ref[0])
    def _():
        x = xs_ref[...]
        hb = _silu(_mm(x, wg_ref[0])) * _mm(x, wu_ref[0])
        ys_ref[...] = _mm(hb, wd_ref[0])

    @pl.when(pl.program_id(0) >= nbu_ref[0])
    def _():
        ys_ref[...] = jnp.zeros(ys_ref.shape, F32)


def _ffn(block_e, nb_used, xs, wg, wu, wd):
    rows, d = xs.shape
    f = wg.shape[2]
    tm = MOE_TILE
    nb = rows // tm
    xmap = lambda b, be, nbu: (jnp.minimum(b, nbu[0] - 1), 0)
    wmap = lambda b, be, nbu: (be[b], 0, 0)
    return pl.pallas_call(
        _ffn_kernel,
        grid_spec=pltpu.PrefetchScalarGridSpec(
            num_scalar_prefetch=2,
            grid=(nb,),
            in_specs=[pl.BlockSpec((tm, d), xmap), pl.BlockSpec((1, d, f), wmap), pl.BlockSpec((1, d, f), wmap),
                      pl.BlockSpec((1, f, d), wmap)],
            out_specs=pl.BlockSpec((tm, d), lambda b, be, nbu: (b, 0))),
        out_shape=jax.ShapeDtypeStruct((rows, d), F32),
        compiler_params=_cparams(("arbitrary",)),
        name="ffn",
    )(block_e, nb_used, xs, wg, wu, wd)


def _combine_kernel(dest_ref, wts_ref, hf_ref, x1_ref, gt2_ref, gpost_ref, wsg_ref, wsu_ref, wsd_ref, ys_ref,
                    y_ref, buf_ref, sem):
    bb, ll, d = x1_ref.shape
    tt = bb * ll

    def body(i, carry):
        for k in range(TOP_K):
            r = dest_ref[i * TOP_K + k]
            pltpu.make_async_copy(ys_ref.at[pl.ds(r, 1)], buf_ref.at[k, pl.ds(i, 1)], sem).start()
        return carry

    lax.fori_loop(0, tt, body, 0)
    hf = hf_ref[...]
    shared = _mm(_silu(_mm(hf, wsg_ref[...])) * _mm(hf, wsu_ref[...]), wsd_ref[...])
    for k in range(TOP_K):
        pltpu.make_async_copy(ys_ref.at[pl.ds(0, tt)], buf_ref.at[k], sem).wait()
    wts = wts_ref[...]
    acc = wts[:, 0:1] * buf_ref[0]
    for k in range(1, TOP_K):
        acc = acc + wts[:, k:k + 1] * buf_ref[k]
    yf = (acc + shared).reshape(bb, ll, d)
    y_ref[...] = x1_ref[...] + gt2_ref[...] * _rms(yf, gpost_ref[...])


def _combine(dest_flat, wts, hf, x1, gt2, gpost, wsg, wsu, wsd, ys):
    b, l, d = x1.shape
    ll = min(l, COMB_TILE)
    bb = COMB_TILE // ll
    tt = COMB_TILE
    nl = l // ll
    fs = wsg.shape[1]
    tmap = lambda i, j: (i, j, 0)
    wmap = lambda i, j: (0, 0)
    return pl.pallas_call(
        _combine_kernel,
        grid=(b // bb, nl),
        in_specs=[pl.BlockSpec((tt * TOP_K,), lambda i, j: (i * nl + j,), memory_space=pltpu.SMEM),
                  pl.BlockSpec((tt, TOP_K), lambda i, j: (i * nl + j, 0)),
                  pl.BlockSpec((tt, d), lambda i, j: (i * nl + j, 0)),
                  pl.BlockSpec((bb, ll, d), tmap),
                  pl.BlockSpec((bb, 1, d), lambda i, j: (i, 0, 0)),
                  pl.BlockSpec((1, d), wmap), pl.BlockSpec((d, fs), wmap), pl.BlockSpec((d, fs), wmap),
                  pl.BlockSpec((fs, d), wmap),
                  pl.BlockSpec(memory_space=pl.ANY)],
        out_specs=pl.BlockSpec((bb, ll, d), tmap),
        out_shape=jax.ShapeDtypeStruct((b, l, d), F32),
        scratch_shapes=[pltpu.VMEM((TOP_K, tt, d), F32), pltpu.SemaphoreType.DMA(())],
        compiler_params=_cparams(("arbitrary", "arbitrary")),
        name="combine",
    )(dest_flat, wts, hf, x1, gt2, gpost, wsg, wsu, wsd, ys)


def _rope_tables(pos, rows):
    half = DH // 2
    freqs = ROPE_BASE ** (-jnp.arange(half, dtype=F32) / half)
    ang = pos.astype(F32)[:, None] * freqs[None, :]
    cos, sin = jnp.cos(ang), jnp.sin(ang)
    cos2 = jnp.concatenate([cos, cos], axis=1)
    sin2 = jnp.concatenate([-sin, sin], axis=1)
    padr = rows - pos.shape[0]
    return jnp.pad(cos2, ((0, padr), (0, 0))), jnp.pad(sin2, ((0, padr), (0, 0)))


def _dispatch_plan(sel, gates, ne):
    t = sel.shape[0]
    mask = (sel[:, :ne] > 0.0).astype(jnp.int32)
    pos = jnp.cumsum(mask, axis=0) - mask
    counts = jnp.sum(mask, axis=0)
    padded = (counts + MOE_TILE - 1) // MOE_TILE * MOE_TILE
    pend = jnp.cumsum(padded)
    destfull = (pend - padded)[None, :] + pos
    slot = jnp.cumsum(mask, axis=1) - mask
    onehot = (mask[:, :, None] > 0) & (slot[:, :, None] == jnp.arange(TOP_K, dtype=jnp.int32)[None, None, :])
    dest = jnp.sum(jnp.where(onehot, destfull[:, :, None], 0), axis=1).astype(jnp.int32)
    wts = jnp.sum(jnp.where(onehot, gates[:, :ne, None], 0.0), axis=1)
    nb = t * TOP_K // MOE_TILE + ne
    starts = jnp.arange(nb, dtype=jnp.int32) * MOE_TILE
    block_e = jnp.minimum(jnp.sum((pend[None, :] <= starts[:, None]).astype(jnp.int32), axis=1),
                          ne - 1).astype(jnp.int32)
    nb_used = (pend[-1:] // MOE_TILE).astype(jnp.int32)
    return dest, wts, block_e, nb_used, nb


def kernel(x_prompt, x_sample, c_prompt, c_sample, state_mlstm_C, state_mlstm_n, state_mlstm_m, state_mlstm_conv, state_ret_S, w_ada, b_ada, g_mix_pre, g_mix_post, g_ffn_pre, g_ffn_post, w_in, b_gates, conv_w, conv_b, m_gn, r_gn, w_out, w_router, b_router, w_exp_gate, w_exp_up, w_exp_down, w_sh_gate, w_sh_up, w_sh_down):
    depth = w_ada.shape[0]
    bp, lp, d = x_prompt.shape
    bs, ls, _ = x_sample.shape
    mh, rh = state_mlstm_C.shape[2], state_ret_S.shape[2]
    mw, rw = mh * DH, rh * DH
    ne = w_router.shape[2]
    tp, ts = bp * lp, bs * ls

    cos_p, sin_p = _rope_tables(jnp.arange(lp, dtype=jnp.int32), lp)
    cos_s, sin_s = _rope_tables(PAST_LEN + jnp.arange(ls, dtype=jnp.int32), CHUNK if ls % CHUNK else ls)

    hp, hs = x_prompt, x_sample
    new_p, new_s = [], []
    for l in range(depth):
        o = 0
        offs = []
        for wdt in (2 * mw, mw, mw, 2 * mh, rw, rw, rw, rw):
            offs.append((o, o + wdt))
            o += wdt
        wi = w_in[l]
        cols = [wi[:, a:b] for (a, b) in offs]
        w_in_r = jnp.concatenate(cols[:3] + cols[4:] + [cols[3], jnp.zeros((d, GATE_LANES - 2 * mh), F32)],
                                 axis=1).astype(MXU_DTYPE)
        bg = jnp.pad(b_gates[l], (0, GATE_LANES - 2 * mh)).reshape(1, GATE_LANES)
        wout = w_out[l].astype(MXU_DTYPE)
        wr = jnp.pad(w_router[l], ((0, 0), (0, GATE_LANES - ne)))
        wr_hi, wr_lo = _split_hi_lo(wr)
        wsg, wsu, wsd = (w_sh_gate[l].astype(MXU_DTYPE), w_sh_up[l].astype(MXU_DTYPE), w_sh_down[l].astype(MXU_DTYPE))
        row = lambda v: v.reshape(1, -1)

        mod = _ada(jnp.concatenate([c_prompt, c_sample], axis=0), w_ada[l], b_ada[l])
        mods_p = [m[:bp].reshape(bp, 1, d) for m in jnp.split(mod, 6, axis=1)]
        mods_s = [m[bp:].reshape(bs, 1, d) for m in jnp.split(mod, 6, axis=1)]

        def lane_m(m):
            return jnp.pad(m, ((0, 0), (mh, GATE_LANES - 2 * mh)))[:, None, :]

        init = (jnp.zeros((bp, mh, DH, DH), F32), jnp.zeros((bp, mh, DH), F32), lane_m(jnp.zeros((bp, mh), F32)),
                jnp.zeros((bp, CONV_W - 1, 2 * mw), F32), jnp.zeros((bp, rh, DH, DH), F32))
        past = (state_mlstm_C[l], state_mlstm_n[l], lane_m(state_mlstm_m[l]), state_mlstm_conv[l], state_ret_S[l])

        def mix(x, mods, state, cos2, sin2):
            sh1, sc1, gt1, sh2, sc2, gt2 = mods
            b, ll, _ = x.shape
            z = _inproj(x, sc1, sh1, row(g_mix_pre[l]), w_in_r).reshape(b, ll, -1)
            o_mix, c_n, n_n, m_n, conv_n, s_n = _mixcore(z, cos2, sin2, *state, conv_w[l], row(conv_b[l]), bg,
                                                         row(m_gn[l]), row(r_gn[l]))
            x1, hf, lg = _mixout(o_mix, x, gt1, sc2, sh2, row(g_mix_post[l]), row(g_ffn_pre[l]), wout, wr_hi, wr_lo)
            gates, sel = _router(lg, b_router[l], ne)
            return x1, hf, gates, sel, (c_n, n_n, m_n[:, 0, mh:2 * mh], conv_n, s_n)

        x1_p, hf_p, gates_p, sel_p, st_p = mix(hp, mods_p, init, cos_p, sin_p)
        x1_s, hf_s, gates_s, sel_s, st_s = mix(hs, mods_s, past, cos_s, sin_s)

        dest, wts, block_e, nb_used, nb = _dispatch_plan(jnp.concatenate([sel_p, sel_s], axis=0),
                                                         jnp.concatenate([gates_p, gates_s], axis=0), ne)
        dest_flat = dest.reshape(-1)
        xs = jnp.zeros((nb * MOE_TILE, d), F32)
        xs = _scatter(dest_flat[:tp * TOP_K], hf_p, xs)
        xs = _scatter(dest_flat[tp * TOP_K:], hf_s, xs)
        ys = _ffn(block_e, nb_used, xs, w_exp_gate[l], w_exp_up[l], w_exp_down[l])
        hp = _combine(dest_flat[:tp * TOP_K], wts[:tp], hf_p, x1_p, mods_p[5], row(g_ffn_post[l]), wsg, wsu, wsd, ys)
        hs = _combine(dest_flat[tp * TOP_K:], wts[tp:], hf_s, x1_s, mods_s[5], row(g_ffn_post[l]), wsg, wsu, wsd, ys)
        new_p.append(st_p)
        new_s.append(st_s)

    p_state = [jnp.stack([st[i] for st in new_p]) for i in range(5)]
    s_state = [jnp.stack([st[i] for st in new_s]) for i in range(5)]
    return (hp, hs, *p_state, *s_state)
```

```python
import functools
import math

import jax
import jax.numpy as jnp
from jax import lax
from jax.experimental import pallas as pl
from jax.experimental.pallas import tpu as pltpu

F32 = jnp.float32
MXU_DTYPE = jnp.bfloat16

NORM_EPS = 1e-6
ROPE_BASE = 10000.0
PAST_LEN = 16384
CHUNK = 128
CONV_W = 4
DH = 128
TOP_K = 8
N_GROUPS = 8
TOPK_GROUPS = 4
ROUTED_SCALE = 2.5
GATE_LANES = 128

V7X_VMEM_BYTES = 64 * 1024 * 1024
VMEM_LIMIT = 48 * 1024 * 1024

TOK_TILE = 256
MOE_TILE = 256
COMB_TILE = 128


def _cparams(sem):
    return pltpu.CompilerParams(dimension_semantics=sem, vmem_limit_bytes=VMEM_LIMIT)


def _silu(x):
    return x * jax.nn.sigmoid(x)


def _rms(x, g):
    return x * lax.rsqrt(jnp.mean(x * x, axis=-1, keepdims=True) + NORM_EPS) * g


def _mm(a, b):
    return jnp.dot(a.astype(MXU_DTYPE), b.astype(MXU_DTYPE), preferred_element_type=F32)


def _mm_nt(a, b):
    return lax.dot_general(a.astype(MXU_DTYPE), b.astype(MXU_DTYPE), (((1,), (1,)), ((), ())),
                           preferred_element_type=F32)


def _mm_tn(a, b):
    return lax.dot_general(a.astype(MXU_DTYPE), b.astype(MXU_DTYPE), (((0,), (0,)), ((), ())),
                           preferred_element_type=F32)


def _ada_kernel(c_ref, w_ref, b_ref, o_ref):
    o_ref[...] = _mm(_silu(c_ref[...]), w_ref[...]) + b_ref[...]


def _ada(c, w_ada, b_ada):
    n, d = c.shape
    dout = w_ada.shape[1]
    tn = 1024
    return pl.pallas_call(
        _ada_kernel,
        grid=(dout // tn,),
        in_specs=[pl.BlockSpec((n, d), lambda j: (0, 0)),
                  pl.BlockSpec((d, tn), lambda j: (0, j)),
                  pl.BlockSpec((1, tn), lambda j: (0, j))],
        out_specs=pl.BlockSpec((n, tn), lambda j: (0, j)),
        out_shape=jax.ShapeDtypeStruct((n, dout), F32),
        compiler_params=_cparams(("arbitrary",)),
        name="ada",
    )(c, w_ada, b_ada.reshape(1, dout))


def _inproj_kernel(x_ref, sc_ref, sh_ref, g_ref, w_ref, z_ref):
    x = x_ref[...]
    h = _rms(x, g_ref[...]) * (1.0 + sc_ref[...]) + sh_ref[...]
    bb, ll, d = x.shape
    z_ref[...] = _mm(h.reshape(bb * ll, d), w_ref[...])


def _inproj(x, sc, sh, g, w):
    b, l, d = x.shape
    n = w.shape[1]
    ll = min(l, TOK_TILE)
    bb = TOK_TILE // ll
    mod_spec = pl.BlockSpec((bb, 1, d), lambda i, j: (i, 0, 0))
    return pl.pallas_call(
        _inproj_kernel,
        grid=(b // bb, l // ll),
        in_specs=[pl.BlockSpec((bb, ll, d), lambda i, j: (i, j, 0)), mod_spec, mod_spec,
                  pl.BlockSpec((1, d), lambda i, j: (0, 0)),
                  pl.BlockSpec((d, n), lambda i, j: (0, 0))],
        out_specs=pl.BlockSpec((bb * ll, n), lambda i, j: (i * (l // ll) + j, 0)),
        out_shape=jax.ShapeDtypeStruct((b * l, n), F32),
        compiler_params=_cparams(("arbitrary", "arbitrary")),
        name="inproj",
    )(x, sc, sh, g, w)


def _cumsum_rows(x):
    n = x.shape[0]
    row = lax.broadcasted_iota(jnp.int32, x.shape, 0)
    s = 1
    while s < n:
        x = x + jnp.where(row >= s, pltpu.roll(x, s, axis=0), 0.0)
        s *= 2
    return x


def _head_norm(y, g):
    mu = jnp.mean(y, axis=-1, keepdims=True)
    yc = y - mu
    var = jnp.mean(yc * yc, axis=-1, keepdims=True)
    return yc * lax.rsqrt(var + NORM_EPS) * g


def _log_sigmoid(x):
    return jnp.minimum(x, 0.0) - jnp.log1p(jnp.exp(-jnp.abs(x)))


def _mixcore_kernel(z_ref, cos_ref, sin_ref, c0_ref, n0_ref, m0_ref, conv0_ref, s0_ref,
                    convw_ref, convb_ref, bg_ref, mgn_ref, rgn_ref,
                    out_ref, c_ref, n_ref, m_ref, conv_ref, s_ref,
                    xcat_ref, *, cl, lv, mh, rh):
    @pl.when(pl.program_id(1) == 0)
    def _():
        c_ref[...] = c0_ref[...]
        n_ref[...] = n0_ref[...]
        m_ref[...] = m0_ref[...]
        s_ref[...] = s0_ref[...]
        xcat_ref[8 - (CONV_W - 1):8, :] = conv0_ref[0]

    mw = mh * DH
    rw = rh * DH

    def rows(lo, hi):
        x = z_ref[0, :, lo:hi]
        if lv == cl:
            return x
        return jnp.concatenate([x, jnp.zeros((cl - lv, hi - lo), F32)], axis=0)

    o_qk, o_mv, o_mo = 0, 2 * mw, 3 * mw
    o_rq = 4 * mw
    o_rk, o_rv, o_rg, o_gt = o_rq + rw, o_rq + 2 * rw, o_rq + 3 * rw, o_rq + 4 * rw

    xcat_ref[8:8 + cl, :] = rows(o_qk, o_qk + 2 * mw)
    qk = convb_ref[...]
    for t in range(CONV_W):
        qk = qk + xcat_ref[8 - (CONV_W - 1) + t:8 - (CONV_W - 1) + t + cl, :] * convw_ref[t:t + 1, :]
    qk = _silu(qk)
    new_conv = xcat_ref[8 + lv - (CONV_W - 1):8 + lv, :]
    conv_ref[0] = new_conv
    xcat_ref[8 - (CONV_W - 1):8, :] = new_conv

    row_c = lax.broadcasted_iota(jnp.int32, (cl, 1), 0)
    ii = lax.broadcasted_iota(jnp.int32, (cl, cl), 0)
    jj = lax.broadcasted_iota(jnp.int32, (cl, cl), 1)
    causal = jj <= ii
    eye = jj == ii
    valid_c = row_c < lv

    def to_row(col):
        return jnp.sum(jnp.where(eye, col, 0.0), axis=0, keepdims=True)

    g = rows(o_gt, o_gt + GATE_LANES) + bg_ref[...]
    logf = jnp.where(valid_c, _log_sigmoid(g), 0.0)
    bcum = _cumsum_rows(logf)
    m_prev = m_ref[0]
    a_all = bcum + m_prev
    lane = lax.broadcasted_iota(jnp.int32, (1, GATE_LANES), 1)
    m_new_row = m_prev

    for h in range(mh):
        q = qk[:, h * DH:(h + 1) * DH]
        k = qk[:, mw + h * DH:mw + (h + 1) * DH] * (DH ** -0.5)
        v = rows(o_mv + h * DH, o_mv + (h + 1) * DH)
        b_col = bcum[:, mh + h:mh + h + 1]
        logi_col = jnp.where(valid_c, g[:, h:h + 1], -jnp.inf)
        a_col = a_all[:, mh + h:mh + h + 1]
        dmat = jnp.where(causal, b_col + to_row(logi_col - b_col), -jnp.inf)
        m_t = jnp.maximum(a_col, jnp.max(dmat, axis=1, keepdims=True))
        w_intra = jnp.exp(dmat - m_t)
        w_inter = jnp.exp(a_col - m_t)
        c_old = c_ref[0, h]
        n_old = n_ref[0, h:h + 1, :]
        s = _mm_nt(q, k) * w_intra
        num = _mm(s, v) + w_inter * _mm(q, c_old)
        den = jnp.sum(s, axis=1, keepdims=True) + w_inter * jnp.sum(q * n_old, axis=1, keepdims=True)
        hh = num / jnp.maximum(jnp.abs(den), jnp.exp(-m_t))
        m_new = m_t[cl - 1:cl, :]
        b_last = b_col[cl - 1:cl, :]
        w_old = jnp.exp(b_last + m_prev[:, mh + h:mh + h + 1] - m_new)
        w_k = jnp.exp(b_last - b_col + logi_col - m_new)
        c_ref[0, h] = w_old * c_old + _mm_tn(k, w_k * v)
        n_ref[0, h:h + 1, :] = w_old * n_old + jnp.sum(w_k * k, axis=0, keepdims=True)
        m_new_row = jnp.where(lane == mh + h, m_new, m_new_row)
        om = _head_norm(hh, mgn_ref[:, h * DH:(h + 1) * DH]) * jax.nn.sigmoid(rows(o_mo + h * DH, o_mo + (h + 1) * DH))
        out_ref[0, :, h * DH:(h + 1) * DH] = om[:lv]
    m_ref[0] = m_new_row

    cos2 = cos_ref[...]
    sin2 = sin_ref[...]
    rel = (ii - jj).astype(F32)
    row_f = row_c.astype(F32)

    def rot(x):
        return x * cos2 + pltpu.roll(x, DH // 2, axis=1) * sin2

    for h in range(rh):
        log_g = math.log1p(-2.0 ** (-5.0 - h))
        q = rot(rows(o_rq + h * DH, o_rq + (h + 1) * DH))
        k = rot(rows(o_rk + h * DH, o_rk + (h + 1) * DH)) * (DH ** -0.5)
        v = rows(o_rv + h * DH, o_rv + (h + 1) * DH)
        decay = jnp.where(causal, jnp.exp(jnp.maximum(rel, 0.0) * log_g), 0.0)
        s_old = s_ref[0, h]
        y = _mm(_mm_nt(q, k) * decay, v) + jnp.exp((row_f + 1.0) * log_g) * _mm(q, s_old)
        k_w = jnp.where(valid_c, jnp.exp((lv - 1.0 - row_f) * log_g), 0.0)
        s_ref[0, h] = math.exp(lv * log_g) * s_old + _mm_tn(k, k_w * v)
        orr = _head_norm(y, rgn_ref[:, h * DH:(h + 1) * DH]) * _silu(rows(o_rg + h * DH, o_rg + (h + 1) * DH))
        out_ref[0, :, mw + h * DH:mw + (h + 1) * DH] = orr[:lv]


def _mixcore(z, cos2, sin2, c0, n0, m0, conv0, s0, convw, convb, bg, mgn, rgn):
    b, l, nz = z.shape
    mh, rh = c0.shape[1], s0.shape[1]
    mw, rw = mh * DH, rh * DH
    cl = CHUNK
    lv = cl if l % cl == 0 else l
    nc = l // lv
    bmap4 = lambda i, c: (i, 0, 0, 0)
    bmap3 = lambda i, c: (i, 0, 0)
    wmap = lambda i, c: (0, 0)
    state_specs = [pl.BlockSpec((1, mh, DH, DH), bmap4), pl.BlockSpec((1, mh, DH), bmap3),
                   pl.BlockSpec((1, 1, GATE_LANES), bmap3), pl.BlockSpec((1, CONV_W - 1, 2 * mw), bmap3),
                   pl.BlockSpec((1, rh, DH, DH), bmap4)]
    state_shapes = [jax.ShapeDtypeStruct(c0.shape, F32), jax.ShapeDtypeStruct(n0.shape, F32),
                    jax.ShapeDtypeStruct(m0.shape, F32), jax.ShapeDtypeStruct(conv0.shape, F32),
                    jax.ShapeDtypeStruct(s0.shape, F32)]
    return pl.pallas_call(
        functools.partial(_mixcore_kernel, cl=cl, lv=lv, mh=mh, rh=rh),
        grid=(b, nc),
        in_specs=[pl.BlockSpec((1, lv, nz), lambda i, c: (i, c, 0)),
                  pl.BlockSpec((cl, DH), lambda i, c: (c, 0)), pl.BlockSpec((cl, DH), lambda i, c: (c, 0))]
                 + state_specs
                 + [pl.BlockSpec(convw.shape, wmap), pl.BlockSpec(convb.shape, wmap), pl.BlockSpec(bg.shape, wmap),
                    pl.BlockSpec(mgn.shape, wmap), pl.BlockSpec(rgn.shape, wmap)],
        out_specs=[pl.BlockSpec((1, lv, mw + rw), lambda i, c: (i, c, 0))] + state_specs,
        out_shape=[jax.ShapeDtypeStruct((b, l, mw + rw), F32)] + state_shapes,
        scratch_shapes=[pltpu.VMEM((cl + 8, 2 * mw), F32)],
        compiler_params=_cparams(("arbitrary", "arbitrary")),
        name="mixcore",
    )(z, cos2, sin2, c0, n0, m0, conv0, s0, convw, convb, bg, mgn, rgn)


def _split_hi_lo(x):
    hi = x.astype(MXU_DTYPE)
    lo = (x - hi.astype(F32)).astype(MXU_DTYPE)
    return hi, lo


def _mixout_kernel(o_ref, x_ref, gt1_ref, sc2_ref, sh2_ref, gpost_ref, gpre_ref, wout_ref, wr_hi_ref, wr_lo_ref,
                   x1_ref, hf_ref, lg_ref):
    bb, ll, d = x_ref.shape
    o = o_ref[...]
    ym = _mm(o.reshape(bb * ll, o.shape[-1]), wout_ref[...]).reshape(bb, ll, d)
    x1 = x_ref[...] + gt1_ref[...] * _rms(ym, gpost_ref[...])
    x1_ref[...] = x1
    hf = (_rms(x1, gpre_ref[...]) * (1.0 + sc2_ref[...]) + sh2_ref[...]).reshape(bb * ll, d)
    hf_ref[...] = hf
    hi, lo = _split_hi_lo(hf)
    lg_ref[...] = _mm(hi, wr_hi_ref[...]) + (_mm(lo, wr_hi_ref[...]) + _mm(hi, wr_lo_ref[...]))


def _mixout(o, x, gt1, sc2, sh2, gpost, gpre, wout, wr_hi, wr_lo):
    b, l, d = x.shape
    w = o.shape[-1]
    ne = wr_hi.shape[1]
    ll = min(l, TOK_TILE)
    bb = TOK_TILE // ll
    tmap = lambda i, j: (i, j, 0)
    fmap = lambda i, j: (i * (l // ll) + j, 0)
    mod_spec = pl.BlockSpec((bb, 1, d), lambda i, j: (i, 0, 0))
    wmap = lambda i, j: (0, 0)
    return pl.pallas_call(
        _mixout_kernel,
        grid=(b // bb, l // ll),
        in_specs=[pl.BlockSpec((bb, ll, w), tmap), pl.BlockSpec((bb, ll, d), tmap), mod_spec, mod_spec, mod_spec,
                  pl.BlockSpec((1, d), wmap), pl.BlockSpec((1, d), wmap), pl.BlockSpec((w, d), wmap),
                  pl.BlockSpec((d, ne), wmap), pl.BlockSpec((d, ne), wmap)],
        out_specs=[pl.BlockSpec((bb, ll, d), tmap), pl.BlockSpec((bb * ll, d), fmap), pl.BlockSpec((bb * ll, ne), fmap)],
        out_shape=[jax.ShapeDtypeStruct((b, l, d), F32), jax.ShapeDtypeStruct((b * l, d), F32),
                   jax.ShapeDtypeStruct((b * l, ne), F32)],
        compiler_params=_cparams(("arbitrary", "arbitrary")),
        name="mixout",
    )(o, x, gt1, sc2, sh2, gpost, gpre, wout, wr_hi, wr_lo)


def _router_kernel(lg_ref, br_ref, gate_ref, sel_ref, *, ne):
    tm = lg_ref.shape[0]
    gsz = ne // N_GROUPS
    s = jax.nn.sigmoid(lg_ref[...].T[:ne, :])
    sb = s + br_ref[...]
    sb3 = sb.reshape(N_GROUPS, gsz, tm)
    e3 = lax.broadcasted_iota(jnp.int32, sb3.shape, 1)
    m1 = jnp.max(sb3, axis=1, keepdims=True)
    first = jnp.min(jnp.where(sb3 == m1, e3, gsz), axis=1, keepdims=True)
    m2 = jnp.max(jnp.where(e3 == first, -jnp.inf, sb3), axis=1, keepdims=True)
    gs = (m1 + m2).reshape(N_GROUPS, tm)
    gi = lax.broadcasted_iota(jnp.int32, gs.shape, 0)
    grank = jnp.zeros(gs.shape, jnp.int32)
    for g in range(N_GROUPS):
        r = gs[g:g + 1, :]
        grank = grank + jnp.where((r > gs) | ((r == gs) & (g < gi)), 1, 0)
    gsel = jnp.where(grank < TOPK_GROUPS, 1.0, 0.0)
    emask = jnp.broadcast_to(gsel.reshape(N_GROUPS, 1, tm), sb3.shape).reshape(ne, tm) > 0.0
    masked = jnp.where(emask, sb, -jnp.inf)
    ei = lax.broadcasted_iota(jnp.int32, masked.shape, 0)
    rank = jnp.zeros(masked.shape, jnp.int32)
    for j in range(ne):
        r = masked[j:j + 1, :]
        rank = rank + jnp.where((r > masked) | ((r == masked) & (j < ei)), 1, 0)
    sel = rank < TOP_K
    w = jnp.where(sel, s, 0.0)
    w = w / jnp.sum(w, axis=0, keepdims=True) * ROUTED_SCALE
    pad = jnp.zeros((gate_ref.shape[1] - ne, tm), F32)
    gate_ref[...] = jnp.concatenate([w, pad], axis=0).T
    sel_ref[...] = jnp.concatenate([jnp.where(sel, 1.0, 0.0), pad], axis=0).T


def _router(logits, b_router, ne):
    t, lanes = logits.shape
    tm = TOK_TILE
    spec = pl.BlockSpec((tm, lanes), lambda i: (i, 0))
    return pl.pallas_call(
        functools.partial(_router_kernel, ne=ne),
        grid=(t // tm,),
        in_specs=[spec, pl.BlockSpec((ne, 1), lambda i: (0, 0))],
        out_specs=[spec, spec],
        out_shape=[jax.ShapeDtypeStruct((t, lanes), F32)] * 2,
        compiler_params=_cparams(("arbitrary",)),
        name="router",
    )(logits, b_router.reshape(ne, 1))


def _scatter_kernel(dest_ref, hf_ref, xs_in_ref, xs_ref, sem):
    del xs_in_ref
    tt = hf_ref.shape[0]

    def body(i, carry):
        for k in range(TOP_K):
            d = dest_ref[i * TOP_K + k]
            pltpu.make_async_copy(hf_ref.at[pl.ds(i, 1)], xs_ref.at[pl.ds(d, 1)], sem).start(priority=k % 2)
        return carry

    lax.fori_loop(0, tt, body, 0)
    for k in range(TOP_K):
        pltpu.make_async_copy(hf_ref, xs_ref.at[pl.ds(0, tt)], sem).wait()


def _scatter(dest_flat, hf, xs):
    t, d = hf.shape
    tt = TOK_TILE
    return pl.pallas_call(
        _scatter_kernel,
        grid=(t // tt,),
        in_specs=[pl.BlockSpec((tt * TOP_K,), lambda i: (i,), memory_space=pltpu.SMEM),
                  pl.BlockSpec((tt, d), lambda i: (i, 0)),
                  pl.BlockSpec(memory_space=pl.ANY)],
        out_specs=pl.BlockSpec(memory_space=pl.ANY),
        out_shape=jax.ShapeDtypeStruct(xs.shape, xs.dtype),
        scratch_shapes=[pltpu.SemaphoreType.DMA(())],
        input_output_aliases={2: 0},
        compiler_params=_cparams(("arbitrary",)),
        name="scatter",
    )(dest_flat, hf, xs)


def _ffn_kernel(be_ref, nbu_ref, xs_ref, wg_ref, wu_ref, wd_ref, ys_ref):
    del be_ref

    @pl.when(pl.program_id(0) < nbu_ref[0])
    def _():
        x = xs_ref[...]
        hb = _silu(_mm(x, wg_ref[0])) * _mm(x, wu_ref[0])
        ys_ref[...] = _mm(hb, wd_ref[0])

    @pl.when(pl.program_id(0) >= nbu_ref[0])
    def _():
        ys_ref[...] = jnp.zeros(ys_ref.shape, F32)


def _ffn(block_e, nb_used, xs, wg, wu, wd):
    rows, d = xs.shape
    f = wg.shape[2]
    tm = MOE_TILE
    nb = rows // tm
    xmap = lambda b, be, nbu: (jnp.minimum(b, nbu[0] - 1), 0)
    wmap = lambda b, be, nbu: (be[b], 0, 0)
    return pl.pallas_call(
        _ffn_kernel,
        grid_spec=pltpu.PrefetchScalarGridSpec(
            num_scalar_prefetch=2,
            grid=(nb,),
            in_specs=[pl.BlockSpec((tm, d), xmap), pl.BlockSpec((1, d, f), wmap), pl.BlockSpec((1, d, f), wmap),
                      pl.BlockSpec((1, f, d), wmap)],
            out_specs=pl.BlockSpec((tm, d), lambda b, be, nbu: (b, 0))),
        out_shape=jax.ShapeDtypeStruct((rows, d), F32),
        compiler_params=_cparams(("arbitrary",)),
        name="ffn",
    )(block_e, nb_used, xs, wg, wu, wd)


def _combine_kernel(dest_ref, wts_ref, hf_ref, x1_ref, gt2_ref, gpost_ref, wsg_ref, wsu_ref, wsd_ref, ys_ref,
                    y_ref, buf_ref, sem):
    bb, ll, d = x1_ref.shape
    tt = bb * ll

    def body(i, carry):
        for k in range(TOP_K):
            r = dest_ref[i * TOP_K + k]
            pltpu.make_async_copy(ys_ref.at[pl.ds(r, 1)], buf_ref.at[k, pl.ds(i, 1)], sem).start(priority=k % 2)
        return carry

    lax.fori_loop(0, tt, body, 0)
    hf = hf_ref[...]
    shared = _mm(_silu(_mm(hf, wsg_ref[...])) * _mm(hf, wsu_ref[...]), wsd_ref[...])
    for k in range(TOP_K):
        pltpu.make_async_copy(ys_ref.at[pl.ds(0, tt)], buf_ref.at[k], sem).wait()
    wts = wts_ref[...]
    acc = wts[:, 0:1] * buf_ref[0]
    for k in range(1, TOP_K):
        acc = acc + wts[:, k:k + 1] * buf_ref[k]
    yf = (acc + shared).reshape(bb, ll, d)
    y_ref[...] = x1_ref[...] + gt2_ref[...] * _rms(yf, gpost_ref[...])


def _combine(dest_flat, wts, hf, x1, gt2, gpost, wsg, wsu, wsd, ys):
    b, l, d = x1.shape
    ll = min(l, COMB_TILE)
    bb = COMB_TILE // ll
    tt = COMB_TILE
    nl = l // ll
    fs = wsg.shape[1]
    tmap = lambda i, j: (i, j, 0)
    wmap = lambda i, j: (0, 0)
    return pl.pallas_call(
        _combine_kernel,
        grid=(b // bb, nl),
        in_specs=[pl.BlockSpec((tt * TOP_K,), lambda i, j: (i * nl + j,), memory_space=pltpu.SMEM),
                  pl.BlockSpec((tt, TOP_K), lambda i, j: (i * nl + j, 0)),
                  pl.BlockSpec((tt, d), lambda i, j: (i * nl + j, 0)),
                  pl.BlockSpec((bb, ll, d), tmap),
                  pl.BlockSpec((bb, 1, d), lambda i, j: (i, 0, 0)),
                  pl.BlockSpec((1, d), wmap), pl.BlockSpec((d, fs), wmap), pl.BlockSpec((d, fs), wmap),
                  pl.BlockSpec((fs, d), wmap),
                  pl.BlockSpec(memory_space=pl.ANY)],
        out_specs=pl.BlockSpec((bb, ll, d), tmap),
        out_shape=jax.ShapeDtypeStruct((b, l, d), F32),
        scratch_shapes=[pltpu.VMEM((TOP_K, tt, d), F32), pltpu.SemaphoreType.DMA(())],
        compiler_params=_cparams(("arbitrary", "arbitrary")),
        name="combine",
    )(dest_flat, wts, hf, x1, gt2, gpost, wsg, wsu, wsd, ys)


def _rope_tables(pos, rows):
    half = DH // 2
    freqs = ROPE_BASE ** (-jnp.arange(half, dtype=F32) / half)
    ang = pos.astype(F32)[:, None] * freqs[None, :]
    cos, sin = jnp.cos(ang), jnp.sin(ang)
    cos2 = jnp.concatenate([cos, cos], axis=1)
    sin2 = jnp.concatenate([-sin, sin], axis=1)
    padr = rows - pos.shape[0]
    return jnp.pad(cos2, ((0, padr), (0, 0))), jnp.pad(sin2, ((0, padr), (0, 0)))


def _dispatch_plan(sel, gates, ne):
    t = sel.shape[0]
    mask = (sel[:, :ne] > 0.0).astype(jnp.int32)
    pos = jnp.cumsum(mask, axis=0) - mask
    counts = jnp.sum(mask, axis=0)
    padded = (counts + MOE_TILE - 1) // MOE_TILE * MOE_TILE
    pend = jnp.cumsum(padded)
    destfull = (pend - padded)[None, :] + pos
    slot = jnp.cumsum(mask, axis=1) - mask
    onehot = (mask[:, :, None] > 0) & (slot[:, :, None] == jnp.arange(TOP_K, dtype=jnp.int32)[None, None, :])
    dest = jnp.sum(jnp.where(onehot, destfull[:, :, None], 0), axis=1).astype(jnp.int32)
    wts = jnp.sum(jnp.where(onehot, gates[:, :ne, None], 0.0), axis=1)
    nb = t * TOP_K // MOE_TILE + ne
    starts = jnp.arange(nb, dtype=jnp.int32) * MOE_TILE
    block_e = jnp.minimum(jnp.sum((pend[None, :] <= starts[:, None]).astype(jnp.int32), axis=1),
                          ne - 1).astype(jnp.int32)
    nb_used = (pend[-1:] // MOE_TILE).astype(jnp.int32)
    return dest, wts, block_e, nb_used, nb


def kernel(x_prompt, x_sample, c_prompt, c_sample, state_mlstm_C, state_mlstm_n, state_mlstm_m, state_mlstm_conv, state_ret_S, w_ada, b_ada, g_mix_pre, g_mix_post, g_ffn_pre, g_ffn_post, w_in, b_gates, conv_w, conv_b, m_gn, r_gn, w_out, w_router, b_router, w_exp_gate, w_exp_up, w_exp_down, w_sh_gate, w_sh_up, w_sh_down):
    depth = w_ada.shape[0]
    bp, lp, d = x_prompt.shape
    bs, ls, _ = x_sample.shape
    mh, rh = state_mlstm_C.shape[2], state_ret_S.shape[2]
    mw, rw = mh * DH, rh * DH
    ne = w_router.shape[2]
    tp, ts = bp * lp, bs * ls

    cos_p, sin_p = _rope_tables(jnp.arange(lp, dtype=jnp.int32), lp)
    cos_s, sin_s = _rope_tables(PAST_LEN + jnp.arange(ls, dtype=jnp.int32), CHUNK if ls % CHUNK else ls)

    hp, hs = x_prompt, x_sample
    new_p, new_s = [], []
    for l in range(depth):
        o = 0
        offs = []
        for wdt in (2 * mw, mw, mw, 2 * mh, rw, rw, rw, rw):
            offs.append((o, o + wdt))
            o += wdt
        wi = w_in[l]
        cols = [wi[:, a:b] for (a, b) in offs]
        w_in_r = jnp.concatenate(cols[:3] + cols[4:] + [cols[3], jnp.zeros((d, GATE_LANES - 2 * mh), F32)],
                                 axis=1).astype(MXU_DTYPE)
        bg = jnp.pad(b_gates[l], (0, GATE_LANES - 2 * mh)).reshape(1, GATE_LANES)
        wout = w_out[l].astype(MXU_DTYPE)
        wr = jnp.pad(w_router[l], ((0, 0), (0, GATE_LANES - ne)))
        wr_hi, wr_lo = _split_hi_lo(wr)
        wsg, wsu, wsd = (w_sh_gate[l].astype(MXU_DTYPE), w_sh_up[l].astype(MXU_DTYPE), w_sh_down[l].astype(MXU_DTYPE))
        row = lambda v: v.reshape(1, -1)

        mod = _ada(jnp.concatenate([c_prompt, c_sample], axis=0), w_ada[l], b_ada[l])
        mods_p = [m[:bp].reshape(bp, 1, d) for m in jnp.split(mod, 6, axis=1)]
        mods_s = [m[bp:].reshape(bs, 1, d) for m in jnp.split(mod, 6, axis=1)]

        def lane_m(m):
            return jnp.pad(m, ((0, 0), (mh, GATE_LANES - 2 * mh)))[:, None, :]

        init = (jnp.zeros((bp, mh, DH, DH), F32), jnp.zeros((bp, mh, DH), F32), lane_m(jnp.zeros((bp, mh), F32)),
                jnp.zeros((bp, CONV_W - 1, 2 * mw), F32), jnp.zeros((bp, rh, DH, DH), F32))
        past = (state_mlstm_C[l], state_mlstm_n[l], lane_m(state_mlstm_m[l]), state_mlstm_conv[l], state_ret_S[l])

        def mix(x, mods, state, cos2, sin2):
            sh1, sc1, gt1, sh2, sc2, gt2 = mods
            b, ll, _ = x.shape
            z = _inproj(x, sc1, sh1, row(g_mix_pre[l]), w_in_r).reshape(b, ll, -1)
            o_mix, c_n, n_n, m_n, conv_n, s_n = _mixcore(z, cos2, sin2, *state, conv_w[l], row(conv_b[l]), bg,
                                                         row(m_gn[l]), row(r_gn[l]))
            x1, hf, lg = _mixout(o_mix, x, gt1, sc2, sh2, row(g_mix_post[l]), row(g_ffn_pre[l]), wout, wr_hi, wr_lo)
            gates, sel = _router(lg, b_router[l], ne)
            return x1, hf, gates, sel, (c_n, n_n, m_n[:, 0, mh:2 * mh], conv_n, s_n)

        x1_p, hf_p, gates_p, sel_p, st_p = mix(hp, mods_p, init, cos_p, sin_p)
        x1_s, hf_s, gates_s, sel_s, st_s = mix(hs, mods_s, past, cos_s, sin_s)

        dest, wts, block_e, nb_used, nb = _dispatch_plan(jnp.concatenate([sel_p, sel_s], axis=0),
                                                         jnp.concatenate([gates_p, gates_s], axis=0), ne)
        dest_flat = dest.reshape(-1)
        xs = jnp.zeros((nb * MOE_TILE, d), F32)
        xs = _scatter(dest_flat[:tp * TOP_K], hf_p, xs)
        xs = _scatter(dest_flat[tp * TOP_K:], hf_s, xs)
        ys = _ffn(block_e, nb_used, xs, w_exp_gate[l], w_exp_up[l], w_exp_down[l])
        hp = _combine(dest_flat[:tp * TOP_K], wts[:tp], hf_p, x1_p, mods_p[5], row(g_ffn_post[l]), wsg, wsu, wsd, ys)
        hs = _combine(dest_flat[tp * TOP_K:], wts[tp:], hf_s, x1_s, mods_s[5], row(g_ffn_post[l]), wsg, wsu, wsd, ys)
        new_p.append(st_p)
        new_s.append(st_s)

    p_state = [jnp.stack([st[i] for st in new_p]) for i in range(5)]
    s_state = [jnp.stack([st[i] for st in new_s]) for i in range(5)]
    return (hp, hs, *p_state, *s_state)
```

```python
import functools
import math

import jax
import jax.numpy as jnp
from jax import lax
from jax.experimental import pallas as pl
from jax.experimental.pallas import tpu as pltpu

F32 = jnp.float32
MXU_DTYPE = jnp.bfloat16

NORM_EPS = 1e-6
ROPE_BASE = 10000.0
PAST_LEN = 16384
CHUNK = 128
CONV_W = 4
DH = 128
TOP_K = 8
N_GROUPS = 8
TOPK_GROUPS = 4
ROUTED_SCALE = 2.5
GATE_LANES = 128

V7X_VMEM_BYTES = 64 * 1024 * 1024
VMEM_LIMIT = 48 * 1024 * 1024

SUB, LANES = 8, 128

TOK_TILE = 256
MOE_TILE = 256


def _cparams(sem):
    return pltpu.CompilerParams(dimension_semantics=sem, vmem_limit_bytes=VMEM_LIMIT)


def _silu(x):
    return x * jax.nn.sigmoid(x)


def _rms(x, g):
    return x * lax.rsqrt(jnp.mean(x * x, axis=-1, keepdims=True) + NORM_EPS) * g


def _mm(a, b):
    return jnp.dot(a.astype(MXU_DTYPE), b.astype(MXU_DTYPE), preferred_element_type=F32)


def _mm_nt(a, b):
    return lax.dot_general(a.astype(MXU_DTYPE), b.astype(MXU_DTYPE), (((1,), (1,)), ((), ())),
                           preferred_element_type=F32)


def _mm_tn(a, b):
    return lax.dot_general(a.astype(MXU_DTYPE), b.astype(MXU_DTYPE), (((0,), (0,)), ((), ())),
                           preferred_element_type=F32)


def _ada_kernel(c_ref, w_ref, b_ref, o_ref):
    o_ref[...] = _mm(_silu(c_ref[...]), w_ref[...]) + b_ref[...]


def _ada(c, w_ada, b_ada):
    n, d = c.shape
    dout = w_ada.shape[1]
    tn = 1024
    return pl.pallas_call(
        _ada_kernel,
        grid=(dout // tn,),
        in_specs=[pl.BlockSpec((n, d), lambda j: (0, 0)),
                  pl.BlockSpec((d, tn), lambda j: (0, j)),
                  pl.BlockSpec((1, tn), lambda j: (0, j))],
        out_specs=pl.BlockSpec((n, tn), lambda j: (0, j)),
        out_shape=jax.ShapeDtypeStruct((n, dout), F32),
        compiler_params=_cparams(("arbitrary",)),
        name="ada",
    )(c, w_ada, b_ada.reshape(1, dout))


def _inproj_kernel(x_ref, sc_ref, sh_ref, g_ref, w_ref, z_ref):
    x = x_ref[...]
    h = _rms(x, g_ref[...]) * (1.0 + sc_ref[...]) + sh_ref[...]
    bb, ll, d = x.shape
    z_ref[...] = _mm(h.reshape(bb * ll, d), w_ref[...])


def _inproj(x, sc, sh, g, w):
    b, l, d = x.shape
    n = w.shape[1]
    ll = min(l, TOK_TILE)
    bb = TOK_TILE // ll
    mod_spec = pl.BlockSpec((bb, 1, d), lambda i, j: (i, 0, 0))
    return pl.pallas_call(
        _inproj_kernel,
        grid=(b // bb, l // ll),
        in_specs=[pl.BlockSpec((bb, ll, d), lambda i, j: (i, j, 0)), mod_spec, mod_spec,
                  pl.BlockSpec((1, d), lambda i, j: (0, 0)),
                  pl.BlockSpec((d, n), lambda i, j: (0, 0))],
        out_specs=pl.BlockSpec((bb * ll, n), lambda i, j: (i * (l // ll) + j, 0)),
        out_shape=jax.ShapeDtypeStruct((b * l, n), F32),
        compiler_params=_cparams(("arbitrary", "arbitrary")),
        name="inproj",
    )(x, sc, sh, g, w)


def _cumsum_rows(x):
    n = x.shape[0]
    row = lax.broadcasted_iota(jnp.int32, x.shape, 0)
    s = 1
    while s < n:
        x = x + jnp.where(row >= s, pltpu.roll(x, s, axis=0), 0.0)
        s *= 2
    return x


def _head_norm(y, g):
    mu = jnp.mean(y, axis=-1, keepdims=True)
    yc = y - mu
    var = jnp.mean(yc * yc, axis=-1, keepdims=True)
    return yc * lax.rsqrt(var + NORM_EPS) * g


def _log_sigmoid(x):
    return jnp.minimum(x, 0.0) - jnp.log1p(jnp.exp(-jnp.abs(x)))


def _mixcore_kernel(z_ref, cos_ref, sin_ref, c0_ref, n0_ref, m0_ref, conv0_ref, s0_ref,
                    convw_ref, convb_ref, bg_ref, mgn_ref, rgn_ref,
                    out_ref, c_ref, n_ref, m_ref, conv_ref, s_ref,
                    xcat_ref, *, cl, lv, mh, rh):
    @pl.when(pl.program_id(1) == 0)
    def _():
        c_ref[...] = c0_ref[...]
        n_ref[...] = n0_ref[...]
        m_ref[...] = m0_ref[...]
        s_ref[...] = s0_ref[...]
        xcat_ref[8 - (CONV_W - 1):8, :] = conv0_ref[0]

    mw = mh * DH
    rw = rh * DH

    def rows(lo, hi):
        x = z_ref[0, :, lo:hi]
        if lv == cl:
            return x
        return jnp.concatenate([x, jnp.zeros((cl - lv, hi - lo), F32)], axis=0)

    o_qk, o_mv, o_mo = 0, 2 * mw, 3 * mw
    o_rq = 4 * mw
    o_rk, o_rv, o_rg, o_gt = o_rq + rw, o_rq + 2 * rw, o_rq + 3 * rw, o_rq + 4 * rw

    xcat_ref[8:8 + cl, :] = rows(o_qk, o_qk + 2 * mw)
    qk = convb_ref[...]
    for t in range(CONV_W):
        qk = qk + xcat_ref[8 - (CONV_W - 1) + t:8 - (CONV_W - 1) + t + cl, :] * convw_ref[t:t + 1, :]
    qk = _silu(qk)
    new_conv = xcat_ref[8 + lv - (CONV_W - 1):8 + lv, :]
    conv_ref[0] = new_conv
    xcat_ref[8 - (CONV_W - 1):8, :] = new_conv

    row_c = lax.broadcasted_iota(jnp.int32, (cl, 1), 0)
    ii = lax.broadcasted_iota(jnp.int32, (cl, cl), 0)
    jj = lax.broadcasted_iota(jnp.int32, (cl, cl), 1)
    causal = jj <= ii
    eye = jj == ii
    valid_c = row_c < lv

    def to_row(col):
        return jnp.sum(jnp.where(eye, col, 0.0), axis=0, keepdims=True)

    g = rows(o_gt, o_gt + GATE_LANES) + bg_ref[...]
    logf = jnp.where(valid_c, _log_sigmoid(g), 0.0)
    bcum = _cumsum_rows(logf)
    m_prev = m_ref[0]
    a_all = bcum + m_prev
    lane = lax.broadcasted_iota(jnp.int32, (1, GATE_LANES), 1)
    m_new_row = m_prev

    for h in range(mh):
        q = qk[:, h * DH:(h + 1) * DH]
        k = qk[:, mw + h * DH:mw + (h + 1) * DH] * (DH ** -0.5)
        v = rows(o_mv + h * DH, o_mv + (h + 1) * DH)
        b_col = bcum[:, mh + h:mh + h + 1]
        logi_col = jnp.where(valid_c, g[:, h:h + 1], -jnp.inf)
        a_col = a_all[:, mh + h:mh + h + 1]
        dmat = jnp.where(causal, b_col + to_row(logi_col - b_col), -jnp.inf)
        m_t = jnp.maximum(a_col, jnp.max(dmat, axis=1, keepdims=True))
        w_intra = jnp.exp(dmat - m_t)
        w_inter = jnp.exp(a_col - m_t)
        c_old = c_ref[0, h]
        n_old = n_ref[0, h:h + 1, :]
        s = _mm_nt(q, k) * w_intra
        num = _mm(s, v) + w_inter * _mm(q, c_old)
        den = jnp.sum(s, axis=1, keepdims=True) + w_inter * jnp.sum(q * n_old, axis=1, keepdims=True)
        hh = num / jnp.maximum(jnp.abs(den), jnp.exp(-m_t))
        m_new = m_t[cl - 1:cl, :]
        b_last = b_col[cl - 1:cl, :]
        w_old = jnp.exp(b_last + m_prev[:, mh + h:mh + h + 1] - m_new)
        w_k = jnp.exp(b_last - b_col + logi_col - m_new)
        c_ref[0, h] = w_old * c_old + _mm_tn(k, w_k * v)
        n_ref[0, h:h + 1, :] = w_old * n_old + jnp.sum(w_k * k, axis=0, keepdims=True)
        m_new_row = jnp.where(lane == mh + h, m_new, m_new_row)
        om = _head_norm(hh, mgn_ref[:, h * DH:(h + 1) * DH]) * jax.nn.sigmoid(rows(o_mo + h * DH, o_mo + (h + 1) * DH))
        out_ref[0, :, h * DH:(h + 1) * DH] = om[:lv]
    m_ref[0] = m_new_row

    cos2 = cos_ref[...]
    sin2 = sin_ref[...]
    rel = (ii - jj).astype(F32)
    row_f = row_c.astype(F32)

    def rot(x):
        return x * cos2 + pltpu.roll(x, DH // 2, axis=1) * sin2

    for h in range(rh):
        log_g = math.log1p(-2.0 ** (-5.0 - h))
        q = rot(rows(o_rq + h * DH, o_rq + (h + 1) * DH))
        k = rot(rows(o_rk + h * DH, o_rk + (h + 1) * DH)) * (DH ** -0.5)
        v = rows(o_rv + h * DH, o_rv + (h + 1) * DH)
        decay = jnp.where(causal, jnp.exp(jnp.maximum(rel, 0.0) * log_g), 0.0)
        s_old = s_ref[0, h]
        y = _mm(_mm_nt(q, k) * decay, v) + jnp.exp((row_f + 1.0) * log_g) * _mm(q, s_old)
        k_w = jnp.where(valid_c, jnp.exp((lv - 1.0 - row_f) * log_g), 0.0)
        s_ref[0, h] = math.exp(lv * log_g) * s_old + _mm_tn(k, k_w * v)
        orr = _head_norm(y, rgn_ref[:, h * DH:(h + 1) * DH]) * _silu(rows(o_rg + h * DH, o_rg + (h + 1) * DH))
        out_ref[0, :, mw + h * DH:mw + (h + 1) * DH] = orr[:lv]


def _mixcore(z, cos2, sin2, c0, n0, m0, conv0, s0, convw, convb, bg, mgn, rgn):
    b, l, nz = z.shape
    mh, rh = c0.shape[1], s0.shape[1]
    mw, rw = mh * DH, rh * DH
    cl = CHUNK
    lv = cl if l % cl == 0 else l
    nc = l // lv
    bmap4 = lambda i, c: (i, 0, 0, 0)
    bmap3 = lambda i, c: (i, 0, 0)
    wmap = lambda i, c: (0, 0)
    state_specs = [pl.BlockSpec((1, mh, DH, DH), bmap4), pl.BlockSpec((1, mh, DH), bmap3),
                   pl.BlockSpec((1, 1, GATE_LANES), bmap3), pl.BlockSpec((1, CONV_W - 1, 2 * mw), bmap3),
                   pl.BlockSpec((1, rh, DH, DH), bmap4)]
    state_shapes = [jax.ShapeDtypeStruct(c0.shape, F32), jax.ShapeDtypeStruct(n0.shape, F32),
                    jax.ShapeDtypeStruct(m0.shape, F32), jax.ShapeDtypeStruct(conv0.shape, F32),
                    jax.ShapeDtypeStruct(s0.shape, F32)]
    return pl.pallas_call(
        functools.partial(_mixcore_kernel, cl=cl, lv=lv, mh=mh, rh=rh),
        grid=(b, nc),
        in_specs=[pl.BlockSpec((1, lv, nz), lambda i, c: (i, c, 0)),
                  pl.BlockSpec((cl, DH), lambda i, c: (c, 0)), pl.BlockSpec((cl, DH), lambda i, c: (c, 0))]
                 + state_specs
                 + [pl.BlockSpec(convw.shape, wmap), pl.BlockSpec(convb.shape, wmap), pl.BlockSpec(bg.shape, wmap),
                    pl.BlockSpec(mgn.shape, wmap), pl.BlockSpec(rgn.shape, wmap)],
        out_specs=[pl.BlockSpec((1, lv, mw + rw), lambda i, c: (i, c, 0))] + state_specs,
        out_shape=[jax.ShapeDtypeStruct((b, l, mw + rw), F32)] + state_shapes,
        scratch_shapes=[pltpu.VMEM((cl + 8, 2 * mw), F32)],
        compiler_params=_cparams(("arbitrary", "arbitrary")),
        name="mixcore",
    )(z, cos2, sin2, c0, n0, m0, conv0, s0, convw, convb, bg, mgn, rgn)


def _split_hi_lo(x):
    hi = x.astype(MXU_DTYPE)
    lo = (x - hi.astype(F32)).astype(MXU_DTYPE)
    return hi, lo


def _to_row_tiles(ref, x):
    n = x.shape[0]
    for j in range(SUB):
        ref[pl.ds(j, n, stride=SUB), :] = x[:, j * LANES:(j + 1) * LANES]


def _from_row_tiles(ref, n):
    return jnp.concatenate([ref[pl.ds(j, n, stride=SUB), :] for j in range(SUB)], axis=1)


def _mixout_kernel(o_ref, x_ref, gt1_ref, sc2_ref, sh2_ref, gpost_ref, gpre_ref, wout_ref, wr_hi_ref, wr_lo_ref,
                   x1_ref, hf3_ref, lg_ref):
    bb, ll, d = x_ref.shape
    o = o_ref[...]
    ym = _mm(o.reshape(bb * ll, o.shape[-1]), wout_ref[...]).reshape(bb, ll, d)
    x1 = x_ref[...] + gt1_ref[...] * _rms(ym, gpost_ref[...])
    x1_ref[...] = x1
    hf = (_rms(x1, gpre_ref[...]) * (1.0 + sc2_ref[...]) + sh2_ref[...]).reshape(bb * ll, d)
    _to_row_tiles(hf3_ref, hf)
    hi, lo = _split_hi_lo(hf)
    lg_ref[...] = _mm(hi, wr_hi_ref[...]) + (_mm(lo, wr_hi_ref[...]) + _mm(hi, wr_lo_ref[...]))


def _mixout(o, x, gt1, sc2, sh2, gpost, gpre, wout, wr_hi, wr_lo):
    b, l, d = x.shape
    assert d == SUB * LANES
    w = o.shape[-1]
    ne = wr_hi.shape[1]
    ll = min(l, TOK_TILE)
    bb = TOK_TILE // ll
    tmap = lambda i, j: (i, j, 0)
    fmap = lambda i, j: (i * (l // ll) + j, 0)
    mod_spec = pl.BlockSpec((bb, 1, d), lambda i, j: (i, 0, 0))
    wmap = lambda i, j: (0, 0)
    return pl.pallas_call(
        _mixout_kernel,
        grid=(b // bb, l // ll),
        in_specs=[pl.BlockSpec((bb, ll, w), tmap), pl.BlockSpec((bb, ll, d), tmap), mod_spec, mod_spec, mod_spec,
                  pl.BlockSpec((1, d), wmap), pl.BlockSpec((1, d), wmap), pl.BlockSpec((w, d), wmap),
                  pl.BlockSpec((d, ne), wmap), pl.BlockSpec((d, ne), wmap)],
        out_specs=[pl.BlockSpec((bb, ll, d), tmap), pl.BlockSpec((bb * ll * SUB, LANES), fmap),
                   pl.BlockSpec((bb * ll, ne), fmap)],
        out_shape=[jax.ShapeDtypeStruct((b, l, d), F32), jax.ShapeDtypeStruct((b * l * SUB, LANES), F32),
                   jax.ShapeDtypeStruct((b * l, ne), F32)],
        compiler_params=_cparams(("arbitrary", "arbitrary")),
        name="mixout",
    )(o, x, gt1, sc2, sh2, gpost, gpre, wout, wr_hi, wr_lo)


def _router_kernel(lg_ref, br_ref, lidx_ref, wts_ref, cnt_ref, seg_ref, *, ne):
    tm = lg_ref.shape[0]
    gsz = ne // N_GROUPS
    s = jax.nn.sigmoid(lg_ref[...].T[:ne, :])
    sb = s + br_ref[...]
    sb3 = sb.reshape(N_GROUPS, gsz, tm)
    e3 = lax.broadcasted_iota(jnp.int32, sb3.shape, 1)
    m1 = jnp.max(sb3, axis=1, keepdims=True)
    first = jnp.min(jnp.where(sb3 == m1, e3, gsz), axis=1, keepdims=True)
    m2 = jnp.max(jnp.where(e3 == first, -jnp.inf, sb3), axis=1, keepdims=True)
    gs = (m1 + m2).reshape(N_GROUPS, tm)
    gi = lax.broadcasted_iota(jnp.int32, gs.shape, 0)
    grank = jnp.zeros(gs.shape, F32)
    for g in range(N_GROUPS):
        r = gs[g:g + 1, :]
        grank = grank + jnp.where(r > gs, 1.0, jnp.where(r == gs, jnp.where(g < gi, 1.0, 0.0), 0.0))
    gsel = jnp.where(grank < TOPK_GROUPS, 1.0, 0.0)
    emask = jnp.broadcast_to(gsel.reshape(N_GROUPS, 1, tm), sb3.shape).reshape(ne, tm) > 0.0
    masked = jnp.where(emask, sb, -jnp.inf)
    ei = lax.broadcasted_iota(jnp.int32, masked.shape, 0)
    rank = jnp.zeros(masked.shape, F32)
    for j in range(ne):
        r = masked[j:j + 1, :]
        rank = rank + jnp.where(r > masked, 1.0, jnp.where(r == masked, jnp.where(j < ei, 1.0, 0.0), 0.0))
    sel = rank < TOP_K
    w = jnp.where(sel, s, 0.0)
    w = w / jnp.sum(w, axis=0, keepdims=True) * ROUTED_SCALE

    self = jnp.where(sel, 1.0, 0.0)
    er = lax.broadcasted_iota(jnp.int32, (ne, ne), 0)
    ec = lax.broadcasted_iota(jnp.int32, (ne, ne), 1)
    slot = _mm(jnp.where(ec < er, 1.0, 0.0), self)
    tr = lax.broadcasted_iota(jnp.int32, (tm, tm), 0)
    tc = lax.broadcasted_iota(jnp.int32, (tm, tm), 1)
    pos = _mm(self, jnp.where(tr < tc, 1.0, 0.0))
    seg = jnp.sum(slot, axis=1, keepdims=True)
    lrow = seg + pos
    lidx_rows, w_rows = [], []
    for k in range(TOP_K):
        pick = sel & (slot == float(k))
        lidx_rows.append(jnp.sum(jnp.where(pick, lrow, 0.0), axis=0, keepdims=True))
        w_rows.append(jnp.sum(jnp.where(pick, w, 0.0), axis=0, keepdims=True))
    lidx_ref[...] = jnp.concatenate(lidx_rows, axis=0).astype(jnp.int32)
    wts_ref[...] = jnp.concatenate(w_rows, axis=0)
    ones = jnp.ones((SUB, tm), F32)
    padz = jnp.zeros((LANES - ne, tm), F32)
    cnt_ref[...] = _mm_nt(ones, jnp.concatenate([self, padz], axis=0)).astype(jnp.int32)
    seg_ref[...] = _mm_nt(ones, jnp.concatenate([slot, padz], axis=0)).astype(jnp.int32)


def _router(logits, b_router, ne):
    t, lanes = logits.shape
    tm = TOK_TILE
    nt = t // tm
    kspec = pl.BlockSpec((TOP_K, tm), lambda i: (0, i))
    cspec = pl.BlockSpec((SUB, LANES), lambda i: (i, 0))
    return pl.pallas_call(
        functools.partial(_router_kernel, ne=ne),
        grid=(nt,),
        in_specs=[pl.BlockSpec((tm, lanes), lambda i: (i, 0)), pl.BlockSpec((ne, 1), lambda i: (0, 0))],
        out_specs=[kspec, kspec, cspec, cspec],
        out_shape=[jax.ShapeDtypeStruct((TOP_K, t), jnp.int32), jax.ShapeDtypeStruct((TOP_K, t), F32),
                   jax.ShapeDtypeStruct((nt * SUB, LANES), jnp.int32),
                   jax.ShapeDtypeStruct((nt * SUB, LANES), jnp.int32)],
        compiler_params=_cparams(("arbitrary",)),
        name="router",
    )(logits, b_router.reshape(ne, 1))


SEG_BITS = TOK_TILE.bit_length()


def _for_each_piece(n, fn, bits):
    for bit in range(bits - 1, -1, -1):
        @pl.when(((n >> bit) & 1) == 1)
        def _():
            fn((n >> (bit + 1)) << (bit + 1), 1 << bit)


def _padfill_kernel(ps_ref, pn_ref, tail_ref, xs_ref, z_ref, sem, *, ne):
    z_ref[...] = jnp.zeros(z_ref.shape, F32)

    def pad_pieces(act):
        def body(e, carry):
            _for_each_piece(pn_ref[e], lambda off, size: act(
                pltpu.make_async_copy(z_ref.at[pl.ds(0, size)], xs_ref.at[pl.ds(ps_ref[e] + off, size)], sem)),
                SEG_BITS - 1)
            return carry
        lax.fori_loop(0, ne, body, 0)

    def tail_blocks(act):
        def body(b, carry):
            act(pltpu.make_async_copy(z_ref, xs_ref.at[pl.ds(tail_ref[0] + b * MOE_TILE, MOE_TILE)], sem))
            return carry
        lax.fori_loop(0, tail_ref[1], body, 0)

    pad_pieces(lambda c: c.start())
    tail_blocks(lambda c: c.start())
    pad_pieces(lambda c: c.wait())
    tail_blocks(lambda c: c.wait())


def _padfill(pad_start, pad_n, tail, rows, ne):
    return pl.pallas_call(
        functools.partial(_padfill_kernel, ne=ne),
        grid_spec=pltpu.PrefetchScalarGridSpec(
            num_scalar_prefetch=3, grid=(1,), in_specs=[],
            out_specs=pl.BlockSpec(memory_space=pl.ANY),
            scratch_shapes=[pltpu.VMEM((MOE_TILE, SUB, LANES), F32), pltpu.SemaphoreType.DMA(())]),
        out_shape=jax.ShapeDtypeStruct((rows, SUB, LANES), F32),
        compiler_params=_cparams(("arbitrary",)),
        name="padfill",
    )(pad_start, pad_n, tail)


def _dispatch_kernel(cnt_ref, seg_ref, gb_ref, lidx_ref, hf3_ref, xs_in_ref, xs_ref, buf0, buf1, sems, *, tile0, ne):
    del xs_in_ref
    i = pl.program_id(0)
    n = pl.num_programs(0)
    tt = lidx_ref.shape[1]
    tile = tile0 + i

    def whole(buf, sem):
        return pltpu.make_async_copy(buf, xs_ref.at[pl.ds(0, tt * TOP_K)], sem)

    def run(buf, sem):
        @pl.when(i >= 2)
        def _():
            whole(buf, sem).wait()

        def sort_body(t, carry):
            v = hf3_ref[pl.ds(pl.multiple_of(t * SUB, SUB), SUB), :]
            for k in range(TOP_K):
                buf[lidx_ref[k, t]] = v
            return carry
        lax.fori_loop(0, tt, sort_body, 0)

        def seg_body(e, carry):
            c, s, g = cnt_ref[tile * ne + e], seg_ref[tile * ne + e], gb_ref[tile * ne + e]
            _for_each_piece(c, lambda off, size: pltpu.make_async_copy(
                buf.at[pl.ds(s + off, size)], xs_ref.at[pl.ds(g + off, size)], sem).start(), SEG_BITS)
            return carry
        lax.fori_loop(0, ne, seg_body, 0)

    @pl.when(i % 2 == 0)
    def _():
        run(buf0, sems.at[0])

    @pl.when(i % 2 == 1)
    def _():
        run(buf1, sems.at[1])

    @pl.when(i == n - 1)
    def _():
        @pl.when(i % 2 == 0)
        def _():
            whole(buf0, sems.at[0]).wait()

            @pl.when(i >= 1)
            def _():
                whole(buf1, sems.at[1]).wait()

        @pl.when(i % 2 == 1)
        def _():
            whole(buf1, sems.at[1]).wait()
            whole(buf0, sems.at[0]).wait()


def _dispatch(cnt, seg, gbase, lidx, hf3, xs, tile0, ne):
    tt = TOK_TILE
    nt = lidx.shape[1] // tt
    return pl.pallas_call(
        functools.partial(_dispatch_kernel, tile0=tile0, ne=ne),
        grid_spec=pltpu.PrefetchScalarGridSpec(
            num_scalar_prefetch=3, grid=(nt,),
            in_specs=[pl.BlockSpec((TOP_K, tt), lambda i, *_: (0, i), memory_space=pltpu.SMEM),
                      pl.BlockSpec((tt * SUB, LANES), lambda i, *_: (i, 0)),
                      pl.BlockSpec(memory_space=pl.ANY)],
            out_specs=pl.BlockSpec(memory_space=pl.ANY),
            scratch_shapes=[pltpu.VMEM((tt * TOP_K, SUB, LANES), F32), pltpu.VMEM((tt * TOP_K, SUB, LANES), F32),
                            pltpu.SemaphoreType.DMA((2,))]),
        out_shape=jax.ShapeDtypeStruct(xs.shape, xs.dtype),
        input_output_aliases={5: 0},
        compiler_params=_cparams(("arbitrary",)),
        name="dispatch",
    )(cnt, seg, gbase, lidx, hf3, xs)


def _ffn_kernel(be_ref, nbu_ref, xs_ref, wg_ref, wu_ref, wd_ref, ys_ref, wg_s, wu_s, wd_s):
    b = pl.program_id(0)
    tm = xs_ref.shape[0] // SUB

    @pl.when(b < nbu_ref[0])
    def _():
        @pl.when((b == 0) | (be_ref[b] != be_ref[jnp.maximum(b - 1, 0)]))
        def _():
            wg_s[...] = wg_ref[0].astype(MXU_DTYPE)
            wu_s[...] = wu_ref[0].astype(MXU_DTYPE)
            wd_s[...] = wd_ref[0].astype(MXU_DTYPE)

        x = _from_row_tiles(xs_ref, tm)
        hb = _silu(_mm(x, wg_s[...])) * _mm(x, wu_s[...])
        _to_row_tiles(ys_ref, _mm(hb, wd_s[...]))

    @pl.when(b >= nbu_ref[0])
    def _():
        ys_ref[...] = jnp.zeros(ys_ref.shape, F32)


def _ffn(block_e, nb_used, xs2, wg, wu, wd):
    rows = xs2.shape[0] // SUB
    d, f = wg.shape[1], wg.shape[2]
    tm = MOE_TILE
    nb = rows // tm
    xmap = lambda b, be, nbu: (jnp.minimum(b, nbu[0] - 1), 0)
    wmap = lambda b, be, nbu: (be[b], 0, 0)
    return pl.pallas_call(
        _ffn_kernel,
        grid_spec=pltpu.PrefetchScalarGridSpec(
            num_scalar_prefetch=2,
            grid=(nb,),
            in_specs=[pl.BlockSpec((tm * SUB, LANES), xmap), pl.BlockSpec((1, d, f), wmap),
                      pl.BlockSpec((1, d, f), wmap), pl.BlockSpec((1, f, d), wmap)],
            out_specs=pl.BlockSpec((tm * SUB, LANES), lambda b, be, nbu: (b, 0)),
            scratch_shapes=[pltpu.VMEM((d, f), MXU_DTYPE), pltpu.VMEM((d, f), MXU_DTYPE),
                            pltpu.VMEM((f, d), MXU_DTYPE)]),
        out_shape=jax.ShapeDtypeStruct((rows * SUB, LANES), F32),
        compiler_params=_cparams(("arbitrary",)),
        name="ffn",
    )(block_e, nb_used, xs2, wg, wu, wd)


def _combine_kernel(cnt_ref, seg_ref, gb_ref, lidx_ref, wts_ref, hf3_ref, x1_ref, gt2_ref, gpost_ref,
                    wsg_ref, wsu_ref, wsd_ref, ys_ref, y_ref, buf0, buf1, y3_ref, sems, *, tile0, ne):
    bb, ll, d = x1_ref.shape
    tt = bb * ll
    i = pl.program_id(0)
    n = pl.num_programs(0)

    def fetch(j, buf, sem):
        def seg_body(e, carry):
            c, s, g = cnt_ref[j * ne + e], seg_ref[j * ne + e], gb_ref[j * ne + e]
            _for_each_piece(c, lambda off, size: pltpu.make_async_copy(
                ys_ref.at[pl.ds(g + off, size)], buf.at[pl.ds(s + off, size)], sem).start(), SEG_BITS)
            return carry
        lax.fori_loop(0, ne, seg_body, 0)

    def run(buf, sem, nbuf, nsem):
        @pl.when(i == 0)
        def _():
            fetch(tile0, buf, sem)

        @pl.when(i + 1 < n)
        def _():
            fetch(tile0 + i + 1, nbuf, nsem)

        pltpu.make_async_copy(ys_ref.at[pl.ds(0, tt * TOP_K)], buf, sem).wait()

        def tok_body(t, carry):
            acc = wts_ref[0, t] * buf[lidx_ref[0, t]]
            for k in range(1, TOP_K):
                acc = acc + wts_ref[k, t] * buf[lidx_ref[k, t]]
            y3_ref[pl.ds(pl.multiple_of(t * SUB, SUB), SUB), :] = acc
            return carry
        lax.fori_loop(0, tt, tok_body, 0)

    @pl.when(i % 2 == 0)
    def _():
        run(buf0, sems.at[0], buf1, sems.at[1])

    @pl.when(i % 2 == 1)
    def _():
        run(buf1, sems.at[1], buf0, sems.at[0])

    hf = _from_row_tiles(hf3_ref, tt)
    shared = _mm(_silu(_mm(hf, wsg_ref[...])) * _mm(hf, wsu_ref[...]), wsd_ref[...])
    yf = (_from_row_tiles(y3_ref, tt) + shared).reshape(bb, ll, d)
    y_ref[...] = x1_ref[...] + gt2_ref[...] * _rms(yf, gpost_ref[...])


def _combine(cnt, seg, gbase, lidx, wts, hf3, x1, gt2, gpost, wsg, wsu, wsd, ys, tile0, ne):
    b, l, d = x1.shape
    tt = TOK_TILE
    ll = min(l, tt)
    bb = tt // ll
    nl = l // ll
    nt = (b // bb) * nl
    fs = wsg.shape[1]
    if nl == 1:
        tmap = lambda i, *_: (i, 0, 0)
        bmap = lambda i, *_: (i, 0, 0)
    else:
        tmap = lambda i, *_: (i // nl, i % nl, 0)
        bmap = lambda i, *_: (i // nl, 0, 0)
    wmap = lambda i, *_: (0, 0)
    kspec = lambda: pl.BlockSpec((TOP_K, tt), lambda i, *_: (0, i), memory_space=pltpu.SMEM)
    return pl.pallas_call(
        functools.partial(_combine_kernel, tile0=tile0, ne=ne),
        grid_spec=pltpu.PrefetchScalarGridSpec(
            num_scalar_prefetch=3, grid=(nt,),
            in_specs=[kspec(), kspec(),
                      pl.BlockSpec((tt * SUB, LANES), lambda i, *_: (i, 0)),
                      pl.BlockSpec((bb, ll, d), tmap), pl.BlockSpec((bb, 1, d), bmap),
                      pl.BlockSpec((1, d), wmap), pl.BlockSpec((d, fs), wmap), pl.BlockSpec((d, fs), wmap),
                      pl.BlockSpec((fs, d), wmap),
                      pl.BlockSpec(memory_space=pl.ANY)],
            out_specs=pl.BlockSpec((bb, ll, d), tmap),
            scratch_shapes=[pltpu.VMEM((tt * TOP_K, SUB, LANES), F32), pltpu.VMEM((tt * TOP_K, SUB, LANES), F32),
                            pltpu.VMEM((tt * SUB, LANES), F32), pltpu.SemaphoreType.DMA((2,))]),
        out_shape=jax.ShapeDtypeStruct((b, l, d), F32),
        compiler_params=_cparams(("arbitrary",)),
        name="combine",
    )(cnt, seg, gbase, lidx, wts, hf3, x1, gt2, gpost, wsg, wsu, wsd, ys)


def _rope_tables(pos, rows):
    half = DH // 2
    freqs = ROPE_BASE ** (-jnp.arange(half, dtype=F32) / half)
    ang = pos.astype(F32)[:, None] * freqs[None, :]
    cos, sin = jnp.cos(ang), jnp.sin(ang)
    cos2 = jnp.concatenate([cos, cos], axis=1)
    sin2 = jnp.concatenate([-sin, sin], axis=1)
    padr = rows - pos.shape[0]
    return jnp.pad(cos2, ((0, padr), (0, 0))), jnp.pad(sin2, ((0, padr), (0, 0)))


def _global_plan(cnt_tiles, n_assign, ne):
    counts = jnp.sum(cnt_tiles, axis=0)
    padded = (counts + MOE_TILE - 1) // MOE_TILE * MOE_TILE
    pend = jnp.cumsum(padded)
    base = pend - padded
    gbase = base[None, :] + jnp.cumsum(cnt_tiles, axis=0) - cnt_tiles
    nb = n_assign // MOE_TILE + ne
    starts = jnp.arange(nb, dtype=jnp.int32) * MOE_TILE
    block_e = jnp.minimum(jnp.sum((pend[None, :] <= starts[:, None]).astype(jnp.int32), axis=1), ne - 1)
    nb_used = pend[-1:] // MOE_TILE
    tail = jnp.concatenate([pend[-1:], nb - nb_used])
    i32 = lambda v: v.astype(jnp.int32)
    return i32(gbase.reshape(-1)), i32(block_e), i32(nb_used), i32(base + counts), i32(padded - counts), i32(tail), nb


def kernel(x_prompt, x_sample, c_prompt, c_sample, state_mlstm_C, state_mlstm_n, state_mlstm_m, state_mlstm_conv, state_ret_S, w_ada, b_ada, g_mix_pre, g_mix_post, g_ffn_pre, g_ffn_post, w_in, b_gates, conv_w, conv_b, m_gn, r_gn, w_out, w_router, b_router, w_exp_gate, w_exp_up, w_exp_down, w_sh_gate, w_sh_up, w_sh_down):
    depth = w_ada.shape[0]
    bp, lp, d = x_prompt.shape
    bs, ls, _ = x_sample.shape
    mh, rh = state_mlstm_C.shape[2], state_ret_S.shape[2]
    mw, rw = mh * DH, rh * DH
    ne = w_router.shape[2]
    tp, ts = bp * lp, bs * ls
    ntp = tp // TOK_TILE

    cos_p, sin_p = _rope_tables(jnp.arange(lp, dtype=jnp.int32), lp)
    cos_s, sin_s = _rope_tables(PAST_LEN + jnp.arange(ls, dtype=jnp.int32), CHUNK if ls % CHUNK else ls)

    hp, hs = x_prompt, x_sample
    new_p, new_s = [], []
    for l in range(depth):
        o = 0
        offs = []
        for wdt in (2 * mw, mw, mw, 2 * mh, rw, rw, rw, rw):
            offs.append((o, o + wdt))
            o += wdt
        wi = w_in[l]
        cols = [wi[:, a:b] for (a, b) in offs]
        w_in_r = jnp.concatenate(cols[:3] + cols[4:] + [cols[3], jnp.zeros((d, GATE_LANES - 2 * mh), F32)],
                                 axis=1).astype(MXU_DTYPE)
        bg = jnp.pad(b_gates[l], (0, GATE_LANES - 2 * mh)).reshape(1, GATE_LANES)
        wout = w_out[l].astype(MXU_DTYPE)
        wr = jnp.pad(w_router[l], ((0, 0), (0, GATE_LANES - ne)))
        wr_hi, wr_lo = _split_hi_lo(wr)
        wsg, wsu, wsd = (w_sh_gate[l].astype(MXU_DTYPE), w_sh_up[l].astype(MXU_DTYPE), w_sh_down[l].astype(MXU_DTYPE))
        row = lambda v: v.reshape(1, -1)

        mod = _ada(jnp.concatenate([c_prompt, c_sample], axis=0), w_ada[l], b_ada[l])
        mods_p = [m[:bp].reshape(bp, 1, d) for m in jnp.split(mod, 6, axis=1)]
        mods_s = [m[bp:].reshape(bs, 1, d) for m in jnp.split(mod, 6, axis=1)]

        def lane_m(m):
            return jnp.pad(m, ((0, 0), (mh, GATE_LANES - 2 * mh)))[:, None, :]

        init = (jnp.zeros((bp, mh, DH, DH), F32), jnp.zeros((bp, mh, DH), F32), lane_m(jnp.zeros((bp, mh), F32)),
                jnp.zeros((bp, CONV_W - 1, 2 * mw), F32), jnp.zeros((bp, rh, DH, DH), F32))
        past = (state_mlstm_C[l], state_mlstm_n[l], lane_m(state_mlstm_m[l]), state_mlstm_conv[l], state_ret_S[l])

        def mix(x, mods, state, cos2, sin2):
            sh1, sc1, gt1, sh2, sc2, gt2 = mods
            b, ll, _ = x.shape
            z = _inproj(x, sc1, sh1, row(g_mix_pre[l]), w_in_r).reshape(b, ll, -1)
            o_mix, c_n, n_n, m_n, conv_n, s_n = _mixcore(z, cos2, sin2, *state, conv_w[l], row(conv_b[l]), bg,
                                                         row(m_gn[l]), row(r_gn[l]))
            x1, hf3, lg = _mixout(o_mix, x, gt1, sc2, sh2, row(g_mix_post[l]), row(g_ffn_pre[l]), wout, wr_hi, wr_lo)
            lidx, wts, cnt, seg = _router(lg, b_router[l], ne)
            return x1, hf3, lidx, wts, cnt[::SUB, :ne], seg[::SUB, :ne], (c_n, n_n, m_n[:, 0, mh:2 * mh], conv_n, s_n)

        x1_p, hf3_p, lidx_p, wts_p, cnt_p, seg_p, st_p = mix(hp, mods_p, init, cos_p, sin_p)
        x1_s, hf3_s, lidx_s, wts_s, cnt_s, seg_s, st_s = mix(hs, mods_s, past, cos_s, sin_s)

        cnt_t = jnp.concatenate([cnt_p, cnt_s], axis=0)
        cnt = cnt_t.reshape(-1)
        seg = jnp.concatenate([seg_p, seg_s], axis=0).reshape(-1)
        gbase, block_e, nb_used, pad_start, pad_n, tail, nb = _global_plan(cnt_t, (tp + ts) * TOP_K, ne)
        xs = _padfill(pad_start, pad_n, tail, nb * MOE_TILE, ne)
        xs = _dispatch(cnt, seg, gbase, lidx_p, hf3_p, xs, 0, ne)
        xs = _dispatch(cnt, seg, gbase, lidx_s, hf3_s, xs, ntp, ne)
        ys = _ffn(block_e, nb_used, xs.reshape(nb * MOE_TILE * SUB, LANES), w_exp_gate[l], w_exp_up[l], w_exp_down[l])
        ys = ys.reshape(nb * MOE_TILE, SUB, LANES)
        hp = _combine(cnt, seg, gbase, lidx_p, wts_p, hf3_p, x1_p, mods_p[5], row(g_ffn_post[l]), wsg, wsu, wsd, ys, 0, ne)
        hs = _combine(cnt, seg, gbase, lidx_s, wts_s, hf3_s, x1_s, mods_s[5], row(g_ffn_post[l]), wsg, wsu, wsd, ys, ntp, ne)
        new_p.append(st_p)
        new_s.append(st_s)

    p_state = [jnp.stack([st[i] for st in new_p]) for i in range(5)]
    s_state = [jnp.stack([st[i] for st in new_s]) for i in range(5)]
    return (hp, hs, *p_state, *s_state)
```

```python
import functools
import math

import jax
import jax.numpy as jnp
from jax import lax
from jax.experimental import pallas as pl
from jax.experimental.pallas import tpu as pltpu

F32 = jnp.float32
MXU_DTYPE = jnp.bfloat16

NORM_EPS = 1e-6
ROPE_BASE = 10000.0
PAST_LEN = 16384
CHUNK = 128
CONV_W = 4
DH = 128
TOP_K = 8
N_GROUPS = 8
TOPK_GROUPS = 4
ROUTED_SCALE = 2.5
GATE_LANES = 128

V7X_VMEM_BYTES = 64 * 1024 * 1024
VMEM_LIMIT = 48 * 1024 * 1024

SUB, LANES = 8, 128

TOK_TILE = 256
MOE_TILE = 512


def _cparams(sem):
    return pltpu.CompilerParams(dimension_semantics=sem, vmem_limit_bytes=VMEM_LIMIT)


def _silu(x):
    return x * jax.nn.sigmoid(x)


def _rms(x, g):
    return x * lax.rsqrt(jnp.mean(x * x, axis=-1, keepdims=True) + NORM_EPS) * g


def _mm(a, b):
    return jnp.dot(a.astype(MXU_DTYPE), b.astype(MXU_DTYPE), preferred_element_type=F32)


def _mm_nt(a, b):
    return lax.dot_general(a.astype(MXU_DTYPE), b.astype(MXU_DTYPE), (((1,), (1,)), ((), ())),
                           preferred_element_type=F32)


def _mm_tn(a, b):
    return lax.dot_general(a.astype(MXU_DTYPE), b.astype(MXU_DTYPE), (((0,), (0,)), ((), ())),
                           preferred_element_type=F32)


def _ada_kernel(c_ref, w_ref, b_ref, o_ref):
    o_ref[...] = _mm(_silu(c_ref[...]), w_ref[...]) + b_ref[...]


def _ada(c, w_ada, b_ada):
    n, d = c.shape
    dout = w_ada.shape[1]
    tn = 1024
    return pl.pallas_call(
        _ada_kernel,
        grid=(dout // tn,),
        in_specs=[pl.BlockSpec((n, d), lambda j: (0, 0)),
                  pl.BlockSpec((d, tn), lambda j: (0, j)),
                  pl.BlockSpec((1, tn), lambda j: (0, j))],
        out_specs=pl.BlockSpec((n, tn), lambda j: (0, j)),
        out_shape=jax.ShapeDtypeStruct((n, dout), F32),
        compiler_params=_cparams(("arbitrary",)),
        name="ada",
    )(c, w_ada, b_ada.reshape(1, dout))


def _inproj_kernel(x_ref, sc_ref, sh_ref, g_ref, w_ref, z_ref):
    x = x_ref[...]
    h = _rms(x, g_ref[...]) * (1.0 + sc_ref[...]) + sh_ref[...]
    bb, ll, d = x.shape
    z_ref[...] = _mm(h.reshape(bb * ll, d), w_ref[...])


def _inproj(x, sc, sh, g, w):
    b, l, d = x.shape
    n = w.shape[1]
    ll = min(l, TOK_TILE)
    bb = TOK_TILE // ll
    mod_spec = pl.BlockSpec((bb, 1, d), lambda i, j: (i, 0, 0))
    return pl.pallas_call(
        _inproj_kernel,
        grid=(b // bb, l // ll),
        in_specs=[pl.BlockSpec((bb, ll, d), lambda i, j: (i, j, 0)), mod_spec, mod_spec,
                  pl.BlockSpec((1, d), lambda i, j: (0, 0)),
                  pl.BlockSpec((d, n), lambda i, j: (0, 0))],
        out_specs=pl.BlockSpec((bb * ll, n), lambda i, j: (i * (l // ll) + j, 0)),
        out_shape=jax.ShapeDtypeStruct((b * l, n), F32),
        compiler_params=_cparams(("arbitrary", "arbitrary")),
        name="inproj",
    )(x, sc, sh, g, w)


def _cumsum_rows(x):
    n = x.shape[0]
    row = lax.broadcasted_iota(jnp.int32, x.shape, 0)
    s = 1
    while s < n:
        x = x + jnp.where(row >= s, pltpu.roll(x, s, axis=0), 0.0)
        s *= 2
    return x


def _head_norm(y, g):
    mu = jnp.mean(y, axis=-1, keepdims=True)
    yc = y - mu
    var = jnp.mean(yc * yc, axis=-1, keepdims=True)
    return yc * lax.rsqrt(var + NORM_EPS) * g


def _log_sigmoid(x):
    return jnp.minimum(x, 0.0) - jnp.log1p(jnp.exp(-jnp.abs(x)))


def _mixcore_kernel(z_ref, cos_ref, sin_ref, c0_ref, n0_ref, m0_ref, conv0_ref, s0_ref,
                    convw_ref, convb_ref, bg_ref, mgn_ref, rgn_ref,
                    out_ref, c_ref, n_ref, m_ref, conv_ref, s_ref,
                    xcat_ref, *, cl, lv, mh, rh):
    @pl.when(pl.program_id(1) == 0)
    def _():
        c_ref[...] = c0_ref[...]
        n_ref[...] = n0_ref[...]
        m_ref[...] = m0_ref[...]
        s_ref[...] = s0_ref[...]
        xcat_ref[8 - (CONV_W - 1):8, :] = conv0_ref[0]

    mw = mh * DH
    rw = rh * DH

    def rows(lo, hi):
        x = z_ref[0, :, lo:hi]
        if lv == cl:
            return x
        return jnp.concatenate([x, jnp.zeros((cl - lv, hi - lo), F32)], axis=0)

    o_qk, o_mv, o_mo = 0, 2 * mw, 3 * mw
    o_rq = 4 * mw
    o_rk, o_rv, o_rg, o_gt = o_rq + rw, o_rq + 2 * rw, o_rq + 3 * rw, o_rq + 4 * rw

    xcat_ref[8:8 + cl, :] = rows(o_qk, o_qk + 2 * mw)
    qk = convb_ref[...]
    for t in range(CONV_W):
        qk = qk + xcat_ref[8 - (CONV_W - 1) + t:8 - (CONV_W - 1) + t + cl, :] * convw_ref[t:t + 1, :]
    qk = _silu(qk)
    new_conv = xcat_ref[8 + lv - (CONV_W - 1):8 + lv, :]
    conv_ref[0] = new_conv
    xcat_ref[8 - (CONV_W - 1):8, :] = new_conv

    row_c = lax.broadcasted_iota(jnp.int32, (cl, 1), 0)
    ii = lax.broadcasted_iota(jnp.int32, (cl, cl), 0)
    jj = lax.broadcasted_iota(jnp.int32, (cl, cl), 1)
    causal = jj <= ii
    eye = jj == ii
    valid_c = row_c < lv

    def to_row(col):
        return jnp.sum(jnp.where(eye, col, 0.0), axis=0, keepdims=True)

    g = rows(o_gt, o_gt + GATE_LANES) + bg_ref[...]
    logf = jnp.where(valid_c, _log_sigmoid(g), 0.0)
    bcum = _cumsum_rows(logf)
    m_prev = m_ref[0]
    a_all = bcum + m_prev
    lane = lax.broadcasted_iota(jnp.int32, (1, GATE_LANES), 1)
    m_new_row = m_prev

    for h in range(mh):
        q = qk[:, h * DH:(h + 1) * DH]
        k = qk[:, mw + h * DH:mw + (h + 1) * DH] * (DH ** -0.5)
        v = rows(o_mv + h * DH, o_mv + (h + 1) * DH)
        b_col = bcum[:, mh + h:mh + h + 1]
        logi_col = jnp.where(valid_c, g[:, h:h + 1], -jnp.inf)
        a_col = a_all[:, mh + h:mh + h + 1]
        dmat = jnp.where(causal, b_col + to_row(logi_col - b_col), -jnp.inf)
        m_t = jnp.maximum(a_col, jnp.max(dmat, axis=1, keepdims=True))
        w_intra = jnp.exp(dmat - m_t)
        w_inter = jnp.exp(a_col - m_t)
        c_old = c_ref[0, h]
        n_old = n_ref[0, h:h + 1, :]
        s = _mm_nt(q, k) * w_intra
        num = _mm(s, v) + w_inter * _mm(q, c_old)
        den = jnp.sum(s, axis=1, keepdims=True) + w_inter * jnp.sum(q * n_old, axis=1, keepdims=True)
        hh = num * (1.0 / jnp.maximum(jnp.abs(den), jnp.exp(-m_t)))
        m_new = m_t[cl - 1:cl, :]
        b_last = b_col[cl - 1:cl, :]
        w_old = jnp.exp(b_last + m_prev[:, mh + h:mh + h + 1] - m_new)
        w_k = jnp.exp(b_last - b_col + logi_col - m_new)
        c_ref[0, h] = w_old * c_old + _mm_tn(k, w_k * v)
        n_ref[0, h:h + 1, :] = w_old * n_old + jnp.sum(w_k * k, axis=0, keepdims=True)
        m_new_row = jnp.where(lane == mh + h, m_new, m_new_row)
        om = _head_norm(hh, mgn_ref[:, h * DH:(h + 1) * DH]) * jax.nn.sigmoid(rows(o_mo + h * DH, o_mo + (h + 1) * DH))
        out_ref[0, :, h * DH:(h + 1) * DH] = om[:lv]
    m_ref[0] = m_new_row

    cos2 = cos_ref[...]
    sin2 = sin_ref[...]
    rel = (ii - jj).astype(F32)
    row_f = row_c.astype(F32)

    def rot(x):
        return x * cos2 + pltpu.roll(x, DH // 2, axis=1) * sin2

    for h in range(rh):
        log_g = math.log1p(-2.0 ** (-5.0 - h))
        q = rot(rows(o_rq + h * DH, o_rq + (h + 1) * DH))
        k = rot(rows(o_rk + h * DH, o_rk + (h + 1) * DH)) * (DH ** -0.5)
        v = rows(o_rv + h * DH, o_rv + (h + 1) * DH)
        decay = jnp.where(causal, jnp.exp(jnp.maximum(rel, 0.0) * log_g), 0.0)
        s_old = s_ref[0, h]
        y = _mm(_mm_nt(q, k) * decay, v) + jnp.exp((row_f + 1.0) * log_g) * _mm(q, s_old)
        k_w = jnp.where(valid_c, jnp.exp((lv - 1.0 - row_f) * log_g), 0.0)
        s_ref[0, h] = math.exp(lv * log_g) * s_old + _mm_tn(k, k_w * v)
        orr = _head_norm(y, rgn_ref[:, h * DH:(h + 1) * DH]) * _silu(rows(o_rg + h * DH, o_rg + (h + 1) * DH))
        out_ref[0, :, mw + h * DH:mw + (h + 1) * DH] = orr[:lv]


def _mixcore(z, cos2, sin2, c0, n0, m0, conv0, s0, convw, convb, bg, mgn, rgn):
    b, l, nz = z.shape
    mh, rh = c0.shape[1], s0.shape[1]
    mw, rw = mh * DH, rh * DH
    cl = CHUNK
    lv = cl if l % cl == 0 else l
    nc = l // lv
    bmap4 = lambda i, c: (i, 0, 0, 0)
    bmap3 = lambda i, c: (i, 0, 0)
    wmap = lambda i, c: (0, 0)
    state_specs = [pl.BlockSpec((1, mh, DH, DH), bmap4), pl.BlockSpec((1, mh, DH), bmap3),
                   pl.BlockSpec((1, 1, GATE_LANES), bmap3), pl.BlockSpec((1, CONV_W - 1, 2 * mw), bmap3),
                   pl.BlockSpec((1, rh, DH, DH), bmap4)]
    state_shapes = [jax.ShapeDtypeStruct(c0.shape, F32), jax.ShapeDtypeStruct(n0.shape, F32),
                    jax.ShapeDtypeStruct(m0.shape, F32), jax.ShapeDtypeStruct(conv0.shape, F32),
                    jax.ShapeDtypeStruct(s0.shape, F32)]
    return pl.pallas_call(
        functools.partial(_mixcore_kernel, cl=cl, lv=lv, mh=mh, rh=rh),
        grid=(b, nc),
        in_specs=[pl.BlockSpec((1, lv, nz), lambda i, c: (i, c, 0)),
                  pl.BlockSpec((cl, DH), lambda i, c: (c, 0)), pl.BlockSpec((cl, DH), lambda i, c: (c, 0))]
                 + state_specs
                 + [pl.BlockSpec(convw.shape, wmap), pl.BlockSpec(convb.shape, wmap), pl.BlockSpec(bg.shape, wmap),
                    pl.BlockSpec(mgn.shape, wmap), pl.BlockSpec(rgn.shape, wmap)],
        out_specs=[pl.BlockSpec((1, lv, mw + rw), lambda i, c: (i, c, 0))] + state_specs,
        out_shape=[jax.ShapeDtypeStruct((b, l, mw + rw), F32)] + state_shapes,
        scratch_shapes=[pltpu.VMEM((cl + 8, 2 * mw), F32)],
        compiler_params=_cparams(("arbitrary", "arbitrary")),
        name="mixcore",
    )(z, cos2, sin2, c0, n0, m0, conv0, s0, convw, convb, bg, mgn, rgn)


def _split_hi_lo(x):
    hi = x.astype(MXU_DTYPE)
    lo = (x - hi.astype(F32)).astype(MXU_DTYPE)
    return hi, lo


def _to_row_tiles(ref, x):
    n = x.shape[0]
    for j in range(SUB):
        ref[pl.ds(j, n, stride=SUB), :] = x[:, j * LANES:(j + 1) * LANES]


def _from_row_tiles(ref, n):
    return jnp.concatenate([ref[pl.ds(j, n, stride=SUB), :] for j in range(SUB)], axis=1)


def _mixout_kernel(o_ref, x_ref, gt1_ref, sc2_ref, sh2_ref, gpost_ref, gpre_ref, wout_ref, wr_hi_ref, wr_lo_ref,
                   x1_ref, hf3_ref, lg_ref):
    bb, ll, d = x_ref.shape
    o = o_ref[...]
    ym = _mm(o.reshape(bb * ll, o.shape[-1]), wout_ref[...]).reshape(bb, ll, d)
    x1 = x_ref[...] + gt1_ref[...] * _rms(ym, gpost_ref[...])
    x1_ref[...] = x1
    hf = (_rms(x1, gpre_ref[...]) * (1.0 + sc2_ref[...]) + sh2_ref[...]).reshape(bb * ll, d)
    _to_row_tiles(hf3_ref, hf)
    hi, lo = _split_hi_lo(hf)
    lg_ref[...] = _mm(hi, wr_hi_ref[...]) + (_mm(lo, wr_hi_ref[...]) + _mm(hi, wr_lo_ref[...]))


def _mixout(o, x, gt1, sc2, sh2, gpost, gpre, wout, wr_hi, wr_lo):
    b, l, d = x.shape
    assert d == SUB * LANES
    w = o.shape[-1]
    ne = wr_hi.shape[1]
    ll = min(l, TOK_TILE)
    bb = TOK_TILE // ll
    tmap = lambda i, j: (i, j, 0)
    fmap = lambda i, j: (i * (l // ll) + j, 0)
    mod_spec = pl.BlockSpec((bb, 1, d), lambda i, j: (i, 0, 0))
    wmap = lambda i, j: (0, 0)
    return pl.pallas_call(
        _mixout_kernel,
        grid=(b // bb, l // ll),
        in_specs=[pl.BlockSpec((bb, ll, w), tmap), pl.BlockSpec((bb, ll, d), tmap), mod_spec, mod_spec, mod_spec,
                  pl.BlockSpec((1, d), wmap), pl.BlockSpec((1, d), wmap), pl.BlockSpec((w, d), wmap),
                  pl.BlockSpec((d, ne), wmap), pl.BlockSpec((d, ne), wmap)],
        out_specs=[pl.BlockSpec((bb, ll, d), tmap), pl.BlockSpec((bb * ll * SUB, LANES), fmap),
                   pl.BlockSpec((bb * ll, ne), fmap)],
        out_shape=[jax.ShapeDtypeStruct((b, l, d), F32), jax.ShapeDtypeStruct((b * l * SUB, LANES), F32),
                   jax.ShapeDtypeStruct((b * l, ne), F32)],
        compiler_params=_cparams(("arbitrary", "arbitrary")),
        name="mixout",
    )(o, x, gt1, sc2, sh2, gpost, gpre, wout, wr_hi, wr_lo)


def _router_kernel(lg_ref, br_ref, lidx_ref, wts_ref, cnt_ref, seg_ref, *, ne):
    tm = lg_ref.shape[0]
    gsz = ne // N_GROUPS
    s = jax.nn.sigmoid(lg_ref[...].T[:ne, :])
    sb = s + br_ref[...]
    sb3 = sb.reshape(N_GROUPS, gsz, tm)
    e3 = lax.broadcasted_iota(jnp.int32, sb3.shape, 1)
    m1 = jnp.max(sb3, axis=1, keepdims=True)
    first = jnp.min(jnp.where(sb3 == m1, e3, gsz), axis=1, keepdims=True)
    m2 = jnp.max(jnp.where(e3 == first, -jnp.inf, sb3), axis=1, keepdims=True)
    gs = (m1 + m2).reshape(N_GROUPS, tm)
    gi = lax.broadcasted_iota(jnp.int32, gs.shape, 0)
    grank = jnp.zeros(gs.shape, F32)
    for g in range(N_GROUPS):
        r = gs[g:g + 1, :]
        grank = grank + jnp.where(r > gs, 1.0, jnp.where(r == gs, jnp.where(g < gi, 1.0, 0.0), 0.0))
    gsel = jnp.where(grank < TOPK_GROUPS, 1.0, 0.0)
    emask = jnp.broadcast_to(gsel.reshape(N_GROUPS, 1, tm), sb3.shape).reshape(ne, tm) > 0.0
    masked = jnp.where(emask, sb, -jnp.inf)
    ei = lax.broadcasted_iota(jnp.int32, masked.shape, 0)
    rank = jnp.zeros(masked.shape, F32)
    for j in range(ne):
        r = masked[j:j + 1, :]
        rank = rank + jnp.where(r > masked, 1.0, jnp.where(r == masked, jnp.where(j < ei, 1.0, 0.0), 0.0))
    sel = rank < TOP_K
    w = jnp.where(sel, s, 0.0)
    w = w / jnp.sum(w, axis=0, keepdims=True) * ROUTED_SCALE

    self = jnp.where(sel, 1.0, 0.0)
    er = lax.broadcasted_iota(jnp.int32, (ne, ne), 0)
    ec = lax.broadcasted_iota(jnp.int32, (ne, ne), 1)
    slot = _mm(jnp.where(ec < er, 1.0, 0.0), self)
    tr = lax.broadcasted_iota(jnp.int32, (tm, tm), 0)
    tc = lax.broadcasted_iota(jnp.int32, (tm, tm), 1)
    pos = _mm(self, jnp.where(tr < tc, 1.0, 0.0))
    seg = jnp.sum(slot, axis=1, keepdims=True)
    lrow = seg + pos
    lidx_rows, w_rows = [], []
    for k in range(TOP_K):
        pick = sel & (slot == float(k))
        lidx_rows.append(jnp.sum(jnp.where(pick, lrow, 0.0), axis=0, keepdims=True))
        w_rows.append(jnp.sum(jnp.where(pick, w, 0.0), axis=0, keepdims=True))
    lidx_ref[...] = jnp.concatenate(lidx_rows, axis=0).astype(jnp.int32)
    wts_ref[...] = jnp.concatenate(w_rows, axis=0)
    ones = jnp.ones((SUB, tm), F32)
    padz = jnp.zeros((LANES - ne, tm), F32)
    cnt_ref[...] = _mm_nt(ones, jnp.concatenate([self, padz], axis=0)).astype(jnp.int32)
    seg_ref[...] = _mm_nt(ones, jnp.concatenate([slot, padz], axis=0)).astype(jnp.int32)


def _router(logits, b_router, ne):
    t, lanes = logits.shape
    tm = TOK_TILE
    nt = t // tm
    kspec = pl.BlockSpec((TOP_K, tm), lambda i: (0, i))
    cspec = pl.BlockSpec((SUB, LANES), lambda i: (i, 0))
    return pl.pallas_call(
        functools.partial(_router_kernel, ne=ne),
        grid=(nt,),
        in_specs=[pl.BlockSpec((tm, lanes), lambda i: (i, 0)), pl.BlockSpec((ne, 1), lambda i: (0, 0))],
        out_specs=[kspec, kspec, cspec, cspec],
        out_shape=[jax.ShapeDtypeStruct((TOP_K, t), jnp.int32), jax.ShapeDtypeStruct((TOP_K, t), F32),
                   jax.ShapeDtypeStruct((nt * SUB, LANES), jnp.int32),
                   jax.ShapeDtypeStruct((nt * SUB, LANES), jnp.int32)],
        compiler_params=_cparams(("arbitrary",)),
        name="router",
    )(logits, b_router.reshape(ne, 1))


SEG_BITS = TOK_TILE.bit_length()
PAD_BITS = (MOE_TILE - 1).bit_length()


RARE_BIT = 6


def _for_each_piece(n, fn, bits):
    def pieces(lo, hi):
        for bit in range(hi - 1, lo - 1, -1):
            @pl.when(((n >> bit) & 1) == 1)
            def _():
                fn((n >> (bit + 1)) << (bit + 1), 1 << bit)

    if bits > RARE_BIT:
        @pl.when(n >= (1 << RARE_BIT))
        def _():
            pieces(RARE_BIT, bits)
    pieces(0, min(bits, RARE_BIT))


SORT_UNROLL = 4


def _zero_rows(ps_ref, pn_ref, tail_ref, xs_ref, z_ref, sem, ne, act):
    def pad_body(e, carry):
        _for_each_piece(pn_ref[e], lambda off, size: act(
            pltpu.make_async_copy(z_ref.at[pl.ds(0, size)], xs_ref.at[pl.ds(ps_ref[e] + off, size)], sem)),
            PAD_BITS)
        return carry
    lax.fori_loop(0, ne, pad_body, 0)

    def tail_body(b, carry):
        act(pltpu.make_async_copy(z_ref, xs_ref.at[pl.ds(tail_ref[0] + b * MOE_TILE, MOE_TILE)], sem))
        return carry
    lax.fori_loop(0, tail_ref[1], tail_body, 0)


def _dispatch_kernel(cnt_ref, seg_ref, gb_ref, ps_ref, pn_ref, tail_ref, lidx_p_ref, lidx_s_ref, hf3_p_ref, hf3_s_ref,
                     xs_ref, buf0, buf1, z_ref, sems, *, ntp, ne):
    i = pl.program_id(0)
    n = pl.num_programs(0)
    tt = lidx_p_ref.shape[0] // TOP_K

    def whole(buf, sem):
        return pltpu.make_async_copy(buf, xs_ref.at[pl.ds(0, tt * TOP_K)], sem)

    @pl.when(i == 0)
    def _():
        z_ref[...] = jnp.zeros(z_ref.shape, F32)
        _zero_rows(ps_ref, pn_ref, tail_ref, xs_ref, z_ref, sems.at[2], ne, lambda c: c.start())

    def run(buf, sem):
        @pl.when(i >= 2)
        def _():
            whole(buf, sem).wait()

        def sort_from(hf3_ref, lidx_ref):
            def sort_body(tb, carry):
                for u in range(SORT_UNROLL):
                    t = tb * SORT_UNROLL + u
                    v = hf3_ref[pl.ds(pl.multiple_of(t * SUB, SUB), SUB), :]
                    for k in range(TOP_K):
                        buf[lidx_ref[t * TOP_K + k]] = v
                return carry
            lax.fori_loop(0, tt // SORT_UNROLL, sort_body, 0)

        @pl.when(i < ntp)
        def _():
            sort_from(hf3_p_ref, lidx_p_ref)

        @pl.when(i >= ntp)
        def _():
            sort_from(hf3_s_ref, lidx_s_ref)

        def seg_body(e, carry):
            c, s, g = cnt_ref[i * ne + e], seg_ref[i * ne + e], gb_ref[i * ne + e]
            _for_each_piece(c, lambda off, size: pltpu.make_async_copy(
                buf.at[pl.ds(s + off, size)], xs_ref.at[pl.ds(g + off, size)], sem).start(), SEG_BITS)
            return carry
        lax.fori_loop(0, ne, seg_body, 0)

    @pl.when(i % 2 == 0)
    def _():
        run(buf0, sems.at[0])

    @pl.when(i % 2 == 1)
    def _():
        run(buf1, sems.at[1])

    @pl.when(i == 0)
    def _():
        _zero_rows(ps_ref, pn_ref, tail_ref, xs_ref, z_ref, sems.at[2], ne, lambda c: c.wait())

    @pl.when(i == n - 1)
    def _():
        @pl.when(i % 2 == 0)
        def _():
            whole(buf0, sems.at[0]).wait()

            @pl.when(i >= 1)
            def _():
                whole(buf1, sems.at[1]).wait()

        @pl.when(i % 2 == 1)
        def _():
            whole(buf1, sems.at[1]).wait()
            whole(buf0, sems.at[0]).wait()


def _dispatch(cnt, seg, gbase, pad_start, pad_n, tail, lidx_p, lidx_s, hf3_p, hf3_s, rows, ne):
    tt = TOK_TILE
    ntp, nts = lidx_p.shape[0] // (tt * TOP_K), lidx_s.shape[0] // (tt * TOP_K)
    pmap = lambda i, *_: jnp.minimum(i, ntp - 1)
    smap = lambda i, *_: jnp.maximum(i - ntp, 0)
    return pl.pallas_call(
        functools.partial(_dispatch_kernel, ntp=ntp, ne=ne),
        grid_spec=pltpu.PrefetchScalarGridSpec(
            num_scalar_prefetch=6, grid=(ntp + nts,),
            in_specs=[pl.BlockSpec((tt * TOP_K,), lambda i, *_: (pmap(i),), memory_space=pltpu.SMEM),
                      pl.BlockSpec((tt * TOP_K,), lambda i, *_: (smap(i),), memory_space=pltpu.SMEM),
                      pl.BlockSpec((tt * SUB, LANES), lambda i, *_: (pmap(i), 0)),
                      pl.BlockSpec((tt * SUB, LANES), lambda i, *_: (smap(i), 0))],
            out_specs=pl.BlockSpec(memory_space=pl.ANY),
            scratch_shapes=[pltpu.VMEM((tt * TOP_K, SUB, LANES), F32), pltpu.VMEM((tt * TOP_K, SUB, LANES), F32),
                            pltpu.VMEM((MOE_TILE, SUB, LANES), F32), pltpu.SemaphoreType.DMA((3,))]),
        out_shape=jax.ShapeDtypeStruct((rows, SUB, LANES), F32),
        compiler_params=_cparams(("arbitrary",)),
        name="dispatch",
    )(cnt, seg, gbase, pad_start, pad_n, tail, lidx_p, lidx_s, hf3_p, hf3_s)


def _ffn_kernel(be_ref, nbu_ref, xs_ref, wg_ref, wu_ref, wd_ref, ys_ref, wg_s, wu_s, wd_s):
    b = pl.program_id(0)
    tm = xs_ref.shape[0] // SUB

    @pl.when(b < nbu_ref[0])
    def _():
        @pl.when((b == 0) | (be_ref[b] != be_ref[jnp.maximum(b - 1, 0)]))
        def _():
            wg_s[...] = wg_ref[0].astype(MXU_DTYPE)
            wu_s[...] = wu_ref[0].astype(MXU_DTYPE)
            wd_s[...] = wd_ref[0].astype(MXU_DTYPE)

        x = _from_row_tiles(xs_ref, tm)
        hb = _silu(_mm(x, wg_s[...])) * _mm(x, wu_s[...])
        _to_row_tiles(ys_ref, _mm(hb, wd_s[...]))

    @pl.when(b >= nbu_ref[0])
    def _():
        ys_ref[...] = jnp.zeros(ys_ref.shape, F32)


def _ffn(block_e, nb_used, xs2, wg, wu, wd):
    rows = xs2.shape[0] // SUB
    d, f = wg.shape[1], wg.shape[2]
    tm = MOE_TILE
    nb = rows // tm
    xmap = lambda b, be, nbu: (jnp.minimum(b, nbu[0] - 1), 0)
    wmap = lambda b, be, nbu: (be[b], 0, 0)
    return pl.pallas_call(
        _ffn_kernel,
        grid_spec=pltpu.PrefetchScalarGridSpec(
            num_scalar_prefetch=2,
            grid=(nb,),
            in_specs=[pl.BlockSpec((tm * SUB, LANES), xmap), pl.BlockSpec((1, d, f), wmap),
                      pl.BlockSpec((1, d, f), wmap), pl.BlockSpec((1, f, d), wmap)],
            out_specs=pl.BlockSpec((tm * SUB, LANES), lambda b, be, nbu: (b, 0)),
            scratch_shapes=[pltpu.VMEM((d, f), MXU_DTYPE), pltpu.VMEM((d, f), MXU_DTYPE),
                            pltpu.VMEM((f, d), MXU_DTYPE)]),
        out_shape=jax.ShapeDtypeStruct((rows * SUB, LANES), F32),
        compiler_params=_cparams(("arbitrary",)),
        name="ffn",
    )(block_e, nb_used, xs2, wg, wu, wd)


def _combine_kernel(cnt_ref, seg_ref, gb_ref, lidx_ref, wts_ref, hf3_ref, x1_ref, gt2_ref, gpost_ref,
                    wsg_ref, wsu_ref, wsd_ref, ys_ref, y_ref, buf0, buf1, y3_ref, sems, *, tile0, ne):
    bb, ll, d = x1_ref.shape
    tt = bb * ll
    i = pl.program_id(0)
    n = pl.num_programs(0)

    def fetch(j, buf, sem):
        def seg_body(e, carry):
            c, s, g = cnt_ref[j * ne + e], seg_ref[j * ne + e], gb_ref[j * ne + e]
            _for_each_piece(c, lambda off, size: pltpu.make_async_copy(
                ys_ref.at[pl.ds(g + off, size)], buf.at[pl.ds(s + off, size)], sem).start(), SEG_BITS)
            return carry
        lax.fori_loop(0, ne, seg_body, 0)

    def run(buf, sem, nbuf, nsem):
        @pl.when(i == 0)
        def _():
            fetch(tile0, buf, sem)

        @pl.when(i + 1 < n)
        def _():
            fetch(tile0 + i + 1, nbuf, nsem)

        pltpu.make_async_copy(ys_ref.at[pl.ds(0, tt * TOP_K)], buf, sem).wait()

        def tok_body(tb, carry):
            for u in range(SORT_UNROLL):
                t = tb * SORT_UNROLL + u
                acc = wts_ref[t * TOP_K] * buf[lidx_ref[t * TOP_K]]
                for k in range(1, TOP_K):
                    acc = acc + wts_ref[t * TOP_K + k] * buf[lidx_ref[t * TOP_K + k]]
                y3_ref[pl.ds(pl.multiple_of(t * SUB, SUB), SUB), :] = acc
            return carry
        lax.fori_loop(0, tt // SORT_UNROLL, tok_body, 0)

    @pl.when(i % 2 == 0)
    def _():
        run(buf0, sems.at[0], buf1, sems.at[1])

    @pl.when(i % 2 == 1)
    def _():
        run(buf1, sems.at[1], buf0, sems.at[0])

    hf = _from_row_tiles(hf3_ref, tt)
    shared = _mm(_silu(_mm(hf, wsg_ref[...])) * _mm(hf, wsu_ref[...]), wsd_ref[...])
    yf = (_from_row_tiles(y3_ref, tt) + shared).reshape(bb, ll, d)
    y_ref[...] = x1_ref[...] + gt2_ref[...] * _rms(yf, gpost_ref[...])


def _combine(cnt, seg, gbase, lidx, wts, hf3, x1, gt2, gpost, wsg, wsu, wsd, ys, tile0, ne):
    b, l, d = x1.shape
    tt = TOK_TILE
    ll = min(l, tt)
    bb = tt // ll
    nl = l // ll
    nt = (b // bb) * nl
    fs = wsg.shape[1]
    if nl == 1:
        tmap = lambda i, *_: (i, 0, 0)
        bmap = lambda i, *_: (i, 0, 0)
    else:
        tmap = lambda i, *_: (i // nl, i % nl, 0)
        bmap = lambda i, *_: (i // nl, 0, 0)
    wmap = lambda i, *_: (0, 0)
    kspec = lambda: pl.BlockSpec((tt * TOP_K,), lambda i, *_: (i,), memory_space=pltpu.SMEM)
    return pl.pallas_call(
        functools.partial(_combine_kernel, tile0=tile0, ne=ne),
        grid_spec=pltpu.PrefetchScalarGridSpec(
            num_scalar_prefetch=3, grid=(nt,),
            in_specs=[kspec(), kspec(),
                      pl.BlockSpec((tt * SUB, LANES), lambda i, *_: (i, 0)),
                      pl.BlockSpec((bb, ll, d), tmap), pl.BlockSpec((bb, 1, d), bmap),
                      pl.BlockSpec((1, d), wmap), pl.BlockSpec((d, fs), wmap), pl.BlockSpec((d, fs), wmap),
                      pl.BlockSpec((fs, d), wmap),
                      pl.BlockSpec(memory_space=pl.ANY)],
            out_specs=pl.BlockSpec((bb, ll, d), tmap),
            scratch_shapes=[pltpu.VMEM((tt * TOP_K, SUB, LANES), F32), pltpu.VMEM((tt * TOP_K, SUB, LANES), F32),
                            pltpu.VMEM((tt * SUB, LANES), F32), pltpu.SemaphoreType.DMA((2,))]),
        out_shape=jax.ShapeDtypeStruct((b, l, d), F32),
        compiler_params=_cparams(("arbitrary",)),
        name="combine",
    )(cnt, seg, gbase, lidx, wts, hf3, x1, gt2, gpost, wsg, wsu, wsd, ys)


def _rope_tables(pos, rows):
    half = DH // 2
    freqs = ROPE_BASE ** (-jnp.arange(half, dtype=F32) / half)
    ang = pos.astype(F32)[:, None] * freqs[None, :]
    cos, sin = jnp.cos(ang), jnp.sin(ang)
    cos2 = jnp.concatenate([cos, cos], axis=1)
    sin2 = jnp.concatenate([-sin, sin], axis=1)
    padr = rows - pos.shape[0]
    return jnp.pad(cos2, ((0, padr), (0, 0))), jnp.pad(sin2, ((0, padr), (0, 0)))


def _global_plan(cnt_tiles, n_assign, ne):
    counts = jnp.sum(cnt_tiles, axis=0)
    padded = (counts + MOE_TILE - 1) // MOE_TILE * MOE_TILE
    pend = jnp.cumsum(padded)
    base = pend - padded
    gbase = base[None, :] + jnp.cumsum(cnt_tiles, axis=0) - cnt_tiles
    nb = n_assign // MOE_TILE + ne
    starts = jnp.arange(nb, dtype=jnp.int32) * MOE_TILE
    block_e = jnp.minimum(jnp.sum((pend[None, :] <= starts[:, None]).astype(jnp.int32), axis=1), ne - 1)
    nb_used = pend[-1:] // MOE_TILE
    tail = jnp.concatenate([pend[-1:], nb - nb_used])
    i32 = lambda v: v.astype(jnp.int32)
    return i32(gbase.reshape(-1)), i32(block_e), i32(nb_used), i32(base + counts), i32(padded - counts), i32(tail), nb


def kernel(x_prompt, x_sample, c_prompt, c_sample, state_mlstm_C, state_mlstm_n, state_mlstm_m, state_mlstm_conv, state_ret_S, w_ada, b_ada, g_mix_pre, g_mix_post, g_ffn_pre, g_ffn_post, w_in, b_gates, conv_w, conv_b, m_gn, r_gn, w_out, w_router, b_router, w_exp_gate, w_exp_up, w_exp_down, w_sh_gate, w_sh_up, w_sh_down):
    depth = w_ada.shape[0]
    bp, lp, d = x_prompt.shape
    bs, ls, _ = x_sample.shape
    mh, rh = state_mlstm_C.shape[2], state_ret_S.shape[2]
    mw, rw = mh * DH, rh * DH
    ne = w_router.shape[2]
    tp, ts = bp * lp, bs * ls
    ntp = tp // TOK_TILE

    cos_p, sin_p = _rope_tables(jnp.arange(lp, dtype=jnp.int32), lp)
    cos_s, sin_s = _rope_tables(PAST_LEN + jnp.arange(ls, dtype=jnp.int32), CHUNK if ls % CHUNK else ls)

    hp, hs = x_prompt, x_sample
    new_p, new_s = [], []
    for l in range(depth):
        o = 0
        offs = []
        for wdt in (2 * mw, mw, mw, 2 * mh, rw, rw, rw, rw):
            offs.append((o, o + wdt))
            o += wdt
        wi = w_in[l]
        cols = [wi[:, a:b] for (a, b) in offs]
        w_in_r = jnp.concatenate(cols[:3] + cols[4:] + [cols[3], jnp.zeros((d, GATE_LANES - 2 * mh), F32)],
                                 axis=1).astype(MXU_DTYPE)
        bg = jnp.pad(b_gates[l], (0, GATE_LANES - 2 * mh)).reshape(1, GATE_LANES)
        wout = w_out[l].astype(MXU_DTYPE)
        wr = jnp.pad(w_router[l], ((0, 0), (0, GATE_LANES - ne)))
        wr_hi, wr_lo = _split_hi_lo(wr)
        wsg, wsu, wsd = (w_sh_gate[l].astype(MXU_DTYPE), w_sh_up[l].astype(MXU_DTYPE), w_sh_down[l].astype(MXU_DTYPE))
        row = lambda v: v.reshape(1, -1)

        mod = _ada(jnp.concatenate([c_prompt, c_sample], axis=0), w_ada[l], b_ada[l])
        mods_p = [m[:bp].reshape(bp, 1, d) for m in jnp.split(mod, 6, axis=1)]
        mods_s = [m[bp:].reshape(bs, 1, d) for m in jnp.split(mod, 6, axis=1)]

        def lane_m(m):
            return jnp.pad(m, ((0, 0), (mh, GATE_LANES - 2 * mh)))[:, None, :]

        init = (jnp.zeros((bp, mh, DH, DH), F32), jnp.zeros((bp, mh, DH), F32), lane_m(jnp.zeros((bp, mh), F32)),
                jnp.zeros((bp, CONV_W - 1, 2 * mw), F32), jnp.zeros((bp, rh, DH, DH), F32))
        past = (state_mlstm_C[l], state_mlstm_n[l], lane_m(state_mlstm_m[l]), state_mlstm_conv[l], state_ret_S[l])

        def mix(x, mods, state, cos2, sin2):
            sh1, sc1, gt1, sh2, sc2, gt2 = mods
            b, ll, _ = x.shape
            z = _inproj(x, sc1, sh1, row(g_mix_pre[l]), w_in_r).reshape(b, ll, -1)
            o_mix, c_n, n_n, m_n, conv_n, s_n = _mixcore(z, cos2, sin2, *state, conv_w[l], row(conv_b[l]), bg,
                                                         row(m_gn[l]), row(r_gn[l]))
            x1, hf3, lg = _mixout(o_mix, x, gt1, sc2, sh2, row(g_mix_post[l]), row(g_ffn_pre[l]), wout, wr_hi, wr_lo)
            lidx, wts, cnt, seg = _router(lg, b_router[l], ne)
            flat = lambda v: v.T.reshape(-1)
            return (x1, hf3, flat(lidx), flat(wts), cnt[::SUB, :ne], seg[::SUB, :ne],
                    (c_n, n_n, m_n[:, 0, mh:2 * mh], conv_n, s_n))

        x1_p, hf3_p, lidx_p, wts_p, cnt_p, seg_p, st_p = mix(hp, mods_p, init, cos_p, sin_p)
        x1_s, hf3_s, lidx_s, wts_s, cnt_s, seg_s, st_s = mix(hs, mods_s, past, cos_s, sin_s)

        cnt_t = jnp.concatenate([cnt_p, cnt_s], axis=0)
        cnt = cnt_t.reshape(-1)
        seg = jnp.concatenate([seg_p, seg_s], axis=0).reshape(-1)
        gbase, block_e, nb_used, pad_start, pad_n, tail, nb = _global_plan(cnt_t, (tp + ts) * TOP_K, ne)
        xs = _dispatch(cnt, seg, gbase, pad_start, pad_n, tail, lidx_p, lidx_s, hf3_p, hf3_s, nb * MOE_TILE, ne)
        ys = _ffn(block_e, nb_used, xs.reshape(nb * MOE_TILE * SUB, LANES), w_exp_gate[l], w_exp_up[l], w_exp_down[l])
        ys = ys.reshape(nb * MOE_TILE, SUB, LANES)
        hp = _combine(cnt, seg, gbase, lidx_p, wts_p, hf3_p, x1_p, mods_p[5], row(g_ffn_post[l]), wsg, wsu, wsd, ys, 0, ne)
        hs = _combine(cnt, seg, gbase, lidx_s, wts_s, hf3_s, x1_s, mods_s[5], row(g_ffn_post[l]), wsg, wsu, wsd, ys, ntp, ne)
        new_p.append(st_p)
        new_s.append(st_s)

    p_state = [jnp.stack([st[i] for st in new_p]) for i in range(5)]
    s_state = [jnp.stack([st[i] for st in new_s]) for i in range(5)]
    return (hp, hs, *p_state, *s_state)
```

```python
import functools
import math

import jax
import jax.numpy as jnp
from jax import lax
from jax.experimental import pallas as pl
from jax.experimental.pallas import tpu as pltpu

F32 = jnp.float32
MXU_DTYPE = jnp.bfloat16

NORM_EPS = 1e-6
ROPE_BASE = 10000.0
PAST_LEN = 16384
CHUNK = 128
CONV_W = 4
DH = 128
TOP_K = 8
N_GROUPS = 8
TOPK_GROUPS = 4
ROUTED_SCALE = 2.5
GATE_LANES = 128

V7X_VMEM_BYTES = 64 * 1024 * 1024
VMEM_LIMIT = 48 * 1024 * 1024
VMEM_LIMIT_COMBINE = 56 * 1024 * 1024

SUB, LANES = 8, 128

TOK_TILE = 256
ROUTE_TILE = 512
MOE_TILE = 512


def _cparams(sem, vmem_limit=VMEM_LIMIT):
    return pltpu.CompilerParams(dimension_semantics=sem, vmem_limit_bytes=vmem_limit)


def _silu(x):
    return x * jax.nn.sigmoid(x)


def _rms(x, g):
    return x * lax.rsqrt(jnp.mean(x * x, axis=-1, keepdims=True) + NORM_EPS) * g


def _mm(a, b):
    return jnp.dot(a.astype(MXU_DTYPE), b.astype(MXU_DTYPE), preferred_element_type=F32)


def _mm_nt(a, b):
    return lax.dot_general(a.astype(MXU_DTYPE), b.astype(MXU_DTYPE), (((1,), (1,)), ((), ())),
                           preferred_element_type=F32)


def _mm_tn(a, b):
    return lax.dot_general(a.astype(MXU_DTYPE), b.astype(MXU_DTYPE), (((0,), (0,)), ((), ())),
                           preferred_element_type=F32)


def _ada_kernel(c_ref, w_ref, b_ref, o_ref):
    o_ref[...] = _mm(_silu(c_ref[...]), w_ref[...]) + b_ref[...]


def _ada(c, w_ada, b_ada):
    n, d = c.shape
    dout = w_ada.shape[1]
    tn = 1024
    return pl.pallas_call(
        _ada_kernel,
        grid=(dout // tn,),
        in_specs=[pl.BlockSpec((n, d), lambda j: (0, 0)),
                  pl.BlockSpec((d, tn), lambda j: (0, j)),
                  pl.BlockSpec((1, tn), lambda j: (0, j))],
        out_specs=pl.BlockSpec((n, tn), lambda j: (0, j)),
        out_shape=jax.ShapeDtypeStruct((n, dout), F32),
        compiler_params=_cparams(("arbitrary",)),
        name="ada",
    )(c, w_ada, b_ada.reshape(1, dout))


def _inproj_kernel(x_ref, sc_ref, sh_ref, g_ref, w_ref, z_ref):
    x = x_ref[...]
    h = _rms(x, g_ref[...]) * (1.0 + sc_ref[...]) + sh_ref[...]
    bb, ll, d = x.shape
    z_ref[...] = _mm(h.reshape(bb * ll, d), w_ref[...])


def _inproj(x, sc, sh, g, w):
    b, l, d = x.shape
    n = w.shape[1]
    ll = min(l, TOK_TILE)
    bb = TOK_TILE // ll
    mod_spec = pl.BlockSpec((bb, 1, d), lambda i, j: (i, 0, 0))
    return pl.pallas_call(
        _inproj_kernel,
        grid=(b // bb, l // ll),
        in_specs=[pl.BlockSpec((bb, ll, d), lambda i, j: (i, j, 0)), mod_spec, mod_spec,
                  pl.BlockSpec((1, d), lambda i, j: (0, 0)),
                  pl.BlockSpec((d, n), lambda i, j: (0, 0))],
        out_specs=pl.BlockSpec((bb * ll, n), lambda i, j: (i * (l // ll) + j, 0)),
        out_shape=jax.ShapeDtypeStruct((b * l, n), F32),
        compiler_params=_cparams(("arbitrary", "arbitrary")),
        name="inproj",
    )(x, sc, sh, g, w)


def _cumsum_rows(x):
    n = x.shape[0]
    row = lax.broadcasted_iota(jnp.int32, x.shape, 0)
    s = 1
    while s < n:
        x = x + jnp.where(row >= s, pltpu.roll(x, s, axis=0), 0.0)
        s *= 2
    return x


def _head_norm(y, g):
    mu = jnp.mean(y, axis=-1, keepdims=True)
    yc = y - mu
    var = jnp.mean(yc * yc, axis=-1, keepdims=True)
    return yc * lax.rsqrt(var + NORM_EPS) * g


def _log_sigmoid(x):
    return jnp.minimum(x, 0.0) - jnp.log1p(jnp.exp(-jnp.abs(x)))


def _chunk_step(rows, put, cos2, sin2, c_ref, n_ref, m_ref, conv_ref, s_ref, xcat_ref,
                convw_ref, convb_ref, bg_ref, mgn_ref, rgn_ref, mh, rh):
    cl = CHUNK
    mw = mh * DH
    rw = rh * DH
    o_qk, o_mv, o_mo = 0, 2 * mw, 3 * mw
    o_rq = 4 * mw
    o_rk, o_rv, o_rg, o_gt = o_rq + rw, o_rq + 2 * rw, o_rq + 3 * rw, o_rq + 4 * rw

    xcat_ref[8:8 + cl, :] = rows(o_qk, o_qk + 2 * mw)
    qk = convb_ref[...]
    for t in range(CONV_W):
        qk = qk + xcat_ref[8 - (CONV_W - 1) + t:8 - (CONV_W - 1) + t + cl, :] * convw_ref[t:t + 1, :]
    qk = _silu(qk)
    new_conv = xcat_ref[8 + cl - (CONV_W - 1):8 + cl, :]
    conv_ref[0] = new_conv
    xcat_ref[8 - (CONV_W - 1):8, :] = new_conv

    row_c = lax.broadcasted_iota(jnp.int32, (cl, 1), 0)
    ii = lax.broadcasted_iota(jnp.int32, (cl, cl), 0)
    jj = lax.broadcasted_iota(jnp.int32, (cl, cl), 1)
    causal = jj <= ii
    eye = jj == ii

    def to_row(col):
        return jnp.sum(jnp.where(eye, col, 0.0), axis=0, keepdims=True)

    g = rows(o_gt, o_gt + GATE_LANES) + bg_ref[...]
    bcum = _cumsum_rows(_log_sigmoid(g))
    m_prev = m_ref[0]
    a_all = bcum + m_prev
    lane = lax.broadcasted_iota(jnp.int32, (1, GATE_LANES), 1)
    m_new_row = m_prev

    for h in range(mh):
        q = qk[:, h * DH:(h + 1) * DH]
        k = qk[:, mw + h * DH:mw + (h + 1) * DH] * (DH ** -0.5)
        v = rows(o_mv + h * DH, o_mv + (h + 1) * DH)
        b_col = bcum[:, mh + h:mh + h + 1]
        logi_col = g[:, h:h + 1]
        a_col = a_all[:, mh + h:mh + h + 1]
        dmat = jnp.where(causal, b_col + to_row(logi_col - b_col), -jnp.inf)
        m_t = jnp.maximum(a_col, jnp.max(dmat, axis=1, keepdims=True))
        w_intra = jnp.exp(dmat - m_t)
        w_inter = jnp.exp(a_col - m_t)
        c_old = c_ref[0, h]
        n_old = n_ref[0, h:h + 1, :]
        s = _mm_nt(q, k) * w_intra
        num = _mm(s, v) + w_inter * _mm(q, c_old)
        den = jnp.sum(s, axis=1, keepdims=True) + w_inter * jnp.sum(q * n_old, axis=1, keepdims=True)
        hh = num * (1.0 / jnp.maximum(jnp.abs(den), jnp.exp(-m_t)))
        m_new = m_t[cl - 1:cl, :]
        b_last = b_col[cl - 1:cl, :]
        w_old = jnp.exp(b_last + m_prev[:, mh + h:mh + h + 1] - m_new)
        w_k = jnp.exp(b_last - b_col + logi_col - m_new)
        c_ref[0, h] = w_old * c_old + _mm_tn(k, w_k * v)
        n_ref[0, h:h + 1, :] = w_old * n_old + jnp.sum(w_k * k, axis=0, keepdims=True)
        m_new_row = jnp.where(lane == mh + h, m_new, m_new_row)
        om = _head_norm(hh, mgn_ref[:, h * DH:(h + 1) * DH]) * jax.nn.sigmoid(rows(o_mo + h * DH, o_mo + (h + 1) * DH))
        put(h * DH, (h + 1) * DH, om)
    m_ref[0] = m_new_row

    rel = (ii - jj).astype(F32)
    row_f = row_c.astype(F32)

    def rot(x):
        return x * cos2 + pltpu.roll(x, DH // 2, axis=1) * sin2

    for h in range(rh):
        log_g = math.log1p(-2.0 ** (-5.0 - h))
        q = rot(rows(o_rq + h * DH, o_rq + (h + 1) * DH))
        k = rot(rows(o_rk + h * DH, o_rk + (h + 1) * DH)) * (DH ** -0.5)
        v = rows(o_rv + h * DH, o_rv + (h + 1) * DH)
        decay = jnp.where(causal, jnp.exp(jnp.maximum(rel, 0.0) * log_g), 0.0)
        s_old = s_ref[0, h]
        y = _mm(_mm_nt(q, k) * decay, v) + jnp.exp((row_f + 1.0) * log_g) * _mm(q, s_old)
        k_w = jnp.exp((cl - 1.0 - row_f) * log_g)
        s_ref[0, h] = math.exp(cl * log_g) * s_old + _mm_tn(k, k_w * v)
        orr = _head_norm(y, rgn_ref[:, h * DH:(h + 1) * DH]) * _silu(rows(o_rg + h * DH, o_rg + (h + 1) * DH))
        put(mw + h * DH, mw + (h + 1) * DH, orr)


MIX_ROWS = 512


def _mixer_kernel(x_ref, sc1_ref, sh1_ref, gt1_ref, sc2_ref, sh2_ref, cos_ref, sin_ref,
                  c0_ref, n0_ref, m0_ref, conv0_ref, s0_ref,
                  gpre1_ref, win_ref, convw_ref, convb_ref, bg_ref, mgn_ref, rgn_ref,
                  gpost_ref, gpre2_ref, wout_ref, wr_hi_ref, wr_lo_ref,
                  x1_ref, hf3_ref, lg_ref, c_ref, n_ref, m_ref, conv_ref, s_ref,
                  z_ref, o_ref, xcat_ref, *, mh, rh):
    @pl.when(pl.program_id(1) == 0)
    def _():
        c_ref[...] = c0_ref[...]
        n_ref[...] = n0_ref[...]
        m_ref[...] = m0_ref[...]
        s_ref[...] = s0_ref[...]
        xcat_ref[8 - (CONV_W - 1):8, :] = conv0_ref[0]

    x = x_ref[...]
    rs, d = x.shape[1], x.shape[2]
    h = _rms(x, gpre1_ref[...]) * (1.0 + sc1_ref[...]) + sh1_ref[...]
    z_ref[...] = _mm(h.reshape(rs, d), win_ref[...])

    def chunk(c, carry):
        r0 = pl.multiple_of(c * CHUNK, CHUNK)

        def put(lo, hi, v):
            o_ref[pl.ds(r0, CHUNK), lo:hi] = v

        _chunk_step(lambda lo, hi: z_ref[pl.ds(r0, CHUNK), lo:hi], put,
                    cos_ref[pl.ds(r0, CHUNK), :], sin_ref[pl.ds(r0, CHUNK), :],
                    c_ref, n_ref, m_ref, conv_ref, s_ref, xcat_ref,
                    convw_ref, convb_ref, bg_ref, mgn_ref, rgn_ref, mh, rh)
        return carry

    lax.fori_loop(0, rs // CHUNK, chunk, 0)
    _mixout_body(o_ref[...].reshape(1, rs, o_ref.shape[1]), x, gt1_ref, sc2_ref, sh2_ref, gpost_ref, gpre2_ref,
                 wout_ref, wr_hi_ref, wr_lo_ref, x1_ref, hf3_ref, lg_ref)


def _mixer(x, mods, cos2, sin2, c0, n0, m0, conv0, s0, gpre1, win, convw, convb, bg, mgn, rgn,
           gpost, gpre2, wout, wr_hi, wr_lo):
    sh1, sc1, gt1, sh2, sc2, gt2 = mods
    b, l, d = x.shape
    assert d == SUB * LANES and l % MIX_ROWS == 0
    mh, rh = c0.shape[1], s0.shape[1]
    mw, rw = mh * DH, rh * DH
    nz, ne = win.shape[1], wr_hi.shape[1]
    rs = MIX_ROWS
    nj = l // rs
    bmap4 = lambda i, j: (i, 0, 0, 0)
    bmap3 = lambda i, j: (i, 0, 0)
    tmap = lambda i, j: (i, j, 0)
    fmap = lambda i, j: (i * nj + j, 0)
    once = lambda a: pl.BlockSpec(a.shape, lambda i, j: (0, 0), pipeline_mode=pl.Buffered(1))
    mod_spec = pl.BlockSpec((1, 1, d), bmap3)
    rope_spec = pl.BlockSpec((rs, DH), lambda i, j: (j, 0))
    state_specs = [pl.BlockSpec((1, mh, DH, DH), bmap4), pl.BlockSpec((1, mh, DH), bmap3),
                   pl.BlockSpec((1, 1, GATE_LANES), bmap3), pl.BlockSpec((1, CONV_W - 1, 2 * mw), bmap3),
                   pl.BlockSpec((1, rh, DH, DH), bmap4)]
    state_shapes = [jax.ShapeDtypeStruct(c0.shape, F32), jax.ShapeDtypeStruct(n0.shape, F32),
                    jax.ShapeDtypeStruct(m0.shape, F32), jax.ShapeDtypeStruct(conv0.shape, F32),
                    jax.ShapeDtypeStruct(s0.shape, F32)]
    return pl.pallas_call(
        functools.partial(_mixer_kernel, mh=mh, rh=rh),
        grid=(b, nj),
        in_specs=[pl.BlockSpec((1, rs, d), tmap)] + [mod_spec] * 5 + [rope_spec, rope_spec] + state_specs
                 + [once(a) for a in (gpre1, win, convw, convb, bg, mgn, rgn, gpost, gpre2, wout, wr_hi, wr_lo)],
        out_specs=[pl.BlockSpec((1, rs, d), tmap), pl.BlockSpec((rs * SUB, LANES), fmap),
                   pl.BlockSpec((rs, ne), fmap)] + state_specs,
        out_shape=[jax.ShapeDtypeStruct((b, l, d), F32), jax.ShapeDtypeStruct((b * l * SUB, LANES), F32),
                   jax.ShapeDtypeStruct((b * l, ne), F32)] + state_shapes,
        scratch_shapes=[pltpu.VMEM((rs, nz), F32), pltpu.VMEM((rs, mw + rw), F32),
                        pltpu.VMEM((CHUNK + 8, 2 * mw), F32)],
        compiler_params=_cparams(("arbitrary", "arbitrary")),
        name="mixer",
    )(x, sc1, sh1, gt1, sc2, sh2, cos2, sin2, c0, n0, m0, conv0, s0,
      gpre1, win, convw, convb, bg, mgn, rgn, gpost, gpre2, wout, wr_hi, wr_lo)


PACK_SEQS = CHUNK // SUB


def _mixcore_packed_kernel(z_ref, cos_ref, sin_ref, c0_ref, n0_ref, m0_ref, conv0_ref, s0_ref,
                           convw_ref, convb_ref, bg_ref, mgn_ref, rgn_ref,
                           out_ref, c_ref, n_ref, m_ref, conv_ref, s_ref, *, mh, rh):
    g_n, l_n, _ = z_ref.shape
    rr = g_n * l_n
    mw, rw = mh * DH, rh * DH
    o_qk, o_mv, o_mo = 0, 2 * mw, 3 * mw
    o_rq = 4 * mw
    o_rk, o_rv, o_rg, o_gt = o_rq + rw, o_rq + 2 * rw, o_rq + 3 * rw, o_rq + 4 * rw

    def rows(lo, hi):
        return z_ref[:, :, lo:hi].reshape(rr, hi - lo)

    def per_seq(x):
        return jnp.broadcast_to(x[:, None, :], (g_n, l_n, x.shape[-1])).reshape(rr, x.shape[-1])

    def last_of_seq(x):
        x3 = x.reshape(g_n, l_n, x.shape[-1])
        return jnp.broadcast_to(x3[:, l_n - 1:l_n, :], x3.shape).reshape(x.shape)

    def full(col):
        return jnp.broadcast_to(col, (rr, DH))

    pos_c = lax.broadcasted_iota(jnp.int32, (rr, 1), 0) % l_n
    ii = lax.broadcasted_iota(jnp.int32, (rr, rr), 0)
    jj = lax.broadcasted_iota(jnp.int32, (rr, rr), 1)
    mask = ((ii // l_n) == (jj // l_n)) & (jj <= ii)
    eye = jj == ii

    def to_row(col):
        return jnp.sum(jnp.where(eye, col, 0.0), axis=0, keepdims=True)

    x = rows(o_qk, o_qk + 2 * mw)
    e = conv0_ref[...].reshape(rr, 2 * mw)
    qk = convb_ref[...]
    for t in range(CONV_W):
        d = CONV_W - 1 - t
        xd = x if d == 0 else jnp.where(pos_c >= d, pltpu.roll(x, d, axis=0),
                                        pltpu.roll(e, (d - (CONV_W - 1)) % rr, axis=0))
        qk = qk + xd * convw_ref[t:t + 1, :]
    qk = _silu(qk)
    conv_ref[...] = x.reshape(g_n, l_n, 2 * mw)[:, l_n - (CONV_W - 1):, :]

    g = rows(o_gt, o_gt + GATE_LANES) + bg_ref[...]
    bcum = _log_sigmoid(g)
    s_ = 1
    while s_ < l_n:
        bcum = bcum + jnp.where(pos_c >= s_, pltpu.roll(bcum, s_, axis=0), 0.0)
        s_ *= 2
    m_prev = per_seq(m0_ref[:, 0, :])
    a_all = bcum + m_prev
    lane = lax.broadcasted_iota(jnp.int32, (1, GATE_LANES), 1)
    even_seq = (lax.broadcasted_iota(jnp.int32, (rr, 1), 0) // l_n) % 2 == 0
    m_new_rows = m_prev

    def state_dot(qb, st_ref, h):
        parts = []
        for s in range(g_n):
            pb, lo = s // 2, (s % 2) * l_n
            res = jnp.dot(qb[pb * 2 * l_n:(pb + 1) * 2 * l_n], st_ref[s, h].astype(MXU_DTYPE),
                          preferred_element_type=F32)
            parts.append(res[lo:lo + l_n])
        return jnp.concatenate(parts, axis=0)

    def state_update(k, wv, decay_full, st_in, st_out, h):
        k_even = jnp.where(even_seq, k, 0.0).astype(MXU_DTYPE)
        k_odd = jnp.where(even_seq, 0.0, k).astype(MXU_DTYPE)
        wvb = wv.astype(MXU_DTYPE)
        for s in range(g_n):
            pb = s // 2
            kp = (k_even if s % 2 == 0 else k_odd)[pb * 2 * l_n:(pb + 1) * 2 * l_n]
            upd = lax.dot_general(kp, wvb[pb * 2 * l_n:(pb + 1) * 2 * l_n], (((0,), (0,)), ((), ())),
                                  preferred_element_type=F32)
            st_out[s, h] = decay_full[s * l_n:s * l_n + 1, :] * st_in[s, h] + upd

    for h in range(mh):
        q = qk[:, h * DH:(h + 1) * DH]
        k = qk[:, mw + h * DH:mw + (h + 1) * DH] * (DH ** -0.5)
        v = rows(o_mv + h * DH, o_mv + (h + 1) * DH)
        b_col = bcum[:, mh + h:mh + h + 1]
        logi_col = g[:, h:h + 1]
        a_col = a_all[:, mh + h:mh + h + 1]
        dmat = jnp.where(mask, b_col + to_row(logi_col - b_col), -jnp.inf)
        m_t = jnp.maximum(a_col, jnp.max(dmat, axis=1, keepdims=True))
        w_intra = jnp.exp(dmat - m_t)
        w_inter = jnp.exp(a_col - m_t)
        n_rows = per_seq(n0_ref[:, h, :])
        s = _mm_nt(q, k) * w_intra
        num = _mm(s, v) + w_inter * state_dot(q.astype(MXU_DTYPE), c0_ref, h)
        den = jnp.sum(s, axis=1, keepdims=True) + w_inter * jnp.sum(q * n_rows, axis=1, keepdims=True)
        hh = num * (1.0 / jnp.maximum(jnp.abs(den), jnp.exp(-m_t)))
        m_new = last_of_seq(full(m_t))
        b_last = last_of_seq(full(b_col))
        w_old = jnp.exp(b_last + full(m_prev[:, mh + h:mh + h + 1]) - m_new)
        w_k = jnp.exp(b_last - full(b_col) + full(logi_col) - m_new)
        state_update(k, w_k * v, w_old, c0_ref, c_ref, h)
        n_ref[:, h, :] = (w_old.reshape(g_n, l_n, DH)[:, 0, :] * n0_ref[:, h, :]
                          + jnp.sum((w_k * k).reshape(g_n, l_n, DH), axis=1))
        m_new_rows = jnp.where(lane == mh + h, m_new, m_new_rows)
        om = _head_norm(hh, mgn_ref[:, h * DH:(h + 1) * DH]) * jax.nn.sigmoid(rows(o_mo + h * DH, o_mo + (h + 1) * DH))
        out_ref[:, :, h * DH:(h + 1) * DH] = om.reshape(g_n, l_n, DH)
    m_ref[...] = m_new_rows.reshape(g_n, l_n, GATE_LANES)[:, 0:1, :]

    cos2 = cos_ref[...]
    sin2 = sin_ref[...]
    rel = (ii - jj).astype(F32)
    pos_f = pos_c.astype(F32)

    def rot(x):
        return x * cos2 + pltpu.roll(x, DH // 2, axis=1) * sin2

    for h in range(rh):
        log_g = math.log1p(-2.0 ** (-5.0 - h))
        q = rot(rows(o_rq + h * DH, o_rq + (h + 1) * DH))
        k = rot(rows(o_rk + h * DH, o_rk + (h + 1) * DH)) * (DH ** -0.5)
        v = rows(o_rv + h * DH, o_rv + (h + 1) * DH)
        decay = jnp.where(mask, jnp.exp(jnp.maximum(rel, 0.0) * log_g), 0.0)
        y = (_mm(_mm_nt(q, k) * decay, v)
             + jnp.exp((pos_f + 1.0) * log_g) * state_dot(q.astype(MXU_DTYPE), s0_ref, h))
        k_w = jnp.exp((l_n - 1.0 - pos_f) * log_g)
        state_update(k, k_w * v, jnp.full((rr, DH), math.exp(l_n * log_g), F32), s0_ref, s_ref, h)
        orr = _head_norm(y, rgn_ref[:, h * DH:(h + 1) * DH]) * _silu(rows(o_rg + h * DH, o_rg + (h + 1) * DH))
        out_ref[:, :, mw + h * DH:mw + (h + 1) * DH] = orr.reshape(g_n, l_n, DH)


def _mixcore_packed(z, cos2, sin2, c0, n0, m0, conv0, s0, convw, convb, bg, mgn, rgn):
    b, l, nz = z.shape
    assert l == SUB and b % PACK_SEQS == 0
    mh, rh = c0.shape[1], s0.shape[1]
    mw, rw = mh * DH, rh * DH
    g_n = PACK_SEQS
    conv0p = jnp.pad(conv0, ((0, 0), (0, l - (CONV_W - 1)), (0, 0)))
    tile = lambda t: jnp.tile(t[:l], (g_n, 1))
    bmap4 = lambda i: (i, 0, 0, 0)
    bmap3 = lambda i: (i, 0, 0)
    wmap = lambda i: (0, 0)
    c_spec, n_spec = pl.BlockSpec((g_n, mh, DH, DH), bmap4), pl.BlockSpec((g_n, mh, DH), bmap3)
    m_spec, s_spec = pl.BlockSpec((g_n, 1, GATE_LANES), bmap3), pl.BlockSpec((g_n, rh, DH, DH), bmap4)
    return pl.pallas_call(
        functools.partial(_mixcore_packed_kernel, mh=mh, rh=rh),
        grid=(b // g_n,),
        in_specs=[pl.BlockSpec((g_n, l, nz), bmap3),
                  pl.BlockSpec((g_n * l, DH), wmap), pl.BlockSpec((g_n * l, DH), wmap),
                  c_spec, n_spec, m_spec, pl.BlockSpec((g_n, l, 2 * mw), bmap3), s_spec,
                  pl.BlockSpec(convw.shape, wmap), pl.BlockSpec(convb.shape, wmap), pl.BlockSpec(bg.shape, wmap),
                  pl.BlockSpec(mgn.shape, wmap), pl.BlockSpec(rgn.shape, wmap)],
        out_specs=[pl.BlockSpec((g_n, l, mw + rw), bmap3), c_spec, n_spec, m_spec,
                   pl.BlockSpec((g_n, CONV_W - 1, 2 * mw), bmap3), s_spec],
        out_shape=[jax.ShapeDtypeStruct((b, l, mw + rw), F32), jax.ShapeDtypeStruct(c0.shape, F32),
                   jax.ShapeDtypeStruct(n0.shape, F32), jax.ShapeDtypeStruct(m0.shape, F32),
                   jax.ShapeDtypeStruct(conv0.shape, F32), jax.ShapeDtypeStruct(s0.shape, F32)],
        compiler_params=_cparams(("arbitrary",)),
        name="mixcore_packed",
    )(z, tile(cos2), tile(sin2), c0, n0, m0, conv0p, s0, convw, convb, bg, mgn, rgn)


def _split_hi_lo(x):
    hi = x.astype(MXU_DTYPE)
    lo = (x - hi.astype(F32)).astype(MXU_DTYPE)
    return hi, lo


def _to_row_tiles(ref, x):
    n = x.shape[0]
    for j in range(SUB):
        ref[pl.ds(j, n, stride=SUB), :] = x[:, j * LANES:(j + 1) * LANES]


def _from_row_tiles(ref, n):
    return jnp.concatenate([ref[pl.ds(j, n, stride=SUB), :] for j in range(SUB)], axis=1)


def _mixout_kernel(o_ref, x_ref, gt1_ref, sc2_ref, sh2_ref, gpost_ref, gpre_ref, wout_ref, wr_hi_ref, wr_lo_ref,
                   x1_ref, hf3_ref, lg_ref):
    _mixout_body(o_ref[...], x_ref[...], gt1_ref, sc2_ref, sh2_ref, gpost_ref, gpre_ref, wout_ref, wr_hi_ref,
                 wr_lo_ref, x1_ref, hf3_ref, lg_ref)


def _mixout_body(o, x, gt1_ref, sc2_ref, sh2_ref, gpost_ref, gpre_ref, wout_ref, wr_hi_ref, wr_lo_ref,
                 x1_ref, hf3_ref, lg_ref):
    bb, ll, d = x.shape
    ym = _mm(o.reshape(bb * ll, o.shape[-1]), wout_ref[...]).reshape(bb, ll, d)
    x1 = x + gt1_ref[...] * _rms(ym, gpost_ref[...])
    x1_ref[...] = x1
    hf = (_rms(x1, gpre_ref[...]) * (1.0 + sc2_ref[...]) + sh2_ref[...]).reshape(bb * ll, d)
    _to_row_tiles(hf3_ref, hf)
    hi, lo = _split_hi_lo(hf)
    lg_ref[...] = _mm(hi, wr_hi_ref[...]) + (_mm(lo, wr_hi_ref[...]) + _mm(hi, wr_lo_ref[...]))


def _mixout(o, x, gt1, sc2, sh2, gpost, gpre, wout, wr_hi, wr_lo):
    b, l, d = x.shape
    assert d == SUB * LANES
    w = o.shape[-1]
    ne = wr_hi.shape[1]
    ll = min(l, TOK_TILE)
    bb = TOK_TILE // ll
    tmap = lambda i, j: (i, j, 0)
    fmap = lambda i, j: (i * (l // ll) + j, 0)
    mod_spec = pl.BlockSpec((bb, 1, d), lambda i, j: (i, 0, 0))
    wmap = lambda i, j: (0, 0)
    return pl.pallas_call(
        _mixout_kernel,
        grid=(b // bb, l // ll),
        in_specs=[pl.BlockSpec((bb, ll, w), tmap), pl.BlockSpec((bb, ll, d), tmap), mod_spec, mod_spec, mod_spec,
                  pl.BlockSpec((1, d), wmap), pl.BlockSpec((1, d), wmap), pl.BlockSpec((w, d), wmap),
                  pl.BlockSpec((d, ne), wmap), pl.BlockSpec((d, ne), wmap)],
        out_specs=[pl.BlockSpec((bb, ll, d), tmap), pl.BlockSpec((bb * ll * SUB, LANES), fmap),
                   pl.BlockSpec((bb * ll, ne), fmap)],
        out_shape=[jax.ShapeDtypeStruct((b, l, d), F32), jax.ShapeDtypeStruct((b * l * SUB, LANES), F32),
                   jax.ShapeDtypeStruct((b * l, ne), F32)],
        compiler_params=_cparams(("arbitrary", "arbitrary")),
        name="mixout",
    )(o, x, gt1, sc2, sh2, gpost, gpre, wout, wr_hi, wr_lo)


def _router_kernel(lg_ref, br_ref, lidx_ref, wts_ref, cnt_ref, seg_ref, *, ne):
    tm = lg_ref.shape[0]
    gsz = ne // N_GROUPS
    s = jax.nn.sigmoid(lg_ref[...].T[:ne, :])
    sb = s + br_ref[...]
    sb3 = sb.reshape(N_GROUPS, gsz, tm)
    e3 = lax.broadcasted_iota(jnp.int32, sb3.shape, 1)
    m1 = jnp.max(sb3, axis=1, keepdims=True)
    first = jnp.min(jnp.where(sb3 == m1, e3, gsz), axis=1, keepdims=True)
    m2 = jnp.max(jnp.where(e3 == first, -jnp.inf, sb3), axis=1, keepdims=True)
    gs = (m1 + m2).reshape(N_GROUPS, tm)
    gi = lax.broadcasted_iota(jnp.int32, gs.shape, 0)
    grank = jnp.zeros(gs.shape, F32)
    for g in range(N_GROUPS):
        r = gs[g:g + 1, :]
        grank = grank + jnp.where(r > gs, 1.0, jnp.where(r == gs, jnp.where(g < gi, 1.0, 0.0), 0.0))
    gsel = jnp.where(grank < TOPK_GROUPS, 1.0, 0.0)
    emask = jnp.broadcast_to(gsel.reshape(N_GROUPS, 1, tm), sb3.shape).reshape(ne, tm) > 0.0
    masked = jnp.where(emask, sb, -jnp.inf)
    ei = lax.broadcasted_iota(jnp.int32, masked.shape, 0)
    cur = masked
    sel_f = jnp.zeros(masked.shape, F32)
    for _ in range(TOP_K):
        best = jnp.max(cur, axis=0, keepdims=True)
        first_e = jnp.min(jnp.where(cur == best, ei, ne), axis=0, keepdims=True)
        pick = ei == first_e
        sel_f = jnp.where(pick, 1.0, sel_f)
        cur = jnp.where(pick, -jnp.inf, cur)
    sel = sel_f > 0.0
    w = jnp.where(sel, s, 0.0)
    w = w / jnp.sum(w, axis=0, keepdims=True) * ROUTED_SCALE

    self = jnp.where(sel, 1.0, 0.0)
    er = lax.broadcasted_iota(jnp.int32, (ne, ne), 0)
    ec = lax.broadcasted_iota(jnp.int32, (ne, ne), 1)
    slot = _mm(jnp.where(ec < er, 1.0, 0.0), self)
    tr = lax.broadcasted_iota(jnp.int32, (tm, tm), 0)
    tc = lax.broadcasted_iota(jnp.int32, (tm, tm), 1)
    pos = _mm(self, jnp.where(tr < tc, 1.0, 0.0))
    seg = jnp.sum(slot, axis=1, keepdims=True)
    lrow = seg + pos
    lidx_rows, w_rows = [], []
    for k in range(TOP_K):
        pick = sel & (slot == float(k))
        lidx_rows.append(jnp.sum(jnp.where(pick, lrow, 0.0), axis=0, keepdims=True))
        w_rows.append(jnp.sum(jnp.where(pick, w, 0.0), axis=0, keepdims=True))
    lidx_ref[...] = jnp.concatenate(lidx_rows, axis=0).astype(jnp.int32)
    wts_ref[...] = jnp.concatenate(w_rows, axis=0)
    ones = jnp.ones((SUB, tm), F32)
    padz = jnp.zeros((LANES - ne, tm), F32)
    cnt_ref[...] = _mm_nt(ones, jnp.concatenate([self, padz], axis=0)).astype(jnp.int32)
    seg_ref[...] = _mm_nt(ones, jnp.concatenate([slot, padz], axis=0)).astype(jnp.int32)


def _router(logits, b_router, ne):
    t, lanes = logits.shape
    tm = ROUTE_TILE
    nt = t // tm
    kspec = pl.BlockSpec((TOP_K, tm), lambda i: (0, i))
    cspec = pl.BlockSpec((SUB, LANES), lambda i: (i, 0))
    return pl.pallas_call(
        functools.partial(_router_kernel, ne=ne),
        grid=(nt,),
        in_specs=[pl.BlockSpec((tm, lanes), lambda i: (i, 0)), pl.BlockSpec((ne, 1), lambda i: (0, 0))],
        out_specs=[kspec, kspec, cspec, cspec],
        out_shape=[jax.ShapeDtypeStruct((TOP_K, t), jnp.int32), jax.ShapeDtypeStruct((TOP_K, t), F32),
                   jax.ShapeDtypeStruct((nt * SUB, LANES), jnp.int32),
                   jax.ShapeDtypeStruct((nt * SUB, LANES), jnp.int32)],
        compiler_params=_cparams(("arbitrary",)),
        name="router",
    )(logits, b_router.reshape(ne, 1))


SEG_BITS = ROUTE_TILE.bit_length()
PAD_BITS = (MOE_TILE - 1).bit_length()


RARE_BIT = 7


def _for_each_piece(n, fn, bits):
    def pieces(lo, hi):
        for bit in range(hi - 1, lo - 1, -1):
            @pl.when(((n >> bit) & 1) == 1)
            def _():
                fn((n >> (bit + 1)) << (bit + 1), 1 << bit)

    if bits > RARE_BIT:
        @pl.when(n >= (1 << RARE_BIT))
        def _():
            pieces(RARE_BIT, bits)
    pieces(0, min(bits, RARE_BIT))


SORT_UNROLL = 4


def _zero_rows(ps_ref, pn_ref, tail_ref, xs_ref, z_ref, sem, ne, act):
    def pad_body(e, carry):
        _for_each_piece(pn_ref[e], lambda off, size: act(
            pltpu.make_async_copy(z_ref.at[pl.ds(0, size)], xs_ref.at[pl.ds(ps_ref[e] + off, size)], sem)),
            PAD_BITS)
        return carry
    lax.fori_loop(0, ne, pad_body, 0)

    def tail_body(b, carry):
        act(pltpu.make_async_copy(z_ref, xs_ref.at[pl.ds(tail_ref[0] + b * MOE_TILE, MOE_TILE)], sem))
        return carry
    lax.fori_loop(0, tail_ref[1], tail_body, 0)


def _dispatch_kernel(cnt_ref, seg_ref, gb_ref, ps_ref, pn_ref, tail_ref, lidx_p_ref, lidx_s_ref, hf3_p_ref, hf3_s_ref,
                     xs_ref, buf0, buf1, z_ref, sems, *, ntp, ne):
    i = pl.program_id(0)
    n = pl.num_programs(0)
    tt = lidx_p_ref.shape[0] // TOP_K

    def whole(buf, sem):
        return pltpu.make_async_copy(buf, xs_ref.at[pl.ds(0, tt * TOP_K)], sem)

    @pl.when(i == 0)
    def _():
        z_ref[...] = jnp.zeros(z_ref.shape, F32)
        _zero_rows(ps_ref, pn_ref, tail_ref, xs_ref, z_ref, sems.at[2], ne, lambda c: c.start())

    def run(buf, sem):
        @pl.when(i >= 2)
        def _():
            whole(buf, sem).wait()

        def sort_from(hf3_ref, lidx_ref):
            def sort_body(tb, carry):
                for u in range(SORT_UNROLL):
                    t = tb * SORT_UNROLL + u
                    v = hf3_ref[pl.ds(pl.multiple_of(t * SUB, SUB), SUB), :]
                    for k in range(TOP_K):
                        buf[lidx_ref[t * TOP_K + k]] = v
                return carry
            lax.fori_loop(0, tt // SORT_UNROLL, sort_body, 0)

        @pl.when(i < ntp)
        def _():
            sort_from(hf3_p_ref, lidx_p_ref)

        @pl.when(i >= ntp)
        def _():
            sort_from(hf3_s_ref, lidx_s_ref)

        def seg_body(e, carry):
            c, s, g = cnt_ref[i * ne + e], seg_ref[i * ne + e], gb_ref[i * ne + e]
            _for_each_piece(c, lambda off, size: pltpu.make_async_copy(
                buf.at[pl.ds(s + off, size)], xs_ref.at[pl.ds(g + off, size)], sem).start(), SEG_BITS)
            return carry
        lax.fori_loop(0, ne, seg_body, 0)

    @pl.when(i % 2 == 0)
    def _():
        run(buf0, sems.at[0])

    @pl.when(i % 2 == 1)
    def _():
        run(buf1, sems.at[1])

    @pl.when(i == 0)
    def _():
        _zero_rows(ps_ref, pn_ref, tail_ref, xs_ref, z_ref, sems.at[2], ne, lambda c: c.wait())

    @pl.when(i == n - 1)
    def _():
        @pl.when(i % 2 == 0)
        def _():
            whole(buf0, sems.at[0]).wait()

            @pl.when(i >= 1)
            def _():
                whole(buf1, sems.at[1]).wait()

        @pl.when(i % 2 == 1)
        def _():
            whole(buf1, sems.at[1]).wait()
            whole(buf0, sems.at[0]).wait()


def _dispatch(cnt, seg, gbase, pad_start, pad_n, tail, lidx_p, lidx_s, hf3_p, hf3_s, rows, ne):
    tt = ROUTE_TILE
    ntp, nts =lidx_p.shape[0] // (tt * TOP_K), lidx_s.shape[0] // (tt * TOP_K)
    pmap = lambda i, *_: jnp.minimum(i, ntp - 1)
    smap = lambda i, *_: jnp.maximum(i - ntp, 0)
    return pl.pallas_call(
        functools.partial(_dispatch_kernel, ntp=ntp, ne=ne),
        grid_spec=pltpu.PrefetchScalarGridSpec(
            num_scalar_prefetch=6, grid=(ntp + nts,),
            in_specs=[pl.BlockSpec((tt * TOP_K,), lambda i, *_: (pmap(i),), memory_space=pltpu.SMEM),
                      pl.BlockSpec((tt * TOP_K,), lambda i, *_: (smap(i),), memory_space=pltpu.SMEM),
                      pl.BlockSpec((tt * SUB, LANES), lambda i, *_: (pmap(i), 0)),
                      pl.BlockSpec((tt * SUB, LANES), lambda i, *_: (smap(i), 0))],
            out_specs=pl.BlockSpec(memory_space=pl.ANY),
            scratch_shapes=[pltpu.VMEM((tt * TOP_K, SUB, LANES), F32), pltpu.VMEM((tt * TOP_K, SUB, LANES), F32),
                            pltpu.VMEM((MOE_TILE, SUB, LANES), F32), pltpu.SemaphoreType.DMA((3,))]),
        out_shape=jax.ShapeDtypeStruct((rows, SUB, LANES), F32),
        compiler_params=_cparams(("arbitrary",)),
        name="dispatch",
    )(cnt, seg, gbase, pad_start, pad_n, tail, lidx_p, lidx_s, hf3_p, hf3_s)


def _ffn_kernel(be_ref, nbu_ref, xs_ref, wg_ref, wu_ref, wd_ref, ys_ref, wg_s, wu_s, wd_s):
    b = pl.program_id(0)
    tm = xs_ref.shape[0] // SUB

    @pl.when(b < nbu_ref[0])
    def _():
        @pl.when((b == 0) | (be_ref[b] != be_ref[jnp.maximum(b - 1, 0)]))
        def _():
            wg_s[...] = wg_ref[0].astype(MXU_DTYPE)
            wu_s[...] = wu_ref[0].astype(MXU_DTYPE)
            wd_s[...] = wd_ref[0].astype(MXU_DTYPE)

        x = _from_row_tiles(xs_ref, tm)
        hb = _silu(_mm(x, wg_s[...])) * _mm(x, wu_s[...])
        _to_row_tiles(ys_ref, _mm(hb, wd_s[...]))

    @pl.when(b >= nbu_ref[0])
    def _():
        ys_ref[...] = jnp.zeros(ys_ref.shape, F32)


def _ffn(block_e, nb_used, xs2, wg, wu, wd):
    rows = xs2.shape[0] // SUB
    d, f = wg.shape[1], wg.shape[2]
    tm = MOE_TILE
    nb = rows // tm
    xmap = lambda b, be, nbu: (jnp.minimum(b, nbu[0] - 1), 0)
    wmap = lambda b, be, nbu: (be[b], 0, 0)
    return pl.pallas_call(
        _ffn_kernel,
        grid_spec=pltpu.PrefetchScalarGridSpec(
            num_scalar_prefetch=2,
            grid=(nb,),
            in_specs=[pl.BlockSpec((tm * SUB, LANES), xmap), pl.BlockSpec((1, d, f), wmap),
                      pl.BlockSpec((1, d, f), wmap), pl.BlockSpec((1, f, d), wmap)],
            out_specs=pl.BlockSpec((tm * SUB, LANES), lambda b, be, nbu: (b, 0)),
            scratch_shapes=[pltpu.VMEM((d, f), MXU_DTYPE), pltpu.VMEM((d, f), MXU_DTYPE),
                            pltpu.VMEM((f, d), MXU_DTYPE)]),
        out_shape=jax.ShapeDtypeStruct((rows * SUB, LANES), F32),
        compiler_params=_cparams(("arbitrary",)),
        name="ffn",
    )(block_e, nb_used, xs2, wg, wu, wd)


def _combine_kernel(cnt_ref, seg_ref, gb_ref, lidx_ref, wts_ref, hf3_ref, x1_ref, gt2_ref, gpost_ref,
                    wsg_ref, wsu_ref, wsd_ref, ys_ref, y_ref, buf0, buf1, y3_ref, sems, *, tile0, ne):
    bb, ll, d = x1_ref.shape
    tt = bb * ll
    i = pl.program_id(0)
    n = pl.num_programs(0)

    def fetch(j, buf, sem):
        def seg_body(e, carry):
            c, s, g = cnt_ref[j * ne + e], seg_ref[j * ne + e], gb_ref[j * ne + e]
            _for_each_piece(c, lambda off, size: pltpu.make_async_copy(
                ys_ref.at[pl.ds(g + off, size)], buf.at[pl.ds(s + off, size)], sem).start(), SEG_BITS)
            return carry
        lax.fori_loop(0, ne, seg_body, 0)

    def run(buf, sem, nbuf, nsem):
        @pl.when(i == 0)
        def _():
            fetch(tile0, buf, sem)

        @pl.when(i + 1 < n)
        def _():
            fetch(tile0 + i + 1, nbuf, nsem)

        pltpu.make_async_copy(ys_ref.at[pl.ds(0, tt * TOP_K)], buf, sem).wait()

        def tok_body(tb, carry):
            for u in range(SORT_UNROLL):
                t = tb * SORT_UNROLL + u
                acc = wts_ref[t * TOP_K] * buf[lidx_ref[t * TOP_K]]
                for k in range(1, TOP_K):
                    acc = acc + wts_ref[t * TOP_K + k] * buf[lidx_ref[t * TOP_K + k]]
                y3_ref[pl.ds(pl.multiple_of(t * SUB, SUB), SUB), :] = acc
            return carry
        lax.fori_loop(0, tt // SORT_UNROLL, tok_body, 0)

    @pl.when(i % 2 == 0)
    def _():
        run(buf0, sems.at[0], buf1, sems.at[1])

    @pl.when(i % 2 == 1)
    def _():
        run(buf1, sems.at[1], buf0, sems.at[0])

    hf = _from_row_tiles(hf3_ref, tt)
    shared = _mm(_silu(_mm(hf, wsg_ref[...])) * _mm(hf, wsu_ref[...]), wsd_ref[...])
    yf = (_from_row_tiles(y3_ref, tt) + shared).reshape(bb, ll, d)
    y_ref[...] = x1_ref[...] + gt2_ref[...] * _rms(yf, gpost_ref[...])


def _combine(cnt, seg, gbase, lidx, wts, hf3, x1, gt2, gpost, wsg, wsu, wsd, ys, tile0, ne):
    b, l, d = x1.shape
    tt = ROUTE_TILE
    ll = min(l, tt)
    bb = tt // ll
    nl = l // ll
    nt = (b // bb) * nl
    fs = wsg.shape[1]
    if nl == 1:
        tmap = lambda i, *_: (i, 0, 0)
        bmap = lambda i, *_: (i, 0, 0)
    else:
        tmap = lambda i, *_: (i // nl, i % nl, 0)
        bmap = lambda i, *_: (i // nl, 0, 0)
    wmap = lambda i, *_: (0, 0)
    kspec = lambda: pl.BlockSpec((tt * TOP_K,), lambda i, *_: (i,), memory_space=pltpu.SMEM)
    return pl.pallas_call(
        functools.partial(_combine_kernel, tile0=tile0, ne=ne),
        grid_spec=pltpu.PrefetchScalarGridSpec(
            num_scalar_prefetch=3, grid=(nt,),
            in_specs=[kspec(), kspec(),
                      pl.BlockSpec((tt * SUB, LANES), lambda i, *_: (i, 0)),
                      pl.BlockSpec((bb, ll, d), tmap), pl.BlockSpec((bb, 1, d), bmap),
                      pl.BlockSpec((1, d), wmap), pl.BlockSpec((d, fs), wmap), pl.BlockSpec((d, fs), wmap),
                      pl.BlockSpec((fs, d), wmap),
                      pl.BlockSpec(memory_space=pl.ANY)],
            out_specs=pl.BlockSpec((bb, ll, d), tmap),
            scratch_shapes=[pltpu.VMEM((tt * TOP_K, SUB, LANES), F32), pltpu.VMEM((tt * TOP_K, SUB, LANES), F32),
                            pltpu.VMEM((tt * SUB, LANES), F32), pltpu.SemaphoreType.DMA((2,))]),
        out_shape=jax.ShapeDtypeStruct((b, l, d), F32),
        compiler_params=_cparams(("arbitrary",), VMEM_LIMIT_COMBINE),
        name="combine",
    )(cnt, seg, gbase, lidx, wts, hf3, x1, gt2, gpost, wsg, wsu, wsd, ys)


def _rope_tables(pos, rows):
    half = DH // 2
    freqs = ROPE_BASE ** (-jnp.arange(half, dtype=F32) / half)
    ang = pos.astype(F32)[:, None] * freqs[None, :]
    cos, sin = jnp.cos(ang), jnp.sin(ang)
    cos2 = jnp.concatenate([cos, cos], axis=1)
    sin2 = jnp.concatenate([-sin, sin], axis=1)
    padr = rows - pos.shape[0]
    return jnp.pad(cos2, ((0, padr), (0, 0))), jnp.pad(sin2, ((0, padr), (0, 0)))


def _global_plan(cnt_tiles, n_assign, ne):
    counts = jnp.sum(cnt_tiles, axis=0)
    padded = (counts + MOE_TILE - 1) // MOE_TILE * MOE_TILE
    pend = jnp.cumsum(padded)
    base = pend - padded
    gbase = base[None, :] + jnp.cumsum(cnt_tiles, axis=0) - cnt_tiles
    nb = n_assign // MOE_TILE + ne
    starts = jnp.arange(nb, dtype=jnp.int32) * MOE_TILE
    block_e = jnp.minimum(jnp.sum((pend[None, :] <= starts[:, None]).astype(jnp.int32), axis=1), ne - 1)
    nb_used = pend[-1:] // MOE_TILE
    tail = jnp.concatenate([pend[-1:], nb - nb_used])
    i32 = lambda v: v.astype(jnp.int32)
    return i32(gbase.reshape(-1)), i32(block_e), i32(nb_used), i32(base + counts), i32(padded - counts), i32(tail), nb


def kernel(x_prompt, x_sample, c_prompt, c_sample, state_mlstm_C, state_mlstm_n, state_mlstm_m, state_mlstm_conv, state_ret_S, w_ada, b_ada, g_mix_pre, g_mix_post, g_ffn_pre, g_ffn_post, w_in, b_gates, conv_w, conv_b, m_gn, r_gn, w_out, w_router, b_router, w_exp_gate, w_exp_up, w_exp_down, w_sh_gate, w_sh_up, w_sh_down):
    depth = w_ada.shape[0]
    bp, lp, d = x_prompt.shape
    bs, ls, _ = x_sample.shape
    mh, rh = state_mlstm_C.shape[2], state_ret_S.shape[2]
    mw, rw = mh * DH, rh * DH
    ne = w_router.shape[2]
    tp, ts = bp * lp, bs * ls
    ntp = tp // ROUTE_TILE

    cos_p, sin_p = _rope_tables(jnp.arange(lp, dtype=jnp.int32), lp)
    cos_s, sin_s = _rope_tables(PAST_LEN + jnp.arange(ls, dtype=jnp.int32), CHUNK if ls % CHUNK else ls)

    hp, hs = x_prompt, x_sample
    new_p, new_s = [], []
    for l in range(depth):
        wi = w_in[l].astype(MXU_DTYPE)
        g0 = 4 * mw
        w_in_r = jnp.concatenate([wi[:, :g0], wi[:, g0 + 2 * mh:], wi[:, g0:g0 + 2 * mh],
                                  jnp.zeros((d, GATE_LANES - 2 * mh), MXU_DTYPE)], axis=1)
        bg = jnp.pad(b_gates[l], (0, GATE_LANES - 2 * mh)).reshape(1, GATE_LANES)
        wout = w_out[l].astype(MXU_DTYPE)
        wr = jnp.pad(w_router[l], ((0, 0), (0, GATE_LANES - ne)))
        wr_hi, wr_lo = _split_hi_lo(wr)
        wsg, wsu, wsd = (w_sh_gate[l].astype(MXU_DTYPE), w_sh_up[l].astype(MXU_DTYPE), w_sh_down[l].astype(MXU_DTYPE))
        row = lambda v: v.reshape(1, -1)

        mod = _ada(jnp.concatenate([c_prompt, c_sample], axis=0), w_ada[l], b_ada[l])
        mods_p = [m[:bp].reshape(bp, 1, d) for m in jnp.split(mod, 6, axis=1)]
        mods_s = [m[bp:].reshape(bs, 1, d) for m in jnp.split(mod, 6, axis=1)]

        def lane_m(m):
            return jnp.pad(m, ((0, 0), (mh, GATE_LANES - 2 * mh)))[:, None, :]

        init = (jnp.zeros((bp, mh, DH, DH), F32), jnp.zeros((bp, mh, DH), F32), lane_m(jnp.zeros((bp, mh), F32)),
                jnp.zeros((bp, CONV_W - 1, 2 * mw), F32), jnp.zeros((bp, rh, DH, DH), F32))
        past = (state_mlstm_C[l], state_mlstm_n[l], lane_m(state_mlstm_m[l]), state_mlstm_conv[l], state_ret_S[l])

        def mix(x, mods, state, cos2, sin2):
            sh1, sc1, gt1, sh2, sc2, gt2 = mods
            b, ll, _ = x.shape
            if ll % MIX_ROWS == 0:
                x1, hf3, lg, c_n, n_n, m_n, conv_n, s_n = _mixer(
                    x, mods, cos2, sin2, *state, row(g_mix_pre[l]), w_in_r, conv_w[l], row(conv_b[l]), bg,
                    row(m_gn[l]), row(r_gn[l]), row(g_mix_post[l]), row(g_ffn_pre[l]), wout, wr_hi, wr_lo)
            else:
                z = _inproj(x, sc1, sh1, row(g_mix_pre[l]), w_in_r).reshape(b, ll, -1)
                o_mix, c_n, n_n, m_n, conv_n, s_n = _mixcore_packed(
                    z, cos2, sin2, *state, conv_w[l], row(conv_b[l]), bg, row(m_gn[l]), row(r_gn[l]))
                x1, hf3, lg = _mixout(o_mix, x, gt1, sc2, sh2, row(g_mix_post[l]), row(g_ffn_pre[l]), wout, wr_hi,
                                      wr_lo)
            lidx, wts, cnt, seg = _router(lg, b_router[l], ne)
            flat = lambda v: v.T.reshape(-1)
            return (x1, hf3, flat(lidx), flat(wts), cnt[::SUB, :ne], seg[::SUB, :ne],
                    (c_n, n_n, m_n[:, 0, mh:2 * mh], conv_n, s_n))

        x1_p, hf3_p, lidx_p, wts_p, cnt_p, seg_p, st_p = mix(hp, mods_p, init, cos_p, sin_p)
        x1_s, hf3_s, lidx_s, wts_s, cnt_s, seg_s, st_s = mix(hs, mods_s, past, cos_s, sin_s)

        cnt_t = jnp.concatenate([cnt_p, cnt_s], axis=0)
        cnt = cnt_t.reshape(-1)
        seg = jnp.concatenate([seg_p, seg_s], axis=0).reshape(-1)
        gbase, block_e, nb_used, pad_start, pad_n, tail, nb = _global_plan(cnt_t, (tp + ts) * TOP_K, ne)
        xs = _dispatch(cnt, seg, gbase, pad_start, pad_n, tail, lidx_p, lidx_s, hf3_p, hf3_s, nb * MOE_TILE, ne)
        ys = _ffn(block_e, nb_used, xs.reshape(nb * MOE_TILE * SUB, LANES), w_exp_gate[l], w_exp_up[l], w_exp_down[l])
        ys = ys.reshape(nb * MOE_TILE, SUB, LANES)
        hp = _combine(cnt, seg, gbase, lidx_p, wts_p, hf3_p, x1_p, mods_p[5], row(g_ffn_post[l]), wsg, wsu, wsd, ys, 0, ne)
        hs = _combine(cnt, seg, gbase, lidx_s, wts_s, hf3_s, x1_s, mods_s[5], row(g_ffn_post[l]), wsg, wsu, wsd, ys, ntp, ne)
        new_p.append(st_p)
        new_s.append(st_s)

    p_state = [jnp.stack([st[i] for st in new_p]) for i in range(5)]
    s_state = [jnp.stack([st[i] for st in new_s]) for i in range(5)]
    return (hp, hs, *p_state, *s_state)
```

```python
import functools
import math

import jax
import jax.numpy as jnp
from jax import lax
from jax.experimental import pallas as pl
from jax.experimental.pallas import tpu as pltpu

F32 = jnp.float32
MXU_DTYPE = jnp.bfloat16

NORM_EPS = 1e-6
ROPE_BASE = 10000.0
PAST_LEN = 16384
CHUNK = 128
CONV_W = 4
DH = 128
TOP_K = 8
N_GROUPS = 8
TOPK_GROUPS = 4
ROUTED_SCALE = 2.5
GATE_LANES = 128

V7X_VMEM_BYTES = 64 * 1024 * 1024
VMEM_LIMIT = 48 * 1024 * 1024
VMEM_LIMIT_COMBINE = 56 * 1024 * 1024

SUB, LANES = 8, 128

TOK_TILE = 256
ROUTE_TILE = 512
MOE_TILE = 512


def _cparams(sem, vmem_limit=VMEM_LIMIT):
    return pltpu.CompilerParams(dimension_semantics=sem, vmem_limit_bytes=vmem_limit)


def _silu(x):
    return x * jax.nn.sigmoid(x)


def _rms(x, g):
    return x * lax.rsqrt(jnp.mean(x * x, axis=-1, keepdims=True) + NORM_EPS) * g


def _mm(a, b):
    return jnp.dot(a.astype(MXU_DTYPE), b.astype(MXU_DTYPE), preferred_element_type=F32)


def _mm_nt(a, b):
    return lax.dot_general(a.astype(MXU_DTYPE), b.astype(MXU_DTYPE), (((1,), (1,)), ((), ())),
                           preferred_element_type=F32)


def _mm_tn(a, b):
    return lax.dot_general(a.astype(MXU_DTYPE), b.astype(MXU_DTYPE), (((0,), (0,)), ((), ())),
                           preferred_element_type=F32)


def _ada_kernel(c_ref, w_ref, b_ref, o_ref):
    o_ref[...] = _mm(_silu(c_ref[...]), w_ref[...]) + b_ref[...]


def _ada(c, w_ada, b_ada):
    n, d = c.shape
    dout = w_ada.shape[1]
    tn = 1024
    return pl.pallas_call(
        _ada_kernel,
        grid=(dout // tn,),
        in_specs=[pl.BlockSpec((n, d), lambda j: (0, 0)),
                  pl.BlockSpec((d, tn), lambda j: (0, j)),
                  pl.BlockSpec((1, tn), lambda j: (0, j))],
        out_specs=pl.BlockSpec((n, tn), lambda j: (0, j)),
        out_shape=jax.ShapeDtypeStruct((n, dout), F32),
        compiler_params=_cparams(("arbitrary",)),
        name="ada",
    )(c, w_ada, b_ada.reshape(1, dout))


def _inproj_kernel(x_ref, sc_ref, sh_ref, g_ref, w_ref, z_ref):
    x = x_ref[...]
    h = _rms(x, g_ref[...]) * (1.0 + sc_ref[...]) + sh_ref[...]
    bb, ll, d = x.shape
    z_ref[...] = _mm(h.reshape(bb * ll, d), w_ref[...])


def _inproj(x, sc, sh, g, w):
    b, l, d = x.shape
    n = w.shape[1]
    ll = min(l, TOK_TILE)
    bb = TOK_TILE // ll
    mod_spec = pl.BlockSpec((bb, 1, d), lambda i, j: (i, 0, 0))
    return pl.pallas_call(
        _inproj_kernel,
        grid=(b // bb, l // ll),
        in_specs=[pl.BlockSpec((bb, ll, d), lambda i, j: (i, j, 0)), mod_spec, mod_spec,
                  pl.BlockSpec((1, d), lambda i, j: (0, 0)),
                  pl.BlockSpec((d, n), lambda i, j: (0, 0))],
        out_specs=pl.BlockSpec((bb * ll, n), lambda i, j: (i * (l // ll) + j, 0)),
        out_shape=jax.ShapeDtypeStruct((b * l, n), F32),
        compiler_params=_cparams(("arbitrary", "arbitrary")),
        name="inproj",
    )(x, sc, sh, g, w)


def _cumsum_rows(x):
    n = x.shape[0]
    row = lax.broadcasted_iota(jnp.int32, x.shape, 0)
    s = 1
    while s < n:
        x = x + jnp.where(row >= s, pltpu.roll(x, s, axis=0), 0.0)
        s *= 2
    return x


def _head_norm(y, g):
    mu = jnp.mean(y, axis=-1, keepdims=True)
    yc = y - mu
    var = jnp.mean(yc * yc, axis=-1, keepdims=True)
    return yc * lax.rsqrt(var + NORM_EPS) * g


def _log_sigmoid(x):
    return jnp.minimum(x, 0.0) - jnp.log1p(jnp.exp(-jnp.abs(x)))


def _chunk_step(rows, put, cos2, sin2, c_ref, n_ref, m_ref, conv_ref, s_ref, xcat_ref,
                convw_ref, convb_ref, bg_ref, mgn_ref, rgn_ref, mh, rh, after_head=None):
    cl = CHUNK
    mw = mh * DH
    rw = rh * DH
    o_qk, o_mv, o_mo = 0, 2 * mw, 3 * mw
    o_rq = 4 * mw
    o_rk, o_rv, o_rg, o_gt = o_rq + rw, o_rq + 2 * rw, o_rq + 3 * rw, o_rq + 4 * rw

    xcat_ref[8:8 + cl, :] = rows(o_qk, o_qk + 2 * mw)
    qk = convb_ref[...]
    for t in range(CONV_W):
        qk = qk + xcat_ref[8 - (CONV_W - 1) + t:8 - (CONV_W - 1) + t + cl, :] * convw_ref[t:t + 1, :]
    qk = _silu(qk)
    new_conv = xcat_ref[8 + cl - (CONV_W - 1):8 + cl, :]
    conv_ref[0] = new_conv
    xcat_ref[8 - (CONV_W - 1):8, :] = new_conv

    row_c = lax.broadcasted_iota(jnp.int32, (cl, 1), 0)
    ii = lax.broadcasted_iota(jnp.int32, (cl, cl), 0)
    jj = lax.broadcasted_iota(jnp.int32, (cl, cl), 1)
    causal = jj <= ii
    eye = jj == ii

    def to_row(col):
        return jnp.sum(jnp.where(eye, col, 0.0), axis=0, keepdims=True)

    g = rows(o_gt, o_gt + GATE_LANES) + bg_ref[...]
    bcum = _cumsum_rows(_log_sigmoid(g))
    m_prev = m_ref[0]
    a_all = bcum + m_prev
    lane = lax.broadcasted_iota(jnp.int32, (1, GATE_LANES), 1)
    m_new_row = m_prev

    for h in range(mh):
        q = qk[:, h * DH:(h + 1) * DH]
        k = qk[:, mw + h * DH:mw + (h + 1) * DH] * (DH ** -0.5)
        v = rows(o_mv + h * DH, o_mv + (h + 1) * DH)
        b_col = bcum[:, mh + h:mh + h + 1]
        logi_col = g[:, h:h + 1]
        a_col = a_all[:, mh + h:mh + h + 1]
        dmat = jnp.where(causal, b_col + to_row(logi_col - b_col), -jnp.inf)
        m_t = jnp.maximum(a_col, jnp.max(dmat, axis=1, keepdims=True))
        w_intra = jnp.exp(dmat - m_t)
        w_inter = jnp.exp(a_col - m_t)
        c_old = c_ref[0, h]
        n_old = n_ref[0, h:h + 1, :]
        s = _mm_nt(q, k) * w_intra
        num = _mm(s, v) + w_inter * _mm(q, c_old)
        den = jnp.sum(s, axis=1, keepdims=True) + w_inter * jnp.sum(q * n_old, axis=1, keepdims=True)
        hh = num * (1.0 / jnp.maximum(jnp.abs(den), jnp.exp(-m_t)))
        m_new = m_t[cl - 1:cl, :]
        b_last = b_col[cl - 1:cl, :]
        w_old = jnp.exp(b_last + m_prev[:, mh + h:mh + h + 1] - m_new)
        w_k = jnp.exp(b_last - b_col + logi_col - m_new)
        c_ref[0, h] = w_old * c_old + _mm_tn(k, w_k * v)
        n_ref[0, h:h + 1, :] = w_old * n_old + jnp.sum(w_k * k, axis=0, keepdims=True)
        m_new_row = jnp.where(lane == mh + h, m_new, m_new_row)
        om = _head_norm(hh, mgn_ref[:, h * DH:(h + 1) * DH]) * jax.nn.sigmoid(rows(o_mo + h * DH, o_mo + (h + 1) * DH))
        put(h * DH, (h + 1) * DH, om)
        if after_head is not None:
            after_head(h)
    m_ref[0] = m_new_row

    rel = (ii - jj).astype(F32)
    row_f = row_c.astype(F32)

    def rot(x):
        return x * cos2 + pltpu.roll(x, DH // 2, axis=1) * sin2

    for h in range(rh):
        log_g = math.log1p(-2.0 ** (-5.0 - h))
        q = rot(rows(o_rq + h * DH, o_rq + (h + 1) * DH))
        k = rot(rows(o_rk + h * DH, o_rk + (h + 1) * DH)) * (DH ** -0.5)
        v = rows(o_rv + h * DH, o_rv + (h + 1) * DH)
        decay = jnp.where(causal, jnp.exp(jnp.maximum(rel, 0.0) * log_g), 0.0)
        s_old = s_ref[0, h]
        y = _mm(_mm_nt(q, k) * decay, v) + jnp.exp((row_f + 1.0) * log_g) * _mm(q, s_old)
        k_w = jnp.exp((cl - 1.0 - row_f) * log_g)
        s_ref[0, h] = math.exp(cl * log_g) * s_old + _mm_tn(k, k_w * v)
        orr = _head_norm(y, rgn_ref[:, h * DH:(h + 1) * DH]) * _silu(rows(o_rg + h * DH, o_rg + (h + 1) * DH))
        put(mw + h * DH, mw + (h + 1) * DH, orr)
        if after_head is not None:
            after_head(mh + h)


MIX_ROWS = 256


def _mixer_kernel(x_ref, xn_ref, sc1_ref, sh1_ref, sc1n_ref, sh1n_ref, gt1_ref, sc2_ref, sh2_ref, cos_ref, sin_ref,
                  c0_ref, n0_ref, m0_ref, conv0_ref, s0_ref,
                  gpre1_ref, win_ref, convw_ref, convb_ref, bg_ref, mgn_ref, rgn_ref,
                  gpost_ref, gpre2_ref, wout_ref, wr_hi_ref, wr_lo_ref,
                  x1_ref, hf3_ref, lg_ref, c_ref, n_ref, m_ref, conv_ref, s_ref,
                  z_ref, o_ref, xcat_ref, *, mh, rh):
    i, j = pl.program_id(0), pl.program_id(1)
    step = i * pl.num_programs(1) + j
    cur = step % 2
    rs, d = x_ref.shape[1], x_ref.shape[2]
    nz = z_ref.shape[2]
    nchunks = rs // CHUNK
    npieces = nchunks * (mh + rh)
    cuts = [(nz // LANES * p // npieces) * LANES for p in range(npieces + 1)]

    def project(x3, sc_ref, sh_ref):
        return (_rms(x3, gpre1_ref[...]) * (1.0 + sc_ref[...]) + sh_ref[...]).reshape(rs, d)

    @pl.when(step == 0)
    def _():
        z_ref[0] = _mm(project(x_ref[...], sc1_ref, sh1_ref), win_ref[...])

    @pl.when(j == 0)
    def _():
        c_ref[...] = c0_ref[...]
        n_ref[...] = n0_ref[...]
        m_ref[...] = m0_ref[...]
        s_ref[...] = s0_ref[...]
        xcat_ref[8 - (CONV_W - 1):8, :] = conv0_ref[0]

    h_next = project(xn_ref[...], sc1n_ref, sh1n_ref).astype(MXU_DTYPE)
    for c in range(nchunks):
        r0 = c * CHUNK
        def project_piece(hd, c=c):
            lo, hi = cuts[c * (mh + rh) + hd], cuts[c * (mh + rh) + hd + 1]
            z_ref[1 - cur, :, lo:hi] = jnp.dot(h_next, win_ref[:, lo:hi], preferred_element_type=F32)

        def put(lo, hi, v, r0=r0):
            o_ref[r0:r0 + CHUNK, lo:hi] = v

        _chunk_step(lambda lo, hi, r0=r0: z_ref[cur, r0:r0 + CHUNK, lo:hi], put,
                    cos_ref[r0:r0 + CHUNK, :], sin_ref[r0:r0 + CHUNK, :],
                    c_ref, n_ref, m_ref, conv_ref, s_ref, xcat_ref,
                    convw_ref, convb_ref, bg_ref, mgn_ref, rgn_ref, mh, rh, after_head=project_piece)
    x = x_ref[...]
    _mixout_body(o_ref[...].reshape(1, rs, o_ref.shape[1]), x, gt1_ref, sc2_ref, sh2_ref, gpost_ref, gpre2_ref,
                 wout_ref, wr_hi_ref, wr_lo_ref, x1_ref, hf3_ref, lg_ref)


def _mixer(x, mods, cos2, sin2, c0, n0, m0, conv0, s0, gpre1, win, convw, convb, bg, mgn, rgn,
           gpost, gpre2, wout, wr_hi, wr_lo):
    sh1, sc1, gt1, sh2, sc2, gt2 = mods
    b, l, d = x.shape
    assert d == SUB * LANES and l % MIX_ROWS == 0
    mh, rh = c0.shape[1], s0.shape[1]
    mw, rw = mh * DH, rh * DH
    nz, ne = win.shape[1], wr_hi.shape[1]
    rs = MIX_ROWS
    assert nz % LANES == 0
    nj = l // rs
    bmap4 = lambda i, j: (i, 0, 0, 0)
    bmap3 = lambda i, j: (i, 0, 0)
    tmap = lambda i, j: (i, j, 0)
    fmap = lambda i, j: (i * nj + j, 0)
    once = lambda a: pl.BlockSpec(a.shape, lambda i, j: (0, 0), pipeline_mode=pl.Buffered(1))
    mod_spec = pl.BlockSpec((1, 1, d), bmap3)
    nxt = lambda i, j: jnp.minimum(i * nj + j + 1, b * nj - 1)
    x_next_spec = pl.BlockSpec((1, rs, d), lambda i, j: (nxt(i, j) // nj, nxt(i, j) % nj, 0))
    mod_next_spec = pl.BlockSpec((1, 1, d), lambda i, j: (nxt(i, j) // nj, 0, 0))
    rope_spec = pl.BlockSpec((rs, DH), lambda i, j: (j, 0))
    state_specs = [pl.BlockSpec((1, mh, DH, DH), bmap4), pl.BlockSpec((1, mh, DH), bmap3),
                   pl.BlockSpec((1, 1, GATE_LANES), bmap3), pl.BlockSpec((1, CONV_W - 1, 2 * mw), bmap3),
                   pl.BlockSpec((1, rh, DH, DH), bmap4)]
    state_shapes = [jax.ShapeDtypeStruct(c0.shape, F32), jax.ShapeDtypeStruct(n0.shape, F32),
                    jax.ShapeDtypeStruct(m0.shape, F32), jax.ShapeDtypeStruct(conv0.shape, F32),
                    jax.ShapeDtypeStruct(s0.shape, F32)]
    return pl.pallas_call(
        functools.partial(_mixer_kernel, mh=mh, rh=rh),
        grid=(b, nj),
        in_specs=[pl.BlockSpec((1, rs, d), tmap), x_next_spec, mod_spec, mod_spec, mod_next_spec, mod_next_spec]
                 + [mod_spec] * 3 + [rope_spec, rope_spec] + state_specs
                 + [once(a) for a in (gpre1, win, convw, convb, bg, mgn, rgn, gpost, gpre2, wout, wr_hi, wr_lo)],
        out_specs=[pl.BlockSpec((1, rs, d), tmap), pl.BlockSpec((rs * SUB, LANES), fmap),
                   pl.BlockSpec((rs, ne), fmap)] + state_specs,
        out_shape=[jax.ShapeDtypeStruct((b, l, d), F32), jax.ShapeDtypeStruct((b * l * SUB, LANES), F32),
                   jax.ShapeDtypeStruct((b * l, ne), F32)] + state_shapes,
        scratch_shapes=[pltpu.VMEM((2, rs, nz), F32), pltpu.VMEM((rs, mw + rw), F32),
                        pltpu.VMEM((CHUNK + 8, 2 * mw), F32)],
        compiler_params=_cparams(("arbitrary", "arbitrary")),
        name="mixer",
    )(x, x, sc1, sh1, sc1, sh1, gt1, sc2, sh2, cos2, sin2, c0, n0, m0, conv0, s0,
      gpre1, win, convw, convb, bg, mgn, rgn, gpost, gpre2, wout, wr_hi, wr_lo)


PACK_SEQS = CHUNK // SUB


def _mixcore_packed_kernel(z_ref, cos_ref, sin_ref, c0_ref, n0_ref, m0_ref, conv0_ref, s0_ref,
                           convw_ref, convb_ref, bg_ref, mgn_ref, rgn_ref,
                           out_ref, c_ref, n_ref, m_ref, conv_ref, s_ref, *, mh, rh):
    g_n, l_n, _ = z_ref.shape
    rr = g_n * l_n
    mw, rw = mh * DH, rh * DH
    o_qk, o_mv, o_mo = 0, 2 * mw, 3 * mw
    o_rq = 4 * mw
    o_rk, o_rv, o_rg, o_gt = o_rq + rw, o_rq + 2 * rw, o_rq + 3 * rw, o_rq + 4 * rw

    def rows(lo, hi):
        return z_ref[:, :, lo:hi].reshape(rr, hi - lo)

    def per_seq(x):
        return jnp.broadcast_to(x[:, None, :], (g_n, l_n, x.shape[-1])).reshape(rr, x.shape[-1])

    def last_of_seq(x):
        x3 = x.reshape(g_n, l_n, x.shape[-1])
        return jnp.broadcast_to(x3[:, l_n - 1:l_n, :], x3.shape).reshape(x.shape)

    def full(col):
        return jnp.broadcast_to(col, (rr, DH))

    pos_c = lax.broadcasted_iota(jnp.int32, (rr, 1), 0) % l_n
    ii = lax.broadcasted_iota(jnp.int32, (rr, rr), 0)
    jj = lax.broadcasted_iota(jnp.int32, (rr, rr), 1)
    mask = ((ii // l_n) == (jj // l_n)) & (jj <= ii)
    eye = jj == ii

    def to_row(col):
        return jnp.sum(jnp.where(eye, col, 0.0), axis=0, keepdims=True)

    x = rows(o_qk, o_qk + 2 * mw)
    e = conv0_ref[...].reshape(rr, 2 * mw)
    qk = convb_ref[...]
    for t in range(CONV_W):
        d = CONV_W - 1 - t
        xd = x if d == 0 else jnp.where(pos_c >= d, pltpu.roll(x, d, axis=0),
                                        pltpu.roll(e, (d - (CONV_W - 1)) % rr, axis=0))
        qk = qk + xd * convw_ref[t:t + 1, :]
    qk = _silu(qk)
    conv_ref[...] = x.reshape(g_n, l_n, 2 * mw)[:, l_n - (CONV_W - 1):, :]

    g = rows(o_gt, o_gt + GATE_LANES) + bg_ref[...]
    bcum = _log_sigmoid(g)
    s_ = 1
    while s_ < l_n:
        bcum = bcum + jnp.where(pos_c >= s_, pltpu.roll(bcum, s_, axis=0), 0.0)
        s_ *= 2
    m_prev = per_seq(m0_ref[:, 0, :])
    a_all = bcum + m_prev
    lane = lax.broadcasted_iota(jnp.int32, (1, GATE_LANES), 1)
    even_seq = (lax.broadcasted_iota(jnp.int32, (rr, 1), 0) // l_n) % 2 == 0
    m_new_rows = m_prev

    def state_dot(qb, st_ref, h):
        parts = []
        for s in range(g_n):
            pb, lo = s // 2, (s % 2) * l_n
            res = jnp.dot(qb[pb * 2 * l_n:(pb + 1) * 2 * l_n], st_ref[s, h].astype(MXU_DTYPE),
                          preferred_element_type=F32)
            parts.append(res[lo:lo + l_n])
        return jnp.concatenate(parts, axis=0)

    def state_update(k, wv, decay_full, st_in, st_out, h):
        k_even = jnp.where(even_seq, k, 0.0).astype(MXU_DTYPE)
        k_odd = jnp.where(even_seq, 0.0, k).astype(MXU_DTYPE)
        wvb = wv.astype(MXU_DTYPE)
        for s in range(g_n):
            pb = s // 2
            kp = (k_even if s % 2 == 0 else k_odd)[pb * 2 * l_n:(pb + 1) * 2 * l_n]
            upd = lax.dot_general(kp, wvb[pb * 2 * l_n:(pb + 1) * 2 * l_n], (((0,), (0,)), ((), ())),
                                  preferred_element_type=F32)
            st_out[s, h] = decay_full[s * l_n:s * l_n + 1, :] * st_in[s, h] + upd

    for h in range(mh):
        q = qk[:, h * DH:(h + 1) * DH]
        k = qk[:, mw + h * DH:mw + (h + 1) * DH] * (DH ** -0.5)
        v = rows(o_mv + h * DH, o_mv + (h + 1) * DH)
        b_col = bcum[:, mh + h:mh + h + 1]
        logi_col = g[:, h:h + 1]
        a_col = a_all[:, mh + h:mh + h + 1]
        dmat = jnp.where(mask, b_col + to_row(logi_col - b_col), -jnp.inf)
        m_t = jnp.maximum(a_col, jnp.max(dmat, axis=1, keepdims=True))
        w_intra = jnp.exp(dmat - m_t)
        w_inter = jnp.exp(a_col - m_t)
        n_rows = per_seq(n0_ref[:, h, :])
        s = _mm_nt(q, k) * w_intra
        num = _mm(s, v) + w_inter * state_dot(q.astype(MXU_DTYPE), c0_ref, h)
        den = jnp.sum(s, axis=1, keepdims=True) + w_inter * jnp.sum(q * n_rows, axis=1, keepdims=True)
        hh = num * (1.0 / jnp.maximum(jnp.abs(den), jnp.exp(-m_t)))
        m_new = last_of_seq(full(m_t))
        b_last = last_of_seq(full(b_col))
        w_old = jnp.exp(b_last + full(m_prev[:, mh + h:mh + h + 1]) - m_new)
        w_k = jnp.exp(b_last - full(b_col) + full(logi_col) - m_new)
        state_update(k, w_k * v, w_old, c0_ref, c_ref, h)
        n_ref[:, h, :] = (w_old.reshape(g_n, l_n, DH)[:, 0, :] * n0_ref[:, h, :]
                          + jnp.sum((w_k * k).reshape(g_n, l_n, DH), axis=1))
        m_new_rows = jnp.where(lane == mh + h, m_new, m_new_rows)
        om = _head_norm(hh, mgn_ref[:, h * DH:(h + 1) * DH]) * jax.nn.sigmoid(rows(o_mo + h * DH, o_mo + (h + 1) * DH))
        out_ref[:, :, h * DH:(h + 1) * DH] = om.reshape(g_n, l_n, DH)
    m_ref[...] = m_new_rows.reshape(g_n, l_n, GATE_LANES)[:, 0:1, :]

    cos2 = cos_ref[...]
    sin2 = sin_ref[...]
    rel = (ii - jj).astype(F32)
    pos_f = pos_c.astype(F32)

    def rot(x):
        return x * cos2 + pltpu.roll(x, DH // 2, axis=1) * sin2

    for h in range(rh):
        log_g = math.log1p(-2.0 ** (-5.0 - h))
        q = rot(rows(o_rq + h * DH, o_rq + (h + 1) * DH))
        k = rot(rows(o_rk + h * DH, o_rk + (h + 1) * DH)) * (DH ** -0.5)
        v = rows(o_rv + h * DH, o_rv + (h + 1) * DH)
        decay = jnp.where(mask, jnp.exp(jnp.maximum(rel, 0.0) * log_g), 0.0)
        y = (_mm(_mm_nt(q, k) * decay, v)
             + jnp.exp((pos_f + 1.0) * log_g) * state_dot(q.astype(MXU_DTYPE), s0_ref, h))
        k_w = jnp.exp((l_n - 1.0 - pos_f) * log_g)
        state_update(k, k_w * v, jnp.full((rr, DH), math.exp(l_n * log_g), F32), s0_ref, s_ref, h)
        orr = _head_norm(y, rgn_ref[:, h * DH:(h + 1) * DH]) * _silu(rows(o_rg + h * DH, o_rg + (h + 1) * DH))
        out_ref[:, :, mw + h * DH:mw + (h + 1) * DH] = orr.reshape(g_n, l_n, DH)


def _mixcore_packed(z, cos2, sin2, c0, n0, m0, conv0, s0, convw, convb, bg, mgn, rgn):
    b, l, nz = z.shape
    assert l == SUB and b % PACK_SEQS == 0
    mh, rh = c0.shape[1], s0.shape[1]
    mw, rw = mh * DH, rh * DH
    g_n = PACK_SEQS
    conv0p = jnp.pad(conv0, ((0, 0), (0, l - (CONV_W - 1)), (0, 0)))
    tile = lambda t: jnp.tile(t[:l], (g_n, 1))
    bmap4 = lambda i: (i, 0, 0, 0)
    bmap3 = lambda i: (i, 0, 0)
    wmap = lambda i: (0, 0)
    c_spec, n_spec = pl.BlockSpec((g_n, mh, DH, DH), bmap4), pl.BlockSpec((g_n, mh, DH), bmap3)
    m_spec, s_spec = pl.BlockSpec((g_n, 1, GATE_LANES), bmap3), pl.BlockSpec((g_n, rh, DH, DH), bmap4)
    return pl.pallas_call(
        functools.partial(_mixcore_packed_kernel, mh=mh, rh=rh),
        grid=(b // g_n,),
        in_specs=[pl.BlockSpec((g_n, l, nz), bmap3),
                  pl.BlockSpec((g_n * l, DH), wmap), pl.BlockSpec((g_n * l, DH), wmap),
                  c_spec, n_spec, m_spec, pl.BlockSpec((g_n, l, 2 * mw), bmap3), s_spec,
                  pl.BlockSpec(convw.shape, wmap), pl.BlockSpec(convb.shape, wmap), pl.BlockSpec(bg.shape, wmap),
                  pl.BlockSpec(mgn.shape, wmap), pl.BlockSpec(rgn.shape, wmap)],
        out_specs=[pl.BlockSpec((g_n, l, mw + rw), bmap3), c_spec, n_spec, m_spec,
                   pl.BlockSpec((g_n, CONV_W - 1, 2 * mw), bmap3), s_spec],
        out_shape=[jax.ShapeDtypeStruct((b, l, mw + rw), F32), jax.ShapeDtypeStruct(c0.shape, F32),
                   jax.ShapeDtypeStruct(n0.shape, F32), jax.ShapeDtypeStruct(m0.shape, F32),
                   jax.ShapeDtypeStruct(conv0.shape, F32), jax.ShapeDtypeStruct(s0.shape, F32)],
        compiler_params=_cparams(("arbitrary",)),
        name="mixcore_packed",
    )(z, tile(cos2), tile(sin2), c0, n0, m0, conv0p, s0, convw, convb, bg, mgn, rgn)


def _split_hi_lo(x):
    hi = x.astype(MXU_DTYPE)
    lo = (x - hi.astype(F32)).astype(MXU_DTYPE)
    return hi, lo


def _to_row_tiles(ref, x):
    n = x.shape[0]
    for j in range(SUB):
        ref[pl.ds(j, n, stride=SUB), :] = x[:, j * LANES:(j + 1) * LANES]


def _from_row_tiles(ref, n):
    return jnp.concatenate([ref[pl.ds(j, n, stride=SUB), :] for j in range(SUB)], axis=1)


def _mixout_kernel(o_ref, x_ref, gt1_ref, sc2_ref, sh2_ref, gpost_ref, gpre_ref, wout_ref, wr_hi_ref, wr_lo_ref,
                   x1_ref, hf3_ref, lg_ref):
    _mixout_body(o_ref[...], x_ref[...], gt1_ref, sc2_ref, sh2_ref, gpost_ref, gpre_ref, wout_ref, wr_hi_ref,
                 wr_lo_ref, x1_ref, hf3_ref, lg_ref)


def _mixout_body(o, x, gt1_ref, sc2_ref, sh2_ref, gpost_ref, gpre_ref, wout_ref, wr_hi_ref, wr_lo_ref,
                 x1_ref, hf3_ref, lg_ref):
    bb, ll, d = x.shape
    ym = _mm(o.reshape(bb * ll, o.shape[-1]), wout_ref[...]).reshape(bb, ll, d)
    x1 = x + gt1_ref[...] * _rms(ym, gpost_ref[...])
    x1_ref[...] = x1
    hf = (_rms(x1, gpre_ref[...]) * (1.0 + sc2_ref[...]) + sh2_ref[...]).reshape(bb * ll, d)
    _to_row_tiles(hf3_ref, hf)
    hi, lo = _split_hi_lo(hf)
    lg_ref[...] = _mm(hi, wr_hi_ref[...]) + (_mm(lo, wr_hi_ref[...]) + _mm(hi, wr_lo_ref[...]))


def _mixout(o, x, gt1, sc2, sh2, gpost, gpre, wout, wr_hi, wr_lo):
    b, l, d = x.shape
    assert d == SUB * LANES
    w = o.shape[-1]
    ne = wr_hi.shape[1]
    ll = min(l, TOK_TILE)
    bb = TOK_TILE // ll
    tmap = lambda i, j: (i, j, 0)
    fmap = lambda i, j: (i * (l // ll) + j, 0)
    mod_spec = pl.BlockSpec((bb, 1, d), lambda i, j: (i, 0, 0))
    wmap = lambda i, j: (0, 0)
    return pl.pallas_call(
        _mixout_kernel,
        grid=(b // bb, l // ll),
        in_specs=[pl.BlockSpec((bb, ll, w), tmap), pl.BlockSpec((bb, ll, d), tmap), mod_spec, mod_spec, mod_spec,
                  pl.BlockSpec((1, d), wmap), pl.BlockSpec((1, d), wmap), pl.BlockSpec((w, d), wmap),
                  pl.BlockSpec((d, ne), wmap), pl.BlockSpec((d, ne), wmap)],
        out_specs=[pl.BlockSpec((bb, ll, d), tmap), pl.BlockSpec((bb * ll * SUB, LANES), fmap),
                   pl.BlockSpec((bb * ll, ne), fmap)],
        out_shape=[jax.ShapeDtypeStruct((b, l, d), F32), jax.ShapeDtypeStruct((b * l * SUB, LANES), F32),
                   jax.ShapeDtypeStruct((b * l, ne), F32)],
        compiler_params=_cparams(("arbitrary", "arbitrary")),
        name="mixout",
    )(o, x, gt1, sc2, sh2, gpost, gpre, wout, wr_hi, wr_lo)


def _router_kernel(lg_ref, br_ref, lidx_ref, wts_ref, cnt_ref, seg_ref, *, ne):
    tm = lg_ref.shape[0]
    gsz = ne // N_GROUPS
    s = jax.nn.sigmoid(lg_ref[...].T[:ne, :])
    sb = s + br_ref[...]
    sb3 = sb.reshape(N_GROUPS, gsz, tm)
    e3 = lax.broadcasted_iota(jnp.int32, sb3.shape, 1)
    m1 = jnp.max(sb3, axis=1, keepdims=True)
    first = jnp.min(jnp.where(sb3 == m1, e3, gsz), axis=1, keepdims=True)
    m2 = jnp.max(jnp.where(e3 == first, -jnp.inf, sb3), axis=1, keepdims=True)
    gs = (m1 + m2).reshape(N_GROUPS, tm)
    gi = lax.broadcasted_iota(jnp.int32, gs.shape, 0)
    grank = jnp.zeros(gs.shape, F32)
    for g in range(N_GROUPS):
        r = gs[g:g + 1, :]
        grank = grank + jnp.where(r > gs, 1.0, jnp.where(r == gs, jnp.where(g < gi, 1.0, 0.0), 0.0))
    gsel = jnp.where(grank < TOPK_GROUPS, 1.0, 0.0)
    emask = jnp.broadcast_to(gsel.reshape(N_GROUPS, 1, tm), sb3.shape).reshape(ne, tm) > 0.0
    masked = jnp.where(emask, sb, -jnp.inf)
    ei = lax.broadcasted_iota(jnp.int32, masked.shape, 0)
    cur = masked
    sel_f = jnp.zeros(masked.shape, F32)
    for _ in range(TOP_K):
        best = jnp.max(cur, axis=0, keepdims=True)
        first_e = jnp.min(jnp.where(cur == best, ei, ne), axis=0, keepdims=True)
        pick = ei == first_e
        sel_f = jnp.where(pick, 1.0, sel_f)
        cur = jnp.where(pick, -jnp.inf, cur)
    sel = sel_f > 0.0
    w = jnp.where(sel, s, 0.0)
    w = w / jnp.sum(w, axis=0, keepdims=True) * ROUTED_SCALE

    self = jnp.where(sel, 1.0, 0.0)
    er = lax.broadcasted_iota(jnp.int32, (ne, ne), 0)
    ec = lax.broadcasted_iota(jnp.int32, (ne, ne), 1)
    slot = _mm(jnp.where(ec < er, 1.0, 0.0), self)
    tr = lax.broadcasted_iota(jnp.int32, (tm, tm), 0)
    tc = lax.broadcasted_iota(jnp.int32, (tm, tm), 1)
    pos = _mm(self, jnp.where(tr < tc, 1.0, 0.0))
    seg = jnp.sum(slot, axis=1, keepdims=True)
    lrow = seg + pos
    lidx_rows, w_rows = [], []
    for k in range(TOP_K):
        pick = sel & (slot == float(k))
        lidx_rows.append(jnp.sum(jnp.where(pick, lrow, 0.0), axis=0, keepdims=True))
        w_rows.append(jnp.sum(jnp.where(pick, w, 0.0), axis=0, keepdims=True))
    lidx_ref[...] = jnp.concatenate(lidx_rows, axis=0).astype(jnp.int32)
    wts_ref[...] = jnp.concatenate(w_rows, axis=0)
    ones = jnp.ones((SUB, tm), F32)
    padz = jnp.zeros((LANES - ne, tm), F32)
    cnt_ref[...] = _mm_nt(ones, jnp.concatenate([self, padz], axis=0)).astype(jnp.int32)
    seg_ref[...] = _mm_nt(ones, jnp.concatenate([slot, padz], axis=0)).astype(jnp.int32)


def _router(logits, b_router, ne):
    t, lanes = logits.shape
    tm = ROUTE_TILE
    nt = t // tm
    kspec = pl.BlockSpec((TOP_K, tm), lambda i: (0, i))
    cspec = pl.BlockSpec((SUB, LANES), lambda i: (i, 0))
    return pl.pallas_call(
        functools.partial(_router_kernel, ne=ne),
        grid=(nt,),
        in_specs=[pl.BlockSpec((tm, lanes), lambda i: (i, 0)), pl.BlockSpec((ne, 1), lambda i: (0, 0))],
        out_specs=[kspec, kspec, cspec, cspec],
        out_shape=[jax.ShapeDtypeStruct((TOP_K, t), jnp.int32), jax.ShapeDtypeStruct((TOP_K, t), F32),
                   jax.ShapeDtypeStruct((nt * SUB, LANES), jnp.int32),
                   jax.ShapeDtypeStruct((nt * SUB, LANES), jnp.int32)],
        compiler_params=_cparams(("arbitrary",)),
        name="router",
    )(logits, b_router.reshape(ne, 1))


SEG_BITS = ROUTE_TILE.bit_length()
PAD_BITS = (MOE_TILE - 1).bit_length()


RARE_BIT = 7


def _for_each_piece(n, fn, bits):
    def pieces(lo, hi):
        for bit in range(hi - 1, lo - 1, -1):
            @pl.when(((n >> bit) & 1) == 1)
            def _():
                fn((n >> (bit + 1)) << (bit + 1), 1 << bit)

    if bits > RARE_BIT:
        @pl.when(n >= (1 << RARE_BIT))
        def _():
            pieces(RARE_BIT, bits)
    pieces(0, min(bits, RARE_BIT))


SORT_UNROLL = 4


def _zero_rows(ps_ref, pn_ref, tail_ref, xs_ref, z_ref, sem, ne, act):
    def pad_body(e, carry):
        _for_each_piece(pn_ref[e], lambda off, size: act(
            pltpu.make_async_copy(z_ref.at[pl.ds(0, size)], xs_ref.at[pl.ds(ps_ref[e] + off, size)], sem)),
            PAD_BITS)
        return carry
    lax.fori_loop(0, ne, pad_body, 0)

    def tail_body(b, carry):
        act(pltpu.make_async_copy(z_ref, xs_ref.at[pl.ds(tail_ref[0] + b * MOE_TILE, MOE_TILE)], sem))
        return carry
    lax.fori_loop(0, tail_ref[1], tail_body, 0)


def _dispatch_kernel(cnt_ref, seg_ref, gb_ref, ps_ref, pn_ref, tail_ref, lidx_p_ref, lidx_s_ref, hf3_p_ref, hf3_s_ref,
                     xs_ref, buf0, buf1, z_ref, sems, *, ntp, ne):
    i = pl.program_id(0)
    n = pl.num_programs(0)
    tt = lidx_p_ref.shape[0] // TOP_K

    def whole(buf, sem):
        return pltpu.make_async_copy(buf, xs_ref.at[pl.ds(0, tt * TOP_K)], sem)

    @pl.when(i == 0)
    def _():
        z_ref[...] = jnp.zeros(z_ref.shape, F32)
        _zero_rows(ps_ref, pn_ref, tail_ref, xs_ref, z_ref, sems.at[2], ne, lambda c: c.start())

    def run(buf, sem):
        @pl.when(i >= 2)
        def _():
            whole(buf, sem).wait()

        def sort_from(hf3_ref, lidx_ref):
            def sort_body(tb, carry):
                for u in range(SORT_UNROLL):
                    t = tb * SORT_UNROLL + u
                    v = hf3_ref[pl.ds(pl.multiple_of(t * SUB, SUB), SUB), :]
                    for k in range(TOP_K):
                        buf[lidx_ref[t * TOP_K + k]] = v
                return carry
            lax.fori_loop(0, tt // SORT_UNROLL, sort_body, 0)

        @pl.when(i < ntp)
        def _():
            sort_from(hf3_p_ref, lidx_p_ref)

        @pl.when(i >= ntp)
        def _():
            sort_from(hf3_s_ref, lidx_s_ref)

        def seg_body(e, carry):
            c, s, g = cnt_ref[i * ne + e], seg_ref[i * ne + e], gb_ref[i * ne + e]
            _for_each_piece(c, lambda off, size: pltpu.make_async_copy(
                buf.at[pl.ds(s + off, size)], xs_ref.at[pl.ds(g + off, size)], sem).start(), SEG_BITS)
            return carry
        lax.fori_loop(0, ne, seg_body, 0)

    @pl.when(i % 2 == 0)
    def _():
        run(buf0, sems.at[0])

    @pl.when(i % 2 == 1)
    def _():
        run(buf1, sems.at[1])

    @pl.when(i == 0)
    def _():
        _zero_rows(ps_ref, pn_ref, tail_ref, xs_ref, z_ref, sems.at[2], ne, lambda c: c.wait())

    @pl.when(i == n - 1)
    def _():
        @pl.when(i % 2 == 0)
        def _():
            whole(buf0, sems.at[0]).wait()

            @pl.when(i >= 1)
            def _():
                whole(buf1, sems.at[1]).wait()

        @pl.when(i % 2 == 1)
        def _():
            whole(buf1, sems.at[1]).wait()
            whole(buf0, sems.at[0]).wait()


def _dispatch(cnt, seg, gbase, pad_start, pad_n, tail, lidx_p, lidx_s, hf3_p, hf3_s, rows, ne):
    tt = ROUTE_TILE
    ntp, nts =lidx_p.shape[0] // (tt * TOP_K), lidx_s.shape[0] // (tt * TOP_K)
    pmap = lambda i, *_: jnp.minimum(i, ntp - 1)
    smap = lambda i, *_: jnp.maximum(i - ntp, 0)
    return pl.pallas_call(
        functools.partial(_dispatch_kernel, ntp=ntp, ne=ne),
        grid_spec=pltpu.PrefetchScalarGridSpec(
            num_scalar_prefetch=6, grid=(ntp + nts,),
            in_specs=[pl.BlockSpec((tt * TOP_K,), lambda i, *_: (pmap(i),), memory_space=pltpu.SMEM),
                      pl.BlockSpec((tt * TOP_K,), lambda i, *_: (smap(i),), memory_space=pltpu.SMEM),
                      pl.BlockSpec((tt * SUB, LANES), lambda i, *_: (pmap(i), 0)),
                      pl.BlockSpec((tt * SUB, LANES), lambda i, *_: (smap(i), 0))],
            out_specs=pl.BlockSpec(memory_space=pl.ANY),
            scratch_shapes=[pltpu.VMEM((tt * TOP_K, SUB, LANES), F32), pltpu.VMEM((tt * TOP_K, SUB, LANES), F32),
                            pltpu.VMEM((MOE_TILE, SUB, LANES), F32), pltpu.SemaphoreType.DMA((3,))]),
        out_shape=jax.ShapeDtypeStruct((rows, SUB, LANES), F32),
        compiler_params=_cparams(("arbitrary",)),
        name="dispatch",
    )(cnt, seg, gbase, pad_start, pad_n, tail, lidx_p, lidx_s, hf3_p, hf3_s)


def _ffn_kernel(be_ref, nbu_ref, xs_ref, wg_ref, wu_ref, wd_ref, ys_ref, wg_s, wu_s, wd_s):
    b = pl.program_id(0)
    tm = xs_ref.shape[0] // SUB

    @pl.when(b < nbu_ref[0])
    def _():
        @pl.when((b == 0) | (be_ref[b] != be_ref[jnp.maximum(b - 1, 0)]))
        def _():
            wg_s[...] = wg_ref[0].astype(MXU_DTYPE)
            wu_s[...] = wu_ref[0].astype(MXU_DTYPE)
            wd_s[...] = wd_ref[0].astype(MXU_DTYPE)

        x = _from_row_tiles(xs_ref, tm)
        hb = _silu(_mm(x, wg_s[...])) * _mm(x, wu_s[...])
        _to_row_tiles(ys_ref, _mm(hb, wd_s[...]))

    @pl.when(b >= nbu_ref[0])
    def _():
        ys_ref[...] = jnp.zeros(ys_ref.shape, F32)


def _ffn(block_e, nb_used, xs2, wg, wu, wd):
    rows = xs2.shape[0] // SUB
    d, f = wg.shape[1], wg.shape[2]
    tm = MOE_TILE
    nb = rows // tm
    xmap = lambda b, be, nbu: (jnp.minimum(b, nbu[0] - 1), 0)
    wmap = lambda b, be, nbu: (be[b], 0, 0)
    return pl.pallas_call(
        _ffn_kernel,
        grid_spec=pltpu.PrefetchScalarGridSpec(
            num_scalar_prefetch=2,
            grid=(nb,),
            in_specs=[pl.BlockSpec((tm * SUB, LANES), xmap), pl.BlockSpec((1, d, f), wmap),
                      pl.BlockSpec((1, d, f), wmap), pl.BlockSpec((1, f, d), wmap)],
            out_specs=pl.BlockSpec((tm * SUB, LANES), lambda b, be, nbu: (b, 0)),
            scratch_shapes=[pltpu.VMEM((d, f), MXU_DTYPE), pltpu.VMEM((d, f), MXU_DTYPE),
                            pltpu.VMEM((f, d), MXU_DTYPE)]),
        out_shape=jax.ShapeDtypeStruct((rows * SUB, LANES), F32),
        compiler_params=_cparams(("arbitrary",)),
        name="ffn",
    )(block_e, nb_used, xs2, wg, wu, wd)


def _combine_kernel(cnt_ref, seg_ref, gb_ref, lidx_ref, wts_ref, hf3_ref, x1_ref, gt2_ref, gpost_ref,
                    wsg_ref, wsu_ref, wsd_ref, ys_ref, y_ref, buf0, buf1, y3_ref, sems, *, tile0, ne):
    bb, ll, d = x1_ref.shape
    tt = bb * ll
    i = pl.program_id(0)
    n = pl.num_programs(0)

    def fetch(j, buf, sem):
        def seg_body(e, carry):
            c, s, g = cnt_ref[j * ne + e], seg_ref[j * ne + e], gb_ref[j * ne + e]
            _for_each_piece(c, lambda off, size: pltpu.make_async_copy(
                ys_ref.at[pl.ds(g + off, size)], buf.at[pl.ds(s + off, size)], sem).start(), SEG_BITS)
            return carry
        lax.fori_loop(0, ne, seg_body, 0)

    def run(buf, sem, nbuf, nsem):
        @pl.when(i == 0)
        def _():
            fetch(tile0, buf, sem)

        @pl.when(i + 1 < n)
        def _():
            fetch(tile0 + i + 1, nbuf, nsem)

        pltpu.make_async_copy(ys_ref.at[pl.ds(0, tt * TOP_K)], buf, sem).wait()

        def tok_body(tb, carry):
            for u in range(SORT_UNROLL):
                t = tb * SORT_UNROLL + u
                acc = wts_ref[t * TOP_K] * buf[lidx_ref[t * TOP_K]]
                for k in range(1, TOP_K):
                    acc = acc + wts_ref[t * TOP_K + k] * buf[lidx_ref[t * TOP_K + k]]
                y3_ref[pl.ds(pl.multiple_of(t * SUB, SUB), SUB), :] = acc
            return carry
        lax.fori_loop(0, tt // SORT_UNROLL, tok_body, 0)

    @pl.when(i % 2 == 0)
    def _():
        run(buf0, sems.at[0], buf1, sems.at[1])

    @pl.when(i % 2 == 1)
    def _():
        run(buf1, sems.at[1], buf0, sems.at[0])

    hf = _from_row_tiles(hf3_ref, tt)
    shared = _mm(_silu(_mm(hf, wsg_ref[...])) * _mm(hf, wsu_ref[...]), wsd_ref[...])
    yf = (_from_row_tiles(y3_ref, tt) + shared).reshape(bb, ll, d)
    y_ref[...] = x1_ref[...] + gt2_ref[...] * _rms(yf, gpost_ref[...])


def _combine(cnt, seg, gbase, lidx, wts, hf3, x1, gt2, gpost, wsg, wsu, wsd, ys, tile0, ne):
    b, l, d = x1.shape
    tt = ROUTE_TILE
    ll = min(l, tt)
    bb = tt // ll
    nl = l // ll
    nt = (b // bb) * nl
    fs = wsg.shape[1]
    if nl == 1:
        tmap = lambda i, *_: (i, 0, 0)
        bmap = lambda i, *_: (i, 0, 0)
    else:
        tmap = lambda i, *_: (i // nl, i % nl, 0)
        bmap = lambda i, *_: (i // nl, 0, 0)
    wmap = lambda i, *_: (0, 0)
    kspec = lambda: pl.BlockSpec((tt * TOP_K,), lambda i, *_: (i,), memory_space=pltpu.SMEM)
    return pl.pallas_call(
        functools.partial(_combine_kernel, tile0=tile0, ne=ne),
        grid_spec=pltpu.PrefetchScalarGridSpec(
            num_scalar_prefetch=3, grid=(nt,),
            in_specs=[kspec(), kspec(),
                      pl.BlockSpec((tt * SUB, LANES), lambda i, *_: (i, 0)),
                      pl.BlockSpec((bb, ll, d), tmap), pl.BlockSpec((bb, 1, d), bmap),
                      pl.BlockSpec((1, d), wmap), pl.BlockSpec((d, fs), wmap), pl.BlockSpec((d, fs), wmap),
                      pl.BlockSpec((fs, d), wmap),
                      pl.BlockSpec(memory_space=pl.ANY)],
            out_specs=pl.BlockSpec((bb, ll, d), tmap),
            scratch_shapes=[pltpu.VMEM((tt * TOP_K, SUB, LANES), F32), pltpu.VMEM((tt * TOP_K, SUB, LANES), F32),
                            pltpu.VMEM((tt * SUB, LANES), F32), pltpu.SemaphoreType.DMA((2,))]),
        out_shape=jax.ShapeDtypeStruct((b, l, d), F32),
        compiler_params=_cparams(("arbitrary",), VMEM_LIMIT_COMBINE),
        name="combine",
    )(cnt, seg, gbase, lidx, wts, hf3, x1, gt2, gpost, wsg, wsu, wsd, ys)


def _rope_tables(pos, rows):
    half = DH // 2
    freqs = ROPE_BASE ** (-jnp.arange(half, dtype=F32) / half)
    ang = pos.astype(F32)[:, None] * freqs[None, :]
    cos, sin = jnp.cos(ang), jnp.sin(ang)
    cos2 = jnp.concatenate([cos, cos], axis=1)
    sin2 = jnp.concatenate([-sin, sin], axis=1)
    padr = rows - pos.shape[0]
    return jnp.pad(cos2, ((0, padr), (0, 0))), jnp.pad(sin2, ((0, padr), (0, 0)))


def _global_plan(cnt_tiles, n_assign, ne):
    counts = jnp.sum(cnt_tiles, axis=0)
    padded = (counts + MOE_TILE - 1) // MOE_TILE * MOE_TILE
    pend = jnp.cumsum(padded)
    base = pend - padded
    gbase = base[None, :] + jnp.cumsum(cnt_tiles, axis=0) - cnt_tiles
    nb = n_assign // MOE_TILE + ne
    starts = jnp.arange(nb, dtype=jnp.int32) * MOE_TILE
    block_e = jnp.minimum(jnp.sum((pend[None, :] <= starts[:, None]).astype(jnp.int32), axis=1), ne - 1)
    nb_used = pend[-1:] // MOE_TILE
    tail = jnp.concatenate([pend[-1:], nb - nb_used])
    i32 = lambda v: v.astype(jnp.int32)
    return i32(gbase.reshape(-1)), i32(block_e), i32(nb_used), i32(base + counts), i32(padded - counts), i32(tail), nb


def kernel(x_prompt, x_sample, c_prompt, c_sample, state_mlstm_C, state_mlstm_n, state_mlstm_m, state_mlstm_conv, state_ret_S, w_ada, b_ada, g_mix_pre, g_mix_post, g_ffn_pre, g_ffn_post, w_in, b_gates, conv_w, conv_b, m_gn, r_gn, w_out, w_router, b_router, w_exp_gate, w_exp_up, w_exp_down, w_sh_gate, w_sh_up, w_sh_down):
    depth = w_ada.shape[0]
    bp, lp, d = x_prompt.shape
    bs, ls, _ = x_sample.shape
    mh, rh = state_mlstm_C.shape[2], state_ret_S.shape[2]
    mw, rw = mh * DH, rh * DH
    ne = w_router.shape[2]
    tp, ts = bp * lp, bs * ls
    ntp = tp // ROUTE_TILE

    cos_p, sin_p = _rope_tables(jnp.arange(lp, dtype=jnp.int32), lp)
    cos_s, sin_s = _rope_tables(PAST_LEN + jnp.arange(ls, dtype=jnp.int32), CHUNK if ls % CHUNK else ls)

    hp, hs = x_prompt, x_sample
    new_p, new_s = [], []
    for l in range(depth):
        wi = w_in[l].astype(MXU_DTYPE)
        g0 = 4 * mw
        w_in_r = jnp.concatenate([wi[:, :g0], wi[:, g0 + 2 * mh:], wi[:, g0:g0 + 2 * mh],
                                  jnp.zeros((d, GATE_LANES - 2 * mh), MXU_DTYPE)], axis=1)
        bg = jnp.pad(b_gates[l], (0, GATE_LANES - 2 * mh)).reshape(1, GATE_LANES)
        wout = w_out[l].astype(MXU_DTYPE)
        wr = jnp.pad(w_router[l], ((0, 0), (0, GATE_LANES - ne)))
        wr_hi, wr_lo = _split_hi_lo(wr)
        wsg, wsu, wsd = (w_sh_gate[l].astype(MXU_DTYPE), w_sh_up[l].astype(MXU_DTYPE), w_sh_down[l].astype(MXU_DTYPE))
        row = lambda v: v.reshape(1, -1)

        mod = _ada(jnp.concatenate([c_prompt, c_sample], axis=0), w_ada[l], b_ada[l])
        mods_p = [m[:bp].reshape(bp, 1, d) for m in jnp.split(mod, 6, axis=1)]
        mods_s = [m[bp:].reshape(bs, 1, d) for m in jnp.split(mod, 6, axis=1)]

        def lane_m(m):
            return jnp.pad(m, ((0, 0), (mh, GATE_LANES - 2 * mh)))[:, None, :]

        init = (jnp.zeros((bp, mh, DH, DH), F32), jnp.zeros((bp, mh, DH), F32), lane_m(jnp.zeros((bp, mh), F32)),
                jnp.zeros((bp, CONV_W - 1, 2 * mw), F32), jnp.zeros((bp, rh, DH, DH), F32))
        past = (state_mlstm_C[l], state_mlstm_n[l], lane_m(state_mlstm_m[l]), state_mlstm_conv[l], state_ret_S[l])

        def mix(x, mods, state, cos2, sin2):
            sh1, sc1, gt1, sh2, sc2, gt2 = mods
            b, ll, _ = x.shape
            if ll % MIX_ROWS == 0:
                x1, hf3, lg, c_n, n_n, m_n, conv_n, s_n = _mixer(
                    x, mods, cos2, sin2, *state, row(g_mix_pre[l]), w_in_r, conv_w[l], row(conv_b[l]), bg,
                    row(m_gn[l]), row(r_gn[l]), row(g_mix_post[l]), row(g_ffn_pre[l]), wout, wr_hi, wr_lo)
            else:
                z = _inproj(x, sc1, sh1, row(g_mix_pre[l]), w_in_r).reshape(b, ll, -1)
                o_mix, c_n, n_n, m_n, conv_n, s_n = _mixcore_packed(
                    z, cos2, sin2, *state, conv_w[l], row(conv_b[l]), bg, row(m_gn[l]), row(r_gn[l]))
                x1, hf3, lg = _mixout(o_mix, x, gt1, sc2, sh2, row(g_mix_post[l]), row(g_ffn_pre[l]), wout, wr_hi,
                                      wr_lo)
            lidx, wts, cnt, seg = _router(lg, b_router[l], ne)
            flat = lambda v: v.T.reshape(-1)
            return (x1, hf3, flat(lidx), flat(wts), cnt[::SUB, :ne], seg[::SUB, :ne],
                    (c_n, n_n, m_n[:, 0, mh:2 * mh], conv_n, s_n))

        x1_p, hf3_p, lidx_p, wts_p, cnt_p, seg_p, st_p = mix(hp, mods_p, init, cos_p, sin_p)
        x1_s, hf3_s, lidx_s, wts_s, cnt_s, seg_s, st_s = mix(hs, mods_s, past, cos_s, sin_s)

        cnt_t = jnp.concatenate([cnt_p, cnt_s], axis=0)
        cnt = cnt_t.reshape(-1)
        seg = jnp.concatenate([seg_p, seg_s], axis=0).reshape(-1)
        gbase, block_e, nb_used, pad_start, pad_n, tail, nb = _global_plan(cnt_t, (tp + ts) * TOP_K, ne)
        xs = _dispatch(cnt, seg, gbase, pad_start, pad_n, tail, lidx_p, lidx_s, hf3_p, hf3_s, nb * MOE_TILE, ne)
        ys = _ffn(block_e, nb_used, xs.reshape(nb * MOE_TILE * SUB, LANES), w_exp_gate[l], w_exp_up[l], w_exp_down[l])
        ys = ys.reshape(nb * MOE_TILE, SUB, LANES)
        hp = _combine(cnt, seg, gbase, lidx_p, wts_p, hf3_p, x1_p, mods_p[5], row(g_ffn_post[l]), wsg, wsu, wsd, ys, 0, ne)
        hs = _combine(cnt, seg, gbase, lidx_s, wts_s, hf3_s, x1_s, mods_s[5], row(g_ffn_post[l]), wsg, wsu, wsd, ys, ntp, ne)
        new_p.append(st_p)
        new_s.append(st_s)

    p_state = [jnp.stack([st[i] for st in new_p]) for i in range(5)]
    s_state = [jnp.stack([st[i] for st in new_s]) for i in range(5)]
    return (hp, hs, *p_state, *s_state)
```

```python
import functools
import math

import jax
import jax.numpy as jnp
from jax import lax
from jax.experimental import pallas as pl
from jax.experimental.pallas import tpu as pltpu

F32 = jnp.float32
MXU_DTYPE = jnp.bfloat16

NORM_EPS = 1e-6
ROPE_BASE = 10000.0
PAST_LEN = 16384
CHUNK = 128
CONV_W = 4
DH = 128
TOP_K = 8
N_GROUPS = 8
TOPK_GROUPS = 4
ROUTED_SCALE = 2.5
GATE_LANES = 128

V7X_VMEM_BYTES = 64 * 1024 * 1024
VMEM_LIMIT = 48 * 1024 * 1024
VMEM_LIMIT_COMBINE = 56 * 1024 * 1024

SUB, LANES = 8, 128

TOK_TILE = 256
ROUTE_TILE = 512
MOE_TILE = 512


def _cparams(sem, vmem_limit=VMEM_LIMIT):
    return pltpu.CompilerParams(dimension_semantics=sem, vmem_limit_bytes=vmem_limit)


def _silu(x):
    return x * jax.nn.sigmoid(x)


def _rms(x, g):
    return x * lax.rsqrt(jnp.mean(x * x, axis=-1, keepdims=True) + NORM_EPS) * g


def _mm(a, b):
    return jnp.dot(a.astype(MXU_DTYPE), b.astype(MXU_DTYPE), preferred_element_type=F32)


def _mm_nt(a, b):
    return lax.dot_general(a.astype(MXU_DTYPE), b.astype(MXU_DTYPE), (((1,), (1,)), ((), ())),
                           preferred_element_type=F32)


def _mm_tn(a, b):
    return lax.dot_general(a.astype(MXU_DTYPE), b.astype(MXU_DTYPE), (((0,), (0,)), ((), ())),
                           preferred_element_type=F32)


def _ada_kernel(c_ref, w_ref, b_ref, o_ref):
    o_ref[...] = _mm(_silu(c_ref[...]), w_ref[...]) + b_ref[...]


def _ada(c, w_ada, b_ada):
    n, d = c.shape
    dout = w_ada.shape[1]
    tn = 1024
    return pl.pallas_call(
        _ada_kernel,
        grid=(dout // tn,),
        in_specs=[pl.BlockSpec((n, d), lambda j: (0, 0)),
                  pl.BlockSpec((d, tn), lambda j: (0, j)),
                  pl.BlockSpec((1, tn), lambda j: (0, j))],
        out_specs=pl.BlockSpec((n, tn), lambda j: (0, j)),
        out_shape=jax.ShapeDtypeStruct((n, dout), F32),
        compiler_params=_cparams(("arbitrary",)),
        name="ada",
    )(c, w_ada, b_ada.reshape(1, dout))


def _inproj_kernel(x_ref, sc_ref, sh_ref, g_ref, w_ref, z_ref):
    x = x_ref[...]
    h = _rms(x, g_ref[...]) * (1.0 + sc_ref[...]) + sh_ref[...]
    bb, ll, d = x.shape
    z_ref[...] = _mm(h.reshape(bb * ll, d), w_ref[...])


def _inproj(x, sc, sh, g, w):
    b, l, d = x.shape
    n = w.shape[1]
    ll = min(l, TOK_TILE)
    bb = TOK_TILE // ll
    mod_spec = pl.BlockSpec((bb, 1, d), lambda i, j: (i, 0, 0))
    return pl.pallas_call(
        _inproj_kernel,
        grid=(b // bb, l // ll),
        in_specs=[pl.BlockSpec((bb, ll, d), lambda i, j: (i, j, 0)), mod_spec, mod_spec,
                  pl.BlockSpec((1, d), lambda i, j: (0, 0)),
                  pl.BlockSpec((d, n), lambda i, j: (0, 0))],
        out_specs=pl.BlockSpec((bb * ll, n), lambda i, j: (i * (l // ll) + j, 0)),
        out_shape=jax.ShapeDtypeStruct((b * l, n), F32),
        compiler_params=_cparams(("arbitrary", "arbitrary")),
        name="inproj",
    )(x, sc, sh, g, w)


def _cumsum_rows(x):
    n = x.shape[0]
    row = lax.broadcasted_iota(jnp.int32, x.shape, 0)
    s = 1
    while s < n:
        x = x + jnp.where(row >= s, pltpu.roll(x, s, axis=0), 0.0)
        s *= 2
    return x


def _head_norm(y, g):
    mu = jnp.mean(y, axis=-1, keepdims=True)
    yc = y - mu
    var = jnp.mean(yc * yc, axis=-1, keepdims=True)
    return yc * lax.rsqrt(var + NORM_EPS) * g


def _log_sigmoid(x):
    return jnp.minimum(x, 0.0) - jnp.log1p(jnp.exp(-jnp.abs(x)))


def _chunk_step(rows, put, cos2, sin2, c_ref, n_ref, m_ref, conv_ref, s_ref, xcat_ref,
                convw_ref, convb_ref, bg_ref, mgn_ref, rgn_ref, mh, rh, after_head=None):
    cl = CHUNK
    mw = mh * DH
    rw = rh * DH
    o_qk, o_mv, o_mo = 0, 2 * mw, 3 * mw
    o_rq = 4 * mw
    o_rk, o_rv, o_rg, o_gt = o_rq + rw, o_rq + 2 * rw, o_rq + 3 * rw, o_rq + 4 * rw

    xcat_ref[8:8 + cl, :] = rows(o_qk, o_qk + 2 * mw)
    qk = convb_ref[...]
    for t in range(CONV_W):
        qk = qk + xcat_ref[8 - (CONV_W - 1) + t:8 - (CONV_W - 1) + t + cl, :] * convw_ref[t:t + 1, :]
    qk = _silu(qk)
    new_conv = xcat_ref[8 + cl - (CONV_W - 1):8 + cl, :]
    conv_ref[0] = new_conv
    xcat_ref[8 - (CONV_W - 1):8, :] = new_conv

    row_c = lax.broadcasted_iota(jnp.int32, (cl, 1), 0)
    ii = lax.broadcasted_iota(jnp.int32, (cl, cl), 0)
    jj = lax.broadcasted_iota(jnp.int32, (cl, cl), 1)
    causal = jj <= ii
    eye = jj == ii

    def to_row(col):
        return jnp.sum(jnp.where(eye, col, 0.0), axis=0, keepdims=True)

    g = rows(o_gt, o_gt + GATE_LANES) + bg_ref[...]
    bcum = _cumsum_rows(_log_sigmoid(g))
    m_prev = m_ref[0]
    a_all = bcum + m_prev
    lane = lax.broadcasted_iota(jnp.int32, (1, GATE_LANES), 1)
    m_new_row = m_prev

    for h in range(mh):
        q = qk[:, h * DH:(h + 1) * DH]
        k = qk[:, mw + h * DH:mw + (h + 1) * DH] * (DH ** -0.5)
        v = rows(o_mv + h * DH, o_mv + (h + 1) * DH)
        b_col = bcum[:, mh + h:mh + h + 1]
        logi_col = g[:, h:h + 1]
        a_col = a_all[:, mh + h:mh + h + 1]
        dmat = jnp.where(causal, b_col + to_row(logi_col - b_col), -jnp.inf)
        m_t = jnp.maximum(a_col, jnp.max(dmat, axis=1, keepdims=True))
        w_intra = jnp.exp(dmat - m_t)
        w_inter = jnp.exp(a_col - m_t)
        c_old = c_ref[0, h]
        n_old = n_ref[0, h:h + 1, :]
        s = _mm_nt(q, k) * w_intra
        num = _mm(s, v) + w_inter * _mm(q, c_old)
        den = jnp.sum(s, axis=1, keepdims=True) + w_inter * jnp.sum(q * n_old, axis=1, keepdims=True)
        hh = num * (1.0 / jnp.maximum(jnp.abs(den), jnp.exp(-m_t)))
        m_new = m_t[cl - 1:cl, :]
        b_last = b_col[cl - 1:cl, :]
        w_old = jnp.exp(b_last + m_prev[:, mh + h:mh + h + 1] - m_new)
        w_k = jnp.exp(b_last - b_col + logi_col - m_new)
        c_ref[0, h] = w_old * c_old + _mm_tn(k, w_k * v)
        n_ref[0, h:h + 1, :] = w_old * n_old + jnp.sum(w_k * k, axis=0, keepdims=True)
        m_new_row = jnp.where(lane == mh + h, m_new, m_new_row)
        om = _head_norm(hh, mgn_ref[:, h * DH:(h + 1) * DH]) * jax.nn.sigmoid(rows(o_mo + h * DH, o_mo + (h + 1) * DH))
        put(h * DH, (h + 1) * DH, om)
        if after_head is not None:
            after_head(h)
    m_ref[0] = m_new_row

    rel = (ii - jj).astype(F32)
    row_f = row_c.astype(F32)

    def rot(x):
        return x * cos2 + pltpu.roll(x, DH // 2, axis=1) * sin2

    for h in range(rh):
        log_g = math.log1p(-2.0 ** (-5.0 - h))
        q = rot(rows(o_rq + h * DH, o_rq + (h + 1) * DH))
        k = rot(rows(o_rk + h * DH, o_rk + (h + 1) * DH)) * (DH ** -0.5)
        v = rows(o_rv + h * DH, o_rv + (h + 1) * DH)
        decay = jnp.where(causal, jnp.exp(jnp.maximum(rel, 0.0) * log_g), 0.0)
        s_old = s_ref[0, h]
        y = _mm(_mm_nt(q, k) * decay, v) + jnp.exp((row_f + 1.0) * log_g) * _mm(q, s_old)
        k_w = jnp.exp((cl - 1.0 - row_f) * log_g)
        s_ref[0, h] = math.exp(cl * log_g) * s_old + _mm_tn(k, k_w * v)
        orr = _head_norm(y, rgn_ref[:, h * DH:(h + 1) * DH]) * _silu(rows(o_rg + h * DH, o_rg + (h + 1) * DH))
        put(mw + h * DH, mw + (h + 1) * DH, orr)
        if after_head is not None:
            after_head(mh + h)


MIX_ROWS = 256


def _mixer_kernel(x_ref, xn_ref, sc1_ref, sh1_ref, sc1n_ref, sh1n_ref, gt1_ref, sc2_ref, sh2_ref, cos_ref, sin_ref,
                  c0_ref, n0_ref, m0_ref, conv0_ref, s0_ref,
                  gpre1_ref, win_ref, convw_ref, convb_ref, bg_ref, mgn_ref, rgn_ref,
                  gpost_ref, gpre2_ref, wout_ref, wr_hi_ref, wr_lo_ref,
                  x1_ref, hf3_ref, lg_ref, c_ref, n_ref, m_ref, conv_ref, s_ref,
                  z_ref, o_ref, xcat_ref, *, mh, rh):
    i, j = pl.program_id(0), pl.program_id(1)
    step = i * pl.num_programs(1) + j
    cur = step % 2
    rs, d = x_ref.shape[1], x_ref.shape[2]
    nz = z_ref.shape[2]
    nchunks = rs // CHUNK
    npieces = nchunks * (mh + rh)
    cuts = [(nz // LANES * p // npieces) * LANES for p in range(npieces + 1)]

    def project(x3, sc_ref, sh_ref):
        return (_rms(x3, gpre1_ref[...]) * (1.0 + sc_ref[...]) + sh_ref[...]).reshape(rs, d)

    @pl.when(step == 0)
    def _():
        z_ref[0] = _mm(project(x_ref[...], sc1_ref, sh1_ref), win_ref[...])

    @pl.when(j == 0)
    def _():
        c_ref[...] = c0_ref[...]
        n_ref[...] = n0_ref[...]
        m_ref[...] = m0_ref[...]
        s_ref[...] = s0_ref[...]
        xcat_ref[8 - (CONV_W - 1):8, :] = conv0_ref[0]

    h_next = project(xn_ref[...], sc1n_ref, sh1n_ref).astype(MXU_DTYPE)
    for c in range(nchunks):
        r0 = c * CHUNK
        def project_piece(hd, c=c):
            lo, hi = cuts[c * (mh + rh) + hd], cuts[c * (mh + rh) + hd + 1]
            z_ref[1 - cur, :, lo:hi] = jnp.dot(h_next, win_ref[:, lo:hi], preferred_element_type=F32)

        def put(lo, hi, v, r0=r0):
            o_ref[r0:r0 + CHUNK, lo:hi] = v

        _chunk_step(lambda lo, hi, r0=r0: z_ref[cur, r0:r0 + CHUNK, lo:hi], put,
                    cos_ref[r0:r0 + CHUNK, :], sin_ref[r0:r0 + CHUNK, :],
                    c_ref, n_ref, m_ref, conv_ref, s_ref, xcat_ref,
                    convw_ref, convb_ref, bg_ref, mgn_ref, rgn_ref, mh, rh, after_head=project_piece)
    x = x_ref[...]
    _mixout_body(o_ref[...].reshape(1, rs, o_ref.shape[1]), x, gt1_ref, sc2_ref, sh2_ref, gpost_ref, gpre2_ref,
                 wout_ref, wr_hi_ref, wr_lo_ref, x1_ref, hf3_ref, lg_ref)


def _mixer(x, mods, cos2, sin2, c0, n0, m0, conv0, s0, gpre1, win, convw, convb, bg, mgn, rgn,
           gpost, gpre2, wout, wr_hi, wr_lo):
    sh1, sc1, gt1, sh2, sc2, gt2 = mods
    b, l, d = x.shape
    assert d == SUB * LANES and l % MIX_ROWS == 0
    mh, rh = c0.shape[1], s0.shape[1]
    mw, rw = mh * DH, rh * DH
    nz, ne = win.shape[1], wr_hi.shape[1]
    rs = MIX_ROWS
    assert nz % LANES == 0
    nj = l // rs
    bmap4 = lambda i, j: (i, 0, 0, 0)
    bmap3 = lambda i, j: (i, 0, 0)
    tmap = lambda i, j: (i, j, 0)
    fmap = lambda i, j: (i * nj + j, 0)
    once = lambda a: pl.BlockSpec(a.shape, lambda i, j: (0, 0), pipeline_mode=pl.Buffered(1))
    mod_spec = pl.BlockSpec((1, 1, d), bmap3)
    nxt = lambda i, j: jnp.minimum(i * nj + j + 1, b * nj - 1)
    x_next_spec = pl.BlockSpec((1, rs, d), lambda i, j: (nxt(i, j) // nj, nxt(i, j) % nj, 0))
    mod_next_spec = pl.BlockSpec((1, 1, d), lambda i, j: (nxt(i, j) // nj, 0, 0))
    rope_spec = pl.BlockSpec((rs, DH), lambda i, j: (j, 0))
    state_specs = [pl.BlockSpec((1, mh, DH, DH), bmap4), pl.BlockSpec((1, mh, DH), bmap3),
                   pl.BlockSpec((1, 1, GATE_LANES), bmap3), pl.BlockSpec((1, CONV_W - 1, 2 * mw), bmap3),
                   pl.BlockSpec((1, rh, DH, DH), bmap4)]
    state_shapes = [jax.ShapeDtypeStruct(c0.shape, F32), jax.ShapeDtypeStruct(n0.shape, F32),
                    jax.ShapeDtypeStruct(m0.shape, F32), jax.ShapeDtypeStruct(conv0.shape, F32),
                    jax.ShapeDtypeStruct(s0.shape, F32)]
    return pl.pallas_call(
        functools.partial(_mixer_kernel, mh=mh, rh=rh),
        grid=(b, nj),
        in_specs=[pl.BlockSpec((1, rs, d), tmap), x_next_spec, mod_spec, mod_spec, mod_next_spec, mod_next_spec]
                 + [mod_spec] * 3 + [rope_spec, rope_spec] + state_specs
                 + [once(a) for a in (gpre1, win, convw, convb, bg, mgn, rgn, gpost, gpre2, wout, wr_hi, wr_lo)],
        out_specs=[pl.BlockSpec((1, rs, d), tmap), pl.BlockSpec((rs * SUB, LANES), fmap),
                   pl.BlockSpec((rs, ne), fmap)] + state_specs,
        out_shape=[jax.ShapeDtypeStruct((b, l, d), F32), jax.ShapeDtypeStruct((b * l * SUB, LANES), F32),
                   jax.ShapeDtypeStruct((b * l, ne), F32)] + state_shapes,
        scratch_shapes=[pltpu.VMEM((2, rs, nz), F32), pltpu.VMEM((rs, mw + rw), F32),
                        pltpu.VMEM((CHUNK + 8, 2 * mw), F32)],
        compiler_params=_cparams(("arbitrary", "arbitrary")),
        name="mixer",
    )(x, x, sc1, sh1, sc1, sh1, gt1, sc2, sh2, cos2, sin2, c0, n0, m0, conv0, s0,
      gpre1, win, convw, convb, bg, mgn, rgn, gpost, gpre2, wout, wr_hi, wr_lo)


PACK_SEQS = CHUNK // SUB


def _mixcore_packed_kernel(z_ref, cos_ref, sin_ref, c0_ref, n0_ref, m0_ref, conv0_ref, s0_ref,
                           convw_ref, convb_ref, bg_ref, mgn_ref, rgn_ref,
                           out_ref, c_ref, n_ref, m_ref, conv_ref, s_ref, *, mh, rh):
    g_n, l_n, _ = z_ref.shape
    rr = g_n * l_n
    mw, rw = mh * DH, rh * DH
    o_qk, o_mv, o_mo = 0, 2 * mw, 3 * mw
    o_rq = 4 * mw
    o_rk, o_rv, o_rg, o_gt = o_rq + rw, o_rq + 2 * rw, o_rq + 3 * rw, o_rq + 4 * rw

    def rows(lo, hi):
        return z_ref[:, :, lo:hi].reshape(rr, hi - lo)

    def per_seq(x):
        return jnp.broadcast_to(x[:, None, :], (g_n, l_n, x.shape[-1])).reshape(rr, x.shape[-1])

    def last_of_seq(x):
        x3 = x.reshape(g_n, l_n, x.shape[-1])
        return jnp.broadcast_to(x3[:, l_n - 1:l_n, :], x3.shape).reshape(x.shape)

    def full(col):
        return jnp.broadcast_to(col, (rr, DH))

    pos_c = lax.broadcasted_iota(jnp.int32, (rr, 1), 0) % l_n
    ii = lax.broadcasted_iota(jnp.int32, (rr, rr), 0)
    jj = lax.broadcasted_iota(jnp.int32, (rr, rr), 1)
    mask = ((ii // l_n) == (jj // l_n)) & (jj <= ii)
    eye = jj == ii

    def to_row(col):
        return jnp.sum(jnp.where(eye, col, 0.0), axis=0, keepdims=True)

    x = rows(o_qk, o_qk + 2 * mw)
    e = conv0_ref[...].reshape(rr, 2 * mw)
    qk = convb_ref[...]
    for t in range(CONV_W):
        d = CONV_W - 1 - t
        xd = x if d == 0 else jnp.where(pos_c >= d, pltpu.roll(x, d, axis=0),
                                        pltpu.roll(e, (d - (CONV_W - 1)) % rr, axis=0))
        qk = qk + xd * convw_ref[t:t + 1, :]
    qk = _silu(qk)
    conv_ref[...] = x.reshape(g_n, l_n, 2 * mw)[:, l_n - (CONV_W - 1):, :]

    g = rows(o_gt, o_gt + GATE_LANES) + bg_ref[...]
    bcum = _log_sigmoid(g)
    s_ = 1
    while s_ < l_n:
        bcum = bcum + jnp.where(pos_c >= s_, pltpu.roll(bcum, s_, axis=0), 0.0)
        s_ *= 2
    m_prev = per_seq(m0_ref[:, 0, :])
    a_all = bcum + m_prev
    lane = lax.broadcasted_iota(jnp.int32, (1, GATE_LANES), 1)
    even_seq = (lax.broadcasted_iota(jnp.int32, (rr, 1), 0) // l_n) % 2 == 0
    m_new_rows = m_prev

    def state_dot(qb, st_ref, h):
        parts = []
        for s in range(g_n):
            pb, lo = s // 2, (s % 2) * l_n
            res = jnp.dot(qb[pb * 2 * l_n:(pb + 1) * 2 * l_n], st_ref[s, h].astype(MXU_DTYPE),
                          preferred_element_type=F32)
            parts.append(res[lo:lo + l_n])
        return jnp.concatenate(parts, axis=0)

    def state_update(k, wv, decay_full, st_in, st_out, h):
        k_even = jnp.where(even_seq, k, 0.0).astype(MXU_DTYPE)
        k_odd = jnp.where(even_seq, 0.0, k).astype(MXU_DTYPE)
        wvb = wv.astype(MXU_DTYPE)
        for s in range(g_n):
            pb = s // 2
            kp = (k_even if s % 2 == 0 else k_odd)[pb * 2 * l_n:(pb + 1) * 2 * l_n]
            upd = lax.dot_general(kp, wvb[pb * 2 * l_n:(pb + 1) * 2 * l_n], (((0,), (0,)), ((), ())),
                                  preferred_element_type=F32)
            st_out[s, h] = decay_full[s * l_n:s * l_n + 1, :] * st_in[s, h] + upd

    for h in range(mh):
        q = qk[:, h * DH:(h + 1) * DH]
        k = qk[:, mw + h * DH:mw + (h + 1) * DH] * (DH ** -0.5)
        v = rows(o_mv + h * DH, o_mv + (h + 1) * DH)
        b_col = bcum[:, mh + h:mh + h + 1]
        logi_col = g[:, h:h + 1]
        a_col = a_all[:, mh + h:mh + h + 1]
        dmat = jnp.where(mask, b_col + to_row(logi_col - b_col), -jnp.inf)
        m_t = jnp.maximum(a_col, jnp.max(dmat, axis=1, keepdims=True))
        w_intra = jnp.exp(dmat - m_t)
        w_inter = jnp.exp(a_col - m_t)
        n_rows = per_seq(n0_ref[:, h, :])
        s = _mm_nt(q, k) * w_intra
        num = _mm(s, v) + w_inter * state_dot(q.astype(MXU_DTYPE), c0_ref, h)
        den = jnp.sum(s, axis=1, keepdims=True) + w_inter * jnp.sum(q * n_rows, axis=1, keepdims=True)
        hh = num * (1.0 / jnp.maximum(jnp.abs(den), jnp.exp(-m_t)))
        m_new = last_of_seq(full(m_t))
        b_last = last_of_seq(full(b_col))
        w_old = jnp.exp(b_last + full(m_prev[:, mh + h:mh + h + 1]) - m_new)
        w_k = jnp.exp(b_last - full(b_col) + full(logi_col) - m_new)
        state_update(k, w_k * v, w_old, c0_ref, c_ref, h)
        n_ref[:, h, :] = (w_old.reshape(g_n, l_n, DH)[:, 0, :] * n0_ref[:, h, :]
                          + jnp.sum((w_k * k).reshape(g_n, l_n, DH), axis=1))
        m_new_rows = jnp.where(lane == mh + h, m_new, m_new_rows)
        om = _head_norm(hh, mgn_ref[:, h * DH:(h + 1) * DH]) * jax.nn.sigmoid(rows(o_mo + h * DH, o_mo + (h + 1) * DH))
        out_ref[:, :, h * DH:(h + 1) * DH] = om.reshape(g_n, l_n, DH)
    m_ref[...] = m_new_rows.reshape(g_n, l_n, GATE_LANES)[:, 0:1, :]

    cos2 = cos_ref[...]
    sin2 = sin_ref[...]
    rel = (ii - jj).astype(F32)
    pos_f = pos_c.astype(F32)

    def rot(x):
        return x * cos2 + pltpu.roll(x, DH // 2, axis=1) * sin2

    for h in range(rh):
        log_g = math.log1p(-2.0 ** (-5.0 - h))
        q = rot(rows(o_rq + h * DH, o_rq + (h + 1) * DH))
        k = rot(rows(o_rk + h * DH, o_rk + (h + 1) * DH)) * (DH ** -0.5)
        v = rows(o_rv + h * DH, o_rv + (h + 1) * DH)
        decay = jnp.where(mask, jnp.exp(jnp.maximum(rel, 0.0) * log_g), 0.0)
        y = (_mm(_mm_nt(q, k) * decay, v)
             + jnp.exp((pos_f + 1.0) * log_g) * state_dot(q.astype(MXU_DTYPE), s0_ref, h))
        k_w = jnp.exp((l_n - 1.0 - pos_f) * log_g)
        state_update(k, k_w * v, jnp.full((rr, DH), math.exp(l_n * log_g), F32), s0_ref, s_ref, h)
        orr = _head_norm(y, rgn_ref[:, h * DH:(h + 1) * DH]) * _silu(rows(o_rg + h * DH, o_rg + (h + 1) * DH))
        out_ref[:, :, mw + h * DH:mw + (h + 1) * DH] = orr.reshape(g_n, l_n, DH)


def _mixcore_packed(z, cos2, sin2, c0, n0, m0, conv0, s0, convw, convb, bg, mgn, rgn):
    b, l, nz = z.shape
    assert l == SUB and b % PACK_SEQS == 0
    mh, rh = c0.shape[1], s0.shape[1]
    mw, rw = mh * DH, rh * DH
    g_n = PACK_SEQS
    conv0p = jnp.pad(conv0, ((0, 0), (0, l - (CONV_W - 1)), (0, 0)))
    tile = lambda t: jnp.tile(t[:l], (g_n, 1))
    bmap4 = lambda i: (i, 0, 0, 0)
    bmap3 = lambda i: (i, 0, 0)
    wmap = lambda i: (0, 0)
    c_spec, n_spec = pl.BlockSpec((g_n, mh, DH, DH), bmap4), pl.BlockSpec((g_n, mh, DH), bmap3)
    m_spec, s_spec = pl.BlockSpec((g_n, 1, GATE_LANES), bmap3), pl.BlockSpec((g_n, rh, DH, DH), bmap4)
    return pl.pallas_call(
        functools.partial(_mixcore_packed_kernel, mh=mh, rh=rh),
        grid=(b // g_n,),
        in_specs=[pl.BlockSpec((g_n, l, nz), bmap3),
                  pl.BlockSpec((g_n * l, DH), wmap), pl.BlockSpec((g_n * l, DH), wmap),
                  c_spec, n_spec, m_spec, pl.BlockSpec((g_n, l, 2 * mw), bmap3), s_spec,
                  pl.BlockSpec(convw.shape, wmap), pl.BlockSpec(convb.shape, wmap), pl.BlockSpec(bg.shape, wmap),
                  pl.BlockSpec(mgn.shape, wmap), pl.BlockSpec(rgn.shape, wmap)],
        out_specs=[pl.BlockSpec((g_n, l, mw + rw), bmap3), c_spec, n_spec, m_spec,
                   pl.BlockSpec((g_n, CONV_W - 1, 2 * mw), bmap3), s_spec],
        out_shape=[jax.ShapeDtypeStruct((b, l, mw + rw), F32), jax.ShapeDtypeStruct(c0.shape, F32),
                   jax.ShapeDtypeStruct(n0.shape, F32), jax.ShapeDtypeStruct(m0.shape, F32),
                   jax.ShapeDtypeStruct(conv0.shape, F32), jax.ShapeDtypeStruct(s0.shape, F32)],
        compiler_params=_cparams(("arbitrary",)),
        name="mixcore_packed",
    )(z, tile(cos2), tile(sin2), c0, n0, m0, conv0p, s0, convw, convb, bg, mgn, rgn)


def _split_hi_lo(x):
    hi = x.astype(MXU_DTYPE)
    lo = (x - hi.astype(F32)).astype(MXU_DTYPE)
    return hi, lo


def _to_row_tiles(ref, x):
    n = x.shape[0]
    for j in range(SUB):
        ref[pl.ds(j, n, stride=SUB), :] = x[:, j * LANES:(j + 1) * LANES]


def _from_row_tiles(ref, n):
    return jnp.concatenate([ref[pl.ds(j, n, stride=SUB), :] for j in range(SUB)], axis=1)


def _mixout_kernel(o_ref, x_ref, gt1_ref, sc2_ref, sh2_ref, gpost_ref, gpre_ref, wout_ref, wr_hi_ref, wr_lo_ref,
                   x1_ref, hf3_ref, lg_ref):
    _mixout_body(o_ref[...], x_ref[...], gt1_ref, sc2_ref, sh2_ref, gpost_ref, gpre_ref, wout_ref, wr_hi_ref,
                 wr_lo_ref, x1_ref, hf3_ref, lg_ref)


def _mixout_body(o, x, gt1_ref, sc2_ref, sh2_ref, gpost_ref, gpre_ref, wout_ref, wr_hi_ref, wr_lo_ref,
                 x1_ref, hf3_ref, lg_ref):
    bb, ll, d = x.shape
    ym = _mm(o.reshape(bb * ll, o.shape[-1]), wout_ref[...]).reshape(bb, ll, d)
    x1 = x + gt1_ref[...] * _rms(ym, gpost_ref[...])
    x1_ref[...] = x1
    hf = (_rms(x1, gpre_ref[...]) * (1.0 + sc2_ref[...]) + sh2_ref[...]).reshape(bb * ll, d)
    _to_row_tiles(hf3_ref, hf)
    hi, lo = _split_hi_lo(hf)
    lg_ref[...] = _mm(hi, wr_hi_ref[...]) + (_mm(lo, wr_hi_ref[...]) + _mm(hi, wr_lo_ref[...]))


def _mixout(o, x, gt1, sc2, sh2, gpost, gpre, wout, wr_hi, wr_lo):
    b, l, d = x.shape
    assert d == SUB * LANES
    w = o.shape[-1]
    ne = wr_hi.shape[1]
    ll = min(l, TOK_TILE)
    bb = TOK_TILE // ll
    tmap = lambda i, j: (i, j, 0)
    fmap = lambda i, j: (i * (l // ll) + j, 0)
    mod_spec = pl.BlockSpec((bb, 1, d), lambda i, j: (i, 0, 0))
    wmap = lambda i, j: (0, 0)
    return pl.pallas_call(
        _mixout_kernel,
        grid=(b // bb, l // ll),
        in_specs=[pl.BlockSpec((bb, ll, w), tmap), pl.BlockSpec((bb, ll, d), tmap), mod_spec, mod_spec, mod_spec,
                  pl.BlockSpec((1, d), wmap), pl.BlockSpec((1, d), wmap), pl.BlockSpec((w, d), wmap),
                  pl.BlockSpec((d, ne), wmap), pl.BlockSpec((d, ne), wmap)],
        out_specs=[pl.BlockSpec((bb, ll, d), tmap), pl.BlockSpec((bb * ll * SUB, LANES), fmap),
                   pl.BlockSpec((bb * ll, ne), fmap)],
        out_shape=[jax.ShapeDtypeStruct((b, l, d), F32), jax.ShapeDtypeStruct((b * l * SUB, LANES), F32),
                   jax.ShapeDtypeStruct((b * l, ne), F32)],
        compiler_params=_cparams(("arbitrary", "arbitrary")),
        name="mixout",
    )(o, x, gt1, sc2, sh2, gpost, gpre, wout, wr_hi, wr_lo)


def _router_kernel(lg_ref, br_ref, lidx_ref, wts_ref, cnt_ref, seg_ref, *, ne):
    tm = lg_ref.shape[0]
    gsz = ne // N_GROUPS
    s = jax.nn.sigmoid(lg_ref[...].T[:ne, :])
    sb = s + br_ref[...]
    sb3 = sb.reshape(N_GROUPS, gsz, tm)
    e3 = lax.broadcasted_iota(jnp.int32, sb3.shape, 1)
    m1 = jnp.max(sb3, axis=1, keepdims=True)
    first = jnp.min(jnp.where(sb3 == m1, e3, gsz), axis=1, keepdims=True)
    m2 = jnp.max(jnp.where(e3 == first, -jnp.inf, sb3), axis=1, keepdims=True)
    gs = (m1 + m2).reshape(N_GROUPS, tm)
    gi = lax.broadcasted_iota(jnp.int32, gs.shape, 0)
    grank = jnp.zeros(gs.shape, F32)
    for g in range(N_GROUPS):
        r = gs[g:g + 1, :]
        grank = grank + jnp.where(r > gs, 1.0, jnp.where(r == gs, jnp.where(g < gi, 1.0, 0.0), 0.0))
    gsel = jnp.where(grank < TOPK_GROUPS, 1.0, 0.0)
    emask = jnp.broadcast_to(gsel.reshape(N_GROUPS, 1, tm), sb3.shape).reshape(ne, tm) > 0.0
    masked = jnp.where(emask, sb, -jnp.inf)
    ei = lax.broadcasted_iota(jnp.int32, masked.shape, 0)
    cur = masked
    sel_f = jnp.zeros(masked.shape, F32)
    for _ in range(TOP_K):
        best = jnp.max(cur, axis=0, keepdims=True)
        first_e = jnp.min(jnp.where(cur == best, ei, ne), axis=0, keepdims=True)
        pick = ei == first_e
        sel_f = jnp.where(pick, 1.0, sel_f)
        cur = jnp.where(pick, -jnp.inf, cur)
    sel = sel_f > 0.0
    w = jnp.where(sel, s, 0.0)
    w = w / jnp.sum(w, axis=0, keepdims=True) * ROUTED_SCALE

    self = jnp.where(sel, 1.0, 0.0)
    er = lax.broadcasted_iota(jnp.int32, (ne, ne), 0)
    ec = lax.broadcasted_iota(jnp.int32, (ne, ne), 1)
    slot = _mm(jnp.where(ec < er, 1.0, 0.0), self)
    tr = lax.broadcasted_iota(jnp.int32, (tm, tm), 0)
    tc = lax.broadcasted_iota(jnp.int32, (tm, tm), 1)
    pos = _mm(self, jnp.where(tr < tc, 1.0, 0.0))
    seg = jnp.sum(slot, axis=1, keepdims=True)
    lrow = seg + pos
    lidx_rows, w_rows = [], []
    for k in range(TOP_K):
        pick = sel & (slot == float(k))
        lidx_rows.append(jnp.sum(jnp.where(pick, lrow, 0.0), axis=0, keepdims=True))
        w_rows.append(jnp.sum(jnp.where(pick, w, 0.0), axis=0, keepdims=True))
    lidx_ref[...] = jnp.concatenate(lidx_rows, axis=0).astype(jnp.int32)
    wts_ref[...] = jnp.concatenate(w_rows, axis=0)
    ones = jnp.ones((SUB, tm), F32)
    padz = jnp.zeros((LANES - ne, tm), F32)
    cnt_ref[...] = _mm_nt(ones, jnp.concatenate([self, padz], axis=0)).astype(jnp.int32)
    seg_ref[...] = _mm_nt(ones, jnp.concatenate([slot, padz], axis=0)).astype(jnp.int32)


def _router(logits, b_router, ne):
    t, lanes = logits.shape
    tm = ROUTE_TILE
    nt = t // tm
    kspec = pl.BlockSpec((TOP_K, tm), lambda i: (0, i))
    cspec = pl.BlockSpec((SUB, LANES), lambda i: (i, 0))
    return pl.pallas_call(
        functools.partial(_router_kernel, ne=ne),
        grid=(nt,),
        in_specs=[pl.BlockSpec((tm, lanes), lambda i: (i, 0)), pl.BlockSpec((ne, 1), lambda i: (0, 0))],
        out_specs=[kspec, kspec, cspec, cspec],
        out_shape=[jax.ShapeDtypeStruct((TOP_K, t), jnp.int32), jax.ShapeDtypeStruct((TOP_K, t), F32),
                   jax.ShapeDtypeStruct((nt * SUB, LANES), jnp.int32),
                   jax.ShapeDtypeStruct((nt * SUB, LANES), jnp.int32)],
        compiler_params=_cparams(("arbitrary",)),
        name="router",
    )(logits, b_router.reshape(ne, 1))


SEG_BITS = ROUTE_TILE.bit_length()
PAD_BITS = (MOE_TILE - 1).bit_length()


RARE_BIT = 7


def _for_each_piece(n, fn, bits):
    def pieces(lo, hi):
        for bit in range(hi - 1, lo - 1, -1):
            @pl.when(((n >> bit) & 1) == 1)
            def _():
                fn((n >> (bit + 1)) << (bit + 1), 1 << bit)

    if bits > RARE_BIT:
        @pl.when(n >= (1 << RARE_BIT))
        def _():
            pieces(RARE_BIT, bits)
    pieces(0, min(bits, RARE_BIT))


SORT_UNROLL = 4


def _zero_rows(ps_ref, pn_ref, tail_ref, xs_ref, z_ref, sem, ne, act):
    def pad_body(e, carry):
        _for_each_piece(pn_ref[e], lambda off, size: act(
            pltpu.make_async_copy(z_ref.at[pl.ds(0, size)], xs_ref.at[pl.ds(ps_ref[e] + off, size)], sem)),
            PAD_BITS)
        return carry
    lax.fori_loop(0, ne, pad_body, 0)

    def tail_body(b, carry):
        act(pltpu.make_async_copy(z_ref, xs_ref.at[pl.ds(tail_ref[0] + b * MOE_TILE, MOE_TILE)], sem))
        return carry
    lax.fori_loop(0, tail_ref[1], tail_body, 0)


def _dispatch_kernel(cnt_ref, seg_ref, gb_ref, ps_ref, pn_ref, tail_ref, lidx_p_ref, lidx_s_ref, hf3_p_ref, hf3_s_ref,
                     xs_ref, buf0, buf1, z_ref, sems, *, ntp, ne):
    i = pl.program_id(0)
    n = pl.num_programs(0)
    tt = lidx_p_ref.shape[0] // TOP_K

    def whole(buf, sem):
        return pltpu.make_async_copy(buf, xs_ref.at[pl.ds(0, tt * TOP_K)], sem)

    @pl.when(i == 0)
    def _():
        z_ref[...] = jnp.zeros(z_ref.shape, F32)
        _zero_rows(ps_ref, pn_ref, tail_ref, xs_ref, z_ref, sems.at[2], ne, lambda c: c.start())

    def run(buf, sem):
        @pl.when(i >= 2)
        def _():
            whole(buf, sem).wait()

        def sort_from(hf3_ref, lidx_ref):
            def sort_body(tb, carry):
                for u in range(SORT_UNROLL):
                    t = tb * SORT_UNROLL + u
                    v = hf3_ref[pl.ds(pl.multiple_of(t * SUB, SUB), SUB), :]
                    for k in range(TOP_K):
                        buf[lidx_ref[t * TOP_K + k]] = v
                return carry
            lax.fori_loop(0, tt // SORT_UNROLL, sort_body, 0)

        @pl.when(i < ntp)
        def _():
            sort_from(hf3_p_ref, lidx_p_ref)

        @pl.when(i >= ntp)
        def _():
            sort_from(hf3_s_ref, lidx_s_ref)

        def seg_body(e, carry):
            c, s, g = cnt_ref[i * ne + e], seg_ref[i * ne + e], gb_ref[i * ne + e]
            _for_each_piece(c, lambda off, size: pltpu.make_async_copy(
                buf.at[pl.ds(s + off, size)], xs_ref.at[pl.ds(g + off, size)], sem).start(
                    priority=size.bit_length() % 2), SEG_BITS)
            return carry
        lax.fori_loop(0, ne, seg_body, 0)

    @pl.when(i % 2 == 0)
    def _():
        run(buf0, sems.at[0])

    @pl.when(i % 2 == 1)
    def _():
        run(buf1, sems.at[1])

    @pl.when(i == 0)
    def _():
        _zero_rows(ps_ref, pn_ref, tail_ref, xs_ref, z_ref, sems.at[2], ne, lambda c: c.wait())

    @pl.when(i == n - 1)
    def _():
        @pl.when(i % 2 == 0)
        def _():
            whole(buf0, sems.at[0]).wait()

            @pl.when(i >= 1)
            def _():
                whole(buf1, sems.at[1]).wait()

        @pl.when(i % 2 == 1)
        def _():
            whole(buf1, sems.at[1]).wait()
            whole(buf0, sems.at[0]).wait()


def _dispatch(cnt, seg, gbase, pad_start, pad_n, tail, lidx_p, lidx_s, hf3_p, hf3_s, rows, ne):
    tt = ROUTE_TILE
    ntp, nts =lidx_p.shape[0] // (tt * TOP_K), lidx_s.shape[0] // (tt * TOP_K)
    pmap = lambda i, *_: jnp.minimum(i, ntp - 1)
    smap = lambda i, *_: jnp.maximum(i - ntp, 0)
    return pl.pallas_call(
        functools.partial(_dispatch_kernel, ntp=ntp, ne=ne),
        grid_spec=pltpu.PrefetchScalarGridSpec(
            num_scalar_prefetch=6, grid=(ntp + nts,),
            in_specs=[pl.BlockSpec((tt * TOP_K,), lambda i, *_: (pmap(i),), memory_space=pltpu.SMEM),
                      pl.BlockSpec((tt * TOP_K,), lambda i, *_: (smap(i),), memory_space=pltpu.SMEM),
                      pl.BlockSpec((tt * SUB, LANES), lambda i, *_: (pmap(i), 0)),
                      pl.BlockSpec((tt * SUB, LANES), lambda i, *_: (smap(i), 0))],
            out_specs=pl.BlockSpec(memory_space=pl.ANY),
            scratch_shapes=[pltpu.VMEM((tt * TOP_K, SUB, LANES), F32), pltpu.VMEM((tt * TOP_K, SUB, LANES), F32),
                            pltpu.VMEM((MOE_TILE, SUB, LANES), F32), pltpu.SemaphoreType.DMA((3,))]),
        out_shape=jax.ShapeDtypeStruct((rows, SUB, LANES), F32),
        compiler_params=_cparams(("arbitrary",)),
        name="dispatch",
    )(cnt, seg, gbase, pad_start, pad_n, tail, lidx_p, lidx_s, hf3_p, hf3_s)


def _ffn_kernel(be_ref, nbu_ref, xs_ref, wg_ref, wu_ref, wd_ref, ys_ref, wg_s, wu_s, wd_s):
    b = pl.program_id(0)
    tm = xs_ref.shape[0] // SUB

    @pl.when(b < nbu_ref[0])
    def _():
        @pl.when((b == 0) | (be_ref[b] != be_ref[jnp.maximum(b - 1, 0)]))
        def _():
            wg_s[...] = wg_ref[0].astype(MXU_DTYPE)
            wu_s[...] = wu_ref[0].astype(MXU_DTYPE)
            wd_s[...] = wd_ref[0].astype(MXU_DTYPE)

        x = _from_row_tiles(xs_ref, tm)
        hb = _silu(_mm(x, wg_s[...])) * _mm(x, wu_s[...])
        _to_row_tiles(ys_ref, _mm(hb, wd_s[...]))

    @pl.when(b >= nbu_ref[0])
    def _():
        ys_ref[...] = jnp.zeros(ys_ref.shape, F32)


def _ffn(block_e, nb_used, xs2, wg, wu, wd):
    rows = xs2.shape[0] // SUB
    d, f = wg.shape[1], wg.shape[2]
    tm = MOE_TILE
    nb = rows // tm
    xmap = lambda b, be, nbu: (jnp.minimum(b, nbu[0] - 1), 0)
    wmap = lambda b, be, nbu: (be[b], 0, 0)
    return pl.pallas_call(
        _ffn_kernel,
        grid_spec=pltpu.PrefetchScalarGridSpec(
            num_scalar_prefetch=2,
            grid=(nb,),
            in_specs=[pl.BlockSpec((tm * SUB, LANES), xmap), pl.BlockSpec((1, d, f), wmap),
                      pl.BlockSpec((1, d, f), wmap), pl.BlockSpec((1, f, d), wmap)],
            out_specs=pl.BlockSpec((tm * SUB, LANES), lambda b, be, nbu: (b, 0)),
            scratch_shapes=[pltpu.VMEM((d, f), MXU_DTYPE), pltpu.VMEM((d, f), MXU_DTYPE),
                            pltpu.VMEM((f, d), MXU_DTYPE)]),
        out_shape=jax.ShapeDtypeStruct((rows * SUB, LANES), F32),
        compiler_params=_cparams(("arbitrary",)),
        name="ffn",
    )(block_e, nb_used, xs2, wg, wu, wd)


def _combine_kernel(cnt_ref, seg_ref, gb_ref, lidx_ref, wts_ref, hf3_ref, x1_ref, gt2_ref, gpost_ref,
                    wsg_ref, wsu_ref, wsd_ref, ys_ref, y_ref, buf0, buf1, y3_ref, sems, *, tile0, ne):
    bb, ll, d = x1_ref.shape
    tt = bb * ll
    i = pl.program_id(0)
    n = pl.num_programs(0)

    def fetch(j, buf, sem):
        def seg_body(e, carry):
            c, s, g = cnt_ref[j * ne + e], seg_ref[j * ne + e], gb_ref[j * ne + e]
            _for_each_piece(c, lambda off, size: pltpu.make_async_copy(
                ys_ref.at[pl.ds(g + off, size)], buf.at[pl.ds(s + off, size)], sem).start(
                    priority=size.bit_length() % 2), SEG_BITS)
            return carry
        lax.fori_loop(0, ne, seg_body, 0)

    def run(buf, sem, nbuf, nsem):
        @pl.when(i == 0)
        def _():
            fetch(tile0, buf, sem)

        @pl.when(i + 1 < n)
        def _():
            fetch(tile0 + i + 1, nbuf, nsem)

        pltpu.make_async_copy(ys_ref.at[pl.ds(0, tt * TOP_K)], buf, sem).wait()

        def tok_body(tb, carry):
            for u in range(SORT_UNROLL):
                t = tb * SORT_UNROLL + u
                acc = wts_ref[t * TOP_K] * buf[lidx_ref[t * TOP_K]]
                for k in range(1, TOP_K):
                    acc = acc + wts_ref[t * TOP_K + k] * buf[lidx_ref[t * TOP_K + k]]
                y3_ref[pl.ds(pl.multiple_of(t * SUB, SUB), SUB), :] = acc
            return carry
        lax.fori_loop(0, tt // SORT_UNROLL, tok_body, 0)

    @pl.when(i % 2 == 0)
    def _():
        run(buf0, sems.at[0], buf1, sems.at[1])

    @pl.when(i % 2 == 1)
    def _():
        run(buf1, sems.at[1], buf0, sems.at[0])

    hf = _from_row_tiles(hf3_ref, tt)
    shared = _mm(_silu(_mm(hf, wsg_ref[...])) * _mm(hf, wsu_ref[...]), wsd_ref[...])
    yf = (_from_row_tiles(y3_ref, tt) + shared).reshape(bb, ll, d)
    y_ref[...] = x1_ref[...] + gt2_ref[...] * _rms(yf, gpost_ref[...])


def _combine(cnt, seg, gbase, lidx, wts, hf3, x1, gt2, gpost, wsg, wsu, wsd, ys, tile0, ne):
    b, l, d = x1.shape
    tt = ROUTE_TILE
    ll = min(l, tt)
    bb = tt // ll
    nl = l // ll
    nt = (b // bb) * nl
    fs = wsg.shape[1]
    if nl == 1:
        tmap = lambda i, *_: (i, 0, 0)
        bmap = lambda i, *_: (i, 0, 0)
    else:
        tmap = lambda i, *_: (i // nl, i % nl, 0)
        bmap = lambda i, *_: (i // nl, 0, 0)
    wmap = lambda i, *_: (0, 0)
    kspec = lambda: pl.BlockSpec((tt * TOP_K,), lambda i, *_: (i,), memory_space=pltpu.SMEM)
    return pl.pallas_call(
        functools.partial(_combine_kernel, tile0=tile0, ne=ne),
        grid_spec=pltpu.PrefetchScalarGridSpec(
            num_scalar_prefetch=3, grid=(nt,),
            in_specs=[kspec(), kspec(),
                      pl.BlockSpec((tt * SUB, LANES), lambda i, *_: (i, 0)),
                      pl.BlockSpec((bb, ll, d), tmap), pl.BlockSpec((bb, 1, d), bmap),
                      pl.BlockSpec((1, d), wmap), pl.BlockSpec((d, fs), wmap), pl.BlockSpec((d, fs), wmap),
                      pl.BlockSpec((fs, d), wmap),
                      pl.BlockSpec(memory_space=pl.ANY)],
            out_specs=pl.BlockSpec((bb, ll, d), tmap),
            scratch_shapes=[pltpu.VMEM((tt * TOP_K, SUB, LANES), F32), pltpu.VMEM((tt * TOP_K, SUB, LANES), F32),
                            pltpu.VMEM((tt * SUB, LANES), F32), pltpu.SemaphoreType.DMA((2,))]),
        out_shape=jax.ShapeDtypeStruct((b, l, d), F32),
        compiler_params=_cparams(("arbitrary",), VMEM_LIMIT_COMBINE),
        name="combine",
    )(cnt, seg, gbase, lidx, wts, hf3, x1, gt2, gpost, wsg, wsu, wsd, ys)


def _rope_tables(pos, rows):
    half = DH // 2
    freqs = ROPE_BASE ** (-jnp.arange(half, dtype=F32) / half)
    ang = pos.astype(F32)[:, None] * freqs[None, :]
    cos, sin = jnp.cos(ang), jnp.sin(ang)
    cos2 = jnp.concatenate([cos, cos], axis=1)
    sin2 = jnp.concatenate([-sin, sin], axis=1)
    padr = rows - pos.shape[0]
    return jnp.pad(cos2, ((0, padr), (0, 0))), jnp.pad(sin2, ((0, padr), (0, 0)))


def _global_plan(cnt_tiles, n_assign, ne):
    counts = jnp.sum(cnt_tiles, axis=0)
    padded = (counts + MOE_TILE - 1) // MOE_TILE * MOE_TILE
    pend = jnp.cumsum(padded)
    base = pend - padded
    gbase = base[None, :] + jnp.cumsum(cnt_tiles, axis=0) - cnt_tiles
    nb = n_assign // MOE_TILE + ne
    starts = jnp.arange(nb, dtype=jnp.int32) * MOE_TILE
    block_e = jnp.minimum(jnp.sum((pend[None, :] <= starts[:, None]).astype(jnp.int32), axis=1), ne - 1)
    nb_used = pend[-1:] // MOE_TILE
    tail = jnp.concatenate([pend[-1:], nb - nb_used])
    i32 = lambda v: v.astype(jnp.int32)
    return i32(gbase.reshape(-1)), i32(block_e), i32(nb_used), i32(base + counts), i32(padded - counts), i32(tail), nb


def kernel(x_prompt, x_sample, c_prompt, c_sample, state_mlstm_C, state_mlstm_n, state_mlstm_m, state_mlstm_conv, state_ret_S, w_ada, b_ada, g_mix_pre, g_mix_post, g_ffn_pre, g_ffn_post, w_in, b_gates, conv_w, conv_b, m_gn, r_gn, w_out, w_router, b_router, w_exp_gate, w_exp_up, w_exp_down, w_sh_gate, w_sh_up, w_sh_down):
    depth = w_ada.shape[0]
    bp, lp, d = x_prompt.shape
    bs, ls, _ = x_sample.shape
    mh, rh = state_mlstm_C.shape[2], state_ret_S.shape[2]
    mw, rw = mh * DH, rh * DH
    ne = w_router.shape[2]
    tp, ts = bp * lp, bs * ls
    ntp = tp // ROUTE_TILE

    cos_p, sin_p = _rope_tables(jnp.arange(lp, dtype=jnp.int32), lp)
    cos_s, sin_s = _rope_tables(PAST_LEN + jnp.arange(ls, dtype=jnp.int32), CHUNK if ls % CHUNK else ls)

    hp, hs = x_prompt, x_sample
    new_p, new_s = [], []
    for l in range(depth):
        wi = w_in[l]
        g0, nmain = 4 * mw, 4 * mw + 4 * rw
        w_in_r = jnp.zeros((d, nmain + GATE_LANES), MXU_DTYPE)
        w_in_r = w_in_r.at[:, :g0].set(wi[:, :g0].astype(MXU_DTYPE))
        w_in_r = w_in_r.at[:, g0:nmain].set(wi[:, g0 + 2 * mh:].astype(MXU_DTYPE))
        w_in_r = w_in_r.at[:, nmain:nmain + 2 * mh].set(wi[:, g0:g0 + 2 * mh].astype(MXU_DTYPE))
        bg = jnp.pad(b_gates[l], (0, GATE_LANES - 2 * mh)).reshape(1, GATE_LANES)
        wout = w_out[l].astype(MXU_DTYPE)
        wr = jnp.pad(w_router[l], ((0, 0), (0, GATE_LANES - ne)))
        wr_hi, wr_lo = _split_hi_lo(wr)
        wsg, wsu, wsd = (w_sh_gate[l].astype(MXU_DTYPE), w_sh_up[l].astype(MXU_DTYPE), w_sh_down[l].astype(MXU_DTYPE))
        row = lambda v: v.reshape(1, -1)

        mod = _ada(jnp.concatenate([c_prompt, c_sample], axis=0), w_ada[l], b_ada[l])
        mods_p = [m[:bp].reshape(bp, 1, d) for m in jnp.split(mod, 6, axis=1)]
        mods_s = [m[bp:].reshape(bs, 1, d) for m in jnp.split(mod, 6, axis=1)]

        def lane_m(m):
            return jnp.pad(m, ((0, 0), (mh, GATE_LANES - 2 * mh)))[:, None, :]

        init = (jnp.zeros((bp, mh, DH, DH), F32), jnp.zeros((bp, mh, DH), F32), lane_m(jnp.zeros((bp, mh), F32)),
                jnp.zeros((bp, CONV_W - 1, 2 * mw), F32), jnp.zeros((bp, rh, DH, DH), F32))
        past = (state_mlstm_C[l], state_mlstm_n[l], lane_m(state_mlstm_m[l]), state_mlstm_conv[l], state_ret_S[l])

        def mix(x, mods, state, cos2, sin2):
            sh1, sc1, gt1, sh2, sc2, gt2 = mods
            b, ll, _ = x.shape
            if ll % MIX_ROWS == 0:
                x1, hf3, lg, c_n, n_n, m_n, conv_n, s_n = _mixer(
                    x, mods, cos2, sin2, *state, row(g_mix_pre[l]), w_in_r, conv_w[l], row(conv_b[l]), bg,
                    row(m_gn[l]), row(r_gn[l]), row(g_mix_post[l]), row(g_ffn_pre[l]), wout, wr_hi, wr_lo)
            else:
                z = _inproj(x, sc1, sh1, row(g_mix_pre[l]), w_in_r).reshape(b, ll, -1)
                o_mix, c_n, n_n, m_n, conv_n, s_n = _mixcore_packed(
                    z, cos2, sin2, *state, conv_w[l], row(conv_b[l]), bg, row(m_gn[l]), row(r_gn[l]))
                x1, hf3, lg = _mixout(o_mix, x, gt1, sc2, sh2, row(g_mix_post[l]), row(g_ffn_pre[l]), wout, wr_hi,
                                      wr_lo)
            lidx, wts, cnt, seg = _router(lg, b_router[l], ne)
            flat = lambda v: v.T.reshape(-1)
            return (x1, hf3, flat(lidx), flat(wts), cnt[::SUB, :ne], seg[::SUB, :ne],
                    (c_n, n_n, m_n[:, 0, mh:2 * mh], conv_n, s_n))

        x1_p, hf3_p, lidx_p, wts_p, cnt_p, seg_p, st_p = mix(hp, mods_p, init, cos_p, sin_p)
        x1_s, hf3_s, lidx_s, wts_s, cnt_s, seg_s, st_s = mix(hs, mods_s, past, cos_s, sin_s)

        cnt_t = jnp.concatenate([cnt_p, cnt_s], axis=0)
        cnt = cnt_t.reshape(-1)
        seg = jnp.concatenate([seg_p, seg_s], axis=0).reshape(-1)
        gbase, block_e, nb_used, pad_start, pad_n, tail, nb = _global_plan(cnt_t, (tp + ts) * TOP_K, ne)
        xs = _dispatch(cnt, seg, gbase, pad_start, pad_n, tail, lidx_p, lidx_s, hf3_p, hf3_s, nb * MOE_TILE, ne)
        ys = _ffn(block_e, nb_used, xs.reshape(nb * MOE_TILE * SUB, LANES), w_exp_gate[l], w_exp_up[l], w_exp_down[l])
        ys = ys.reshape(nb * MOE_TILE, SUB, LANES)
        hp = _combine(cnt, seg, gbase, lidx_p, wts_p, hf3_p, x1_p, mods_p[5], row(g_ffn_post[l]), wsg, wsu, wsd, ys, 0, ne)
        hs = _combine(cnt, seg, gbase, lidx_s, wts_s, hf3_s, x1_s, mods_s[5], row(g_ffn_post[l]), wsg, wsu, wsd, ys, ntp, ne)
        new_p.append(st_p)
        new_s.append(st_s)

    p_state = [jnp.stack([st[i] for st in new_p]) for i in range(5)]
    s_state = [jnp.stack([st[i] for st in new_s]) for i in range(5)]
    return (hp, hs, *p_state, *s_state)
```

```python
import functools
import math

import jax
import jax.numpy as jnp
from jax import lax
from jax.experimental import pallas as pl
from jax.experimental.pallas import tpu as pltpu

F32 = jnp.float32
MXU_DTYPE = jnp.bfloat16

NORM_EPS = 1e-6
ROPE_BASE = 10000.0
PAST_LEN = 16384
CHUNK = 128
CONV_W = 4
DH = 128
TOP_K = 8
N_GROUPS = 8
TOPK_GROUPS = 4
ROUTED_SCALE = 2.5
GATE_LANES = 128

V7X_VMEM_BYTES = 64 * 1024 * 1024
VMEM_LIMIT = 48 * 1024 * 1024
VMEM_LIMIT_COMBINE = 56 * 1024 * 1024

SUB, LANES = 8, 128

TOK_TILE = 256
ROUTE_TILE = 512
MOE_TILE = 512


def _cparams(sem, vmem_limit=VMEM_LIMIT):
    return pltpu.CompilerParams(dimension_semantics=sem, vmem_limit_bytes=vmem_limit)


def _silu(x):
    return x * jax.nn.sigmoid(x)


def _rms(x, g):
    return x * lax.rsqrt(jnp.mean(x * x, axis=-1, keepdims=True) + NORM_EPS) * g


def _mm(a, b):
    return jnp.dot(a.astype(MXU_DTYPE), b.astype(MXU_DTYPE), preferred_element_type=F32)


def _mm_nt(a, b):
    return lax.dot_general(a.astype(MXU_DTYPE), b.astype(MXU_DTYPE), (((1,), (1,)), ((), ())),
                           preferred_element_type=F32)


def _mm_tn(a, b):
    return lax.dot_general(a.astype(MXU_DTYPE), b.astype(MXU_DTYPE), (((0,), (0,)), ((), ())),
                           preferred_element_type=F32)


def _ada_kernel(c_ref, w_ref, b_ref, o_ref):
    o_ref[...] = _mm(_silu(c_ref[...]), w_ref[...]) + b_ref[...]


def _ada(c, w_ada, b_ada):
    n, d = c.shape
    dout = w_ada.shape[1]
    tn = 1024
    return pl.pallas_call(
        _ada_kernel,
        grid=(dout // tn,),
        in_specs=[pl.BlockSpec((n, d), lambda j: (0, 0)),
                  pl.BlockSpec((d, tn), lambda j: (0, j)),
                  pl.BlockSpec((1, tn), lambda j: (0, j))],
        out_specs=pl.BlockSpec((n, tn), lambda j: (0, j)),
        out_shape=jax.ShapeDtypeStruct((n, dout), F32),
        compiler_params=_cparams(("arbitrary",)),
        name="ada",
    )(c, w_ada, b_ada.reshape(1, dout))


def _inproj_kernel(x_ref, sc_ref, sh_ref, g_ref, w_ref, z_ref):
    x = x_ref[...]
    h = _rms(x, g_ref[...]) * (1.0 + sc_ref[...]) + sh_ref[...]
    bb, ll, d = x.shape
    z_ref[...] = _mm(h.reshape(bb * ll, d), w_ref[...])


def _inproj(x, sc, sh, g, w):
    b, l, d = x.shape
    n = w.shape[1]
    ll = min(l, TOK_TILE)
    bb = TOK_TILE // ll
    mod_spec = pl.BlockSpec((bb, 1, d), lambda i, j: (i, 0, 0))
    return pl.pallas_call(
        _inproj_kernel,
        grid=(b // bb, l // ll),
        in_specs=[pl.BlockSpec((bb, ll, d), lambda i, j: (i, j, 0)), mod_spec, mod_spec,
                  pl.BlockSpec((1, d), lambda i, j: (0, 0)),
                  pl.BlockSpec((d, n), lambda i, j: (0, 0))],
        out_specs=pl.BlockSpec((bb * ll, n), lambda i, j: (i * (l // ll) + j, 0)),
        out_shape=jax.ShapeDtypeStruct((b * l, n), F32),
        compiler_params=_cparams(("arbitrary", "arbitrary")),
        name="inproj",
    )(x, sc, sh, g, w)


def _cumsum_rows(x):
    n = x.shape[0]
    row = lax.broadcasted_iota(jnp.int32, x.shape, 0)
    s = 1
    while s < n:
        x = x + jnp.where(row >= s, pltpu.roll(x, s, axis=0), 0.0)
        s *= 2
    return x


def _head_norm(y, g):
    mu = jnp.mean(y, axis=-1, keepdims=True)
    yc = y - mu
    var = jnp.mean(yc * yc, axis=-1, keepdims=True)
    return yc * lax.rsqrt(var + NORM_EPS) * g


def _log_sigmoid(x):
    return jnp.minimum(x, 0.0) - jnp.log1p(jnp.exp(-jnp.abs(x)))


def _chunk_step(rows, put, cos2, sin2, c_ref, n_ref, m_ref, conv_ref, s_ref, xcat_ref,
                convw_ref, convb_ref, bg_ref, mgn_ref, rgn_ref, mh, rh, after_head=None):
    cl = CHUNK
    mw = mh * DH
    rw = rh * DH
    o_qk, o_mv, o_mo = 0, 2 * mw, 3 * mw
    o_rq = 4 * mw
    o_rk, o_rv, o_rg, o_gt = o_rq + rw, o_rq + 2 * rw, o_rq + 3 * rw, o_rq + 4 * rw

    xcat_ref[8:8 + cl, :] = rows(o_qk, o_qk + 2 * mw)
    qk = convb_ref[...]
    for t in range(CONV_W):
        qk = qk + xcat_ref[8 - (CONV_W - 1) + t:8 - (CONV_W - 1) + t + cl, :] * convw_ref[t:t + 1, :]
    qk = _silu(qk)
    new_conv = xcat_ref[8 + cl - (CONV_W - 1):8 + cl, :]
    conv_ref[0] = new_conv
    xcat_ref[8 - (CONV_W - 1):8, :] = new_conv

    row_c = lax.broadcasted_iota(jnp.int32, (cl, 1), 0)
    ii = lax.broadcasted_iota(jnp.int32, (cl, cl), 0)
    jj = lax.broadcasted_iota(jnp.int32, (cl, cl), 1)
    causal = jj <= ii
    eye = jj == ii

    def to_row(col):
        return jnp.sum(jnp.where(eye, col, 0.0), axis=0, keepdims=True)

    g = rows(o_gt, o_gt + GATE_LANES) + bg_ref[...]
    bcum = _cumsum_rows(_log_sigmoid(g))
    m_prev = m_ref[0]
    a_all = bcum + m_prev
    lane = lax.broadcasted_iota(jnp.int32, (1, GATE_LANES), 1)
    m_new_row = m_prev

    for h in range(mh):
        q = qk[:, h * DH:(h + 1) * DH]
        k = qk[:, mw + h * DH:mw + (h + 1) * DH] * (DH ** -0.5)
        v = rows(o_mv + h * DH, o_mv + (h + 1) * DH)
        b_col = bcum[:, mh + h:mh + h + 1]
        logi_col = g[:, h:h + 1]
        a_col = a_all[:, mh + h:mh + h + 1]
        dmat = jnp.where(causal, b_col + to_row(logi_col - b_col), -jnp.inf)
        m_t = jnp.maximum(a_col, jnp.max(dmat, axis=1, keepdims=True))
        w_intra = jnp.exp(dmat - m_t)
        w_inter = jnp.exp(a_col - m_t)
        c_old = c_ref[0, h]
        n_old = n_ref[0, h:h + 1, :]
        s = _mm_nt(q, k) * w_intra
        num = _mm(s, v) + w_inter * _mm(q, c_old)
        den = jnp.sum(s, axis=1, keepdims=True) + w_inter * jnp.sum(q * n_old, axis=1, keepdims=True)
        hh = num * (1.0 / jnp.maximum(jnp.abs(den), jnp.exp(-m_t)))
        m_new = m_t[cl - 1:cl, :]
        b_last = b_col[cl - 1:cl, :]
        w_old = jnp.exp(b_last + m_prev[:, mh + h:mh + h + 1] - m_new)
        w_k = jnp.exp(b_last - b_col + logi_col - m_new)
        c_ref[0, h] = w_old * c_old + _mm_tn(k, w_k * v)
        n_ref[0, h:h + 1, :] = w_old * n_old + jnp.sum(w_k * k, axis=0, keepdims=True)
        m_new_row = jnp.where(lane == mh + h, m_new, m_new_row)
        om = _head_norm(hh, mgn_ref[:, h * DH:(h + 1) * DH]) * jax.nn.sigmoid(rows(o_mo + h * DH, o_mo + (h + 1) * DH))
        put(h * DH, (h + 1) * DH, om)
        if after_head is not None:
            after_head(h)
    m_ref[0] = m_new_row

    rel = (ii - jj).astype(F32)
    row_f = row_c.astype(F32)

    def rot(x):
        return x * cos2 + pltpu.roll(x, DH // 2, axis=1) * sin2

    for h in range(rh):
        log_g = math.log1p(-2.0 ** (-5.0 - h))
        q = rot(rows(o_rq + h * DH, o_rq + (h + 1) * DH))
        k = rot(rows(o_rk + h * DH, o_rk + (h + 1) * DH)) * (DH ** -0.5)
        v = rows(o_rv + h * DH, o_rv + (h + 1) * DH)
        decay = jnp.where(causal, jnp.exp(jnp.maximum(rel, 0.0) * log_g), 0.0)
        s_old = s_ref[0, h]
        y = _mm(_mm_nt(q, k) * decay, v) + jnp.exp((row_f + 1.0) * log_g) * _mm(q, s_old)
        k_w = jnp.exp((cl - 1.0 - row_f) * log_g)
        s_ref[0, h] = math.exp(cl * log_g) * s_old + _mm_tn(k, k_w * v)
        orr = _head_norm(y, rgn_ref[:, h * DH:(h + 1) * DH]) * _silu(rows(o_rg + h * DH, o_rg + (h + 1) * DH))
        put(mw + h * DH, mw + (h + 1) * DH, orr)
        if after_head is not None:
            after_head(mh + h)


MIX_ROWS = 256


def _mixer_kernel(x_ref, xn_ref, sc1_ref, sh1_ref, sc1n_ref, sh1n_ref, gt1_ref, sc2_ref, sh2_ref, cos_ref, sin_ref,
                  c0_ref, n0_ref, m0_ref, conv0_ref, s0_ref,
                  gpre1_ref, win_ref, convw_ref, convb_ref, bg_ref, mgn_ref, rgn_ref,
                  gpost_ref, gpre2_ref, wout_ref, wr_hi_ref, wr_lo_ref,
                  x1_ref, hf3_ref, lg_ref, c_ref, n_ref, m_ref, conv_ref, s_ref,
                  z_ref, o_ref, xcat_ref, *, mh, rh):
    i, j = pl.program_id(0), pl.program_id(1)
    step = i * pl.num_programs(1) + j
    cur = step % 2
    rs, d = x_ref.shape[1], x_ref.shape[2]
    nz = z_ref.shape[2]
    nchunks = rs // CHUNK
    npieces = nchunks * (mh + rh)
    cuts = [(nz // LANES * p // npieces) * LANES for p in range(npieces + 1)]

    def project(x3, sc_ref, sh_ref):
        return (_rms(x3, gpre1_ref[...]) * (1.0 + sc_ref[...]) + sh_ref[...]).reshape(rs, d)

    @pl.when(step == 0)
    def _():
        z_ref[0] = _mm(project(x_ref[...], sc1_ref, sh1_ref), win_ref[...])

    @pl.when(j == 0)
    def _():
        c_ref[...] = c0_ref[...]
        n_ref[...] = n0_ref[...]
        m_ref[...] = m0_ref[...]
        s_ref[...] = s0_ref[...]
        xcat_ref[8 - (CONV_W - 1):8, :] = conv0_ref[0]

    h_next = project(xn_ref[...], sc1n_ref, sh1n_ref).astype(MXU_DTYPE)
    for c in range(nchunks):
        r0 = c * CHUNK
        def project_piece(hd, c=c):
            lo, hi = cuts[c * (mh + rh) + hd], cuts[c * (mh + rh) + hd + 1]
            z_ref[1 - cur, :, lo:hi] = jnp.dot(h_next, win_ref[:, lo:hi], preferred_element_type=F32)

        def put(lo, hi, v, r0=r0):
            o_ref[r0:r0 + CHUNK, lo:hi] = v

        _chunk_step(lambda lo, hi, r0=r0: z_ref[cur, r0:r0 + CHUNK, lo:hi], put,
                    cos_ref[r0:r0 + CHUNK, :], sin_ref[r0:r0 + CHUNK, :],
                    c_ref, n_ref, m_ref, conv_ref, s_ref, xcat_ref,
                    convw_ref, convb_ref, bg_ref, mgn_ref, rgn_ref, mh, rh, after_head=project_piece)
    x = x_ref[...]
    _mixout_body(o_ref[...].reshape(1, rs, o_ref.shape[1]), x, gt1_ref, sc2_ref, sh2_ref, gpost_ref, gpre2_ref,
                 wout_ref, wr_hi_ref, wr_lo_ref, x1_ref, hf3_ref, lg_ref)


def _mixer(x, mods, cos2, sin2, c0, n0, m0, conv0, s0, gpre1, win, convw, convb, bg, mgn, rgn,
           gpost, gpre2, wout, wr_hi, wr_lo):
    sh1, sc1, gt1, sh2, sc2, gt2 = mods
    b, l, d = x.shape
    assert d == SUB * LANES and l % MIX_ROWS == 0
    mh, rh = c0.shape[1], s0.shape[1]
    mw, rw = mh * DH, rh * DH
    nz, ne = win.shape[1], wr_hi.shape[1]
    rs = MIX_ROWS
    assert nz % LANES == 0
    nj = l // rs
    bmap4 = lambda i, j: (i, 0, 0, 0)
    bmap3 = lambda i, j: (i, 0, 0)
    tmap = lambda i, j: (i, j, 0)
    fmap = lambda i, j: (i * nj + j, 0)
    once = lambda a: pl.BlockSpec(a.shape, lambda i, j: (0, 0), pipeline_mode=pl.Buffered(1))
    mod_spec = pl.BlockSpec((1, 1, d), bmap3)
    nxt = lambda i, j: jnp.minimum(i * nj + j + 1, b * nj - 1)
    x_next_spec = pl.BlockSpec((1, rs, d), lambda i, j: (nxt(i, j) // nj, nxt(i, j) % nj, 0))
    mod_next_spec = pl.BlockSpec((1, 1, d), lambda i, j: (nxt(i, j) // nj, 0, 0))
    rope_spec = pl.BlockSpec((rs, DH), lambda i, j: (j, 0))
    state_specs = [pl.BlockSpec((1, mh, DH, DH), bmap4), pl.BlockSpec((1, mh, DH), bmap3),
                   pl.BlockSpec((1, 1, GATE_LANES), bmap3), pl.BlockSpec((1, CONV_W - 1, 2 * mw), bmap3),
                   pl.BlockSpec((1, rh, DH, DH), bmap4)]
    state_shapes = [jax.ShapeDtypeStruct(c0.shape, F32), jax.ShapeDtypeStruct(n0.shape, F32),
                    jax.ShapeDtypeStruct(m0.shape, F32), jax.ShapeDtypeStruct(conv0.shape, F32),
                    jax.ShapeDtypeStruct(s0.shape, F32)]
    return pl.pallas_call(
        functools.partial(_mixer_kernel, mh=mh, rh=rh),
        grid=(b, nj),
        in_specs=[pl.BlockSpec((1, rs, d), tmap), x_next_spec, mod_spec, mod_spec, mod_next_spec, mod_next_spec]
                 + [mod_spec] * 3 + [rope_spec, rope_spec] + state_specs
                 + [once(a) for a in (gpre1, win, convw, convb, bg, mgn, rgn, gpost, gpre2, wout, wr_hi, wr_lo)],
        out_specs=[pl.BlockSpec((1, rs, d), tmap), pl.BlockSpec((rs * SUB, LANES), fmap),
                   pl.BlockSpec((rs, ne), fmap)] + state_specs,
        out_shape=[jax.ShapeDtypeStruct((b, l, d), F32), jax.ShapeDtypeStruct((b * l * SUB, LANES), F32),
                   jax.ShapeDtypeStruct((b * l, ne), F32)] + state_shapes,
        scratch_shapes=[pltpu.VMEM((2, rs, nz), F32), pltpu.VMEM((rs, mw + rw), F32),
                        pltpu.VMEM((CHUNK + 8, 2 * mw), F32)],
        compiler_params=_cparams(("arbitrary", "arbitrary")),
        name="mixer",
    )(x, x, sc1, sh1, sc1, sh1, gt1, sc2, sh2, cos2, sin2, c0, n0, m0, conv0, s0,
      gpre1, win, convw, convb, bg, mgn, rgn, gpost, gpre2, wout, wr_hi, wr_lo)


PACK_SEQS = CHUNK // SUB


def _mixcore_packed_kernel(z_ref, cos_ref, sin_ref, c0_ref, n0_ref, m0_ref, conv0_ref, s0_ref,
                           convw_ref, convb_ref, bg_ref, mgn_ref, rgn_ref,
                           out_ref, c_ref, n_ref, m_ref, conv_ref, s_ref, *, mh, rh):
    g_n, l_n, _ = z_ref.shape
    rr = g_n * l_n
    mw, rw = mh * DH, rh * DH
    o_qk, o_mv, o_mo = 0, 2 * mw, 3 * mw
    o_rq = 4 * mw
    o_rk, o_rv, o_rg, o_gt = o_rq + rw, o_rq + 2 * rw, o_rq + 3 * rw, o_rq + 4 * rw

    def rows(lo, hi):
        return z_ref[:, :, lo:hi].reshape(rr, hi - lo)

    def per_seq(x):
        return jnp.broadcast_to(x[:, None, :], (g_n, l_n, x.shape[-1])).reshape(rr, x.shape[-1])

    def last_of_seq(x):
        x3 = x.reshape(g_n, l_n, x.shape[-1])
        return jnp.broadcast_to(x3[:, l_n - 1:l_n, :], x3.shape).reshape(x.shape)

    def full(col):
        return jnp.broadcast_to(col, (rr, DH))

    pos_c = lax.broadcasted_iota(jnp.int32, (rr, 1), 0) % l_n
    ii = lax.broadcasted_iota(jnp.int32, (rr, rr), 0)
    jj = lax.broadcasted_iota(jnp.int32, (rr, rr), 1)
    mask = ((ii // l_n) == (jj // l_n)) & (jj <= ii)
    eye = jj == ii

    def to_row(col):
        return jnp.sum(jnp.where(eye, col, 0.0), axis=0, keepdims=True)

    x = rows(o_qk, o_qk + 2 * mw)
    e = conv0_ref[...].reshape(rr, 2 * mw)
    qk = convb_ref[...]
    for t in range(CONV_W):
        d = CONV_W - 1 - t
        xd = x if d == 0 else jnp.where(pos_c >= d, pltpu.roll(x, d, axis=0),
                                        pltpu.roll(e, (d - (CONV_W - 1)) % rr, axis=0))
        qk = qk + xd * convw_ref[t:t + 1, :]
    qk = _silu(qk)
    conv_ref[...] = x.reshape(g_n, l_n, 2 * mw)[:, l_n - (CONV_W - 1):, :]

    g = rows(o_gt, o_gt + GATE_LANES) + bg_ref[...]
    bcum = _log_sigmoid(g)
    s_ = 1
    while s_ < l_n:
        bcum = bcum + jnp.where(pos_c >= s_, pltpu.roll(bcum, s_, axis=0), 0.0)
        s_ *= 2
    m_prev = per_seq(m0_ref[:, 0, :])
    a_all = bcum + m_prev
    lane = lax.broadcasted_iota(jnp.int32, (1, GATE_LANES), 1)
    even_seq = (lax.broadcasted_iota(jnp.int32, (rr, 1), 0) // l_n) % 2 == 0
    m_new_rows = m_prev

    def state_dot(qb, st_ref, h):
        parts = []
        for s in range(g_n):
            pb, lo = s // 2, (s % 2) * l_n
            res = jnp.dot(qb[pb * 2 * l_n:(pb + 1) * 2 * l_n], st_ref[s, h].astype(MXU_DTYPE),
                          preferred_element_type=F32)
            parts.append(res[lo:lo + l_n])
        return jnp.concatenate(parts, axis=0)

    def state_update(k, wv, decay_full, st_in, st_out, h):
        k_even = jnp.where(even_seq, k, 0.0).astype(MXU_DTYPE)
        k_odd = jnp.where(even_seq, 0.0, k).astype(MXU_DTYPE)
        wvb = wv.astype(MXU_DTYPE)
        for s in range(g_n):
            pb = s // 2
            kp = (k_even if s % 2 == 0 else k_odd)[pb * 2 * l_n:(pb + 1) * 2 * l_n]
            upd = lax.dot_general(kp, wvb[pb * 2 * l_n:(pb + 1) * 2 * l_n], (((0,), (0,)), ((), ())),
                                  preferred_element_type=F32)
            st_out[s, h] = decay_full[s * l_n:s * l_n + 1, :] * st_in[s, h] + upd

    for h in range(mh):
        q = qk[:, h * DH:(h + 1) * DH]
        k = qk[:, mw + h * DH:mw + (h + 1) * DH] * (DH ** -0.5)
        v = rows(o_mv + h * DH, o_mv + (h + 1) * DH)
        b_col = bcum[:, mh + h:mh + h + 1]
        logi_col = g[:, h:h + 1]
        a_col = a_all[:, mh + h:mh + h + 1]
        dmat = jnp.where(mask, b_col + to_row(logi_col - b_col), -jnp.inf)
        m_t = jnp.maximum(a_col, jnp.max(dmat, axis=1, keepdims=True))
        w_intra = jnp.exp(dmat - m_t)
        w_inter = jnp.exp(a_col - m_t)
        n_rows = per_seq(n0_ref[:, h, :])
        s = _mm_nt(q, k) * w_intra
        num = _mm(s, v) + w_inter * state_dot(q.astype(MXU_DTYPE), c0_ref, h)
        den = jnp.sum(s, axis=1, keepdims=True) + w_inter * jnp.sum(q * n_rows, axis=1, keepdims=True)
        hh = num * (1.0 / jnp.maximum(jnp.abs(den), jnp.exp(-m_t)))
        m_new = last_of_seq(full(m_t))
        b_last = last_of_seq(full(b_col))
        w_old = jnp.exp(b_last + full(m_prev[:, mh + h:mh + h + 1]) - m_new)
        w_k = jnp.exp(b_last - full(b_col) + full(logi_col) - m_new)
        state_update(k, w_k * v, w_old, c0_ref, c_ref, h)
        n_ref[:, h, :] = (w_old.reshape(g_n, l_n, DH)[:, 0, :] * n0_ref[:, h, :]
                          + jnp.sum((w_k * k).reshape(g_n, l_n, DH), axis=1))
        m_new_rows = jnp.where(lane == mh + h, m_new, m_new_rows)
        om = _head_norm(hh, mgn_ref[:, h * DH:(h + 1) * DH]) * jax.nn.sigmoid(rows(o_mo + h * DH, o_mo + (h + 1) * DH))
        out_ref[:, :, h * DH:(h + 1) * DH] = om.reshape(g_n, l_n, DH)
    m_ref[...] = m_new_rows.reshape(g_n, l_n, GATE_LANES)[:, 0:1, :]

    cos2 = cos_ref[...]
    sin2 = sin_ref[...]
    rel = (ii - jj).astype(F32)
    pos_f = pos_c.astype(F32)

    def rot(x):
        return x * cos2 + pltpu.roll(x, DH // 2, axis=1) * sin2

    for h in range(rh):
        log_g = math.log1p(-2.0 ** (-5.0 - h))
        q = rot(rows(o_rq + h * DH, o_rq + (h + 1) * DH))
        k = rot(rows(o_rk + h * DH, o_rk + (h + 1) * DH)) * (DH ** -0.5)
        v = rows(o_rv + h * DH, o_rv + (h + 1) * DH)
        decay = jnp.where(mask, jnp.exp(jnp.maximum(rel, 0.0) * log_g), 0.0)
        y = (_mm(_mm_nt(q, k) * decay, v)
             + jnp.exp((pos_f + 1.0) * log_g) * state_dot(q.astype(MXU_DTYPE), s0_ref, h))
        k_w = jnp.exp((l_n - 1.0 - pos_f) * log_g)
        state_update(k, k_w * v, jnp.full((rr, DH), math.exp(l_n * log_g), F32), s0_ref, s_ref, h)
        orr = _head_norm(y, rgn_ref[:, h * DH:(h + 1) * DH]) * _silu(rows(o_rg + h * DH, o_rg + (h + 1) * DH))
        out_ref[:, :, mw + h * DH:mw + (h + 1) * DH] = orr.reshape(g_n, l_n, DH)


def _mixcore_packed(z, cos2, sin2, c0, n0, m0, conv0, s0, convw, convb, bg, mgn, rgn):
    b, l, nz = z.shape
    assert l == SUB and b % PACK_SEQS == 0
    mh, rh = c0.shape[1], s0.shape[1]
    mw, rw = mh * DH, rh * DH
    g_n = PACK_SEQS
    conv0p = jnp.pad(conv0, ((0, 0), (0, l - (CONV_W - 1)), (0, 0)))
    tile = lambda t: jnp.tile(t[:l], (g_n, 1))
    bmap4 = lambda i: (i, 0, 0, 0)
    bmap3 = lambda i: (i, 0, 0)
    wmap = lambda i: (0, 0)
    c_spec, n_spec = pl.BlockSpec((g_n, mh, DH, DH), bmap4), pl.BlockSpec((g_n, mh, DH), bmap3)
    m_spec, s_spec = pl.BlockSpec((g_n, 1, GATE_LANES), bmap3), pl.BlockSpec((g_n, rh, DH, DH), bmap4)
    return pl.pallas_call(
        functools.partial(_mixcore_packed_kernel, mh=mh, rh=rh),
        grid=(b // g_n,),
        in_specs=[pl.BlockSpec((g_n, l, nz), bmap3),
                  pl.BlockSpec((g_n * l, DH), wmap), pl.BlockSpec((g_n * l, DH), wmap),
                  c_spec, n_spec, m_spec, pl.BlockSpec((g_n, l, 2 * mw), bmap3), s_spec,
                  pl.BlockSpec(convw.shape, wmap), pl.BlockSpec(convb.shape, wmap), pl.BlockSpec(bg.shape, wmap),
                  pl.BlockSpec(mgn.shape, wmap), pl.BlockSpec(rgn.shape, wmap)],
        out_specs=[pl.BlockSpec((g_n, l, mw + rw), bmap3), c_spec, n_spec, m_spec,
                   pl.BlockSpec((g_n, CONV_W - 1, 2 * mw), bmap3), s_spec],
        out_shape=[jax.ShapeDtypeStruct((b, l, mw + rw), F32), jax.ShapeDtypeStruct(c0.shape, F32),
                   jax.ShapeDtypeStruct(n0.shape, F32), jax.ShapeDtypeStruct(m0.shape, F32),
                   jax.ShapeDtypeStruct(conv0.shape, F32), jax.ShapeDtypeStruct(s0.shape, F32)],
        compiler_params=_cparams(("arbitrary",)),
        name="mixcore_packed",
    )(z, tile(cos2), tile(sin2), c0, n0, m0, conv0p, s0, convw, convb, bg, mgn, rgn)


def _split_hi_lo(x):
    hi = x.astype(MXU_DTYPE)
    lo = (x - hi.astype(F32)).astype(MXU_DTYPE)
    return hi, lo


def _to_row_tiles(ref, x):
    n = x.shape[0]
    for j in range(SUB):
        ref[pl.ds(j, n, stride=SUB), :] = x[:, j * LANES:(j + 1) * LANES]


def _from_row_tiles(ref, n):
    return jnp.concatenate([ref[pl.ds(j, n, stride=SUB), :] for j in range(SUB)], axis=1)


def _mixout_kernel(o_ref, x_ref, gt1_ref, sc2_ref, sh2_ref, gpost_ref, gpre_ref, wout_ref, wr_hi_ref, wr_lo_ref,
                   x1_ref, hf3_ref, lg_ref):
    _mixout_body(o_ref[...], x_ref[...], gt1_ref, sc2_ref, sh2_ref, gpost_ref, gpre_ref, wout_ref, wr_hi_ref,
                 wr_lo_ref, x1_ref, hf3_ref, lg_ref)


def _mixout_body(o, x, gt1_ref, sc2_ref, sh2_ref, gpost_ref, gpre_ref, wout_ref, wr_hi_ref, wr_lo_ref,
                 x1_ref, hf3_ref, lg_ref):
    bb, ll, d = x.shape
    ym = _mm(o.reshape(bb * ll, o.shape[-1]), wout_ref[...]).reshape(bb, ll, d)
    x1 = x + gt1_ref[...] * _rms(ym, gpost_ref[...])
    x1_ref[...] = x1
    hf = (_rms(x1, gpre_ref[...]) * (1.0 + sc2_ref[...]) + sh2_ref[...]).reshape(bb * ll, d)
    _to_row_tiles(hf3_ref, hf)
    hi, lo = _split_hi_lo(hf)
    lg_ref[...] = _mm(hi, wr_hi_ref[...]) + (_mm(lo, wr_hi_ref[...]) + _mm(hi, wr_lo_ref[...]))


def _mixout(o, x, gt1, sc2, sh2, gpost, gpre, wout, wr_hi, wr_lo):
    b, l, d = x.shape
    assert d == SUB * LANES
    w = o.shape[-1]
    ne = wr_hi.shape[1]
    ll = min(l, TOK_TILE)
    bb = TOK_TILE // ll
    tmap = lambda i, j: (i, j, 0)
    fmap = lambda i, j: (i * (l // ll) + j, 0)
    mod_spec = pl.BlockSpec((bb, 1, d), lambda i, j: (i, 0, 0))
    wmap = lambda i, j: (0, 0)
    return pl.pallas_call(
        _mixout_kernel,
        grid=(b // bb, l // ll),
        in_specs=[pl.BlockSpec((bb, ll, w), tmap), pl.BlockSpec((bb, ll, d), tmap), mod_spec, mod_spec, mod_spec,
                  pl.BlockSpec((1, d), wmap), pl.BlockSpec((1, d), wmap), pl.BlockSpec((w, d), wmap),
                  pl.BlockSpec((d, ne), wmap), pl.BlockSpec((d, ne), wmap)],
        out_specs=[pl.BlockSpec((bb, ll, d), tmap), pl.BlockSpec((bb * ll * SUB, LANES), fmap),
                   pl.BlockSpec((bb * ll, ne), fmap)],
        out_shape=[jax.ShapeDtypeStruct((b, l, d), F32), jax.ShapeDtypeStruct((b * l * SUB, LANES), F32),
                   jax.ShapeDtypeStruct((b * l, ne), F32)],
        compiler_params=_cparams(("arbitrary", "arbitrary")),
        name="mixout",
    )(o, x, gt1, sc2, sh2, gpost, gpre, wout, wr_hi, wr_lo)


def _router_kernel(lg_ref, br_ref, lidx_ref, wts_ref, cnt_ref, seg_ref, *, ne):
    tm = lg_ref.shape[0]
    gsz = ne // N_GROUPS
    s = jax.nn.sigmoid(lg_ref[...].T[:ne, :])
    sb = s + br_ref[...]
    sb3 = sb.reshape(N_GROUPS, gsz, tm)
    e3 = lax.broadcasted_iota(jnp.int32, sb3.shape, 1)
    m1 = jnp.max(sb3, axis=1, keepdims=True)
    first = jnp.min(jnp.where(sb3 == m1, e3, gsz), axis=1, keepdims=True)
    m2 = jnp.max(jnp.where(e3 == first, -jnp.inf, sb3), axis=1, keepdims=True)
    gs = (m1 + m2).reshape(N_GROUPS, tm)
    gi = lax.broadcasted_iota(jnp.int32, gs.shape, 0)
    grank = jnp.zeros(gs.shape, F32)
    for g in range(N_GROUPS):
        r = gs[g:g + 1, :]
        grank = grank + jnp.where(r > gs, 1.0, jnp.where(r == gs, jnp.where(g < gi, 1.0, 0.0), 0.0))
    gsel = jnp.where(grank < TOPK_GROUPS, 1.0, 0.0)
    emask = jnp.broadcast_to(gsel.reshape(N_GROUPS, 1, tm), sb3.shape).reshape(ne, tm) > 0.0
    masked = jnp.where(emask, sb, -jnp.inf)
    ei = lax.broadcasted_iota(jnp.int32, masked.shape, 0)
    cur = masked
    sel_f = jnp.zeros(masked.shape, F32)
    for _ in range(TOP_K):
        best = jnp.max(cur, axis=0, keepdims=True)
        first_e = jnp.min(jnp.where(cur == best, ei, ne), axis=0, keepdims=True)
        pick = ei == first_e
        sel_f = jnp.where(pick, 1.0, sel_f)
        cur = jnp.where(pick, -jnp.inf, cur)
    sel = sel_f > 0.0
    w = jnp.where(sel, s, 0.0)
    w = w / jnp.sum(w, axis=0, keepdims=True) * ROUTED_SCALE

    self = jnp.where(sel, 1.0, 0.0)
    er = lax.broadcasted_iota(jnp.int32, (ne, ne), 0)
    ec = lax.broadcasted_iota(jnp.int32, (ne, ne), 1)
    slot = _mm(jnp.where(ec < er, 1.0, 0.0), self)
    tr = lax.broadcasted_iota(jnp.int32, (tm, tm), 0)
    tc = lax.broadcasted_iota(jnp.int32, (tm, tm), 1)
    pos = _mm(self, jnp.where(tr < tc, 1.0, 0.0))
    seg = jnp.sum(slot, axis=1, keepdims=True)
    lrow = seg + pos
    lidx_rows, w_rows = [], []
    for k in range(TOP_K):
        pick = sel & (slot == float(k))
        lidx_rows.append(jnp.sum(jnp.where(pick, lrow, 0.0), axis=0, keepdims=True))
        w_rows.append(jnp.sum(jnp.where(pick, w, 0.0), axis=0, keepdims=True))
    lidx_ref[...] = jnp.concatenate(lidx_rows, axis=0).astype(jnp.int32)
    wts_ref[...] = jnp.concatenate(w_rows, axis=0)
    ones = jnp.ones((SUB, tm), F32)
    padz = jnp.zeros((LANES - ne, tm), F32)
    cnt_ref[...] = _mm_nt(ones, jnp.concatenate([self, padz], axis=0)).astype(jnp.int32)
    seg_ref[...] = _mm_nt(ones, jnp.concatenate([slot, padz], axis=0)).astype(jnp.int32)


def _router(logits, b_router, ne):
    t, lanes = logits.shape
    tm = ROUTE_TILE
    nt = t // tm
    kspec = pl.BlockSpec((TOP_K, tm), lambda i: (0, i))
    cspec = pl.BlockSpec((SUB, LANES), lambda i: (i, 0))
    return pl.pallas_call(
        functools.partial(_router_kernel, ne=ne),
        grid=(nt,),
        in_specs=[pl.BlockSpec((tm, lanes), lambda i: (i, 0)), pl.BlockSpec((ne, 1), lambda i: (0, 0))],
        out_specs=[kspec, kspec, cspec, cspec],
        out_shape=[jax.ShapeDtypeStruct((TOP_K, t), jnp.int32), jax.ShapeDtypeStruct((TOP_K, t), F32),
                   jax.ShapeDtypeStruct((nt * SUB, LANES), jnp.int32),
                   jax.ShapeDtypeStruct((nt * SUB, LANES), jnp.int32)],
        compiler_params=_cparams(("arbitrary",)),
        name="router",
    )(logits, b_router.reshape(ne, 1))


SEG_BITS = ROUTE_TILE.bit_length()
PAD_BITS = (MOE_TILE - 1).bit_length()


RARE_BIT = 7


def _for_each_piece(n, fn, bits):
    def pieces(lo, hi):
        for bit in range(hi - 1, lo - 1, -1):
            @pl.when(((n >> bit) & 1) == 1)
            def _():
                fn((n >> (bit + 1)) << (bit + 1), 1 << bit)

    if bits > RARE_BIT:
        @pl.when(n >= (1 << RARE_BIT))
        def _():
            pieces(RARE_BIT, bits)
    pieces(0, min(bits, RARE_BIT))


SORT_UNROLL = 4


def _zero_rows(ps_ref, pn_ref, tail_ref, xs_ref, z_ref, sem, ne, act):
    def pad_body(e, carry):
        _for_each_piece(pn_ref[e], lambda off, size: act(
            pltpu.make_async_copy(z_ref.at[pl.ds(0, size)], xs_ref.at[pl.ds(ps_ref[e] + off, size)], sem)),
            PAD_BITS)
        return carry
    lax.fori_loop(0, ne, pad_body, 0)

    def tail_body(b, carry):
        act(pltpu.make_async_copy(z_ref, xs_ref.at[pl.ds(tail_ref[0] + b * MOE_TILE, MOE_TILE)], sem))
        return carry
    lax.fori_loop(0, tail_ref[1], tail_body, 0)


def _dispatch_kernel(cnt_ref, seg_ref, gb_ref, ps_ref, pn_ref, tail_ref, lidx_p_ref, lidx_s_ref, hf3_p_ref, hf3_s_ref,
                     xs_ref, buf0, buf1, z_ref, sems, *, ntp, ne):
    i = pl.program_id(0)
    n = pl.num_programs(0)
    tt = lidx_p_ref.shape[0] // TOP_K

    def whole(buf, sem):
        return pltpu.make_async_copy(buf, xs_ref.at[pl.ds(0, tt * TOP_K)], sem)

    @pl.when(i == 0)
    def _():
        z_ref[...] = jnp.zeros(z_ref.shape, F32)
        _zero_rows(ps_ref, pn_ref, tail_ref, xs_ref, z_ref, sems.at[2], ne, lambda c: c.start())

    def run(buf, sem):
        @pl.when(i >= 2)
        def _():
            whole(buf, sem).wait()

        def sort_from(hf3_ref, lidx_ref):
            def sort_body(tb, carry):
                for u in range(SORT_UNROLL):
                    t = tb * SORT_UNROLL + u
                    v = hf3_ref[pl.ds(pl.multiple_of(t * SUB, SUB), SUB), :]
                    for k in range(TOP_K):
                        buf[lidx_ref[t * TOP_K + k]] = v
                return carry
            lax.fori_loop(0, tt // SORT_UNROLL, sort_body, 0)

        @pl.when(i < ntp)
        def _():
            sort_from(hf3_p_ref, lidx_p_ref)

        @pl.when(i >= ntp)
        def _():
            sort_from(hf3_s_ref, lidx_s_ref)

        def seg_body(e, carry):
            c, s, g = cnt_ref[i * ne + e], seg_ref[i * ne + e], gb_ref[i * ne + e]
            _for_each_piece(c, lambda off, size: pltpu.make_async_copy(
                buf.at[pl.ds(s + off, size)], xs_ref.at[pl.ds(g + off, size)], sem).start(), SEG_BITS)
            return carry
        lax.fori_loop(0, ne, seg_body, 0)

    @pl.when(i % 2 == 0)
    def _():
        run(buf0, sems.at[0])

    @pl.when(i % 2 == 1)
    def _():
        run(buf1, sems.at[1])

    @pl.when(i == 0)
    def _():
        _zero_rows(ps_ref, pn_ref, tail_ref, xs_ref, z_ref, sems.at[2], ne, lambda c: c.wait())

    @pl.when(i == n - 1)
    def _():
        @pl.when(i % 2 == 0)
        def _():
            whole(buf0, sems.at[0]).wait()

            @pl.when(i >= 1)
            def _():
                whole(buf1, sems.at[1]).wait()

        @pl.when(i % 2 == 1)
        def _():
            whole(buf1, sems.at[1]).wait()
            whole(buf0, sems.at[0]).wait()


def _dispatch(cnt, seg, gbase, pad_start, pad_n, tail, lidx_p, lidx_s, hf3_p, hf3_s, rows, ne):
    tt = ROUTE_TILE
    ntp, nts =lidx_p.shape[0] // (tt * TOP_K), lidx_s.shape[0] // (tt * TOP_K)
    pmap = lambda i, *_: jnp.minimum(i, ntp - 1)
    smap = lambda i, *_: jnp.maximum(i - ntp, 0)
    return pl.pallas_call(
        functools.partial(_dispatch_kernel, ntp=ntp, ne=ne),
        grid_spec=pltpu.PrefetchScalarGridSpec(
            num_scalar_prefetch=6, grid=(ntp + nts,),
            in_specs=[pl.BlockSpec((tt * TOP_K,), lambda i, *_: (pmap(i),), memory_space=pltpu.SMEM),
                      pl.BlockSpec((tt * TOP_K,), lambda i, *_: (smap(i),), memory_space=pltpu.SMEM),
                      pl.BlockSpec((tt * SUB, LANES), lambda i, *_: (pmap(i), 0)),
                      pl.BlockSpec((tt * SUB, LANES), lambda i, *_: (smap(i), 0))],
            out_specs=pl.BlockSpec(memory_space=pl.ANY),
            scratch_shapes=[pltpu.VMEM((tt * TOP_K, SUB, LANES), F32), pltpu.VMEM((tt * TOP_K, SUB, LANES), F32),
                            pltpu.VMEM((MOE_TILE, SUB, LANES), F32), pltpu.SemaphoreType.DMA((3,))]),
        out_shape=jax.ShapeDtypeStruct((rows, SUB, LANES), F32),
        compiler_params=_cparams(("arbitrary",)),
        name="dispatch",
    )(cnt, seg, gbase, pad_start, pad_n, tail, lidx_p, lidx_s, hf3_p, hf3_s)


FFN_RING = 3


def _ffn_kernel(be_ref, nbu_ref, xs_ref, wg_ref, wu_ref, wd_ref, ys_ref, xbuf, wg_s, wu_s, wd_s, sems):
    b = pl.program_id(0)
    nbu = nbu_ref[0]
    tm = ys_ref.shape[0] // SUB
    blk = tm * SUB

    def fetch(k):
        slot = k % FFN_RING
        start = k * blk if isinstance(k, int) else pl.multiple_of(k * blk, blk)
        return pltpu.make_async_copy(xs_ref.at[pl.ds(start, blk)], xbuf.at[slot], sems.at[slot])

    @pl.when(b == 0)
    def _():
        for k in range(FFN_RING - 1):
            @pl.when(k < nbu)
            def _():
                fetch(k).start()

    @pl.when(b + FFN_RING - 1 < nbu)
    def _():
        fetch(b + FFN_RING - 1).start()

    @pl.when(b < nbu)
    def _():
        @pl.when((b == 0) | (be_ref[b] != be_ref[jnp.maximum(b - 1, 0)]))
        def _():
            wg_s[...] = wg_ref[0].astype(MXU_DTYPE)
            wu_s[...] = wu_ref[0].astype(MXU_DTYPE)
            wd_s[...] = wd_ref[0].astype(MXU_DTYPE)

        fetch(b).wait()
        x = _from_row_tiles(xbuf.at[b % FFN_RING], tm)
        hb = _silu(_mm(x, wg_s[...])) * _mm(x, wu_s[...])
        _to_row_tiles(ys_ref, _mm(hb, wd_s[...]))

    @pl.when(b >= nbu_ref[0])
    def _():
        ys_ref[...] = jnp.zeros(ys_ref.shape, F32)


def _ffn(block_e, nb_used, xs2, wg, wu, wd):
    rows = xs2.shape[0] // SUB
    d, f = wg.shape[1], wg.shape[2]
    tm = MOE_TILE
    nb = rows // tm
    wmap = lambda b, be, nbu: (be[b], 0, 0)
    return pl.pallas_call(
        _ffn_kernel,
        grid_spec=pltpu.PrefetchScalarGridSpec(
            num_scalar_prefetch=2,
            grid=(nb,),
            in_specs=[pl.BlockSpec(memory_space=pl.ANY), pl.BlockSpec((1, d, f), wmap),
                      pl.BlockSpec((1, d, f), wmap), pl.BlockSpec((1, f, d), wmap)],
            out_specs=pl.BlockSpec((tm * SUB, LANES), lambda b, be, nbu: (b, 0)),
            scratch_shapes=[pltpu.VMEM((FFN_RING, tm * SUB, LANES), F32),
                            pltpu.VMEM((d, f), MXU_DTYPE), pltpu.VMEM((d, f), MXU_DTYPE),
                            pltpu.VMEM((f, d), MXU_DTYPE), pltpu.SemaphoreType.DMA((FFN_RING,))]),
        out_shape=jax.ShapeDtypeStruct((rows * SUB, LANES), F32),
        compiler_params=_cparams(("arbitrary",)),
        name="ffn",
    )(block_e, nb_used, xs2, wg, wu, wd)


def _combine_kernel(cnt_ref, seg_ref, gb_ref, lidx_ref, wts_ref, hf3_ref, x1_ref, gt2_ref, gpost_ref,
                    wsg_ref, wsu_ref, wsd_ref, ys_ref, y_ref, buf0, buf1, y3_ref, sems, *, tile0, ne):
    bb, ll, d = x1_ref.shape
    tt = bb * ll
    i = pl.program_id(0)
    n = pl.num_programs(0)

    def fetch(j, buf, sem):
        def seg_body(e, carry):
            c, s, g = cnt_ref[j * ne + e], seg_ref[j * ne + e], gb_ref[j * ne + e]
            _for_each_piece(c, lambda off, size: pltpu.make_async_copy(
                ys_ref.at[pl.ds(g + off, size)], buf.at[pl.ds(s + off, size)], sem).start(), SEG_BITS)
            return carry
        lax.fori_loop(0, ne, seg_body, 0)

    def run(buf, sem, nbuf, nsem):
        @pl.when(i == 0)
        def _():
            fetch(tile0, buf, sem)

        @pl.when(i + 1 < n)
        def _():
            fetch(tile0 + i + 1, nbuf, nsem)

        pltpu.make_async_copy(ys_ref.at[pl.ds(0, tt * TOP_K)], buf, sem).wait()

        def tok_body(tb, carry):
            for u in range(SORT_UNROLL):
                t = tb * SORT_UNROLL + u
                acc = wts_ref[t * TOP_K] * buf[lidx_ref[t * TOP_K]]
                for k in range(1, TOP_K):
                    acc = acc + wts_ref[t * TOP_K + k] * buf[lidx_ref[t * TOP_K + k]]
                y3_ref[pl.ds(pl.multiple_of(t * SUB, SUB), SUB), :] = acc
            return carry
        lax.fori_loop(0, tt // SORT_UNROLL, tok_body, 0)

    @pl.when(i % 2 == 0)
    def _():
        run(buf0, sems.at[0], buf1, sems.at[1])

    @pl.when(i % 2 == 1)
    def _():
        run(buf1, sems.at[1], buf0, sems.at[0])

    hf = _from_row_tiles(hf3_ref, tt)
    shared = _mm(_silu(_mm(hf, wsg_ref[...])) * _mm(hf, wsu_ref[...]), wsd_ref[...])
    yf = (_from_row_tiles(y3_ref, tt) + shared).reshape(bb, ll, d)
    y_ref[...] = x1_ref[...] + gt2_ref[...] * _rms(yf, gpost_ref[...])


def _combine(cnt, seg, gbase, lidx, wts, hf3, x1, gt2, gpost, wsg, wsu, wsd, ys, tile0, ne):
    b, l, d = x1.shape
    tt = ROUTE_TILE
    ll = min(l, tt)
    bb = tt // ll
    nl = l // ll
    nt = (b // bb) * nl
    fs = wsg.shape[1]
    if nl == 1:
        tmap = lambda i, *_: (i, 0, 0)
        bmap = lambda i, *_: (i, 0, 0)
    else:
        tmap = lambda i, *_: (i // nl, i % nl, 0)
        bmap = lambda i, *_: (i // nl, 0, 0)
    wmap = lambda i, *_: (0, 0)
    kspec = lambda: pl.BlockSpec((tt * TOP_K,), lambda i, *_: (i,), memory_space=pltpu.SMEM)
    return pl.pallas_call(
        functools.partial(_combine_kernel, tile0=tile0, ne=ne),
        grid_spec=pltpu.PrefetchScalarGridSpec(
            num_scalar_prefetch=3, grid=(nt,),
            in_specs=[kspec(), kspec(),
                      pl.BlockSpec((tt * SUB, LANES), lambda i, *_: (i, 0)),
                      pl.BlockSpec((bb, ll, d), tmap), pl.BlockSpec((bb, 1, d), bmap),
                      pl.BlockSpec((1, d), wmap), pl.BlockSpec((d, fs), wmap), pl.BlockSpec((d, fs), wmap),
                      pl.BlockSpec((fs, d), wmap),
                      pl.BlockSpec(memory_space=pl.ANY)],
            out_specs=pl.BlockSpec((bb, ll, d), tmap),
            scratch_shapes=[pltpu.VMEM((tt * TOP_K, SUB, LANES), F32), pltpu.VMEM((tt * TOP_K, SUB, LANES), F32),
                            pltpu.VMEM((tt * SUB, LANES), F32), pltpu.SemaphoreType.DMA((2,))]),
        out_shape=jax.ShapeDtypeStruct((b, l, d), F32),
        compiler_params=_cparams(("arbitrary",), VMEM_LIMIT_COMBINE),
        name="combine",
    )(cnt, seg, gbase, lidx, wts, hf3, x1, gt2, gpost, wsg, wsu, wsd, ys)


def _rope_tables(pos, rows):
    half = DH // 2
    freqs = ROPE_BASE ** (-jnp.arange(half, dtype=F32) / half)
    ang = pos.astype(F32)[:, None] * freqs[None, :]
    cos, sin = jnp.cos(ang), jnp.sin(ang)
    cos2 = jnp.concatenate([cos, cos], axis=1)
    sin2 = jnp.concatenate([-sin, sin], axis=1)
    padr = rows - pos.shape[0]
    return jnp.pad(cos2, ((0, padr), (0, 0))), jnp.pad(sin2, ((0, padr), (0, 0)))


def _global_plan(cnt_tiles, n_assign, ne):
    counts = jnp.sum(cnt_tiles, axis=0)
    padded = (counts + MOE_TILE - 1) // MOE_TILE * MOE_TILE
    pend = jnp.cumsum(padded)
    base = pend - padded
    gbase = base[None, :] + jnp.cumsum(cnt_tiles, axis=0) - cnt_tiles
    nb = n_assign // MOE_TILE + ne
    starts = jnp.arange(nb, dtype=jnp.int32) * MOE_TILE
    block_e = jnp.minimum(jnp.sum((pend[None, :] <= starts[:, None]).astype(jnp.int32), axis=1), ne - 1)
    nb_used = pend[-1:] // MOE_TILE
    tail = jnp.concatenate([pend[-1:], nb - nb_used])
    i32 = lambda v: v.astype(jnp.int32)
    return i32(gbase.reshape(-1)), i32(block_e), i32(nb_used), i32(base + counts), i32(padded - counts), i32(tail), nb


def kernel(x_prompt, x_sample, c_prompt, c_sample, state_mlstm_C, state_mlstm_n, state_mlstm_m, state_mlstm_conv, state_ret_S, w_ada, b_ada, g_mix_pre, g_mix_post, g_ffn_pre, g_ffn_post, w_in, b_gates, conv_w, conv_b, m_gn, r_gn, w_out, w_router, b_router, w_exp_gate, w_exp_up, w_exp_down, w_sh_gate, w_sh_up, w_sh_down):
    depth = w_ada.shape[0]
    bp, lp, d = x_prompt.shape
    bs, ls, _ = x_sample.shape
    mh, rh = state_mlstm_C.shape[2], state_ret_S.shape[2]
    mw, rw = mh * DH, rh * DH
    ne = w_router.shape[2]
    tp, ts = bp * lp, bs * ls
    ntp = tp // ROUTE_TILE

    cos_p, sin_p = _rope_tables(jnp.arange(lp, dtype=jnp.int32), lp)
    cos_s, sin_s = _rope_tables(PAST_LEN + jnp.arange(ls, dtype=jnp.int32), CHUNK if ls % CHUNK else ls)

    hp, hs = x_prompt, x_sample
    new_p, new_s = [], []
    for l in range(depth):
        wi = w_in[l]
        g0, nmain = 4 * mw, 4 * mw + 4 * rw
        w_in_r = jnp.zeros((d, nmain + GATE_LANES), MXU_DTYPE)
        w_in_r = w_in_r.at[:, :g0].set(wi[:, :g0].astype(MXU_DTYPE))
        w_in_r = w_in_r.at[:, g0:nmain].set(wi[:, g0 + 2 * mh:].astype(MXU_DTYPE))
        w_in_r = w_in_r.at[:, nmain:nmain + 2 * mh].set(wi[:, g0:g0 + 2 * mh].astype(MXU_DTYPE))
        bg = jnp.pad(b_gates[l], (0, GATE_LANES - 2 * mh)).reshape(1, GATE_LANES)
        wout = w_out[l].astype(MXU_DTYPE)
        wr = jnp.pad(w_router[l], ((0, 0), (0, GATE_LANES - ne)))
        wr_hi, wr_lo = _split_hi_lo(wr)
        wsg, wsu, wsd = (w_sh_gate[l].astype(MXU_DTYPE), w_sh_up[l].astype(MXU_DTYPE), w_sh_down[l].astype(MXU_DTYPE))
        row = lambda v: v.reshape(1, -1)

        mod = _ada(jnp.concatenate([c_prompt, c_sample], axis=0), w_ada[l], b_ada[l])
        mods_p = [m[:bp].reshape(bp, 1, d) for m in jnp.split(mod, 6, axis=1)]
        mods_s = [m[bp:].reshape(bs, 1, d) for m in jnp.split(mod, 6, axis=1)]

        def lane_m(m):
            return jnp.pad(m, ((0, 0), (mh, GATE_LANES - 2 * mh)))[:, None, :]

        init = (jnp.zeros((bp, mh, DH, DH), F32), jnp.zeros((bp, mh, DH), F32), lane_m(jnp.zeros((bp, mh), F32)),
                jnp.zeros((bp, CONV_W - 1, 2 * mw), F32), jnp.zeros((bp, rh, DH, DH), F32))
        past = (state_mlstm_C[l], state_mlstm_n[l], lane_m(state_mlstm_m[l]), state_mlstm_conv[l], state_ret_S[l])

        def mix(x, mods, state, cos2, sin2):
            sh1, sc1, gt1, sh2, sc2, gt2 = mods
            b, ll, _ = x.shape
            if ll % MIX_ROWS == 0:
                x1, hf3, lg, c_n, n_n, m_n, conv_n, s_n = _mixer(
                    x, mods, cos2, sin2, *state, row(g_mix_pre[l]), w_in_r, conv_w[l], row(conv_b[l]), bg,
                    row(m_gn[l]), row(r_gn[l]), row(g_mix_post[l]), row(g_ffn_pre[l]), wout, wr_hi, wr_lo)
            else:
                z = _inproj(x, sc1, sh1, row(g_mix_pre[l]), w_in_r).reshape(b, ll, -1)
                o_mix, c_n, n_n, m_n, conv_n, s_n = _mixcore_packed(
                    z, cos2, sin2, *state, conv_w[l], row(conv_b[l]), bg, row(m_gn[l]), row(r_gn[l]))
                x1, hf3, lg = _mixout(o_mix, x, gt1, sc2, sh2, row(g_mix_post[l]), row(g_ffn_pre[l]), wout, wr_hi,
                                      wr_lo)
            lidx, wts, cnt, seg = _router(lg, b_router[l], ne)
            flat = lambda v: v.T.reshape(-1)
            return (x1, hf3, flat(lidx), flat(wts), cnt[::SUB, :ne], seg[::SUB, :ne],
                    (c_n, n_n, m_n[:, 0, mh:2 * mh], conv_n, s_n))

        x1_p, hf3_p, lidx_p, wts_p, cnt_p, seg_p, st_p = mix(hp, mods_p, init, cos_p, sin_p)
        x1_s, hf3_s, lidx_s, wts_s, cnt_s, seg_s, st_s = mix(hs, mods_s, past, cos_s, sin_s)

        cnt_t = jnp.concatenate([cnt_p, cnt_s], axis=0)
        cnt = cnt_t.reshape(-1)
        seg = jnp.concatenate([seg_p, seg_s], axis=0).reshape(-1)
        gbase, block_e, nb_used, pad_start, pad_n, tail, nb = _global_plan(cnt_t, (tp + ts) * TOP_K, ne)
        xs = _dispatch(cnt, seg, gbase, pad_start, pad_n, tail, lidx_p, lidx_s, hf3_p, hf3_s, nb * MOE_TILE, ne)
        ys = _ffn(block_e, nb_used, xs.reshape(nb * MOE_TILE * SUB, LANES), w_exp_gate[l], w_exp_up[l], w_exp_down[l])
        ys = ys.reshape(nb * MOE_TILE, SUB, LANES)
        hp = _combine(cnt, seg, gbase, lidx_p, wts_p, hf3_p, x1_p, mods_p[5], row(g_ffn_post[l]), wsg, wsu, wsd, ys, 0, ne)
        hs = _combine(cnt, seg, gbase, lidx_s, wts_s, hf3_s, x1_s, mods_s[5], row(g_ffn_post[l]), wsg, wsu, wsd, ys, ntp, ne)
        new_p.append(st_p)
        new_s.append(st_s)

    p_state = [jnp.stack([st[i] for st in new_p]) for i in range(5)]
    s_state = [jnp.stack([st[i] for st in new_s]) for i in range(5)]
    return (hp, hs, *p_state, *s_state)
```

```python
import functools
import math

import jax
import jax.numpy as jnp
from jax import lax
from jax.experimental import pallas as pl
from jax.experimental.pallas import tpu as pltpu

F32 = jnp.float32
MXU_DTYPE = jnp.bfloat16

NORM_EPS = 1e-6
ROPE_BASE = 10000.0
PAST_LEN = 16384
CHUNK = 128
CONV_W = 4
DH = 128
TOP_K = 8
N_GROUPS = 8
TOPK_GROUPS = 4
ROUTED_SCALE = 2.5
GATE_LANES = 128

V7X_VMEM_BYTES = 64 * 1024 * 1024
VMEM_LIMIT = 48 * 1024 * 1024
VMEM_LIMIT_COMBINE = 56 * 1024 * 1024

SUB, LANES = 8, 128

TOK_TILE = 256
ROUTE_TILE = 512
MOE_TILE = 512


def _cparams(sem, vmem_limit=VMEM_LIMIT):
    return pltpu.CompilerParams(dimension_semantics=sem, vmem_limit_bytes=vmem_limit)


def _silu(x):
    return x * jax.nn.sigmoid(x)


def _rms(x, g):
    return x * lax.rsqrt(jnp.mean(x * x, axis=-1, keepdims=True) + NORM_EPS) * g


def _mm(a, b):
    return jnp.dot(a.astype(MXU_DTYPE), b.astype(MXU_DTYPE), preferred_element_type=F32)


def _mm_nt(a, b):
    return lax.dot_general(a.astype(MXU_DTYPE), b.astype(MXU_DTYPE), (((1,), (1,)), ((), ())),
                           preferred_element_type=F32)


def _mm_tn(a, b):
    return lax.dot_general(a.astype(MXU_DTYPE), b.astype(MXU_DTYPE), (((0,), (0,)), ((), ())),
                           preferred_element_type=F32)


def _ada_kernel(c_ref, w_ref, b_ref, o_ref):
    o_ref[...] = _mm(_silu(c_ref[...]), w_ref[...]) + b_ref[...]


def _ada(c, w_ada, b_ada):
    n, d = c.shape
    dout = w_ada.shape[1]
    tn = 1024
    return pl.pallas_call(
        _ada_kernel,
        grid=(dout // tn,),
        in_specs=[pl.BlockSpec((n, d), lambda j: (0, 0)),
                  pl.BlockSpec((d, tn), lambda j: (0, j)),
                  pl.BlockSpec((1, tn), lambda j: (0, j))],
        out_specs=pl.BlockSpec((n, tn), lambda j: (0, j)),
        out_shape=jax.ShapeDtypeStruct((n, dout), F32),
        compiler_params=_cparams(("arbitrary",)),
        name="ada",
    )(c, w_ada, b_ada.reshape(1, dout))


def _inproj_kernel(x_ref, sc_ref, sh_ref, g_ref, w_ref, z_ref):
    x = x_ref[...]
    h = _rms(x, g_ref[...]) * (1.0 + sc_ref[...]) + sh_ref[...]
    bb, ll, d = x.shape
    z_ref[...] = _mm(h.reshape(bb * ll, d), w_ref[...])


def _inproj(x, sc, sh, g, w):
    b, l, d = x.shape
    n = w.shape[1]
    ll = min(l, TOK_TILE)
    bb = TOK_TILE // ll
    mod_spec = pl.BlockSpec((bb, 1, d), lambda i, j: (i, 0, 0))
    return pl.pallas_call(
        _inproj_kernel,
        grid=(b // bb, l // ll),
        in_specs=[pl.BlockSpec((bb, ll, d), lambda i, j: (i, j, 0)), mod_spec, mod_spec,
                  pl.BlockSpec((1, d), lambda i, j: (0, 0)),
                  pl.BlockSpec((d, n), lambda i, j: (0, 0))],
        out_specs=pl.BlockSpec((bb * ll, n), lambda i, j: (i * (l // ll) + j, 0)),
        out_shape=jax.ShapeDtypeStruct((b * l, n), F32),
        compiler_params=_cparams(("arbitrary", "arbitrary")),
        name="inproj",
    )(x, sc, sh, g, w)


def _cumsum_rows(x):
    n = x.shape[0]
    row = lax.broadcasted_iota(jnp.int32, x.shape, 0)
    s = 1
    while s < n:
        x = x + jnp.where(row >= s, pltpu.roll(x, s, axis=0), 0.0)
        s *= 2
    return x


def _head_norm(y, g):
    mu = jnp.mean(y, axis=-1, keepdims=True)
    yc = y - mu
    var = jnp.mean(yc * yc, axis=-1, keepdims=True)
    return yc * lax.rsqrt(var + NORM_EPS) * g


def _log_sigmoid(x):
    return jnp.minimum(x, 0.0) - jnp.log1p(jnp.exp(-jnp.abs(x)))


def _chunk_step(rows, put, cos2, sin2, c_ref, n_ref, m_ref, conv_ref, s_ref, xcat_ref,
                convw_ref, convb_ref, bg_ref, mgn_ref, rgn_ref, mh, rh, after_head=None):
    cl = CHUNK
    mw = mh * DH
    rw = rh * DH
    o_qk, o_mv, o_mo = 0, 2 * mw, 3 * mw
    o_rq = 4 * mw
    o_rk, o_rv, o_rg, o_gt = o_rq + rw, o_rq + 2 * rw, o_rq + 3 * rw, o_rq + 4 * rw

    xcat_ref[8:8 + cl, :] = rows(o_qk, o_qk + 2 * mw)
    qk = convb_ref[...]
    for t in range(CONV_W):
        qk = qk + xcat_ref[8 - (CONV_W - 1) + t:8 - (CONV_W - 1) + t + cl, :] * convw_ref[t:t + 1, :]
    qk = _silu(qk)
    new_conv = xcat_ref[8 + cl - (CONV_W - 1):8 + cl, :]
    conv_ref[0] = new_conv
    xcat_ref[8 - (CONV_W - 1):8, :] = new_conv

    row_c = lax.broadcasted_iota(jnp.int32, (cl, 1), 0)
    ii = lax.broadcasted_iota(jnp.int32, (cl, cl), 0)
    jj = lax.broadcasted_iota(jnp.int32, (cl, cl), 1)
    causal = jj <= ii
    eye = jj == ii

    def to_row(col):
        return jnp.sum(jnp.where(eye, col, 0.0), axis=0, keepdims=True)

    g = rows(o_gt, o_gt + GATE_LANES) + bg_ref[...]
    bcum = _cumsum_rows(_log_sigmoid(g))
    m_prev = m_ref[0]
    a_all = bcum + m_prev
    lane = lax.broadcasted_iota(jnp.int32, (1, GATE_LANES), 1)
    m_new_row = m_prev

    for h in range(mh):
        q = qk[:, h * DH:(h + 1) * DH]
        k = qk[:, mw + h * DH:mw + (h + 1) * DH] * (DH ** -0.5)
        v = rows(o_mv + h * DH, o_mv + (h + 1) * DH)
        b_col = bcum[:, mh + h:mh + h + 1]
        logi_col = g[:, h:h + 1]
        a_col = a_all[:, mh + h:mh + h + 1]
        dmat = jnp.where(causal, b_col + to_row(logi_col - b_col), -jnp.inf)
        m_t = jnp.maximum(a_col, jnp.max(dmat, axis=1, keepdims=True))
        w_intra = jnp.exp(dmat - m_t)
        w_inter = jnp.exp(a_col - m_t)
        c_old = c_ref[0, h]
        n_old = n_ref[0, h:h + 1, :]
        s = _mm_nt(q, k) * w_intra
        num = _mm(s, v) + w_inter * _mm(q, c_old)
        den = jnp.sum(s, axis=1, keepdims=True) + w_inter * jnp.sum(q * n_old, axis=1, keepdims=True)
        hh = num * (1.0 / jnp.maximum(jnp.abs(den), jnp.exp(-m_t)))
        m_new = m_t[cl - 1:cl, :]
        b_last = b_col[cl - 1:cl, :]
        w_old = jnp.exp(b_last + m_prev[:, mh + h:mh + h + 1] - m_new)
        w_k = jnp.exp(b_last - b_col + logi_col - m_new)
        c_ref[0, h] = w_old * c_old + _mm_tn(k, w_k * v)
        n_ref[0, h:h + 1, :] = w_old * n_old + jnp.sum(w_k * k, axis=0, keepdims=True)
        m_new_row = jnp.where(lane == mh + h, m_new, m_new_row)
        om = _head_norm(hh, mgn_ref[:, h * DH:(h + 1) * DH]) * jax.nn.sigmoid(rows(o_mo + h * DH, o_mo + (h + 1) * DH))
        put(h * DH, (h + 1) * DH, om)
        if after_head is not None:
            after_head(h)
    m_ref[0] = m_new_row

    rel = (ii - jj).astype(F32)
    row_f = row_c.astype(F32)

    def rot(x):
        return x * cos2 + pltpu.roll(x, DH // 2, axis=1) * sin2

    for h in range(rh):
        log_g = math.log1p(-2.0 ** (-5.0 - h))
        q = rot(rows(o_rq + h * DH, o_rq + (h + 1) * DH))
        k = rot(rows(o_rk + h * DH, o_rk + (h + 1) * DH)) * (DH ** -0.5)
        v = rows(o_rv + h * DH, o_rv + (h + 1) * DH)
        decay = jnp.where(causal, jnp.exp(jnp.maximum(rel, 0.0) * log_g), 0.0)
        s_old = s_ref[0, h]
        y = _mm(_mm_nt(q, k) * decay, v) + jnp.exp((row_f + 1.0) * log_g) * _mm(q, s_old)
        k_w = jnp.exp((cl - 1.0 - row_f) * log_g)
        s_ref[0, h] = math.exp(cl * log_g) * s_old + _mm_tn(k, k_w * v)
        orr = _head_norm(y, rgn_ref[:, h * DH:(h + 1) * DH]) * _silu(rows(o_rg + h * DH, o_rg + (h + 1) * DH))
        put(mw + h * DH, mw + (h + 1) * DH, orr)
        if after_head is not None:
            after_head(mh + h)


MIX_ROWS = 256


def _mixer_kernel(x_ref, xn_ref, sc1_ref, sh1_ref, sc1n_ref, sh1n_ref, gt1_ref, sc2_ref, sh2_ref, cos_ref, sin_ref,
                  c0_ref, n0_ref, m0_ref, conv0_ref, s0_ref,
                  gpre1_ref, win_ref, convw_ref, convb_ref, bg_ref, mgn_ref, rgn_ref,
                  gpost_ref, gpre2_ref, wout_ref, wr_hi_ref, wr_lo_ref,
                  x1_ref, hf3_ref, lg_ref, c_ref, n_ref, m_ref, conv_ref, s_ref,
                  z_ref, o_ref, xcat_ref, *, mh, rh):
    i, j = pl.program_id(0), pl.program_id(1)
    step = i * pl.num_programs(1) + j
    cur = step % 2
    rs, d = x_ref.shape[1], x_ref.shape[2]
    nz = z_ref.shape[2]
    nchunks = rs // CHUNK
    npieces = nchunks * (mh + rh)
    cuts = [(nz // LANES * p // npieces) * LANES for p in range(npieces + 1)]

    def project(x3, sc_ref, sh_ref):
        return (_rms(x3, gpre1_ref[...]) * (1.0 + sc_ref[...]) + sh_ref[...]).reshape(rs, d)

    @pl.when(step == 0)
    def _():
        z_ref[0] = _mm(project(x_ref[...], sc1_ref, sh1_ref), win_ref[...])

    @pl.when(j == 0)
    def _():
        c_ref[...] = c0_ref[...]
        n_ref[...] = n0_ref[...]
        m_ref[...] = m0_ref[...]
        s_ref[...] = s0_ref[...]
        xcat_ref[8 - (CONV_W - 1):8, :] = conv0_ref[0]

    h_next = project(xn_ref[...], sc1n_ref, sh1n_ref).astype(MXU_DTYPE)
    for c in range(nchunks):
        r0 = c * CHUNK
        def project_piece(hd, c=c):
            lo, hi = cuts[c * (mh + rh) + hd], cuts[c * (mh + rh) + hd + 1]
            z_ref[1 - cur, :, lo:hi] = jnp.dot(h_next, win_ref[:, lo:hi], preferred_element_type=F32)

        def put(lo, hi, v, r0=r0):
            o_ref[r0:r0 + CHUNK, lo:hi] = v

        _chunk_step(lambda lo, hi, r0=r0: z_ref[cur, r0:r0 + CHUNK, lo:hi], put,
                    cos_ref[r0:r0 + CHUNK, :], sin_ref[r0:r0 + CHUNK, :],
                    c_ref, n_ref, m_ref, conv_ref, s_ref, xcat_ref,
                    convw_ref, convb_ref, bg_ref, mgn_ref, rgn_ref, mh, rh, after_head=project_piece)
    x = x_ref[...]
    _mixout_body(o_ref[...].reshape(1, rs, o_ref.shape[1]), x, gt1_ref, sc2_ref, sh2_ref, gpost_ref, gpre2_ref,
                 wout_ref, wr_hi_ref, wr_lo_ref, x1_ref, hf3_ref, lg_ref)


def _mixer(x, mods, cos2, sin2, c0, n0, m0, conv0, s0, gpre1, win, convw, convb, bg, mgn, rgn,
           gpost, gpre2, wout, wr_hi, wr_lo):
    sh1, sc1, gt1, sh2, sc2, gt2 = mods
    b, l, d = x.shape
    assert d == SUB * LANES and l % MIX_ROWS == 0
    mh, rh = c0.shape[1], s0.shape[1]
    mw, rw = mh * DH, rh * DH
    nz, ne = win.shape[1], wr_hi.shape[1]
    rs = MIX_ROWS
    assert nz % LANES == 0
    nj = l // rs
    bmap4 = lambda i, j: (i, 0, 0, 0)
    bmap3 = lambda i, j: (i, 0, 0)
    tmap = lambda i, j: (i, j, 0)
    fmap = lambda i, j: (i * nj + j, 0)
    once = lambda a: pl.BlockSpec(a.shape, lambda i, j: (0, 0), pipeline_mode=pl.Buffered(1))
    mod_spec = pl.BlockSpec((1, 1, d), bmap3)
    nxt = lambda i, j: jnp.minimum(i * nj + j + 1, b * nj - 1)
    x_next_spec = pl.BlockSpec((1, rs, d), lambda i, j: (nxt(i, j) // nj, nxt(i, j) % nj, 0))
    mod_next_spec = pl.BlockSpec((1, 1, d), lambda i, j: (nxt(i, j) // nj, 0, 0))
    rope_spec = pl.BlockSpec((rs, DH), lambda i, j: (j, 0))
    state_specs = [pl.BlockSpec((1, mh, DH, DH), bmap4), pl.BlockSpec((1, mh, DH), bmap3),
                   pl.BlockSpec((1, 1, GATE_LANES), bmap3), pl.BlockSpec((1, CONV_W - 1, 2 * mw), bmap3),
                   pl.BlockSpec((1, rh, DH, DH), bmap4)]
    state_shapes = [jax.ShapeDtypeStruct(c0.shape, F32), jax.ShapeDtypeStruct(n0.shape, F32),
                    jax.ShapeDtypeStruct(m0.shape, F32), jax.ShapeDtypeStruct(conv0.shape, F32),
                    jax.ShapeDtypeStruct(s0.shape, F32)]
    return pl.pallas_call(
        functools.partial(_mixer_kernel, mh=mh, rh=rh),
        grid=(b, nj),
        in_specs=[pl.BlockSpec((1, rs, d), tmap), x_next_spec, mod_spec, mod_spec, mod_next_spec, mod_next_spec]
                 + [mod_spec] * 3 + [rope_spec, rope_spec] + state_specs
                 + [once(a) for a in (gpre1, win, convw, convb, bg, mgn, rgn, gpost, gpre2, wout, wr_hi, wr_lo)],
        out_specs=[pl.BlockSpec((1, rs, d), tmap), pl.BlockSpec((rs * SUB, LANES), fmap),
                   pl.BlockSpec((rs, ne), fmap)] + state_specs,
        out_shape=[jax.ShapeDtypeStruct((b, l, d), F32), jax.ShapeDtypeStruct((b * l * SUB, LANES), F32),
                   jax.ShapeDtypeStruct((b * l, ne), F32)] + state_shapes,
        scratch_shapes=[pltpu.VMEM((2, rs, nz), F32), pltpu.VMEM((rs, mw + rw), F32),
                        pltpu.VMEM((CHUNK + 8, 2 * mw), F32)],
        compiler_params=_cparams(("arbitrary", "arbitrary")),
        name="mixer",
    )(x, x, sc1, sh1, sc1, sh1, gt1, sc2, sh2, cos2, sin2, c0, n0, m0, conv0, s0,
      gpre1, win, convw, convb, bg, mgn, rgn, gpost, gpre2, wout, wr_hi, wr_lo)


PACK_SEQS = CHUNK // SUB


def _mixcore_packed_kernel(z_ref, cos_ref, sin_ref, c0_ref, n0_ref, m0_ref, conv0_ref, s0_ref,
                           convw_ref, convb_ref, bg_ref, mgn_ref, rgn_ref,
                           out_ref, c_ref, n_ref, m_ref, conv_ref, s_ref, *, mh, rh):
    g_n, l_n, _ = z_ref.shape
    rr = g_n * l_n
    mw, rw = mh * DH, rh * DH
    o_qk, o_mv, o_mo = 0, 2 * mw, 3 * mw
    o_rq = 4 * mw
    o_rk, o_rv, o_rg, o_gt = o_rq + rw, o_rq + 2 * rw, o_rq + 3 * rw, o_rq + 4 * rw

    def rows(lo, hi):
        return z_ref[:, :, lo:hi].reshape(rr, hi - lo)

    def per_seq(x):
        return jnp.broadcast_to(x[:, None, :], (g_n, l_n, x.shape[-1])).reshape(rr, x.shape[-1])

    def last_of_seq(x):
        x3 = x.reshape(g_n, l_n, x.shape[-1])
        return jnp.broadcast_to(x3[:, l_n - 1:l_n, :], x3.shape).reshape(x.shape)

    def full(col):
        return jnp.broadcast_to(col, (rr, DH))

    pos_c = lax.broadcasted_iota(jnp.int32, (rr, 1), 0) % l_n
    ii = lax.broadcasted_iota(jnp.int32, (rr, rr), 0)
    jj = lax.broadcasted_iota(jnp.int32, (rr, rr), 1)
    mask = ((ii // l_n) == (jj // l_n)) & (jj <= ii)
    eye = jj == ii

    def to_row(col):
        return jnp.sum(jnp.where(eye, col, 0.0), axis=0, keepdims=True)

    x = rows(o_qk, o_qk + 2 * mw)
    e = conv0_ref[...].reshape(rr, 2 * mw)
    qk = convb_ref[...]
    for t in range(CONV_W):
        d = CONV_W - 1 - t
        xd = x if d == 0 else jnp.where(pos_c >= d, pltpu.roll(x, d, axis=0),
                                        pltpu.roll(e, (d - (CONV_W - 1)) % rr, axis=0))
        qk = qk + xd * convw_ref[t:t + 1, :]
    qk = _silu(qk)
    conv_ref[...] = x.reshape(g_n, l_n, 2 * mw)[:, l_n - (CONV_W - 1):, :]

    g = rows(o_gt, o_gt + GATE_LANES) + bg_ref[...]
    bcum = _log_sigmoid(g)
    s_ = 1
    while s_ < l_n:
        bcum = bcum + jnp.where(pos_c >= s_, pltpu.roll(bcum, s_, axis=0), 0.0)
        s_ *= 2
    m_prev = per_seq(m0_ref[:, 0, :])
    a_all = bcum + m_prev
    lane = lax.broadcasted_iota(jnp.int32, (1, GATE_LANES), 1)
    even_seq = (lax.broadcasted_iota(jnp.int32, (rr, 1), 0) // l_n) % 2 == 0
    m_new_rows = m_prev

    def state_dot(qb, st_ref, h):
        parts = []
        for s in range(g_n):
            pb, lo = s // 2, (s % 2) * l_n
            res = jnp.dot(qb[pb * 2 * l_n:(pb + 1) * 2 * l_n], st_ref[s, h].astype(MXU_DTYPE),
                          preferred_element_type=F32)
            parts.append(res[lo:lo + l_n])
        return jnp.concatenate(parts, axis=0)

    def state_update(k, wv, decay_full, st_in, st_out, h):
        k_even = jnp.where(even_seq, k, 0.0).astype(MXU_DTYPE)
        k_odd = jnp.where(even_seq, 0.0, k).astype(MXU_DTYPE)
        wvb = wv.astype(MXU_DTYPE)
        for s in range(g_n):
            pb = s // 2
            kp = (k_even if s % 2 == 0 else k_odd)[pb * 2 * l_n:(pb + 1) * 2 * l_n]
            upd = lax.dot_general(kp, wvb[pb * 2 * l_n:(pb + 1) * 2 * l_n], (((0,), (0,)), ((), ())),
                                  preferred_element_type=F32)
            st_out[s, h] = decay_full[s * l_n:s * l_n + 1, :] * st_in[s, h] + upd

    for h in range(mh):
        q = qk[:, h * DH:(h + 1) * DH]
        k = qk[:, mw + h * DH:mw + (h + 1) * DH] * (DH ** -0.5)
        v = rows(o_mv + h * DH, o_mv + (h + 1) * DH)
        b_col = bcum[:, mh + h:mh + h + 1]
        logi_col = g[:, h:h + 1]
        a_col = a_all[:, mh + h:mh + h + 1]
        dmat = jnp.where(mask, b_col + to_row(logi_col - b_col), -jnp.inf)
        m_t = jnp.maximum(a_col, jnp.max(dmat, axis=1, keepdims=True))
        w_intra = jnp.exp(dmat - m_t)
        w_inter = jnp.exp(a_col - m_t)
        n_rows = per_seq(n0_ref[:, h, :])
        s = _mm_nt(q, k) * w_intra
        num = _mm(s, v) + w_inter * state_dot(q.astype(MXU_DTYPE), c0_ref, h)
        den = jnp.sum(s, axis=1, keepdims=True) + w_inter * jnp.sum(q * n_rows, axis=1, keepdims=True)
        hh = num * (1.0 / jnp.maximum(jnp.abs(den), jnp.exp(-m_t)))
        m_new = last_of_seq(full(m_t))
        b_last = last_of_seq(full(b_col))
        w_old = jnp.exp(b_last + full(m_prev[:, mh + h:mh + h + 1]) - m_new)
        w_k = jnp.exp(b_last - full(b_col) + full(logi_col) - m_new)
        state_update(k, w_k * v, w_old, c0_ref, c_ref, h)
        n_ref[:, h, :] = (w_old.reshape(g_n, l_n, DH)[:, 0, :] * n0_ref[:, h, :]
                          + jnp.sum((w_k * k).reshape(g_n, l_n, DH), axis=1))
        m_new_rows = jnp.where(lane == mh + h, m_new, m_new_rows)
        om = _head_norm(hh, mgn_ref[:, h * DH:(h + 1) * DH]) * jax.nn.sigmoid(rows(o_mo + h * DH, o_mo + (h + 1) * DH))
        out_ref[:, :, h * DH:(h + 1) * DH] = om.reshape(g_n, l_n, DH)
    m_ref[...] = m_new_rows.reshape(g_n, l_n, GATE_LANES)[:, 0:1, :]

    cos2 = cos_ref[...]
    sin2 = sin_ref[...]
    rel = (ii - jj).astype(F32)
    pos_f = pos_c.astype(F32)

    def rot(x):
        return x * cos2 + pltpu.roll(x, DH // 2, axis=1) * sin2

    for h in range(rh):
        log_g = math.log1p(-2.0 ** (-5.0 - h))
        q = rot(rows(o_rq + h * DH, o_rq + (h + 1) * DH))
        k = rot(rows(o_rk + h * DH, o_rk + (h + 1) * DH)) * (DH ** -0.5)
        v = rows(o_rv + h * DH, o_rv + (h + 1) * DH)
        decay = jnp.where(mask, jnp.exp(jnp.maximum(rel, 0.0) * log_g), 0.0)
        y = (_mm(_mm_nt(q, k) * decay, v)
             + jnp.exp((pos_f + 1.0) * log_g) * state_dot(q.astype(MXU_DTYPE), s0_ref, h))
        k_w = jnp.exp((l_n - 1.0 - pos_f) * log_g)
        state_update(k, k_w * v, jnp.full((rr, DH), math.exp(l_n * log_g), F32), s0_ref, s_ref, h)
        orr = _head_norm(y, rgn_ref[:, h * DH:(h + 1) * DH]) * _silu(rows(o_rg + h * DH, o_rg + (h + 1) * DH))
        out_ref[:, :, mw + h * DH:mw + (h + 1) * DH] = orr.reshape(g_n, l_n, DH)


def _mixcore_packed(z, cos2, sin2, c0, n0, m0, conv0, s0, convw, convb, bg, mgn, rgn):
    b, l, nz = z.shape
    assert l == SUB and b % PACK_SEQS == 0
    mh, rh = c0.shape[1], s0.shape[1]
    mw, rw = mh * DH, rh * DH
    g_n = PACK_SEQS
    conv0p = jnp.pad(conv0, ((0, 0), (0, l - (CONV_W - 1)), (0, 0)))
    tile = lambda t: jnp.tile(t[:l], (g_n, 1))
    bmap4 = lambda i: (i, 0, 0, 0)
    bmap3 = lambda i: (i, 0, 0)
    wmap = lambda i: (0, 0)
    c_spec, n_spec = pl.BlockSpec((g_n, mh, DH, DH), bmap4), pl.BlockSpec((g_n, mh, DH), bmap3)
    m_spec, s_spec = pl.BlockSpec((g_n, 1, GATE_LANES), bmap3), pl.BlockSpec((g_n, rh, DH, DH), bmap4)
    return pl.pallas_call(
        functools.partial(_mixcore_packed_kernel, mh=mh, rh=rh),
        grid=(b // g_n,),
        in_specs=[pl.BlockSpec((g_n, l, nz), bmap3),
                  pl.BlockSpec((g_n * l, DH), wmap), pl.BlockSpec((g_n * l, DH), wmap),
                  c_spec, n_spec, m_spec, pl.BlockSpec((g_n, l, 2 * mw), bmap3), s_spec,
                  pl.BlockSpec(convw.shape, wmap), pl.BlockSpec(convb.shape, wmap), pl.BlockSpec(bg.shape, wmap),
                  pl.BlockSpec(mgn.shape, wmap), pl.BlockSpec(rgn.shape, wmap)],
        out_specs=[pl.BlockSpec((g_n, l, mw + rw), bmap3), c_spec, n_spec, m_spec,
                   pl.BlockSpec((g_n, CONV_W - 1, 2 * mw), bmap3), s_spec],
        out_shape=[jax.ShapeDtypeStruct((b, l, mw + rw), F32), jax.ShapeDtypeStruct(c0.shape, F32),
                   jax.ShapeDtypeStruct(n0.shape, F32), jax.ShapeDtypeStruct(m0.shape, F32),
                   jax.ShapeDtypeStruct(conv0.shape, F32), jax.ShapeDtypeStruct(s0.shape, F32)],
        compiler_params=_cparams(("arbitrary",)),
        name="mixcore_packed",
    )(z, tile(cos2), tile(sin2), c0, n0, m0, conv0p, s0, convw, convb, bg, mgn, rgn)


def _split_hi_lo(x):
    hi = x.astype(MXU_DTYPE)
    lo = (x - hi.astype(F32)).astype(MXU_DTYPE)
    return hi, lo


def _to_row_tiles(ref, x):
    n = x.shape[0]
    for j in range(SUB):
        ref[pl.ds(j, n, stride=SUB), :] = x[:, j * LANES:(j + 1) * LANES]


def _from_row_tiles(ref, n):
    return jnp.concatenate([ref[pl.ds(j, n, stride=SUB), :] for j in range(SUB)], axis=1)


def _mixout_kernel(o_ref, x_ref, gt1_ref, sc2_ref, sh2_ref, gpost_ref, gpre_ref, wout_ref, wr_hi_ref, wr_lo_ref,
                   x1_ref, hf3_ref, lg_ref):
    _mixout_body(o_ref[...], x_ref[...], gt1_ref, sc2_ref, sh2_ref, gpost_ref, gpre_ref, wout_ref, wr_hi_ref,
                 wr_lo_ref, x1_ref, hf3_ref, lg_ref)


def _mixout_body(o, x, gt1_ref, sc2_ref, sh2_ref, gpost_ref, gpre_ref, wout_ref, wr_hi_ref, wr_lo_ref,
                 x1_ref, hf3_ref, lg_ref):
    bb, ll, d = x.shape
    ym = _mm(o.reshape(bb * ll, o.shape[-1]), wout_ref[...]).reshape(bb, ll, d)
    x1 = x + gt1_ref[...] * _rms(ym, gpost_ref[...])
    x1_ref[...] = x1
    hf = (_rms(x1, gpre_ref[...]) * (1.0 + sc2_ref[...]) + sh2_ref[...]).reshape(bb * ll, d)
    _to_row_tiles(hf3_ref, hf)
    hi, lo = _split_hi_lo(hf)
    lg_ref[...] = _mm(hi, wr_hi_ref[...]) + (_mm(lo, wr_hi_ref[...]) + _mm(hi, wr_lo_ref[...]))


def _mixout(o, x, gt1, sc2, sh2, gpost, gpre, wout, wr_hi, wr_lo):
    b, l, d = x.shape
    assert d == SUB * LANES
    w = o.shape[-1]
    ne = wr_hi.shape[1]
    ll = min(l, TOK_TILE)
    bb = TOK_TILE // ll
    tmap = lambda i, j: (i, j, 0)
    fmap = lambda i, j: (i * (l // ll) + j, 0)
    mod_spec = pl.BlockSpec((bb, 1, d), lambda i, j: (i, 0, 0))
    wmap = lambda i, j: (0, 0)
    return pl.pallas_call(
        _mixout_kernel,
        grid=(b // bb, l // ll),
        in_specs=[pl.BlockSpec((bb, ll, w), tmap), pl.BlockSpec((bb, ll, d), tmap), mod_spec, mod_spec, mod_spec,
                  pl.BlockSpec((1, d), wmap), pl.BlockSpec((1, d), wmap), pl.BlockSpec((w, d), wmap),
                  pl.BlockSpec((d, ne), wmap), pl.BlockSpec((d, ne), wmap)],
        out_specs=[pl.BlockSpec((bb, ll, d), tmap), pl.BlockSpec((bb * ll * SUB, LANES), fmap),
                   pl.BlockSpec((bb * ll, ne), fmap)],
        out_shape=[jax.ShapeDtypeStruct((b, l, d), F32), jax.ShapeDtypeStruct((b * l * SUB, LANES), F32),
                   jax.ShapeDtypeStruct((b * l, ne), F32)],
        compiler_params=_cparams(("arbitrary", "arbitrary")),
        name="mixout",
    )(o, x, gt1, sc2, sh2, gpost, gpre, wout, wr_hi, wr_lo)


def _router_kernel(lg_ref, br_ref, lidx_ref, wts_ref, cnt_ref, seg_ref, *, ne):
    tm = lg_ref.shape[0]
    gsz = ne // N_GROUPS
    s = jax.nn.sigmoid(lg_ref[...].T[:ne, :])
    sb = s + br_ref[...]
    sb3 = sb.reshape(N_GROUPS, gsz, tm)
    e3 = lax.broadcasted_iota(jnp.int32, sb3.shape, 1)
    m1 = jnp.max(sb3, axis=1, keepdims=True)
    first = jnp.min(jnp.where(sb3 == m1, e3, gsz), axis=1, keepdims=True)
    m2 = jnp.max(jnp.where(e3 == first, -jnp.inf, sb3), axis=1, keepdims=True)
    gs = (m1 + m2).reshape(N_GROUPS, tm)
    gi = lax.broadcasted_iota(jnp.int32, gs.shape, 0)
    grank = jnp.zeros(gs.shape, F32)
    for g in range(N_GROUPS):
        r = gs[g:g + 1, :]
        grank = grank + jnp.where(r > gs, 1.0, jnp.where(r == gs, jnp.where(g < gi, 1.0, 0.0), 0.0))
    gsel = jnp.where(grank < TOPK_GROUPS, 1.0, 0.0)
    emask = jnp.broadcast_to(gsel.reshape(N_GROUPS, 1, tm), sb3.shape).reshape(ne, tm) > 0.0
    masked = jnp.where(emask, sb, -jnp.inf)
    ei = lax.broadcasted_iota(jnp.int32, masked.shape, 0)
    cur = masked
    sel_f = jnp.zeros(masked.shape, F32)
    for _ in range(TOP_K):
        best = jnp.max(cur, axis=0, keepdims=True)
        first_e = jnp.min(jnp.where(cur == best, ei, ne), axis=0, keepdims=True)
        pick = ei == first_e
        sel_f = jnp.where(pick, 1.0, sel_f)
        cur = jnp.where(pick, -jnp.inf, cur)
    sel = sel_f > 0.0
    w = jnp.where(sel, s, 0.0)
    w = w / jnp.sum(w, axis=0, keepdims=True) * ROUTED_SCALE

    self = jnp.where(sel, 1.0, 0.0)
    er = lax.broadcasted_iota(jnp.int32, (ne, ne), 0)
    ec = lax.broadcasted_iota(jnp.int32, (ne, ne), 1)
    slot = _mm(jnp.where(ec < er, 1.0, 0.0), self)
    tr = lax.broadcasted_iota(jnp.int32, (tm, tm), 0)
    tc = lax.broadcasted_iota(jnp.int32, (tm, tm), 1)
    pos = _mm(self, jnp.where(tr < tc, 1.0, 0.0))
    seg = jnp.sum(slot, axis=1, keepdims=True)
    lrow = seg + pos
    lidx_rows, w_rows = [], []
    for k in range(TOP_K):
        pick = sel & (slot == float(k))
        lidx_rows.append(jnp.sum(jnp.where(pick, lrow, 0.0), axis=0, keepdims=True))
        w_rows.append(jnp.sum(jnp.where(pick, w, 0.0), axis=0, keepdims=True))
    lidx_ref[...] = jnp.concatenate(lidx_rows, axis=0).astype(jnp.int32)
    wts_ref[...] = jnp.concatenate(w_rows, axis=0)
    ones = jnp.ones((SUB, tm), F32)
    padz = jnp.zeros((LANES - ne, tm), F32)
    cnt_ref[...] = _mm_nt(ones, jnp.concatenate([self, padz], axis=0)).astype(jnp.int32)
    seg_ref[...] = _mm_nt(ones, jnp.concatenate([slot, padz], axis=0)).astype(jnp.int32)


def _router(logits, b_router, ne):
    t, lanes = logits.shape
    tm = ROUTE_TILE
    nt = t // tm
    kspec = pl.BlockSpec((TOP_K, tm), lambda i: (0, i))
    cspec = pl.BlockSpec((SUB, LANES), lambda i: (i, 0))
    return pl.pallas_call(
        functools.partial(_router_kernel, ne=ne),
        grid=(nt,),
        in_specs=[pl.BlockSpec((tm, lanes), lambda i: (i, 0)), pl.BlockSpec((ne, 1), lambda i: (0, 0))],
        out_specs=[kspec, kspec, cspec, cspec],
        out_shape=[jax.ShapeDtypeStruct((TOP_K, t), jnp.int32), jax.ShapeDtypeStruct((TOP_K, t), F32),
                   jax.ShapeDtypeStruct((nt * SUB, LANES), jnp.int32),
                   jax.ShapeDtypeStruct((nt * SUB, LANES), jnp.int32)],
        compiler_params=_cparams(("arbitrary",)),
        name="router",
    )(logits, b_router.reshape(ne, 1))


SEG_BITS = ROUTE_TILE.bit_length()
PAD_BITS = (MOE_TILE - 1).bit_length()


RARE_BIT = 7


def _for_each_piece(n, fn, bits):
    def pieces(lo, hi):
        for bit in range(hi - 1, lo - 1, -1):
            @pl.when(((n >> bit) & 1) == 1)
            def _():
                fn((n >> (bit + 1)) << (bit + 1), 1 << bit)

    if bits > RARE_BIT:
        @pl.when(n >= (1 << RARE_BIT))
        def _():
            pieces(RARE_BIT, bits)
    pieces(0, min(bits, RARE_BIT))


SORT_UNROLL = 4


def _zero_rows(ps_ref, pn_ref, tail_ref, xs_ref, z_ref, sem, ne, act):
    def pad_body(e, carry):
        _for_each_piece(pn_ref[e], lambda off, size: act(
            pltpu.make_async_copy(z_ref.at[pl.ds(0, size)], xs_ref.at[pl.ds(ps_ref[e] + off, size)], sem)),
            PAD_BITS)
        return carry
    lax.fori_loop(0, ne, pad_body, 0)

    def tail_body(b, carry):
        act(pltpu.make_async_copy(z_ref, xs_ref.at[pl.ds(tail_ref[0] + b * MOE_TILE, MOE_TILE)], sem))
        return carry
    lax.fori_loop(0, tail_ref[1], tail_body, 0)


def _dispatch_kernel(cnt_ref, seg_ref, gb_ref, ps_ref, pn_ref, tail_ref, lidx_p_ref, lidx_s_ref, hf3_p_ref, hf3_s_ref,
                     xs_ref, buf0, buf1, z_ref, sems, *, ntp, ne):
    i = pl.program_id(0)
    n = pl.num_programs(0)
    tt = lidx_p_ref.shape[0] // TOP_K

    def whole(buf, sem):
        return pltpu.make_async_copy(buf, xs_ref.at[pl.ds(0, tt * TOP_K)], sem)

    @pl.when(i == 0)
    def _():
        z_ref[...] = jnp.zeros(z_ref.shape, F32)
        _zero_rows(ps_ref, pn_ref, tail_ref, xs_ref, z_ref, sems.at[2], ne, lambda c: c.start())

    def run(buf, sem):
        @pl.when(i >= 2)
        def _():
            whole(buf, sem).wait()

        def sort_from(hf3_ref, lidx_ref):
            def sort_body(tb, carry):
                for u in range(SORT_UNROLL):
                    t = tb * SORT_UNROLL + u
                    v = hf3_ref[pl.ds(pl.multiple_of(t * SUB, SUB), SUB), :]
                    for k in range(TOP_K):
                        buf[lidx_ref[t * TOP_K + k]] = v
                return carry
            lax.fori_loop(0, tt // SORT_UNROLL, sort_body, 0)

        @pl.when(i < ntp)
        def _():
            sort_from(hf3_p_ref, lidx_p_ref)

        @pl.when(i >= ntp)
        def _():
            sort_from(hf3_s_ref, lidx_s_ref)

        def seg_body(e, carry):
            c, s, g = cnt_ref[i * ne + e], seg_ref[i * ne + e], gb_ref[i * ne + e]
            _for_each_piece(c, lambda off, size: pltpu.make_async_copy(
                buf.at[pl.ds(s + off, size)], xs_ref.at[pl.ds(g + off, size)], sem).start(), SEG_BITS)
            return carry
        lax.fori_loop(0, ne, seg_body, 0)

    @pl.when(i % 2 == 0)
    def _():
        run(buf0, sems.at[0])

    @pl.when(i % 2 == 1)
    def _():
        run(buf1, sems.at[1])

    @pl.when(i == 0)
    def _():
        _zero_rows(ps_ref, pn_ref, tail_ref, xs_ref, z_ref, sems.at[2], ne, lambda c: c.wait())

    @pl.when(i == n - 1)
    def _():
        @pl.when(i % 2 == 0)
        def _():
            whole(buf0, sems.at[0]).wait()

            @pl.when(i >= 1)
            def _():
                whole(buf1, sems.at[1]).wait()

        @pl.when(i % 2 == 1)
        def _():
            whole(buf1, sems.at[1]).wait()
            whole(buf0, sems.at[0]).wait()


def _dispatch(cnt, seg, gbase, pad_start, pad_n, tail, lidx_p, lidx_s, hf3_p, hf3_s, rows, ne):
    tt = ROUTE_TILE
    ntp, nts =lidx_p.shape[0] // (tt * TOP_K), lidx_s.shape[0] // (tt * TOP_K)
    pmap = lambda i, *_: jnp.minimum(i, ntp - 1)
    smap = lambda i, *_: jnp.maximum(i - ntp, 0)
    return pl.pallas_call(
        functools.partial(_dispatch_kernel, ntp=ntp, ne=ne),
        grid_spec=pltpu.PrefetchScalarGridSpec(
            num_scalar_prefetch=6, grid=(ntp + nts,),
            in_specs=[pl.BlockSpec((tt * TOP_K,), lambda i, *_: (pmap(i),), memory_space=pltpu.SMEM),
                      pl.BlockSpec((tt * TOP_K,), lambda i, *_: (smap(i),), memory_space=pltpu.SMEM),
                      pl.BlockSpec((tt * SUB, LANES), lambda i, *_: (pmap(i), 0)),
                      pl.BlockSpec((tt * SUB, LANES), lambda i, *_: (smap(i), 0))],
            out_specs=pl.BlockSpec(memory_space=pl.ANY),
            scratch_shapes=[pltpu.VMEM((tt * TOP_K, SUB, LANES), F32), pltpu.VMEM((tt * TOP_K, SUB, LANES), F32),
                            pltpu.VMEM((MOE_TILE, SUB, LANES), F32), pltpu.SemaphoreType.DMA((3,))]),
        out_shape=jax.ShapeDtypeStruct((rows, SUB, LANES), F32),
        compiler_params=_cparams(("arbitrary",)),
        name="dispatch",
    )(cnt, seg, gbase, pad_start, pad_n, tail, lidx_p, lidx_s, hf3_p, hf3_s)


FFN_RING = 3


def _ffn_kernel(be_ref, nbu_ref, xs_ref, wg_ref, wu_ref, wd_ref, ys_ref, xbuf, ybuf, wg_s, wu_s, wd_s, sems, osems):
    b = pl.program_id(0)
    nb = pl.num_programs(0)
    nbu = nbu_ref[0]
    blk = xbuf.shape[1]
    tm = blk // SUB

    def fetch(k):
        slot = k % FFN_RING
        start = k * blk if isinstance(k, int) else pl.multiple_of(k * blk, blk)
        return pltpu.make_async_copy(xs_ref.at[pl.ds(start, blk)], xbuf.at[slot], sems.at[slot])

    def flush(k):
        slot = k % FFN_RING
        return pltpu.make_async_copy(ybuf.at[slot], ys_ref.at[pl.ds(pl.multiple_of(k * blk, blk), blk)],
                                     osems.at[slot])

    @pl.when(b >= FFN_RING)
    def _():
        flush(b - FFN_RING).wait()
    yslot = ybuf.at[b % FFN_RING]

    @pl.when(b == 0)
    def _():
        for k in range(FFN_RING - 1):
            @pl.when(k < nbu)
            def _():
                fetch(k).start()

    @pl.when(b + FFN_RING - 1 < nbu)
    def _():
        fetch(b + FFN_RING - 1).start()

    @pl.when(b < nbu)
    def _():
        @pl.when((b == 0) | (be_ref[b] != be_ref[jnp.maximum(b - 1, 0)]))
        def _():
            wg_s[...] = wg_ref[0].astype(MXU_DTYPE)
            wu_s[...] = wu_ref[0].astype(MXU_DTYPE)
            wd_s[...] = wd_ref[0].astype(MXU_DTYPE)

        fetch(b).wait()
        x = _from_row_tiles(xbuf.at[b % FFN_RING], tm)
        hb = _silu(_mm(x, wg_s[...])) * _mm(x, wu_s[...])
        _to_row_tiles(yslot, _mm(hb, wd_s[...]))

    @pl.when(b >= nbu)
    def _():
        yslot[...] = jnp.zeros(yslot.shape, F32)

    flush(b).start()

    @pl.when(b == nb - 1)
    def _():
        for back in range(FFN_RING):
            @pl.when(b - back >= 0)
            def _():
                flush(b - back).wait()


def _ffn(block_e, nb_used, xs2, wg, wu, wd):
    rows = xs2.shape[0] // SUB
    d, f = wg.shape[1], wg.shape[2]
    tm = MOE_TILE
    nb = rows // tm
    wmap = lambda b, be, nbu: (be[b], 0, 0)
    return pl.pallas_call(
        _ffn_kernel,
        grid_spec=pltpu.PrefetchScalarGridSpec(
            num_scalar_prefetch=2,
            grid=(nb,),
            in_specs=[pl.BlockSpec(memory_space=pl.ANY), pl.BlockSpec((1, d, f), wmap),
                      pl.BlockSpec((1, d, f), wmap), pl.BlockSpec((1, f, d), wmap)],
            out_specs=pl.BlockSpec(memory_space=pl.ANY),
            scratch_shapes=[pltpu.VMEM((FFN_RING, tm * SUB, LANES), F32), pltpu.VMEM((FFN_RING, tm * SUB, LANES), F32),
                            pltpu.VMEM((d, f), MXU_DTYPE), pltpu.VMEM((d, f), MXU_DTYPE),
                            pltpu.VMEM((f, d), MXU_DTYPE), pltpu.SemaphoreType.DMA((FFN_RING,)),
                            pltpu.SemaphoreType.DMA((FFN_RING,))]),
        out_shape=jax.ShapeDtypeStruct((rows * SUB, LANES), F32),
        compiler_params=_cparams(("arbitrary",)),
        name="ffn",
    )(block_e, nb_used, xs2, wg, wu, wd)


def _combine_kernel(cnt_ref, seg_ref, gb_ref, lidx_ref, wts_ref, hf3_ref, x1_ref, gt2_ref, gpost_ref,
                    wsg_ref, wsu_ref, wsd_ref, ys_ref, y_ref, buf0, buf1, y3_ref, sems, *, tile0, ne):
    bb, ll, d = x1_ref.shape
    tt = bb * ll
    i = pl.program_id(0)
    n = pl.num_programs(0)

    def fetch(j, buf, sem):
        def seg_body(e, carry):
            c, s, g = cnt_ref[j * ne + e], seg_ref[j * ne + e], gb_ref[j * ne + e]
            _for_each_piece(c, lambda off, size: pltpu.make_async_copy(
                ys_ref.at[pl.ds(g + off, size)], buf.at[pl.ds(s + off, size)], sem).start(), SEG_BITS)
            return carry
        lax.fori_loop(0, ne, seg_body, 0)

    def run(buf, sem, nbuf, nsem):
        @pl.when(i == 0)
        def _():
            fetch(tile0, buf, sem)

        @pl.when(i + 1 < n)
        def _():
            fetch(tile0 + i + 1, nbuf, nsem)

        pltpu.make_async_copy(ys_ref.at[pl.ds(0, tt * TOP_K)], buf, sem).wait()

        def tok_body(tb, carry):
            for u in range(SORT_UNROLL):
                t = tb * SORT_UNROLL + u
                acc = wts_ref[t * TOP_K] * buf[lidx_ref[t * TOP_K]]
                for k in range(1, TOP_K):
                    acc = acc + wts_ref[t * TOP_K + k] * buf[lidx_ref[t * TOP_K + k]]
                y3_ref[pl.ds(pl.multiple_of(t * SUB, SUB), SUB), :] = acc
            return carry
        lax.fori_loop(0, tt // SORT_UNROLL, tok_body, 0)

    @pl.when(i % 2 == 0)
    def _():
        run(buf0, sems.at[0], buf1, sems.at[1])

    @pl.when(i % 2 == 1)
    def _():
        run(buf1, sems.at[1], buf0, sems.at[0])

    hf = _from_row_tiles(hf3_ref, tt)
    shared = _mm(_silu(_mm(hf, wsg_ref[...])) * _mm(hf, wsu_ref[...]), wsd_ref[...])
    yf = (_from_row_tiles(y3_ref, tt) + shared).reshape(bb, ll, d)
    y_ref[...] = x1_ref[...] + gt2_ref[...] * _rms(yf, gpost_ref[...])


def _combine(cnt, seg, gbase, lidx, wts, hf3, x1, gt2, gpost, wsg, wsu, wsd, ys, tile0, ne):
    b, l, d = x1.shape
    tt = ROUTE_TILE
    ll = min(l, tt)
    bb = tt // ll
    nl = l // ll
    nt = (b // bb) * nl
    fs = wsg.shape[1]
    if nl == 1:
        tmap = lambda i, *_: (i, 0, 0)
        bmap = lambda i, *_: (i, 0, 0)
    else:
        tmap = lambda i, *_: (i // nl, i % nl, 0)
        bmap = lambda i, *_: (i // nl, 0, 0)
    wmap = lambda i, *_: (0, 0)
    kspec = lambda: pl.BlockSpec((tt * TOP_K,), lambda i, *_: (i,), memory_space=pltpu.SMEM)
    return pl.pallas_call(
        functools.partial(_combine_kernel, tile0=tile0, ne=ne),
        grid_spec=pltpu.PrefetchScalarGridSpec(
            num_scalar_prefetch=3, grid=(nt,),
            in_specs=[kspec(), kspec(),
                      pl.BlockSpec((tt * SUB, LANES), lambda i, *_: (i, 0)),
                      pl.BlockSpec((bb, ll, d), tmap), pl.BlockSpec((bb, 1, d), bmap),
                      pl.BlockSpec((1, d), wmap), pl.BlockSpec((d, fs), wmap), pl.BlockSpec((d, fs), wmap),
                      pl.BlockSpec((fs, d), wmap),
                      pl.BlockSpec(memory_space=pl.ANY)],
            out_specs=pl.BlockSpec((bb, ll, d), tmap),
            scratch_shapes=[pltpu.VMEM((tt * TOP_K, SUB, LANES), F32), pltpu.VMEM((tt * TOP_K, SUB, LANES), F32),
                            pltpu.VMEM((tt * SUB, LANES), F32), pltpu.SemaphoreType.DMA((2,))]),
        out_shape=jax.ShapeDtypeStruct((b, l, d), F32),
        compiler_params=_cparams(("arbitrary",), VMEM_LIMIT_COMBINE),
        name="combine",
    )(cnt, seg, gbase, lidx, wts, hf3, x1, gt2, gpost, wsg, wsu, wsd, ys)


def _rope_tables(pos, rows):
    half = DH // 2
    freqs = ROPE_BASE ** (-jnp.arange(half, dtype=F32) / half)
    ang = pos.astype(F32)[:, None] * freqs[None, :]
    cos, sin = jnp.cos(ang), jnp.sin(ang)
    cos2 = jnp.concatenate([cos, cos], axis=1)
    sin2 = jnp.concatenate([-sin, sin], axis=1)
    padr = rows - pos.shape[0]
    return jnp.pad(cos2, ((0, padr), (0, 0))), jnp.pad(sin2, ((0, padr), (0, 0)))


def _global_plan(cnt_tiles, n_assign, ne):
    counts = jnp.sum(cnt_tiles, axis=0)
    padded = (counts + MOE_TILE - 1) // MOE_TILE * MOE_TILE
    pend = jnp.cumsum(padded)
    base = pend - padded
    gbase = base[None, :] + jnp.cumsum(cnt_tiles, axis=0) - cnt_tiles
    nb = n_assign // MOE_TILE + ne
    starts = jnp.arange(nb, dtype=jnp.int32) * MOE_TILE
    block_e = jnp.minimum(jnp.sum((pend[None, :] <= starts[:, None]).astype(jnp.int32), axis=1), ne - 1)
    nb_used = pend[-1:] // MOE_TILE
    tail = jnp.concatenate([pend[-1:], nb - nb_used])
    i32 = lambda v: v.astype(jnp.int32)
    return i32(gbase.reshape(-1)), i32(block_e), i32(nb_used), i32(base + counts), i32(padded - counts), i32(tail), nb


def kernel(x_prompt, x_sample, c_prompt, c_sample, state_mlstm_C, state_mlstm_n, state_mlstm_m, state_mlstm_conv, state_ret_S, w_ada, b_ada, g_mix_pre, g_mix_post, g_ffn_pre, g_ffn_post, w_in, b_gates, conv_w, conv_b, m_gn, r_gn, w_out, w_router, b_router, w_exp_gate, w_exp_up, w_exp_down, w_sh_gate, w_sh_up, w_sh_down):
    depth = w_ada.shape[0]
    bp, lp, d = x_prompt.shape
    bs, ls, _ = x_sample.shape
    mh, rh = state_mlstm_C.shape[2], state_ret_S.shape[2]
    mw, rw = mh * DH, rh * DH
    ne = w_router.shape[2]
    tp, ts = bp * lp, bs * ls
    ntp = tp // ROUTE_TILE

    cos_p, sin_p = _rope_tables(jnp.arange(lp, dtype=jnp.int32), lp)
    cos_s, sin_s = _rope_tables(PAST_LEN + jnp.arange(ls, dtype=jnp.int32), CHUNK if ls % CHUNK else ls)

    hp, hs = x_prompt, x_sample
    new_p, new_s = [], []
    for l in range(depth):
        wi = w_in[l]
        g0, nmain = 4 * mw, 4 * mw + 4 * rw
        w_in_r = jnp.zeros((d, nmain + GATE_LANES), MXU_DTYPE)
        w_in_r = w_in_r.at[:, :g0].set(wi[:, :g0].astype(MXU_DTYPE))
        w_in_r = w_in_r.at[:, g0:nmain].set(wi[:, g0 + 2 * mh:].astype(MXU_DTYPE))
        w_in_r = w_in_r.at[:, nmain:nmain + 2 * mh].set(wi[:, g0:g0 + 2 * mh].astype(MXU_DTYPE))
        bg = jnp.pad(b_gates[l], (0, GATE_LANES - 2 * mh)).reshape(1, GATE_LANES)
        wout = w_out[l].astype(MXU_DTYPE)
        wr = jnp.pad(w_router[l], ((0, 0), (0, GATE_LANES - ne)))
        wr_hi, wr_lo = _split_hi_lo(wr)
        wsg, wsu, wsd = (w_sh_gate[l].astype(MXU_DTYPE), w_sh_up[l].astype(MXU_DTYPE), w_sh_down[l].astype(MXU_DTYPE))
        row = lambda v: v.reshape(1, -1)

        mod = _ada(jnp.concatenate([c_prompt, c_sample], axis=0), w_ada[l], b_ada[l])
        mods_p = [m[:bp].reshape(bp, 1, d) for m in jnp.split(mod, 6, axis=1)]
        mods_s = [m[bp:].reshape(bs, 1, d) for m in jnp.split(mod, 6, axis=1)]

        def lane_m(m):
            return jnp.pad(m, ((0, 0), (mh, GATE_LANES - 2 * mh)))[:, None, :]

        init = (jnp.zeros((bp, mh, DH, DH), F32), jnp.zeros((bp, mh, DH), F32), lane_m(jnp.zeros((bp, mh), F32)),
                jnp.zeros((bp, CONV_W - 1, 2 * mw), F32), jnp.zeros((bp, rh, DH, DH), F32))
        past = (state_mlstm_C[l], state_mlstm_n[l], lane_m(state_mlstm_m[l]), state_mlstm_conv[l], state_ret_S[l])

        def mix(x, mods, state, cos2, sin2):
            sh1, sc1, gt1, sh2, sc2, gt2 = mods
            b, ll, _ = x.shape
            if ll % MIX_ROWS == 0:
                x1, hf3, lg, c_n, n_n, m_n, conv_n, s_n = _mixer(
                    x, mods, cos2, sin2, *state, row(g_mix_pre[l]), w_in_r, conv_w[l], row(conv_b[l]), bg,
                    row(m_gn[l]), row(r_gn[l]), row(g_mix_post[l]), row(g_ffn_pre[l]), wout, wr_hi, wr_lo)
            else:
                z = _inproj(x, sc1, sh1, row(g_mix_pre[l]), w_in_r).reshape(b, ll, -1)
                o_mix, c_n, n_n, m_n, conv_n, s_n = _mixcore_packed(
                    z, cos2, sin2, *state, conv_w[l], row(conv_b[l]), bg, row(m_gn[l]), row(r_gn[l]))
                x1, hf3, lg = _mixout(o_mix, x, gt1, sc2, sh2, row(g_mix_post[l]), row(g_ffn_pre[l]), wout, wr_hi,
                                      wr_lo)
            lidx, wts, cnt, seg = _router(lg, b_router[l], ne)
            flat = lambda v: v.T.reshape(-1)
            return (x1, hf3, flat(lidx), flat(wts), cnt[::SUB, :ne], seg[::SUB, :ne],
                    (c_n, n_n, m_n[:, 0, mh:2 * mh], conv_n, s_n))

        x1_p, hf3_p, lidx_p, wts_p, cnt_p, seg_p, st_p = mix(hp, mods_p, init, cos_p, sin_p)
        x1_s, hf3_s, lidx_s, wts_s, cnt_s, seg_s, st_s = mix(hs, mods_s, past, cos_s, sin_s)

        cnt_t = jnp.concatenate([cnt_p, cnt_s], axis=0)
        cnt = cnt_t.reshape(-1)
        seg = jnp.concatenate([seg_p, seg_s], axis=0).reshape(-1)
        gbase, block_e, nb_used, pad_start, pad_n, tail, nb = _global_plan(cnt_t, (tp + ts) * TOP_K, ne)
        xs = _dispatch(cnt, seg, gbase, pad_start, pad_n, tail, lidx_p, lidx_s, hf3_p, hf3_s, nb * MOE_TILE, ne)
        ys = _ffn(block_e, nb_used, xs.reshape(nb * MOE_TILE * SUB, LANES), w_exp_gate[l], w_exp_up[l], w_exp_down[l])
        ys = ys.reshape(nb * MOE_TILE, SUB, LANES)
        hp = _combine(cnt, seg, gbase, lidx_p, wts_p, hf3_p, x1_p, mods_p[5], row(g_ffn_post[l]), wsg, wsu, wsd, ys, 0, ne)
        hs = _combine(cnt, seg, gbase, lidx_s, wts_s, hf3_s, x1_s, mods_s[5], row(g_ffn_post[l]), wsg, wsu, wsd, ys, ntp, ne)
        new_p.append(st_p)
        new_s.append(st_s)

    p_state = [jnp.stack([st[i] for st in new_p]) for i in range(5)]
    s_state = [jnp.stack([st[i] for st in new_s]) for i in range(5)]
    return (hp, hs, *p_state, *s_state)
```

```python
import functools
import math

import jax
import jax.numpy as jnp
from jax import lax
from jax.experimental import pallas as pl
from jax.experimental.pallas import tpu as pltpu

F32 = jnp.float32
MXU_DTYPE = jnp.bfloat16

NORM_EPS = 1e-6
ROPE_BASE = 10000.0
PAST_LEN = 16384
CHUNK = 128
CONV_W = 4
DH = 128
TOP_K = 8
N_GROUPS = 8
TOPK_GROUPS = 4
ROUTED_SCALE = 2.5
GATE_LANES = 128

V7X_VMEM_BYTES = 64 * 1024 * 1024
VMEM_LIMIT = 48 * 1024 * 1024
VMEM_LIMIT_COMBINE = 56 * 1024 * 1024
assert VMEM_LIMIT < VMEM_LIMIT_COMBINE < V7X_VMEM_BYTES

SUB, LANES = 8, 128

TOK_TILE = 256
ROUTE_TILE = 512
MOE_TILE = 512


def _cparams(sem, vmem_limit=VMEM_LIMIT):
    return pltpu.CompilerParams(dimension_semantics=sem, vmem_limit_bytes=vmem_limit)


def _silu(x):
    return x * jax.nn.sigmoid(x)


def _rms(x, g):
    return x * lax.rsqrt(jnp.mean(x * x, axis=-1, keepdims=True) + NORM_EPS) * g


def _mm(a, b):
    return jnp.dot(a.astype(MXU_DTYPE), b.astype(MXU_DTYPE), preferred_element_type=F32)


def _mm_nt(a, b):
    return lax.dot_general(a.astype(MXU_DTYPE), b.astype(MXU_DTYPE), (((1,), (1,)), ((), ())),
                           preferred_element_type=F32)


def _mm_tn(a, b):
    return lax.dot_general(a.astype(MXU_DTYPE), b.astype(MXU_DTYPE), (((0,), (0,)), ((), ())),
                           preferred_element_type=F32)


def _ada_kernel(c_ref, w_ref, b_ref, o_ref):
    o_ref[...] = _mm(_silu(c_ref[...]), w_ref[...]) + b_ref[...]


def _ada(c, w_ada, b_ada):
    n, d = c.shape
    dout = w_ada.shape[1]
    tn = 1024
    return pl.pallas_call(
        _ada_kernel,
        grid=(dout // tn,),
        in_specs=[pl.BlockSpec((n, d), lambda j: (0, 0)),
                  pl.BlockSpec((d, tn), lambda j: (0, j)),
                  pl.BlockSpec((1, tn), lambda j: (0, j))],
        out_specs=pl.BlockSpec((n, tn), lambda j: (0, j)),
        out_shape=jax.ShapeDtypeStruct((n, dout), F32),
        compiler_params=_cparams(("arbitrary",)),
        name="ada",
    )(c, w_ada, b_ada.reshape(1, dout))


def _inproj_kernel(x_ref, sc_ref, sh_ref, g_ref, w_ref, z_ref):
    x = x_ref[...]
    h = _rms(x, g_ref[...]) * (1.0 + sc_ref[...]) + sh_ref[...]
    bb, ll, d = x.shape
    z_ref[...] = _mm(h.reshape(bb * ll, d), w_ref[...])


def _inproj(x, sc, sh, g, w):
    b, l, d = x.shape
    n = w.shape[1]
    ll = min(l, TOK_TILE)
    bb = TOK_TILE // ll
    mod_spec = pl.BlockSpec((bb, 1, d), lambda i, j: (i, 0, 0))
    return pl.pallas_call(
        _inproj_kernel,
        grid=(b // bb, l // ll),
        in_specs=[pl.BlockSpec((bb, ll, d), lambda i, j: (i, j, 0)), mod_spec, mod_spec,
                  pl.BlockSpec((1, d), lambda i, j: (0, 0)),
                  pl.BlockSpec((d, n), lambda i, j: (0, 0))],
        out_specs=pl.BlockSpec((bb * ll, n), lambda i, j: (i * (l // ll) + j, 0)),
        out_shape=jax.ShapeDtypeStruct((b * l, n), F32),
        compiler_params=_cparams(("arbitrary", "arbitrary")),
        name="inproj",
    )(x, sc, sh, g, w)


def _cumsum_rows(x):
    n = x.shape[0]
    row = lax.broadcasted_iota(jnp.int32, x.shape, 0)
    s = 1
    while s < n:
        x = x + jnp.where(row >= s, pltpu.roll(x, s, axis=0), 0.0)
        s *= 2
    return x


def _head_norm(y, g):
    mu = jnp.mean(y, axis=-1, keepdims=True)
    yc = y - mu
    var = jnp.mean(yc * yc, axis=-1, keepdims=True)
    return yc * lax.rsqrt(var + NORM_EPS) * g


def _log_sigmoid(x):
    return jnp.minimum(x, 0.0) - jnp.log1p(jnp.exp(-jnp.abs(x)))


def _chunk_step(rows, put, cos2, sin2, c_ref, n_ref, m_ref, conv_ref, s_ref, xcat_ref,
                convw_ref, convb_ref, bg_ref, mgn_ref, rgn_ref, mh, rh, after_head=None):
    cl = CHUNK
    mw = mh * DH
    rw = rh * DH
    o_qk, o_mv, o_mo = 0, 2 * mw, 3 * mw
    o_rq = 4 * mw
    o_rk, o_rv, o_rg, o_gt = o_rq + rw, o_rq + 2 * rw, o_rq + 3 * rw, o_rq + 4 * rw

    xcat_ref[8:8 + cl, :] = rows(o_qk, o_qk + 2 * mw)
    qk = convb_ref[...]
    for t in range(CONV_W):
        qk = qk + xcat_ref[8 - (CONV_W - 1) + t:8 - (CONV_W - 1) + t + cl, :] * convw_ref[t:t + 1, :]
    qk = _silu(qk)
    new_conv = xcat_ref[8 + cl - (CONV_W - 1):8 + cl, :]
    conv_ref[0] = new_conv
    xcat_ref[8 - (CONV_W - 1):8, :] = new_conv

    row_c = lax.broadcasted_iota(jnp.int32, (cl, 1), 0)
    ii = lax.broadcasted_iota(jnp.int32, (cl, cl), 0)
    jj = lax.broadcasted_iota(jnp.int32, (cl, cl), 1)
    causal = jj <= ii
    eye = jj == ii

    def to_row(col):
        return jnp.sum(jnp.where(eye, col, 0.0), axis=0, keepdims=True)

    g = rows(o_gt, o_gt + GATE_LANES) + bg_ref[...]
    bcum = _cumsum_rows(_log_sigmoid(g))
    m_prev = m_ref[0]
    a_all = bcum + m_prev
    lane = lax.broadcasted_iota(jnp.int32, (1, GATE_LANES), 1)
    m_new_row = m_prev

    for h in range(mh):
        q = qk[:, h * DH:(h + 1) * DH]
        k = qk[:, mw + h * DH:mw + (h + 1) * DH] * (DH ** -0.5)
        v = rows(o_mv + h * DH, o_mv + (h + 1) * DH)
        b_col = bcum[:, mh + h:mh + h + 1]
        logi_col = g[:, h:h + 1]
        a_col = a_all[:, mh + h:mh + h + 1]
        dmat = jnp.where(causal, b_col + to_row(logi_col - b_col), -jnp.inf)
        m_t = jnp.maximum(a_col, jnp.max(dmat, axis=1, keepdims=True))
        w_intra = jnp.exp(dmat - m_t)
        w_inter = jnp.exp(a_col - m_t)
        c_old = c_ref[0, h]
        n_old = n_ref[0, h:h + 1, :]
        s = _mm_nt(q, k) * w_intra
        num = _mm(s, v) + w_inter * _mm(q, c_old)
        den = jnp.sum(s, axis=1, keepdims=True) + w_inter * jnp.sum(q * n_old, axis=1, keepdims=True)
        hh = num * (1.0 / jnp.maximum(jnp.abs(den), jnp.exp(-m_t)))
        m_new = m_t[cl - 1:cl, :]
        b_last = b_col[cl - 1:cl, :]
        w_old = jnp.exp(b_last + m_prev[:, mh + h:mh + h + 1] - m_new)
        w_k = jnp.exp(b_last - b_col + logi_col - m_new)
        c_ref[0, h] = w_old * c_old + _mm_tn(k, w_k * v)
        n_ref[0, h:h + 1, :] = w_old * n_old + jnp.sum(w_k * k, axis=0, keepdims=True)
        m_new_row = jnp.where(lane == mh + h, m_new, m_new_row)
        om = _head_norm(hh, mgn_ref[:, h * DH:(h + 1) * DH]) * jax.nn.sigmoid(rows(o_mo + h * DH, o_mo + (h + 1) * DH))
        put(h * DH, (h + 1) * DH, om)
        if after_head is not None:
            after_head(h)
    m_ref[0] = m_new_row

    rel = (ii - jj).astype(F32)
    row_f = row_c.astype(F32)

    def rot(x):
        return x * cos2 + pltpu.roll(x, DH // 2, axis=1) * sin2

    for h in range(rh):
        log_g = math.log1p(-2.0 ** (-5.0 - h))
        q = rot(rows(o_rq + h * DH, o_rq + (h + 1) * DH))
        k = rot(rows(o_rk + h * DH, o_rk + (h + 1) * DH)) * (DH ** -0.5)
        v = rows(o_rv + h * DH, o_rv + (h + 1) * DH)
        decay = jnp.where(causal, jnp.exp(jnp.maximum(rel, 0.0) * log_g), 0.0)
        s_old = s_ref[0, h]
        y = _mm(_mm_nt(q, k) * decay, v) + jnp.exp((row_f + 1.0) * log_g) * _mm(q, s_old)
        k_w = jnp.exp((cl - 1.0 - row_f) * log_g)
        s_ref[0, h] = math.exp(cl * log_g) * s_old + _mm_tn(k, k_w * v)
        orr = _head_norm(y, rgn_ref[:, h * DH:(h + 1) * DH]) * _silu(rows(o_rg + h * DH, o_rg + (h + 1) * DH))
        put(mw + h * DH, mw + (h + 1) * DH, orr)
        if after_head is not None:
            after_head(mh + h)


MIX_ROWS = 256


def _mixer_kernel(x_ref, xn_ref, sc1_ref, sh1_ref, sc1n_ref, sh1n_ref, gt1_ref, sc2_ref, sh2_ref, cos_ref, sin_ref,
                  c0_ref, n0_ref, m0_ref, conv0_ref, s0_ref,
                  gpre1_ref, win_ref, convw_ref, convb_ref, bg_ref, mgn_ref, rgn_ref,
                  gpost_ref, gpre2_ref, wout_ref, wr_hi_ref, wr_lo_ref,
                  x1_ref, hf3_ref, lg_ref, c_ref, n_ref, m_ref, conv_ref, s_ref,
                  z_ref, o_ref, xcat_ref, *, mh, rh):
    i, j = pl.program_id(0), pl.program_id(1)
    step = i * pl.num_programs(1) + j
    cur = step % 2
    rs, d = x_ref.shape[1], x_ref.shape[2]
    nz = z_ref.shape[2]
    nchunks = rs // CHUNK
    npieces = nchunks * (mh + rh)
    cuts = [(nz // LANES * p // npieces) * LANES for p in range(npieces + 1)]

    def project(x3, sc_ref, sh_ref):
        return (_rms(x3, gpre1_ref[...]) * (1.0 + sc_ref[...]) + sh_ref[...]).reshape(rs, d)

    @pl.when(step == 0)
    def _():
        z_ref[0] = _mm(project(x_ref[...], sc1_ref, sh1_ref), win_ref[...])

    @pl.when(j == 0)
    def _():
        c_ref[...] = c0_ref[...]
        n_ref[...] = n0_ref[...]
        m_ref[...] = m0_ref[...]
        s_ref[...] = s0_ref[...]
        xcat_ref[8 - (CONV_W - 1):8, :] = conv0_ref[0]

    h_next = project(xn_ref[...], sc1n_ref, sh1n_ref).astype(MXU_DTYPE)
    for c in range(nchunks):
        r0 = c * CHUNK

        def project_piece(hd, c=c):
            lo, hi = cuts[c * (mh + rh) + hd], cuts[c * (mh + rh) + hd + 1]
            z_ref[1 - cur, :, lo:hi] = jnp.dot(h_next, win_ref[:, lo:hi], preferred_element_type=F32)

        def put(lo, hi, v, r0=r0):
            o_ref[r0:r0 + CHUNK, lo:hi] = v

        _chunk_step(lambda lo, hi, r0=r0: z_ref[cur, r0:r0 + CHUNK, lo:hi], put,
                    cos_ref[r0:r0 + CHUNK, :], sin_ref[r0:r0 + CHUNK, :],
                    c_ref, n_ref, m_ref, conv_ref, s_ref, xcat_ref,
                    convw_ref, convb_ref, bg_ref, mgn_ref, rgn_ref, mh, rh, after_head=project_piece)
    x = x_ref[...]
    _mixout_body(o_ref[...].reshape(1, rs, o_ref.shape[1]), x, gt1_ref, sc2_ref, sh2_ref, gpost_ref, gpre2_ref,
                 wout_ref, wr_hi_ref, wr_lo_ref, x1_ref, hf3_ref, lg_ref)


def _mixer(x, mods, cos2, sin2, c0, n0, m0, conv0, s0, gpre1, win, convw, convb, bg, mgn, rgn,
           gpost, gpre2, wout, wr_hi, wr_lo):
    sh1, sc1, gt1, sh2, sc2, gt2 = mods
    b, l, d = x.shape
    assert d == SUB * LANES and l % MIX_ROWS == 0
    mh, rh = c0.shape[1], s0.shape[1]
    mw, rw = mh * DH, rh * DH
    nz, ne = win.shape[1], wr_hi.shape[1]
    rs = MIX_ROWS
    assert nz % LANES == 0
    nj = l // rs
    bmap4 = lambda i, j: (i, 0, 0, 0)
    bmap3 = lambda i, j: (i, 0, 0)
    tmap = lambda i, j: (i, j, 0)
    fmap = lambda i, j: (i * nj + j, 0)
    once = lambda a: pl.BlockSpec(a.shape, lambda i, j: (0, 0), pipeline_mode=pl.Buffered(1))
    mod_spec = pl.BlockSpec((1, 1, d), bmap3)
    nxt = lambda i, j: jnp.minimum(i * nj + j + 1, b * nj - 1)
    x_next_spec = pl.BlockSpec((1, rs, d), lambda i, j: (nxt(i, j) // nj, nxt(i, j) % nj, 0))
    mod_next_spec = pl.BlockSpec((1, 1, d), lambda i, j: (nxt(i, j) // nj, 0, 0))
    rope_spec = pl.BlockSpec((rs, DH), lambda i, j: (j, 0))
    state_specs = [pl.BlockSpec((1, mh, DH, DH), bmap4), pl.BlockSpec((1, mh, DH), bmap3),
                   pl.BlockSpec((1, 1, GATE_LANES), bmap3), pl.BlockSpec((1, CONV_W - 1, 2 * mw), bmap3),
                   pl.BlockSpec((1, rh, DH, DH), bmap4)]
    state_shapes = [jax.ShapeDtypeStruct(c0.shape, F32), jax.ShapeDtypeStruct(n0.shape, F32),
                    jax.ShapeDtypeStruct(m0.shape, F32), jax.ShapeDtypeStruct(conv0.shape, F32),
                    jax.ShapeDtypeStruct(s0.shape, F32)]
    return pl.pallas_call(
        functools.partial(_mixer_kernel, mh=mh, rh=rh),
        grid=(b, nj),
        in_specs=[pl.BlockSpec((1, rs, d), tmap), x_next_spec, mod_spec, mod_spec, mod_next_spec, mod_next_spec]
                 + [mod_spec] * 3 + [rope_spec, rope_spec] + state_specs
                 + [once(a) for a in (gpre1, win, convw, convb, bg, mgn, rgn, gpost, gpre2, wout, wr_hi, wr_lo)],
        out_specs=[pl.BlockSpec((1, rs, d), tmap), pl.BlockSpec((rs * SUB, LANES), fmap),
                   pl.BlockSpec((rs, ne), fmap)] + state_specs,
        out_shape=[jax.ShapeDtypeStruct((b, l, d), F32), jax.ShapeDtypeStruct((b * l * SUB, LANES), F32),
                   jax.ShapeDtypeStruct((b * l, ne), F32)] + state_shapes,
        scratch_shapes=[pltpu.VMEM((2, rs, nz), F32), pltpu.VMEM((rs, mw + rw), F32),
                        pltpu.VMEM((CHUNK + 8, 2 * mw), F32)],
        compiler_params=_cparams(("arbitrary", "arbitrary")),
        name="mixer",
    )(x, x, sc1, sh1, sc1, sh1, gt1, sc2, sh2, cos2, sin2, c0, n0, m0, conv0, s0,
      gpre1, win, convw, convb, bg, mgn, rgn, gpost, gpre2, wout, wr_hi, wr_lo)


PACK_SEQS = CHUNK // SUB


def _mixcore_packed_kernel(z_ref, cos_ref, sin_ref, c0_ref, n0_ref, m0_ref, conv0_ref, s0_ref,
                           convw_ref, convb_ref, bg_ref, mgn_ref, rgn_ref,
                           out_ref, c_ref, n_ref, m_ref, conv_ref, s_ref, *, mh, rh):
    g_n, l_n, _ = z_ref.shape
    rr = g_n * l_n
    mw, rw = mh * DH, rh * DH
    o_qk, o_mv, o_mo = 0, 2 * mw, 3 * mw
    o_rq = 4 * mw
    o_rk, o_rv, o_rg, o_gt = o_rq + rw, o_rq + 2 * rw, o_rq + 3 * rw, o_rq + 4 * rw

    def rows(lo, hi):
        return z_ref[:, :, lo:hi].reshape(rr, hi - lo)

    def per_seq(x):
        return jnp.broadcast_to(x[:, None, :], (g_n, l_n, x.shape[-1])).reshape(rr, x.shape[-1])

    def last_of_seq(x):
        x3 = x.reshape(g_n, l_n, x.shape[-1])
        return jnp.broadcast_to(x3[:, l_n - 1:l_n, :], x3.shape).reshape(x.shape)

    def full(col):
        return jnp.broadcast_to(col, (rr, DH))

    pos_c = lax.broadcasted_iota(jnp.int32, (rr, 1), 0) % l_n
    ii = lax.broadcasted_iota(jnp.int32, (rr, rr), 0)
    jj = lax.broadcasted_iota(jnp.int32, (rr, rr), 1)
    mask = ((ii // l_n) == (jj // l_n)) & (jj <= ii)
    eye = jj == ii

    def to_row(col):
        return jnp.sum(jnp.where(eye, col, 0.0), axis=0, keepdims=True)

    x = rows(o_qk, o_qk + 2 * mw)
    e = conv0_ref[...].reshape(rr, 2 * mw)
    qk = convb_ref[...]
    for t in range(CONV_W):
        d = CONV_W - 1 - t
        xd = x if d == 0 else jnp.where(pos_c >= d, pltpu.roll(x, d, axis=0),
                                        pltpu.roll(e, (d - (CONV_W - 1)) % rr, axis=0))
        qk = qk + xd * convw_ref[t:t + 1, :]
    qk = _silu(qk)
    conv_ref[...] = x.reshape(g_n, l_n, 2 * mw)[:, l_n - (CONV_W - 1):, :]

    g = rows(o_gt, o_gt + GATE_LANES) + bg_ref[...]
    bcum = _log_sigmoid(g)
    s_ = 1
    while s_ < l_n:
        bcum = bcum + jnp.where(pos_c >= s_, pltpu.roll(bcum, s_, axis=0), 0.0)
        s_ *= 2
    m_prev = per_seq(m0_ref[:, 0, :])
    a_all = bcum + m_prev
    lane = lax.broadcasted_iota(jnp.int32, (1, GATE_LANES), 1)
    even_seq = (lax.broadcasted_iota(jnp.int32, (rr, 1), 0) // l_n) % 2 == 0
    m_new_rows = m_prev

    def state_dot(qb, st_ref, h):
        parts = []
        for s in range(g_n):
            pb, lo = s // 2, (s % 2) * l_n
            res = jnp.dot(qb[pb * 2 * l_n:(pb + 1) * 2 * l_n], st_ref[s, h].astype(MXU_DTYPE),
                          preferred_element_type=F32)
            parts.append(res[lo:lo + l_n])
        return jnp.concatenate(parts, axis=0)

    def state_update(k, wv, decay_full, st_in, st_out, h):
        k_even = jnp.where(even_seq, k, 0.0).astype(MXU_DTYPE)
        k_odd = jnp.where(even_seq, 0.0, k).astype(MXU_DTYPE)
        wvb = wv.astype(MXU_DTYPE)
        for s in range(g_n):
            pb = s // 2
            kp = (k_even if s % 2 == 0 else k_odd)[pb * 2 * l_n:(pb + 1) * 2 * l_n]
            upd = lax.dot_general(kp, wvb[pb * 2 * l_n:(pb + 1) * 2 * l_n], (((0,), (0,)), ((), ())),
                                  preferred_element_type=F32)
            st_out[s, h] = decay_full[s * l_n:s * l_n + 1, :] * st_in[s, h] + upd

    for h in range(mh):
        q = qk[:, h * DH:(h + 1) * DH]
        k = qk[:, mw + h * DH:mw + (h + 1) * DH] * (DH ** -0.5)
        v = rows(o_mv + h * DH, o_mv + (h + 1) * DH)
        b_col = bcum[:, mh + h:mh + h + 1]
        logi_col = g[:, h:h + 1]
        a_col = a_all[:, mh + h:mh + h + 1]
        dmat = jnp.where(mask, b_col + to_row(logi_col - b_col), -jnp.inf)
        m_t = jnp.maximum(a_col, jnp.max(dmat, axis=1, keepdims=True))
        w_intra = jnp.exp(dmat - m_t)
        w_inter = jnp.exp(a_col - m_t)
        n_rows = per_seq(n0_ref[:, h, :])
        s = _mm_nt(q, k) * w_intra
        num = _mm(s, v) + w_inter * state_dot(q.astype(MXU_DTYPE), c0_ref, h)
        den = jnp.sum(s, axis=1, keepdims=True) + w_inter * jnp.sum(q * n_rows, axis=1, keepdims=True)
        hh = num * (1.0 / jnp.maximum(jnp.abs(den), jnp.exp(-m_t)))
        m_new = last_of_seq(full(m_t))
        b_last = last_of_seq(full(b_col))
        w_old = jnp.exp(b_last + full(m_prev[:, mh + h:mh + h + 1]) - m_new)
        w_k = jnp.exp(b_last - full(b_col) + full(logi_col) - m_new)
        state_update(k, w_k * v, w_old, c0_ref, c_ref, h)
        n_ref[:, h, :] = (w_old.reshape(g_n, l_n, DH)[:, 0, :] * n0_ref[:, h, :]
                          + jnp.sum((w_k * k).reshape(g_n, l_n, DH), axis=1))
        m_new_rows = jnp.where(lane == mh + h, m_new, m_new_rows)
        om = _head_norm(hh, mgn_ref[:, h * DH:(h + 1) * DH]) * jax.nn.sigmoid(rows(o_mo + h * DH, o_mo + (h + 1) * DH))
        out_ref[:, :, h * DH:(h + 1) * DH] = om.reshape(g_n, l_n, DH)
    m_ref[...] = m_new_rows.reshape(g_n, l_n, GATE_LANES)[:, 0:1, :]

    cos2 = cos_ref[...]
    sin2 = sin_ref[...]
    rel = (ii - jj).astype(F32)
    pos_f = pos_c.astype(F32)

    def rot(x):
        return x * cos2 + pltpu.roll(x, DH // 2, axis=1) * sin2

    for h in range(rh):
        log_g = math.log1p(-2.0 ** (-5.0 - h))
        q = rot(rows(o_rq + h * DH, o_rq + (h + 1) * DH))
        k = rot(rows(o_rk + h * DH, o_rk + (h + 1) * DH)) * (DH ** -0.5)
        v = rows(o_rv + h * DH, o_rv + (h + 1) * DH)
        decay = jnp.where(mask, jnp.exp(jnp.maximum(rel, 0.0) * log_g), 0.0)
        y = (_mm(_mm_nt(q, k) * decay, v)
             + jnp.exp((pos_f + 1.0) * log_g) * state_dot(q.astype(MXU_DTYPE), s0_ref, h))
        k_w = jnp.exp((l_n - 1.0 - pos_f) * log_g)
        state_update(k, k_w * v, jnp.full((rr, DH), math.exp(l_n * log_g), F32), s0_ref, s_ref, h)
        orr = _head_norm(y, rgn_ref[:, h * DH:(h + 1) * DH]) * _silu(rows(o_rg + h * DH, o_rg + (h + 1) * DH))
        out_ref[:, :, mw + h * DH:mw + (h + 1) * DH] = orr.reshape(g_n, l_n, DH)


def _mixcore_packed(z, cos2, sin2, c0, n0, m0, conv0, s0, convw, convb, bg, mgn, rgn):
    b, l, nz = z.shape
    assert l == SUB and b % PACK_SEQS == 0
    mh, rh = c0.shape[1], s0.shape[1]
    mw, rw = mh * DH, rh * DH
    g_n = PACK_SEQS
    conv0p = jnp.pad(conv0, ((0, 0), (0, l - (CONV_W - 1)), (0, 0)))
    tile = lambda t: jnp.tile(t[:l], (g_n, 1))
    bmap4 = lambda i: (i, 0, 0, 0)
    bmap3 = lambda i: (i, 0, 0)
    wmap = lambda i: (0, 0)
    c_spec, n_spec = pl.BlockSpec((g_n, mh, DH, DH), bmap4), pl.BlockSpec((g_n, mh, DH), bmap3)
    m_spec, s_spec = pl.BlockSpec((g_n, 1, GATE_LANES), bmap3), pl.BlockSpec((g_n, rh, DH, DH), bmap4)
    return pl.pallas_call(
        functools.partial(_mixcore_packed_kernel, mh=mh, rh=rh),
        grid=(b // g_n,),
        in_specs=[pl.BlockSpec((g_n, l, nz), bmap3),
                  pl.BlockSpec((g_n * l, DH), wmap), pl.BlockSpec((g_n * l, DH), wmap),
                  c_spec, n_spec, m_spec, pl.BlockSpec((g_n, l, 2 * mw), bmap3), s_spec,
                  pl.BlockSpec(convw.shape, wmap), pl.BlockSpec(convb.shape, wmap), pl.BlockSpec(bg.shape, wmap),
                  pl.BlockSpec(mgn.shape, wmap), pl.BlockSpec(rgn.shape, wmap)],
        out_specs=[pl.BlockSpec((g_n, l, mw + rw), bmap3), c_spec, n_spec, m_spec,
                   pl.BlockSpec((g_n, CONV_W - 1, 2 * mw), bmap3), s_spec],
        out_shape=[jax.ShapeDtypeStruct((b, l, mw + rw), F32), jax.ShapeDtypeStruct(c0.shape, F32),
                   jax.ShapeDtypeStruct(n0.shape, F32), jax.ShapeDtypeStruct(m0.shape, F32),
                   jax.ShapeDtypeStruct(conv0.shape, F32), jax.ShapeDtypeStruct(s0.shape, F32)],
        compiler_params=_cparams(("arbitrary",)),
        name="mixcore_packed",
    )(z, tile(cos2), tile(sin2), c0, n0, m0, conv0p, s0, convw, convb, bg, mgn, rgn)


def _split_hi_lo(x):
    hi = x.astype(MXU_DTYPE)
    lo = (x - hi.astype(F32)).astype(MXU_DTYPE)
    return hi, lo


def _to_row_tiles(ref, x):
    n = x.shape[0]
    for j in range(SUB):
        ref[pl.ds(j, n, stride=SUB), :] = x[:, j * LANES:(j + 1) * LANES]


def _from_row_tiles(ref, n):
    return jnp.concatenate([ref[pl.ds(j, n, stride=SUB), :] for j in range(SUB)], axis=1)


def _mixout_kernel(o_ref, x_ref, gt1_ref, sc2_ref, sh2_ref, gpost_ref, gpre_ref, wout_ref, wr_hi_ref, wr_lo_ref,
                   x1_ref, hf3_ref, lg_ref):
    _mixout_body(o_ref[...], x_ref[...], gt1_ref, sc2_ref, sh2_ref, gpost_ref, gpre_ref, wout_ref, wr_hi_ref,
                 wr_lo_ref, x1_ref, hf3_ref, lg_ref)


def _mixout_body(o, x, gt1_ref, sc2_ref, sh2_ref, gpost_ref, gpre_ref, wout_ref, wr_hi_ref, wr_lo_ref,
                 x1_ref, hf3_ref, lg_ref):
    bb, ll, d = x.shape
    ym = _mm(o.reshape(bb * ll, o.shape[-1]), wout_ref[...]).reshape(bb, ll, d)
    x1 = x + gt1_ref[...] * _rms(ym, gpost_ref[...])
    x1_ref[...] = x1
    hf = (_rms(x1, gpre_ref[...]) * (1.0 + sc2_ref[...]) + sh2_ref[...]).reshape(bb * ll, d)
    _to_row_tiles(hf3_ref, hf)
    hi, lo = _split_hi_lo(hf)
    lg_ref[...] = _mm(hi, wr_hi_ref[...]) + (_mm(lo, wr_hi_ref[...]) + _mm(hi, wr_lo_ref[...]))


def _mixout(o, x, gt1, sc2, sh2, gpost, gpre, wout, wr_hi, wr_lo):
    b, l, d = x.shape
    assert d == SUB * LANES
    w = o.shape[-1]
    ne = wr_hi.shape[1]
    ll = min(l, TOK_TILE)
    bb = TOK_TILE // ll
    tmap = lambda i, j: (i, j, 0)
    fmap = lambda i, j: (i * (l // ll) + j, 0)
    mod_spec = pl.BlockSpec((bb, 1, d), lambda i, j: (i, 0, 0))
    wmap = lambda i, j: (0, 0)
    return pl.pallas_call(
        _mixout_kernel,
        grid=(b // bb, l // ll),
        in_specs=[pl.BlockSpec((bb, ll, w), tmap), pl.BlockSpec((bb, ll, d), tmap), mod_spec, mod_spec, mod_spec,
                  pl.BlockSpec((1, d), wmap), pl.BlockSpec((1, d), wmap), pl.BlockSpec((w, d), wmap),
                  pl.BlockSpec((d, ne), wmap), pl.BlockSpec((d, ne), wmap)],
        out_specs=[pl.BlockSpec((bb, ll, d), tmap), pl.BlockSpec((bb * ll * SUB, LANES), fmap),
                   pl.BlockSpec((bb * ll, ne), fmap)],
        out_shape=[jax.ShapeDtypeStruct((b, l, d), F32), jax.ShapeDtypeStruct((b * l * SUB, LANES), F32),
                   jax.ShapeDtypeStruct((b * l, ne), F32)],
        compiler_params=_cparams(("arbitrary", "arbitrary")),
        name="mixout",
    )(o, x, gt1, sc2, sh2, gpost, gpre, wout, wr_hi, wr_lo)


def _router_kernel(lg_ref, br_ref, lidx_ref, wts_ref, cnt_ref, seg_ref, *, ne):
    tm = lg_ref.shape[0]
    gsz = ne // N_GROUPS
    s = jax.nn.sigmoid(lg_ref[...].T[:ne, :])
    sb = s + br_ref[...]
    sb3 = sb.reshape(N_GROUPS, gsz, tm)
    e3 = lax.broadcasted_iota(jnp.int32, sb3.shape, 1)
    m1 = jnp.max(sb3, axis=1, keepdims=True)
    first = jnp.min(jnp.where(sb3 == m1, e3, gsz), axis=1, keepdims=True)
    m2 = jnp.max(jnp.where(e3 == first, -jnp.inf, sb3), axis=1, keepdims=True)
    gs = (m1 + m2).reshape(N_GROUPS, tm)
    gi = lax.broadcasted_iota(jnp.int32, gs.shape, 0)
    grank = jnp.zeros(gs.shape, F32)
    for g in range(N_GROUPS):
        r = gs[g:g + 1, :]
        grank = grank + jnp.where(r > gs, 1.0, jnp.where(r == gs, jnp.where(g < gi, 1.0, 0.0), 0.0))
    gsel = jnp.where(grank < TOPK_GROUPS, 1.0, 0.0)
    emask = jnp.broadcast_to(gsel.reshape(N_GROUPS, 1, tm), sb3.shape).reshape(ne, tm) > 0.0
    masked = jnp.where(emask, sb, -jnp.inf)
    ei = lax.broadcasted_iota(jnp.int32, masked.shape, 0)
    cur = masked
    sel_f = jnp.zeros(masked.shape, F32)
    for _ in range(TOP_K):
        best = jnp.max(cur, axis=0, keepdims=True)
        first_e = jnp.min(jnp.where(cur == best, ei, ne), axis=0, keepdims=True)
        pick = ei == first_e
        sel_f = jnp.where(pick, 1.0, sel_f)
        cur = jnp.where(pick, -jnp.inf, cur)
    sel = sel_f > 0.0
    w = jnp.where(sel, s, 0.0)
    w = w / jnp.sum(w, axis=0, keepdims=True) * ROUTED_SCALE

    self = jnp.where(sel, 1.0, 0.0)
    er = lax.broadcasted_iota(jnp.int32, (ne, ne), 0)
    ec = lax.broadcasted_iota(jnp.int32, (ne, ne), 1)
    slot = _mm(jnp.where(ec < er, 1.0, 0.0), self)
    tr = lax.broadcasted_iota(jnp.int32, (tm, tm), 0)
    tc = lax.broadcasted_iota(jnp.int32, (tm, tm), 1)
    pos = _mm(self, jnp.where(tr < tc, 1.0, 0.0))
    seg = jnp.sum(slot, axis=1, keepdims=True)
    lrow = seg + pos
    lidx_rows, w_rows = [], []
    for k in range(TOP_K):
        pick = sel & (slot == float(k))
        lidx_rows.append(jnp.sum(jnp.where(pick, lrow, 0.0), axis=0, keepdims=True))
        w_rows.append(jnp.sum(jnp.where(pick, w, 0.0), axis=0, keepdims=True))
    lidx_ref[...] = jnp.concatenate(lidx_rows, axis=0).astype(jnp.int32)
    wts_ref[...] = jnp.concatenate(w_rows, axis=0)
    ones = jnp.ones((SUB, tm), F32)
    padz = jnp.zeros((LANES - ne, tm), F32)
    cnt_ref[...] = _mm_nt(ones, jnp.concatenate([self, padz], axis=0)).astype(jnp.int32)
    seg_ref[...] = _mm_nt(ones, jnp.concatenate([slot, padz], axis=0)).astype(jnp.int32)


def _router(logits, b_router, ne):
    t, lanes = logits.shape
    tm = ROUTE_TILE
    nt = t // tm
    kspec = pl.BlockSpec((TOP_K, tm), lambda i: (0, i))
    cspec = pl.BlockSpec((SUB, LANES), lambda i: (i, 0))
    return pl.pallas_call(
        functools.partial(_router_kernel, ne=ne),
        grid=(nt,),
        in_specs=[pl.BlockSpec((tm, lanes), lambda i: (i, 0)), pl.BlockSpec((ne, 1), lambda i: (0, 0))],
        out_specs=[kspec, kspec, cspec, cspec],
        out_shape=[jax.ShapeDtypeStruct((TOP_K, t), jnp.int32), jax.ShapeDtypeStruct((TOP_K, t), F32),
                   jax.ShapeDtypeStruct((nt * SUB, LANES), jnp.int32),
                   jax.ShapeDtypeStruct((nt * SUB, LANES), jnp.int32)],
        compiler_params=_cparams(("arbitrary",)),
        name="router",
    )(logits, b_router.reshape(ne, 1))


SEG_BITS = ROUTE_TILE.bit_length()
PAD_BITS = (MOE_TILE - 1).bit_length()


RARE_BIT = 7


def _for_each_piece(n, fn, bits):
    def pieces(lo, hi):
        for bit in range(hi - 1, lo - 1, -1):
            @pl.when(((n >> bit) & 1) == 1)
            def _():
                fn((n >> (bit + 1)) << (bit + 1), 1 << bit)

    if bits > RARE_BIT:
        @pl.when(n >= (1 << RARE_BIT))
        def _():
            pieces(RARE_BIT, bits)
    pieces(0, min(bits, RARE_BIT))


SORT_UNROLL = 4


def _zero_rows(ps_ref, pn_ref, tail_ref, xs_ref, z_ref, sem, ne, act):
    def pad_body(e, carry):
        _for_each_piece(pn_ref[e], lambda off, size: act(
            pltpu.make_async_copy(z_ref.at[pl.ds(0, size)], xs_ref.at[pl.ds(ps_ref[e] + off, size)], sem)),
            PAD_BITS)
        return carry
    lax.fori_loop(0, ne, pad_body, 0)

    def tail_body(b, carry):
        act(pltpu.make_async_copy(z_ref, xs_ref.at[pl.ds(tail_ref[0] + b * MOE_TILE, MOE_TILE)], sem))
        return carry
    lax.fori_loop(0, tail_ref[1], tail_body, 0)


def _dispatch_kernel(cnt_ref, seg_ref, gb_ref, ps_ref, pn_ref, tail_ref, lidx_p_ref, lidx_s_ref, hf3_p_ref, hf3_s_ref,
                     xs_ref, buf0, buf1, z_ref, sems, *, ntp, ne):
    i = pl.program_id(0)
    n = pl.num_programs(0)
    tt = lidx_p_ref.shape[0] // TOP_K

    def whole(buf, sem):
        return pltpu.make_async_copy(buf, xs_ref.at[pl.ds(0, tt * TOP_K)], sem)

    @pl.when(i == 0)
    def _():
        z_ref[...] = jnp.zeros(z_ref.shape, F32)
        _zero_rows(ps_ref, pn_ref, tail_ref, xs_ref, z_ref, sems.at[2], ne, lambda c: c.start())

    def run(buf, sem):
        @pl.when(i >= 2)
        def _():
            whole(buf, sem).wait()

        def sort_from(hf3_ref, lidx_ref):
            def sort_body(tb, carry):
                for u in range(SORT_UNROLL):
                    t = tb * SORT_UNROLL + u
                    v = hf3_ref[pl.ds(pl.multiple_of(t * SUB, SUB), SUB), :]
                    for k in range(TOP_K):
                        buf[lidx_ref[t * TOP_K + k]] = v
                return carry
            lax.fori_loop(0, tt // SORT_UNROLL, sort_body, 0)

        @pl.when(i < ntp)
        def _():
            sort_from(hf3_p_ref, lidx_p_ref)

        @pl.when(i >= ntp)
        def _():
            sort_from(hf3_s_ref, lidx_s_ref)

        def seg_body(e, carry):
            c, s, g = cnt_ref[i * ne + e], seg_ref[i * ne + e], gb_ref[i * ne + e]
            _for_each_piece(c, lambda off, size: pltpu.make_async_copy(
                buf.at[pl.ds(s + off, size)], xs_ref.at[pl.ds(g + off, size)], sem).start(), SEG_BITS)
            return carry
        lax.fori_loop(0, ne, seg_body, 0)

    @pl.when(i % 2 == 0)
    def _():
        run(buf0, sems.at[0])

    @pl.when(i % 2 == 1)
    def _():
        run(buf1, sems.at[1])

    @pl.when(i == 0)
    def _():
        _zero_rows(ps_ref, pn_ref, tail_ref, xs_ref, z_ref, sems.at[2], ne, lambda c: c.wait())

    @pl.when(i == n - 1)
    def _():
        @pl.when(i % 2 == 0)
        def _():
            whole(buf0, sems.at[0]).wait()

            @pl.when(i >= 1)
            def _():
                whole(buf1, sems.at[1]).wait()

        @pl.when(i % 2 == 1)
        def _():
            whole(buf1, sems.at[1]).wait()
            whole(buf0, sems.at[0]).wait()


def _dispatch(cnt, seg, gbase, pad_start, pad_n, tail, lidx_p, lidx_s, hf3_p, hf3_s, rows, ne):
    tt = ROUTE_TILE
    ntp, nts = lidx_p.shape[0] // (tt * TOP_K), lidx_s.shape[0] // (tt * TOP_K)
    pmap = lambda i, *_: jnp.minimum(i, ntp - 1)
    smap = lambda i, *_: jnp.maximum(i - ntp, 0)
    return pl.pallas_call(
        functools.partial(_dispatch_kernel, ntp=ntp, ne=ne),
        grid_spec=pltpu.PrefetchScalarGridSpec(
            num_scalar_prefetch=6, grid=(ntp + nts,),
            in_specs=[pl.BlockSpec((tt * TOP_K,), lambda i, *_: (pmap(i),), memory_space=pltpu.SMEM),
                      pl.BlockSpec((tt * TOP_K,), lambda i, *_: (smap(i),), memory_space=pltpu.SMEM),
                      pl.BlockSpec((tt * SUB, LANES), lambda i, *_: (pmap(i), 0)),
                      pl.BlockSpec((tt * SUB, LANES), lambda i, *_: (smap(i), 0))],
            out_specs=pl.BlockSpec(memory_space=pl.ANY),
            scratch_shapes=[pltpu.VMEM((tt * TOP_K, SUB, LANES), F32), pltpu.VMEM((tt * TOP_K, SUB, LANES), F32),
                            pltpu.VMEM((MOE_TILE, SUB, LANES), F32), pltpu.SemaphoreType.DMA((3,))]),
        out_shape=jax.ShapeDtypeStruct((rows, SUB, LANES), F32),
        compiler_params=_cparams(("arbitrary",)),
        name="dispatch",
    )(cnt, seg, gbase, pad_start, pad_n, tail, lidx_p, lidx_s, hf3_p, hf3_s)


FFN_RING = 3


def _ffn_kernel(be_ref, nbu_ref, xs_ref, wg_ref, wu_ref, wd_ref, ys_ref, xbuf, wg_s, wu_s, wd_s, sems):
    b = pl.program_id(0)
    nbu = nbu_ref[0]
    tm = ys_ref.shape[0] // SUB
    blk = tm * SUB

    def fetch(k):
        slot = k % FFN_RING
        start = k * blk if isinstance(k, int) else pl.multiple_of(k * blk, blk)
        return pltpu.make_async_copy(xs_ref.at[pl.ds(start, blk)], xbuf.at[slot], sems.at[slot])

    @pl.when(b == 0)
    def _():
        for k in range(FFN_RING - 1):
            @pl.when(k < nbu)
            def _():
                fetch(k).start()

    @pl.when(b + FFN_RING - 1 < nbu)
    def _():
        fetch(b + FFN_RING - 1).start()

    @pl.when(b < nbu)
    def _():
        @pl.when((b == 0) | (be_ref[b] != be_ref[jnp.maximum(b - 1, 0)]))
        def _():
            wg_s[...] = wg_ref[0].astype(MXU_DTYPE)
            wu_s[...] = wu_ref[0].astype(MXU_DTYPE)
            wd_s[...] = wd_ref[0].astype(MXU_DTYPE)

        fetch(b).wait()
        x = _from_row_tiles(xbuf.at[b % FFN_RING], tm)
        hb = _silu(_mm(x, wg_s[...])) * _mm(x, wu_s[...])
        _to_row_tiles(ys_ref, _mm(hb, wd_s[...]))

    @pl.when(b >= nbu_ref[0])
    def _():
        ys_ref[...] = jnp.zeros(ys_ref.shape, F32)


def _ffn(block_e, nb_used, xs2, wg, wu, wd):
    rows = xs2.shape[0] // SUB
    d, f = wg.shape[1], wg.shape[2]
    tm = MOE_TILE
    nb = rows // tm
    wmap = lambda b, be, nbu: (be[b], 0, 0)
    return pl.pallas_call(
        _ffn_kernel,
        grid_spec=pltpu.PrefetchScalarGridSpec(
            num_scalar_prefetch=2,
            grid=(nb,),
            in_specs=[pl.BlockSpec(memory_space=pl.ANY), pl.BlockSpec((1, d, f), wmap),
                      pl.BlockSpec((1, d, f), wmap), pl.BlockSpec((1, f, d), wmap)],
            out_specs=pl.BlockSpec((tm * SUB, LANES), lambda b, be, nbu: (b, 0)),
            scratch_shapes=[pltpu.VMEM((FFN_RING, tm * SUB, LANES), F32),
                            pltpu.VMEM((d, f), MXU_DTYPE), pltpu.VMEM((d, f), MXU_DTYPE),
                            pltpu.VMEM((f, d), MXU_DTYPE), pltpu.SemaphoreType.DMA((FFN_RING,))]),
        out_shape=jax.ShapeDtypeStruct((rows * SUB, LANES), F32),
        compiler_params=_cparams(("arbitrary",)),
        name="ffn",
    )(block_e, nb_used, xs2, wg, wu, wd)


def _combine_kernel(cnt_ref, seg_ref, gb_ref, lidx_ref, wts_ref, hf3_ref, x1_ref, gt2_ref, gpost_ref,
                    wsg_ref, wsu_ref, wsd_ref, ys_ref, y_ref, buf0, buf1, y3_ref, sems, *, tile0, ne):
    bb, ll, d = x1_ref.shape
    tt = bb * ll
    i = pl.program_id(0)
    n = pl.num_programs(0)

    def fetch(j, buf, sem):
        def seg_body(e, carry):
            c, s, g = cnt_ref[j * ne + e], seg_ref[j * ne + e], gb_ref[j * ne + e]
            _for_each_piece(c, lambda off, size: pltpu.make_async_copy(
                ys_ref.at[pl.ds(g + off, size)], buf.at[pl.ds(s + off, size)], sem).start(), SEG_BITS)
            return carry
        lax.fori_loop(0, ne, seg_body, 0)

    def run(buf, sem, nbuf, nsem):
        @pl.when(i == 0)
        def _():
            fetch(tile0, buf, sem)

        @pl.when(i + 1 < n)
        def _():
            fetch(tile0 + i + 1, nbuf, nsem)

        pltpu.make_async_copy(ys_ref.at[pl.ds(0, tt * TOP_K)], buf, sem).wait()

        def tok_body(tb, carry):
            for u in range(SORT_UNROLL):
                t = tb * SORT_UNROLL + u
                acc = wts_ref[t * TOP_K] * buf[lidx_ref[t * TOP_K]]
                for k in range(1, TOP_K):
                    acc = acc + wts_ref[t * TOP_K + k] * buf[lidx_ref[t * TOP_K + k]]
                y3_ref[pl.ds(pl.multiple_of(t * SUB, SUB), SUB), :] = acc
            return carry
        lax.fori_loop(0, tt // SORT_UNROLL, tok_body, 0)

    @pl.when(i % 2 == 0)
    def _():
        run(buf0, sems.at[0], buf1, sems.at[1])

    @pl.when(i % 2 == 1)
    def _():
        run(buf1, sems.at[1], buf0, sems.at[0])

    hf = _from_row_tiles(hf3_ref, tt)
    shared = _mm(_silu(_mm(hf, wsg_ref[...])) * _mm(hf, wsu_ref[...]), wsd_ref[...])
    yf = (_from_row_tiles(y3_ref, tt) + shared).reshape(bb, ll, d)
    y_ref[...] = x1_ref[...] + gt2_ref[...] * _rms(yf, gpost_ref[...])


def _combine(cnt, seg, gbase, lidx, wts, hf3, x1, gt2, gpost, wsg, wsu, wsd, ys, tile0, ne):
    b, l, d = x1.shape
    tt = ROUTE_TILE
    ll = min(l, tt)
    bb = tt // ll
    nl = l // ll
    nt = (b // bb) * nl
    fs = wsg.shape[1]
    if nl == 1:
        tmap = lambda i, *_: (i, 0, 0)
        bmap = lambda i, *_: (i, 0, 0)
    else:
        tmap = lambda i, *_: (i // nl, i % nl, 0)
        bmap = lambda i, *_: (i // nl, 0, 0)
    wmap = lambda i, *_: (0, 0)
    kspec = lambda: pl.BlockSpec((tt * TOP_K,), lambda i, *_: (i,), memory_space=pltpu.SMEM)
    return pl.pallas_call(
        functools.partial(_combine_kernel, tile0=tile0, ne=ne),
        grid_spec=pltpu.PrefetchScalarGridSpec(
            num_scalar_prefetch=3, grid=(nt,),
            in_specs=[kspec(), kspec(),
                      pl.BlockSpec((tt * SUB, LANES), lambda i, *_: (i, 0)),
                      pl.BlockSpec((bb, ll, d), tmap), pl.BlockSpec((bb, 1, d), bmap),
                      pl.BlockSpec((1, d), wmap), pl.BlockSpec((d, fs), wmap), pl.BlockSpec((d, fs), wmap),
                      pl.BlockSpec((fs, d), wmap),
                      pl.BlockSpec(memory_space=pl.ANY)],
            out_specs=pl.BlockSpec((bb, ll, d), tmap),
            scratch_shapes=[pltpu.VMEM((tt * TOP_K, SUB, LANES), F32), pltpu.VMEM((tt * TOP_K, SUB, LANES), F32),
                            pltpu.VMEM((tt * SUB, LANES), F32), pltpu.SemaphoreType.DMA((2,))]),
        out_shape=jax.ShapeDtypeStruct((b, l, d), F32),
        compiler_params=_cparams(("arbitrary",), VMEM_LIMIT_COMBINE),
        name="combine",
    )(cnt, seg, gbase, lidx, wts, hf3, x1, gt2, gpost, wsg, wsu, wsd, ys)


def _rope_tables(pos, rows):
    half = DH // 2
    freqs = ROPE_BASE ** (-jnp.arange(half, dtype=F32) / half)
    ang = pos.astype(F32)[:, None] * freqs[None, :]
    cos, sin = jnp.cos(ang), jnp.sin(ang)
    cos2 = jnp.concatenate([cos, cos], axis=1)
    sin2 = jnp.concatenate([-sin, sin], axis=1)
    padr = rows - pos.shape[0]
    return jnp.pad(cos2, ((0, padr), (0, 0))), jnp.pad(sin2, ((0, padr), (0, 0)))


def _global_plan(cnt_tiles, n_assign, ne):
    counts = jnp.sum(cnt_tiles, axis=0)
    padded = (counts + MOE_TILE - 1) // MOE_TILE * MOE_TILE
    pend = jnp.cumsum(padded)
    base = pend - padded
    gbase = base[None, :] + jnp.cumsum(cnt_tiles, axis=0) - cnt_tiles
    nb = n_assign // MOE_TILE + ne
    starts = jnp.arange(nb, dtype=jnp.int32) * MOE_TILE
    block_e = jnp.minimum(jnp.sum((pend[None, :] <= starts[:, None]).astype(jnp.int32), axis=1), ne - 1)
    nb_used = pend[-1:] // MOE_TILE
    tail = jnp.concatenate([pend[-1:], nb - nb_used])
    i32 = lambda v: v.astype(jnp.int32)
    return i32(gbase.reshape(-1)), i32(block_e), i32(nb_used), i32(base + counts), i32(padded - counts), i32(tail), nb


def kernel(x_prompt, x_sample, c_prompt, c_sample, state_mlstm_C, state_mlstm_n, state_mlstm_m, state_mlstm_conv, state_ret_S, w_ada, b_ada, g_mix_pre, g_mix_post, g_ffn_pre, g_ffn_post, w_in, b_gates, conv_w, conv_b, m_gn, r_gn, w_out, w_router, b_router, w_exp_gate, w_exp_up, w_exp_down, w_sh_gate, w_sh_up, w_sh_down):
    depth = w_ada.shape[0]
    bp, lp, d = x_prompt.shape
    bs, ls, _ = x_sample.shape
    mh, rh = state_mlstm_C.shape[2], state_ret_S.shape[2]
    mw, rw = mh * DH, rh * DH
    ne = w_router.shape[2]
    tp, ts = bp * lp, bs * ls
    ntp = tp // ROUTE_TILE

    cos_p, sin_p = _rope_tables(jnp.arange(lp, dtype=jnp.int32), lp)
    cos_s, sin_s = _rope_tables(PAST_LEN + jnp.arange(ls, dtype=jnp.int32), CHUNK if ls % CHUNK else ls)

    hp, hs = x_prompt, x_sample
    new_p, new_s = [], []
    for l in range(depth):
        wi = w_in[l]
        g0, nmain = 4 * mw, 4 * mw + 4 * rw
        w_in_r = jnp.zeros((d, nmain + GATE_LANES), MXU_DTYPE)
        w_in_r = w_in_r.at[:, :g0].set(wi[:, :g0].astype(MXU_DTYPE))
        w_in_r = w_in_r.at[:, g0:nmain].set(wi[:, g0 + 2 * mh:].astype(MXU_DTYPE))
        w_in_r = w_in_r.at[:, nmain:nmain + 2 * mh].set(wi[:, g0:g0 + 2 * mh].astype(MXU_DTYPE))
        bg = jnp.pad(b_gates[l], (0, GATE_LANES - 2 * mh)).reshape(1, GATE_LANES)
        wout = w_out[l].astype(MXU_DTYPE)
        wr = jnp.pad(w_router[l], ((0, 0), (0, GATE_LANES - ne)))
        wr_hi, wr_lo = _split_hi_lo(wr)
        wsg, wsu, wsd = (w_sh_gate[l].astype(MXU_DTYPE), w_sh_up[l].astype(MXU_DTYPE), w_sh_down[l].astype(MXU_DTYPE))
        row = lambda v: v.reshape(1, -1)

        mod = _ada(jnp.concatenate([c_prompt, c_sample], axis=0), w_ada[l], b_ada[l])
        mods_p = [m[:bp].reshape(bp, 1, d) for m in jnp.split(mod, 6, axis=1)]
        mods_s = [m[bp:].reshape(bs, 1, d) for m in jnp.split(mod, 6, axis=1)]

        def lane_m(m):
            return jnp.pad(m, ((0, 0), (mh, GATE_LANES - 2 * mh)))[:, None, :]

        init = (jnp.zeros((bp, mh, DH, DH), F32), jnp.zeros((bp, mh, DH), F32), lane_m(jnp.zeros((bp, mh), F32)),
                jnp.zeros((bp, CONV_W - 1, 2 * mw), F32), jnp.zeros((bp, rh, DH, DH), F32))
        past = (state_mlstm_C[l], state_mlstm_n[l], lane_m(state_mlstm_m[l]), state_mlstm_conv[l], state_ret_S[l])

        def mix(x, mods, state, cos2, sin2):
            sh1, sc1, gt1, sh2, sc2, gt2 = mods
            b, ll, _ = x.shape
            if ll % MIX_ROWS == 0:
                x1, hf3, lg, c_n, n_n, m_n, conv_n, s_n = _mixer(
                    x, mods, cos2, sin2, *state, row(g_mix_pre[l]), w_in_r, conv_w[l], row(conv_b[l]), bg,
                    row(m_gn[l]), row(r_gn[l]), row(g_mix_post[l]), row(g_ffn_pre[l]), wout, wr_hi, wr_lo)
            else:
                z = _inproj(x, sc1, sh1, row(g_mix_pre[l]), w_in_r).reshape(b, ll, -1)
                o_mix, c_n, n_n, m_n, conv_n, s_n = _mixcore_packed(
                    z, cos2, sin2, *state, conv_w[l], row(conv_b[l]), bg, row(m_gn[l]), row(r_gn[l]))
                x1, hf3, lg = _mixout(o_mix, x, gt1, sc2, sh2, row(g_mix_post[l]), row(g_ffn_pre[l]), wout, wr_hi,
                                      wr_lo)
            lidx, wts, cnt, seg = _router(lg, b_router[l], ne)
            flat = lambda v: v.T.reshape(-1)
            return (x1, hf3, flat(lidx), flat(wts), cnt[::SUB, :ne], seg[::SUB, :ne],
                    (c_n, n_n, m_n[:, 0, mh:2 * mh], conv_n, s_n))

        x1_p, hf3_p, lidx_p, wts_p, cnt_p, seg_p, st_p = mix(hp, mods_p, init, cos_p, sin_p)
        x1_s, hf3_s, lidx_s, wts_s, cnt_s, seg_s, st_s = mix(hs, mods_s, past, cos_s, sin_s)

        cnt_t = jnp.concatenate([cnt_p, cnt_s], axis=0)
        cnt = cnt_t.reshape(-1)
        seg = jnp.concatenate([seg_p, seg_s], axis=0).reshape(-1)
        gbase, block_e, nb_used, pad_start, pad_n, tail, nb = _global_plan(cnt_t, (tp + ts) * TOP_K, ne)
        xs = _dispatch(cnt, seg, gbase, pad_start, pad_n, tail, lidx_p, lidx_s, hf3_p, hf3_s, nb * MOE_TILE, ne)
        ys = _ffn(block_e, nb_used, xs.reshape(nb * MOE_TILE * SUB, LANES), w_exp_gate[l], w_exp_up[l], w_exp_down[l])
        ys = ys.reshape(nb * MOE_TILE, SUB, LANES)
        hp = _combine(cnt, seg, gbase, lidx_p, wts_p, hf3_p, x1_p, mods_p[5], row(g_ffn_post[l]), wsg, wsu, wsd, ys, 0, ne)
        hs = _combine(cnt, seg, gbase, lidx_s, wts_s, hf3_s, x1_s, mods_s[5], row(g_ffn_post[l]), wsg, wsu, wsd, ys, ntp, ne)
        new_p.append(st_p)
        new_s.append(st_s)

    p_state = [jnp.stack([st[i] for st in new_p]) for i in range(5)]
    s_state = [jnp.stack([st[i] for st in new_s]) for i in range(5)]
    return (hp, hs, *p_state, *s_state)
```

```python
import functools
import math

import jax
import jax.numpy as jnp
from jax import lax
from jax.experimental import pallas as pl
from jax.experimental.pallas import tpu as pltpu

F32 = jnp.float32
MXU_DTYPE = jnp.bfloat16

NORM_EPS = 1e-6
ROPE_BASE = 10000.0
PAST_LEN = 16384
CHUNK = 128
CONV_W = 4
DH = 128
TOP_K = 8
N_GROUPS = 8
TOPK_GROUPS = 4
ROUTED_SCALE = 2.5
GATE_LANES = 128

V7X_VMEM_BYTES = 64 * 1024 * 1024
VMEM_LIMIT = 48 * 1024 * 1024
VMEM_LIMIT_COMBINE = 56 * 1024 * 1024
assert VMEM_LIMIT < VMEM_LIMIT_COMBINE < V7X_VMEM_BYTES

SUB, LANES = 8, 128

TOK_TILE = 256
ROUTE_TILE = 512
MOE_TILE = 512


def _cparams(sem, vmem_limit=VMEM_LIMIT):
    return pltpu.CompilerParams(dimension_semantics=sem, vmem_limit_bytes=vmem_limit)


def _silu(x):
    return x * jax.nn.sigmoid(x)


def _rms(x, g):
    return x * lax.rsqrt(jnp.mean(x * x, axis=-1, keepdims=True) + NORM_EPS) * g


def _mm(a, b):
    return jnp.dot(a.astype(MXU_DTYPE), b.astype(MXU_DTYPE), preferred_element_type=F32)


def _mm_nt(a, b):
    return lax.dot_general(a.astype(MXU_DTYPE), b.astype(MXU_DTYPE), (((1,), (1,)), ((), ())),
                           preferred_element_type=F32)


def _mm_tn(a, b):
    return lax.dot_general(a.astype(MXU_DTYPE), b.astype(MXU_DTYPE), (((0,), (0,)), ((), ())),
                           preferred_element_type=F32)


def _ada_kernel(c_ref, w_ref, b_ref, o_ref):
    o_ref[...] = _mm(_silu(c_ref[...]), w_ref[...]) + b_ref[...]


def _ada(c, w_ada, b_ada):
    n, d = c.shape
    dout = w_ada.shape[1]
    tn = 1024
    return pl.pallas_call(
        _ada_kernel,
        grid=(dout // tn,),
        in_specs=[pl.BlockSpec((n, d), lambda j: (0, 0)),
                  pl.BlockSpec((d, tn), lambda j: (0, j)),
                  pl.BlockSpec((1, tn), lambda j: (0, j))],
        out_specs=pl.BlockSpec((n, tn), lambda j: (0, j)),
        out_shape=jax.ShapeDtypeStruct((n, dout), F32),
        compiler_params=_cparams(("arbitrary",)),
        name="ada",
    )(c, w_ada, b_ada.reshape(1, dout))


def _inproj_kernel(x_ref, sc_ref, sh_ref, g_ref, w_ref, z_ref):
    x = x_ref[...]
    h = _rms(x, g_ref[...]) * (1.0 + sc_ref[...]) + sh_ref[...]
    bb, ll, d = x.shape
    z_ref[...] = _mm(h.reshape(bb * ll, d), w_ref[...])


def _inproj(x, sc, sh, g, w):
    b, l, d = x.shape
    n = w.shape[1]
    ll = min(l, TOK_TILE)
    bb = TOK_TILE // ll
    mod_spec = pl.BlockSpec((bb, 1, d), lambda i, j: (i, 0, 0))
    return pl.pallas_call(
        _inproj_kernel,
        grid=(b // bb, l // ll),
        in_specs=[pl.BlockSpec((bb, ll, d), lambda i, j: (i, j, 0)), mod_spec, mod_spec,
                  pl.BlockSpec((1, d), lambda i, j: (0, 0)),
                  pl.BlockSpec((d, n), lambda i, j: (0, 0))],
        out_specs=pl.BlockSpec((bb * ll, n), lambda i, j: (i * (l // ll) + j, 0)),
        out_shape=jax.ShapeDtypeStruct((b * l, n), F32),
        compiler_params=_cparams(("arbitrary", "arbitrary")),
        name="inproj",
    )(x, sc, sh, g, w)


def _cumsum_rows(x):
    n = x.shape[0]
    row = lax.broadcasted_iota(jnp.int32, x.shape, 0)
    s = 1
    while s < n:
        x = x + jnp.where(row >= s, pltpu.roll(x, s, axis=0), 0.0)
        s *= 2
    return x


def _head_norm(y, g):
    mu = jnp.mean(y, axis=-1, keepdims=True)
    yc = y - mu
    var = jnp.mean(yc * yc, axis=-1, keepdims=True)
    return yc * lax.rsqrt(var + NORM_EPS) * g


def _log_sigmoid(x):
    return jnp.minimum(x, 0.0) - jnp.log1p(jnp.exp(-jnp.abs(x)))


def _chunk_step(rows, put, cos2, sin2, c_ref, n_ref, m_ref, conv_ref, s_ref, xcat_ref,
                convw_ref, convb_ref, bg_ref, mgn_ref, rgn_ref, mh, rh, after_head=None):
    cl = CHUNK
    mw = mh * DH
    rw = rh * DH
    o_qk, o_mv, o_mo = 0, 2 * mw, 3 * mw
    o_rq = 4 * mw
    o_rk, o_rv, o_rg, o_gt = o_rq + rw, o_rq + 2 * rw, o_rq + 3 * rw, o_rq + 4 * rw

    xcat_ref[8:8 + cl, :] = rows(o_qk, o_qk + 2 * mw)
    qk = convb_ref[...]
    for t in range(CONV_W):
        qk = qk + xcat_ref[8 - (CONV_W - 1) + t:8 - (CONV_W - 1) + t + cl, :] * convw_ref[t:t + 1, :]
    qk = _silu(qk)
    new_conv = xcat_ref[8 + cl - (CONV_W - 1):8 + cl, :]
    conv_ref[0] = new_conv
    xcat_ref[8 - (CONV_W - 1):8, :] = new_conv

    row_c = lax.broadcasted_iota(jnp.int32, (cl, 1), 0)
    ii = lax.broadcasted_iota(jnp.int32, (cl, cl), 0)
    jj = lax.broadcasted_iota(jnp.int32, (cl, cl), 1)
    causal = jj <= ii
    eye = jj == ii

    def to_row(col):
        return jnp.sum(jnp.where(eye, col, 0.0), axis=0, keepdims=True)

    g = rows(o_gt, o_gt + GATE_LANES) + bg_ref[...]
    bcum = _cumsum_rows(_log_sigmoid(g))
    m_prev = m_ref[0]
    a_all = bcum + m_prev
    lane = lax.broadcasted_iota(jnp.int32, (1, GATE_LANES), 1)
    m_new_row = m_prev

    for h in range(mh):
        q = qk[:, h * DH:(h + 1) * DH]
        k = qk[:, mw + h * DH:mw + (h + 1) * DH] * (DH ** -0.5)
        v = rows(o_mv + h * DH, o_mv + (h + 1) * DH)
        b_col = bcum[:, mh + h:mh + h + 1]
        logi_col = g[:, h:h + 1]
        a_col = a_all[:, mh + h:mh + h + 1]
        dmat = jnp.where(causal, b_col + to_row(logi_col - b_col), -jnp.inf)
        m_t = jnp.maximum(a_col, jnp.max(dmat, axis=1, keepdims=True))
        w_intra = jnp.exp(dmat - m_t)
        w_inter = jnp.exp(a_col - m_t)
        c_old = c_ref[0, h]
        n_old = n_ref[0, h:h + 1, :]
        s = _mm_nt(q, k) * w_intra
        num = _mm(s, v) + w_inter * _mm(q, c_old)
        den = jnp.sum(s, axis=1, keepdims=True) + w_inter * jnp.sum(q * n_old, axis=1, keepdims=True)
        hh = num * (1.0 / jnp.maximum(jnp.abs(den), jnp.exp(-m_t)))
        m_new = m_t[cl - 1:cl, :]
        b_last = b_col[cl - 1:cl, :]
        w_old = jnp.exp(b_last + m_prev[:, mh + h:mh + h + 1] - m_new)
        w_k = jnp.exp(b_last - b_col + logi_col - m_new)
        c_ref[0, h] = w_old * c_old + _mm_tn(k, w_k * v)
        n_ref[0, h:h + 1, :] = w_old * n_old + jnp.sum(w_k * k, axis=0, keepdims=True)
        m_new_row = jnp.where(lane == mh + h, m_new, m_new_row)
        om = _head_norm(hh, mgn_ref[:, h * DH:(h + 1) * DH]) * jax.nn.sigmoid(rows(o_mo + h * DH, o_mo + (h + 1) * DH))
        put(h * DH, (h + 1) * DH, om)
        if after_head is not None:
            after_head(h)
    m_ref[0] = m_new_row

    rel = (ii - jj).astype(F32)
    row_f = row_c.astype(F32)

    def rot(x):
        return x * cos2 + pltpu.roll(x, DH // 2, axis=1) * sin2

    for h in range(rh):
        log_g = math.log1p(-2.0 ** (-5.0 - h))
        q = rot(rows(o_rq + h * DH, o_rq + (h + 1) * DH))
        k = rot(rows(o_rk + h * DH, o_rk + (h + 1) * DH)) * (DH ** -0.5)
        v = rows(o_rv + h * DH, o_rv + (h + 1) * DH)
        decay = jnp.where(causal, jnp.exp(jnp.maximum(rel, 0.0) * log_g), 0.0)
        s_old = s_ref[0, h]
        y = _mm(_mm_nt(q, k) * decay, v) + jnp.exp((row_f + 1.0) * log_g) * _mm(q, s_old)
        k_w = jnp.exp((cl - 1.0 - row_f) * log_g)
        s_ref[0, h] = math.exp(cl * log_g) * s_old + _mm_tn(k, k_w * v)
        orr = _head_norm(y, rgn_ref[:, h * DH:(h + 1) * DH]) * _silu(rows(o_rg + h * DH, o_rg + (h + 1) * DH))
        put(mw + h * DH, mw + (h + 1) * DH, orr)
        if after_head is not None:
            after_head(mh + h)


MIX_ROWS = 256


def _mixer_kernel(x_ref, xn_ref, sc1_ref, sh1_ref, sc1n_ref, sh1n_ref, gt1_ref, sc2_ref, sh2_ref, cos_ref, sin_ref,
                  c0_ref, n0_ref, m0_ref, conv0_ref, s0_ref,
                  gpre1_ref, win_ref, convw_ref, convb_ref, bg_ref, mgn_ref, rgn_ref,
                  gpost_ref, gpre2_ref, wout_ref, wr_hi_ref, wr_lo_ref,
                  x1_ref, hf3_ref, lg_ref, c_ref, n_ref, m_ref, conv_ref, s_ref,
                  z_ref, o_ref, xcat_ref, *, mh, rh):
    i, j = pl.program_id(0), pl.program_id(1)
    step = i * pl.num_programs(1) + j
    cur = step % 2
    rs, d = x_ref.shape[1], x_ref.shape[2]
    nz = z_ref.shape[2]
    nchunks = rs // CHUNK
    npieces = nchunks * (mh + rh)
    cuts = [(nz // LANES * p // npieces) * LANES for p in range(npieces + 1)]

    def project(x3, sc_ref, sh_ref):
        return (_rms(x3, gpre1_ref[...]) * (1.0 + sc_ref[...]) + sh_ref[...]).reshape(rs, d)

    @pl.when(step == 0)
    def _():
        z_ref[0] = _mm(project(x_ref[...], sc1_ref, sh1_ref), win_ref[...])

    @pl.when(j == 0)
    def _():
        c_ref[...] = c0_ref[...]
        n_ref[...] = n0_ref[...]
        m_ref[...] = m0_ref[...]
        s_ref[...] = s0_ref[...]
        xcat_ref[8 - (CONV_W - 1):8, :] = conv0_ref[0]

    h_next = project(xn_ref[...], sc1n_ref, sh1n_ref).astype(MXU_DTYPE)
    for c in range(nchunks):
        r0 = c * CHUNK

        def project_piece(hd, c=c):
            lo, hi = cuts[c * (mh + rh) + hd], cuts[c * (mh + rh) + hd + 1]
            z_ref[1 - cur, :, lo:hi] = jnp.dot(h_next, win_ref[:, lo:hi], preferred_element_type=F32)

        def put(lo, hi, v, r0=r0):
            o_ref[r0:r0 + CHUNK, lo:hi] = v

        _chunk_step(lambda lo, hi, r0=r0: z_ref[cur, r0:r0 + CHUNK, lo:hi], put,
                    cos_ref[r0:r0 + CHUNK, :], sin_ref[r0:r0 + CHUNK, :],
                    c_ref, n_ref, m_ref, conv_ref, s_ref, xcat_ref,
                    convw_ref, convb_ref, bg_ref, mgn_ref, rgn_ref, mh, rh, after_head=project_piece)
    x = x_ref[...]
    _mixout_body(o_ref[...].reshape(1, rs, o_ref.shape[1]), x, gt1_ref, sc2_ref, sh2_ref, gpost_ref, gpre2_ref,
                 wout_ref, wr_hi_ref, wr_lo_ref, x1_ref, hf3_ref, lg_ref)


def _mixer(x, mods, cos2, sin2, c0, n0, m0, conv0, s0, gpre1, win, convw, convb, bg, mgn, rgn,
           gpost, gpre2, wout, wr_hi, wr_lo):
    sh1, sc1, gt1, sh2, sc2, gt2 = mods
    b, l, d = x.shape
    assert d == SUB * LANES and l % MIX_ROWS == 0
    mh, rh = c0.shape[1], s0.shape[1]
    mw, rw = mh * DH, rh * DH
    nz, ne = win.shape[1], wr_hi.shape[1]
    rs = MIX_ROWS
    assert nz % LANES == 0
    nj = l // rs
    bmap4 = lambda i, j: (i, 0, 0, 0)
    bmap3 = lambda i, j: (i, 0, 0)
    tmap = lambda i, j: (i, j, 0)
    fmap = lambda i, j: (i * nj + j, 0)
    once = lambda a: pl.BlockSpec(a.shape, lambda i, j: (0, 0), pipeline_mode=pl.Buffered(1))
    mod_spec = pl.BlockSpec((1, 1, d), bmap3)
    nxt = lambda i, j: jnp.minimum(i * nj + j + 1, b * nj - 1)
    x_next_spec = pl.BlockSpec((1, rs, d), lambda i, j: (nxt(i, j) // nj, nxt(i, j) % nj, 0))
    mod_next_spec = pl.BlockSpec((1, 1, d), lambda i, j: (nxt(i, j) // nj, 0, 0))
    rope_spec = pl.BlockSpec((rs, DH), lambda i, j: (j, 0))
    state_specs = [pl.BlockSpec((1, mh, DH, DH), bmap4), pl.BlockSpec((1, mh, DH), bmap3),
                   pl.BlockSpec((1, 1, GATE_LANES), bmap3), pl.BlockSpec((1, CONV_W - 1, 2 * mw), bmap3),
                   pl.BlockSpec((1, rh, DH, DH), bmap4)]
    state_shapes = [jax.ShapeDtypeStruct(c0.shape, F32), jax.ShapeDtypeStruct(n0.shape, F32),
                    jax.ShapeDtypeStruct(m0.shape, F32), jax.ShapeDtypeStruct(conv0.shape, F32),
                    jax.ShapeDtypeStruct(s0.shape, F32)]
    return pl.pallas_call(
        functools.partial(_mixer_kernel, mh=mh, rh=rh),
        grid=(b, nj),
        in_specs=[pl.BlockSpec((1, rs, d), tmap), x_next_spec, mod_spec, mod_spec, mod_next_spec, mod_next_spec]
                 + [mod_spec] * 3 + [rope_spec, rope_spec] + state_specs
                 + [once(a) for a in (gpre1, win, convw, convb, bg, mgn, rgn, gpost, gpre2, wout, wr_hi, wr_lo)],
        out_specs=[pl.BlockSpec((1, rs, d), tmap), pl.BlockSpec((rs * SUB, LANES), fmap),
                   pl.BlockSpec((rs, ne), fmap)] + state_specs,
        out_shape=[jax.ShapeDtypeStruct((b, l, d), F32), jax.ShapeDtypeStruct((b * l * SUB, LANES), F32),
                   jax.ShapeDtypeStruct((b * l, ne), F32)] + state_shapes,
        scratch_shapes=[pltpu.VMEM((2, rs, nz), F32), pltpu.VMEM((rs, mw + rw), F32),
                        pltpu.VMEM((CHUNK + 8, 2 * mw), F32)],
        compiler_params=_cparams(("arbitrary", "arbitrary")),
        name="mixer",
    )(x, x, sc1, sh1, sc1, sh1, gt1, sc2, sh2, cos2, sin2, c0, n0, m0, conv0, s0,
      gpre1, win, convw, convb, bg, mgn, rgn, gpost, gpre2, wout, wr_hi, wr_lo)


PACK_SEQS = CHUNK // SUB


def _mixcore_packed_kernel(z_ref, cos_ref, sin_ref, c0_ref, n0_ref, m0_ref, conv0_ref, s0_ref,
                           convw_ref, convb_ref, bg_ref, mgn_ref, rgn_ref,
                           out_ref, c_ref, n_ref, m_ref, conv_ref, s_ref, *, mh, rh):
    g_n, l_n, _ = z_ref.shape
    rr = g_n * l_n
    mw, rw = mh * DH, rh * DH
    o_qk, o_mv, o_mo = 0, 2 * mw, 3 * mw
    o_rq = 4 * mw
    o_rk, o_rv, o_rg, o_gt = o_rq + rw, o_rq + 2 * rw, o_rq + 3 * rw, o_rq + 4 * rw

    def rows(lo, hi):
        return z_ref[:, :, lo:hi].reshape(rr, hi - lo)

    def per_seq(x):
        return jnp.broadcast_to(x[:, None, :], (g_n, l_n, x.shape[-1])).reshape(rr, x.shape[-1])

    def last_of_seq(x):
        x3 = x.reshape(g_n, l_n, x.shape[-1])
        return jnp.broadcast_to(x3[:, l_n - 1:l_n, :], x3.shape).reshape(x.shape)

    def full(col):
        return jnp.broadcast_to(col, (rr, DH))

    pos_c = lax.broadcasted_iota(jnp.int32, (rr, 1), 0) % l_n
    ii = lax.broadcasted_iota(jnp.int32, (rr, rr), 0)
    jj = lax.broadcasted_iota(jnp.int32, (rr, rr), 1)
    mask = ((ii // l_n) == (jj // l_n)) & (jj <= ii)
    eye = jj == ii

    def to_row(col):
        return jnp.sum(jnp.where(eye, col, 0.0), axis=0, keepdims=True)

    x = rows(o_qk, o_qk + 2 * mw)
    e = conv0_ref[...].reshape(rr, 2 * mw)
    qk = convb_ref[...]
    for t in range(CONV_W):
        d = CONV_W - 1 - t
        xd = x if d == 0 else jnp.where(pos_c >= d, pltpu.roll(x, d, axis=0),
                                        pltpu.roll(e, (d - (CONV_W - 1)) % rr, axis=0))
        qk = qk + xd * convw_ref[t:t + 1, :]
    qk = _silu(qk)
    conv_ref[...] = x.reshape(g_n, l_n, 2 * mw)[:, l_n - (CONV_W - 1):, :]

    g = rows(o_gt, o_gt + GATE_LANES) + bg_ref[...]
    bcum = _log_sigmoid(g)
    s_ = 1
    while s_ < l_n:
        bcum = bcum + jnp.where(pos_c >= s_, pltpu.roll(bcum, s_, axis=0), 0.0)
        s_ *= 2
    m_prev = per_seq(m0_ref[:, 0, :])
    a_all = bcum + m_prev
    lane = lax.broadcasted_iota(jnp.int32, (1, GATE_LANES), 1)
    even_seq = (lax.broadcasted_iota(jnp.int32, (rr, 1), 0) // l_n) % 2 == 0
    m_new_rows = m_prev

    def state_dot(qb, st_ref, h):
        parts = []
        for s in range(g_n):
            pb, lo = s // 2, (s % 2) * l_n
            res = jnp.dot(qb[pb * 2 * l_n:(pb + 1) * 2 * l_n], st_ref[s, h].astype(MXU_DTYPE),
                          preferred_element_type=F32)
            parts.append(res[lo:lo + l_n])
        return jnp.concatenate(parts, axis=0)

    def state_update(k, wv, decay_full, st_in, st_out, h):
        k_even = jnp.where(even_seq, k, 0.0).astype(MXU_DTYPE)
        k_odd = jnp.where(even_seq, 0.0, k).astype(MXU_DTYPE)
        wvb = wv.astype(MXU_DTYPE)
        for s in range(g_n):
            pb = s // 2
            kp = (k_even if s % 2 == 0 else k_odd)[pb * 2 * l_n:(pb + 1) * 2 * l_n]
            upd = lax.dot_general(kp, wvb[pb * 2 * l_n:(pb + 1) * 2 * l_n], (((0,), (0,)), ((), ())),
                                  preferred_element_type=F32)
            st_out[s, h] = decay_full[s * l_n:s * l_n + 1, :] * st_in[s, h] + upd

    for h in range(mh):
        q = qk[:, h * DH:(h + 1) * DH]
        k = qk[:, mw + h * DH:mw + (h + 1) * DH] * (DH ** -0.5)
        v = rows(o_mv + h * DH, o_mv + (h + 1) * DH)
        b_col = bcum[:, mh + h:mh + h + 1]
        logi_col = g[:, h:h + 1]
        a_col = a_all[:, mh + h:mh + h + 1]
        dmat = jnp.where(mask, b_col + to_row(logi_col - b_col), -jnp.inf)
        m_t = jnp.maximum(a_col, jnp.max(dmat, axis=1, keepdims=True))
        w_intra = jnp.exp(dmat - m_t)
        w_inter = jnp.exp(a_col - m_t)
        n_rows = per_seq(n0_ref[:, h, :])
        s = _mm_nt(q, k) * w_intra
        num = _mm(s, v) + w_inter * state_dot(q.astype(MXU_DTYPE), c0_ref, h)
        den = jnp.sum(s, axis=1, keepdims=True) + w_inter * jnp.sum(q * n_rows, axis=1, keepdims=True)
        hh = num * (1.0 / jnp.maximum(jnp.abs(den), jnp.exp(-m_t)))
        m_new = last_of_seq(full(m_t))
        b_last = last_of_seq(full(b_col))
        w_old = jnp.exp(b_last + full(m_prev[:, mh + h:mh + h + 1]) - m_new)
        w_k = jnp.exp(b_last - full(b_col) + full(logi_col) - m_new)
        state_update(k, w_k * v, w_old, c0_ref, c_ref, h)
        n_ref[:, h, :] = (w_old.reshape(g_n, l_n, DH)[:, 0, :] * n0_ref[:, h, :]
                          + jnp.sum((w_k * k).reshape(g_n, l_n, DH), axis=1))
        m_new_rows = jnp.where(lane == mh + h, m_new, m_new_rows)
        om = _head_norm(hh, mgn_ref[:, h * DH:(h + 1) * DH]) * jax.nn.sigmoid(rows(o_mo + h * DH, o_mo + (h + 1) * DH))
        out_ref[:, :, h * DH:(h + 1) * DH] = om.reshape(g_n, l_n, DH)
    m_ref[...] = m_new_rows.reshape(g_n, l_n, GATE_LANES)[:, 0:1, :]

    cos2 = cos_ref[...]
    sin2 = sin_ref[...]
    rel = (ii - jj).astype(F32)
    pos_f = pos_c.astype(F32)

    def rot(x):
        return x * cos2 + pltpu.roll(x, DH // 2, axis=1) * sin2

    for h in range(rh):
        log_g = math.log1p(-2.0 ** (-5.0 - h))
        q = rot(rows(o_rq + h * DH, o_rq + (h + 1) * DH))
        k = rot(rows(o_rk + h * DH, o_rk + (h + 1) * DH)) * (DH ** -0.5)
        v = rows(o_rv + h * DH, o_rv + (h + 1) * DH)
        decay = jnp.where(mask, jnp.exp(jnp.maximum(rel, 0.0) * log_g), 0.0)
        y = (_mm(_mm_nt(q, k) * decay, v)
             + jnp.exp((pos_f + 1.0) * log_g) * state_dot(q.astype(MXU_DTYPE), s0_ref, h))
        k_w = jnp.exp((l_n - 1.0 - pos_f) * log_g)
        state_update(k, k_w * v, jnp.full((rr, DH), math.exp(l_n * log_g), F32), s0_ref, s_ref, h)
        orr = _head_norm(y, rgn_ref[:, h * DH:(h + 1) * DH]) * _silu(rows(o_rg + h * DH, o_rg + (h + 1) * DH))
        out_ref[:, :, mw + h * DH:mw + (h + 1) * DH] = orr.reshape(g_n, l_n, DH)


def _mixcore_packed(z, cos2, sin2, c0, n0, m0, conv0, s0, convw, convb, bg, mgn, rgn):
    b, l, nz = z.shape
    assert l == SUB and b % PACK_SEQS == 0
    mh, rh = c0.shape[1], s0.shape[1]
    mw, rw = mh * DH, rh * DH
    g_n = PACK_SEQS
    conv0p = jnp.pad(conv0, ((0, 0), (0, l - (CONV_W - 1)), (0, 0)))
    tile = lambda t: jnp.tile(t[:l], (g_n, 1))
    bmap4 = lambda i: (i, 0, 0, 0)
    bmap3 = lambda i: (i, 0, 0)
    wmap = lambda i: (0, 0)
    c_spec, n_spec = pl.BlockSpec((g_n, mh, DH, DH), bmap4), pl.BlockSpec((g_n, mh, DH), bmap3)
    m_spec, s_spec = pl.BlockSpec((g_n, 1, GATE_LANES), bmap3), pl.BlockSpec((g_n, rh, DH, DH), bmap4)
    return pl.pallas_call(
        functools.partial(_mixcore_packed_kernel, mh=mh, rh=rh),
        grid=(b // g_n,),
        in_specs=[pl.BlockSpec((g_n, l, nz), bmap3),
                  pl.BlockSpec((g_n * l, DH), wmap), pl.BlockSpec((g_n * l, DH), wmap),
                  c_spec, n_spec, m_spec, pl.BlockSpec((g_n, l, 2 * mw), bmap3), s_spec,
                  pl.BlockSpec(convw.shape, wmap), pl.BlockSpec(convb.shape, wmap), pl.BlockSpec(bg.shape, wmap),
                  pl.BlockSpec(mgn.shape, wmap), pl.BlockSpec(rgn.shape, wmap)],
        out_specs=[pl.BlockSpec((g_n, l, mw + rw), bmap3), c_spec, n_spec, m_spec,
                   pl.BlockSpec((g_n, CONV_W - 1, 2 * mw), bmap3), s_spec],
        out_shape=[jax.ShapeDtypeStruct((b, l, mw + rw), F32), jax.ShapeDtypeStruct(c0.shape, F32),
                   jax.ShapeDtypeStruct(n0.shape, F32), jax.ShapeDtypeStruct(m0.shape, F32),
                   jax.ShapeDtypeStruct(conv0.shape, F32), jax.ShapeDtypeStruct(s0.shape, F32)],
        compiler_params=_cparams(("arbitrary",)),
        name="mixcore_packed",
    )(z, tile(cos2), tile(sin2), c0, n0, m0, conv0p, s0, convw, convb, bg, mgn, rgn)


def _split_hi_lo(x):
    hi = x.astype(MXU_DTYPE)
    lo = (x - hi.astype(F32)).astype(MXU_DTYPE)
    return hi, lo


def _to_row_tiles(ref, x):
    n = x.shape[0]
    for j in range(SUB):
        ref[pl.ds(j, n, stride=SUB), :] = x[:, j * LANES:(j + 1) * LANES]


def _from_row_tiles(ref, n):
    return jnp.concatenate([ref[pl.ds(j, n, stride=SUB), :] for j in range(SUB)], axis=1)


def _mixout_kernel(o_ref, x_ref, gt1_ref, sc2_ref, sh2_ref, gpost_ref, gpre_ref, wout_ref, wr_hi_ref, wr_lo_ref,
                   x1_ref, hf3_ref, lg_ref):
    _mixout_body(o_ref[...], x_ref[...], gt1_ref, sc2_ref, sh2_ref, gpost_ref, gpre_ref, wout_ref, wr_hi_ref,
                 wr_lo_ref, x1_ref, hf3_ref, lg_ref)


def _mixout_body(o, x, gt1_ref, sc2_ref, sh2_ref, gpost_ref, gpre_ref, wout_ref, wr_hi_ref, wr_lo_ref,
                 x1_ref, hf3_ref, lg_ref):
    bb, ll, d = x.shape
    ym = _mm(o.reshape(bb * ll, o.shape[-1]), wout_ref[...]).reshape(bb, ll, d)
    x1 = x + gt1_ref[...] * _rms(ym, gpost_ref[...])
    x1_ref[...] = x1
    hf = (_rms(x1, gpre_ref[...]) * (1.0 + sc2_ref[...]) + sh2_ref[...]).reshape(bb * ll, d)
    _to_row_tiles(hf3_ref, hf)
    hi, lo = _split_hi_lo(hf)
    lg_ref[...] = _mm(hi, wr_hi_ref[...]) + (_mm(lo, wr_hi_ref[...]) + _mm(hi, wr_lo_ref[...]))


def _mixout(o, x, gt1, sc2, sh2, gpost, gpre, wout, wr_hi, wr_lo):
    b, l, d = x.shape
    assert d == SUB * LANES
    w = o.shape[-1]
    ne = wr_hi.shape[1]
    ll = min(l, TOK_TILE)
    bb = TOK_TILE // ll
    tmap = lambda i, j: (i, j, 0)
    fmap = lambda i, j: (i * (l // ll) + j, 0)
    mod_spec = pl.BlockSpec((bb, 1, d), lambda i, j: (i, 0, 0))
    wmap = lambda i, j: (0, 0)
    return pl.pallas_call(
        _mixout_kernel,
        grid=(b // bb, l // ll),
        in_specs=[pl.BlockSpec((bb, ll, w), tmap), pl.BlockSpec((bb, ll, d), tmap), mod_spec, mod_spec, mod_spec,
                  pl.BlockSpec((1, d), wmap), pl.BlockSpec((1, d), wmap), pl.BlockSpec((w, d), wmap),
                  pl.BlockSpec((d, ne), wmap), pl.BlockSpec((d, ne), wmap)],
        out_specs=[pl.BlockSpec((bb, ll, d), tmap), pl.BlockSpec((bb * ll * SUB, LANES), fmap),
                   pl.BlockSpec((bb * ll, ne), fmap)],
        out_shape=[jax.ShapeDtypeStruct((b, l, d), F32), jax.ShapeDtypeStruct((b * l * SUB, LANES), F32),
                   jax.ShapeDtypeStruct((b * l, ne), F32)],
        compiler_params=_cparams(("arbitrary", "arbitrary")),
        name="mixout",
    )(o, x, gt1, sc2, sh2, gpost, gpre, wout, wr_hi, wr_lo)


def _router_kernel(lg_ref, br_ref, lidx_ref, wts_ref, cnt_ref, seg_ref, *, ne):
    tm = lg_ref.shape[0]
    gsz = ne // N_GROUPS
    s = jax.nn.sigmoid(lg_ref[...].T[:ne, :])
    sb = s + br_ref[...]
    sb3 = sb.reshape(N_GROUPS, gsz, tm)
    e3 = lax.broadcasted_iota(jnp.int32, sb3.shape, 1)
    m1 = jnp.max(sb3, axis=1, keepdims=True)
    first = jnp.min(jnp.where(sb3 == m1, e3, gsz), axis=1, keepdims=True)
    m2 = jnp.max(jnp.where(e3 == first, -jnp.inf, sb3), axis=1, keepdims=True)
    gs = (m1 + m2).reshape(N_GROUPS, tm)
    gi = lax.broadcasted_iota(jnp.int32, gs.shape, 0)
    grank = jnp.zeros(gs.shape, F32)
    for g in range(N_GROUPS):
        r = gs[g:g + 1, :]
        grank = grank + jnp.where(r > gs, 1.0, jnp.where(r == gs, jnp.where(g < gi, 1.0, 0.0), 0.0))
    gsel = jnp.where(grank < TOPK_GROUPS, 1.0, 0.0)
    emask = jnp.broadcast_to(gsel.reshape(N_GROUPS, 1, tm), sb3.shape).reshape(ne, tm) > 0.0
    masked = jnp.where(emask, sb, -jnp.inf)
    ei = lax.broadcasted_iota(jnp.int32, masked.shape, 0)
    cur = masked
    sel_f = jnp.zeros(masked.shape, F32)
    for _ in range(TOP_K):
        best = jnp.max(cur, axis=0, keepdims=True)
        first_e = jnp.min(jnp.where(cur == best, ei, ne), axis=0, keepdims=True)
        pick = ei == first_e
        sel_f = jnp.where(pick, 1.0, sel_f)
        cur = jnp.where(pick, -jnp.inf, cur)
    sel = sel_f > 0.0
    w = jnp.where(sel, s, 0.0)
    w = w / jnp.sum(w, axis=0, keepdims=True) * ROUTED_SCALE

    self = jnp.where(sel, 1.0, 0.0)
    er = lax.broadcasted_iota(jnp.int32, (ne, ne), 0)
    ec = lax.broadcasted_iota(jnp.int32, (ne, ne), 1)
    slot = _mm(jnp.where(ec < er, 1.0, 0.0), self)
    tr = lax.broadcasted_iota(jnp.int32, (tm, tm), 0)
    tc = lax.broadcasted_iota(jnp.int32, (tm, tm), 1)
    pos = _mm(self, jnp.where(tr < tc, 1.0, 0.0))
    seg = jnp.sum(slot, axis=1, keepdims=True)
    lrow = seg + pos
    lidx_rows, w_rows = [], []
    for k in range(TOP_K):
        pick = sel & (slot == float(k))
        lidx_rows.append(jnp.sum(jnp.where(pick, lrow, 0.0), axis=0, keepdims=True))
        w_rows.append(jnp.sum(jnp.where(pick, w, 0.0), axis=0, keepdims=True))
    lidx_ref[...] = jnp.concatenate(lidx_rows, axis=0).astype(jnp.int32)
    wts_ref[...] = jnp.concatenate(w_rows, axis=0)
    ones = jnp.ones((SUB, tm), F32)
    padz = jnp.zeros((LANES - ne, tm), F32)
    cnt_ref[...] = _mm_nt(ones, jnp.concatenate([self, padz], axis=0)).astype(jnp.int32)
    seg_ref[...] = _mm_nt(ones, jnp.concatenate([slot, padz], axis=0)).astype(jnp.int32)


def _router(logits, b_router, ne):
    t, lanes = logits.shape
    tm = ROUTE_TILE
    nt = t // tm
    kspec = pl.BlockSpec((TOP_K, tm), lambda i: (0, i))
    cspec = pl.BlockSpec((SUB, LANES), lambda i: (i, 0))
    return pl.pallas_call(
        functools.partial(_router_kernel, ne=ne),
        grid=(nt,),
        in_specs=[pl.BlockSpec((tm, lanes), lambda i: (i, 0)), pl.BlockSpec((ne, 1), lambda i: (0, 0))],
        out_specs=[kspec, kspec, cspec, cspec],
        out_shape=[jax.ShapeDtypeStruct((TOP_K, t), jnp.int32), jax.ShapeDtypeStruct((TOP_K, t), F32),
                   jax.ShapeDtypeStruct((nt * SUB, LANES), jnp.int32),
                   jax.ShapeDtypeStruct((nt * SUB, LANES), jnp.int32)],
        compiler_params=_cparams(("arbitrary",)),
        name="router",
    )(logits, b_router.reshape(ne, 1))


SEG_BITS = ROUTE_TILE.bit_length()
PAD_BITS = (MOE_TILE - 1).bit_length()


RARE_BIT = 7


def _for_each_piece(n, fn, bits):
    def pieces(lo, hi):
        for bit in range(hi - 1, lo - 1, -1):
            @pl.when(((n >> bit) & 1) == 1)
            def _():
                fn((n >> (bit + 1)) << (bit + 1), 1 << bit)

    if bits > RARE_BIT:
        @pl.when(n >= (1 << RARE_BIT))
        def _():
            pieces(RARE_BIT, bits)
    pieces(0, min(bits, RARE_BIT))


SORT_UNROLL = 4


def _zero_rows(ps_ref, pn_ref, tail_ref, xs_ref, z_ref, sem, ne, act):
    def pad_body(e, carry):
        _for_each_piece(pn_ref[e], lambda off, size: act(
            pltpu.make_async_copy(z_ref.at[pl.ds(0, size)], xs_ref.at[pl.ds(ps_ref[e] + off, size)], sem)),
            PAD_BITS)
        return carry
    lax.fori_loop(0, ne, pad_body, 0)

    def tail_body(b, carry):
        act(pltpu.make_async_copy(z_ref, xs_ref.at[pl.ds(tail_ref[0] + b * MOE_TILE, MOE_TILE)], sem))
        return carry
    lax.fori_loop(0, tail_ref[1], tail_body, 0)


def _dispatch_kernel(cnt_ref, seg_ref, gb_ref, ps_ref, pn_ref, tail_ref, lidx_p_ref, lidx_s_ref, hf3_p_ref, hf3_s_ref,
                     xs_ref, buf0, buf1, z_ref, sems, *, ntp, ne):
    i = pl.program_id(0)
    n = pl.num_programs(0)
    tt = lidx_p_ref.shape[0] // TOP_K

    def whole(buf, sem):
        return pltpu.make_async_copy(buf, xs_ref.at[pl.ds(0, tt * TOP_K)], sem)

    @pl.when(i == 0)
    def _():
        z_ref[...] = jnp.zeros(z_ref.shape, F32)
        _zero_rows(ps_ref, pn_ref, tail_ref, xs_ref, z_ref, sems.at[2], ne, lambda c: c.start())

    def run(buf, sem):
        @pl.when(i >= 2)
        def _():
            whole(buf, sem).wait()

        def sort_from(hf3_ref, lidx_ref):
            def sort_body(tb, carry):
                for u in range(SORT_UNROLL):
                    t = tb * SORT_UNROLL + u
                    v = hf3_ref[pl.ds(pl.multiple_of(t * SUB, SUB), SUB), :]
                    for k in range(TOP_K):
                        buf[lidx_ref[t * TOP_K + k]] = v
                return carry
            lax.fori_loop(0, tt // SORT_UNROLL, sort_body, 0)

        @pl.when(i < ntp)
        def _():
            sort_from(hf3_p_ref, lidx_p_ref)

        @pl.when(i >= ntp)
        def _():
            sort_from(hf3_s_ref, lidx_s_ref)

        def seg_body(e, carry):
            c, s, g = cnt_ref[i * ne + e], seg_ref[i * ne + e], gb_ref[i * ne + e]
            _for_each_piece(c, lambda off, size: pltpu.make_async_copy(
                buf.at[pl.ds(s + off, size)], xs_ref.at[pl.ds(g + off, size)], sem).start(), SEG_BITS)
            return carry
        lax.fori_loop(0, ne, seg_body, 0)

    @pl.when(i % 2 == 0)
    def _():
        run(buf0, sems.at[0])

    @pl.when(i % 2 == 1)
    def _():
        run(buf1, sems.at[1])

    @pl.when(i == 0)
    def _():
        _zero_rows(ps_ref, pn_ref, tail_ref, xs_ref, z_ref, sems.at[2], ne, lambda c: c.wait())

    @pl.when(i == n - 1)
    def _():
        @pl.when(i % 2 == 0)
        def _():
            whole(buf0, sems.at[0]).wait()

            @pl.when(i >= 1)
            def _():
                whole(buf1, sems.at[1]).wait()

        @pl.when(i % 2 == 1)
        def _():
            whole(buf1, sems.at[1]).wait()
            whole(buf0, sems.at[0]).wait()


def _dispatch(cnt, seg, gbase, pad_start, pad_n, tail, lidx_p, lidx_s, hf3_p, hf3_s, rows, ne):
    tt = ROUTE_TILE
    ntp, nts = lidx_p.shape[0] // (tt * TOP_K), lidx_s.shape[0] // (tt * TOP_K)
    pmap = lambda i, *_: jnp.minimum(i, ntp - 1)
    smap = lambda i, *_: jnp.maximum(i - ntp, 0)
    return pl.pallas_call(
        functools.partial(_dispatch_kernel, ntp=ntp, ne=ne),
        grid_spec=pltpu.PrefetchScalarGridSpec(
            num_scalar_prefetch=6, grid=(ntp + nts,),
            in_specs=[pl.BlockSpec((tt * TOP_K,), lambda i, *_: (pmap(i),), memory_space=pltpu.SMEM),
                      pl.BlockSpec((tt * TOP_K,), lambda i, *_: (smap(i),), memory_space=pltpu.SMEM),
                      pl.BlockSpec((tt * SUB, LANES), lambda i, *_: (pmap(i), 0)),
                      pl.BlockSpec((tt * SUB, LANES), lambda i, *_: (smap(i), 0))],
            out_specs=pl.BlockSpec(memory_space=pl.ANY),
            scratch_shapes=[pltpu.VMEM((tt * TOP_K, SUB, LANES), F32), pltpu.VMEM((tt * TOP_K, SUB, LANES), F32),
                            pltpu.VMEM((MOE_TILE, SUB, LANES), F32), pltpu.SemaphoreType.DMA((3,))]),
        out_shape=jax.ShapeDtypeStruct((rows, SUB, LANES), F32),
        compiler_params=_cparams(("arbitrary",)),
        name="dispatch",
    )(cnt, seg, gbase, pad_start, pad_n, tail, lidx_p, lidx_s, hf3_p, hf3_s)


FFN_RING = 4


def _ffn_kernel(be_ref, nbu_ref, xs_ref, wg_ref, wu_ref, wd_ref, ys_ref, xbuf, wg_s, wu_s, wd_s, sems):
    b = pl.program_id(0)
    nbu = nbu_ref[0]
    tm = ys_ref.shape[0] // SUB
    blk = tm * SUB

    def fetch(k):
        slot = k % FFN_RING
        start = k * blk if isinstance(k, int) else pl.multiple_of(k * blk, blk)
        return pltpu.make_async_copy(xs_ref.at[pl.ds(start, blk)], xbuf.at[slot], sems.at[slot])

    @pl.when(b == 0)
    def _():
        for k in range(FFN_RING - 1):
            @pl.when(k < nbu)
            def _():
                fetch(k).start()

    @pl.when(b + FFN_RING - 1 < nbu)
    def _():
        fetch(b + FFN_RING - 1).start()

    @pl.when(b < nbu)
    def _():
        @pl.when((b == 0) | (be_ref[b] != be_ref[jnp.maximum(b - 1, 0)]))
        def _():
            wg_s[...] = wg_ref[0].astype(MXU_DTYPE)
            wu_s[...] = wu_ref[0].astype(MXU_DTYPE)
            wd_s[...] = wd_ref[0].astype(MXU_DTYPE)

        fetch(b).wait()
        x = _from_row_tiles(xbuf.at[b % FFN_RING], tm)
        hb = _silu(_mm(x, wg_s[...])) * _mm(x, wu_s[...])
        _to_row_tiles(ys_ref, _mm(hb, wd_s[...]))

    @pl.when(b >= nbu_ref[0])
    def _():
        ys_ref[...] = jnp.zeros(ys_ref.shape, F32)


def _ffn(block_e, nb_used, xs2, wg, wu, wd):
    rows = xs2.shape[0] // SUB
    d, f = wg.shape[1], wg.shape[2]
    tm = MOE_TILE
    nb = rows // tm
    wmap = lambda b, be, nbu: (be[b], 0, 0)
    return pl.pallas_call(
        _ffn_kernel,
        grid_spec=pltpu.PrefetchScalarGridSpec(
            num_scalar_prefetch=2,
            grid=(nb,),
            in_specs=[pl.BlockSpec(memory_space=pl.ANY), pl.BlockSpec((1, d, f), wmap),
                      pl.BlockSpec((1, d, f), wmap), pl.BlockSpec((1, f, d), wmap)],
            out_specs=pl.BlockSpec((tm * SUB, LANES), lambda b, be, nbu: (b, 0)),
            scratch_shapes=[pltpu.VMEM((FFN_RING, tm * SUB, LANES), F32),
                            pltpu.VMEM((d, f), MXU_DTYPE), pltpu.VMEM((d, f), MXU_DTYPE),
                            pltpu.VMEM((f, d), MXU_DTYPE), pltpu.SemaphoreType.DMA((FFN_RING,))]),
        out_shape=jax.ShapeDtypeStruct((rows * SUB, LANES), F32),
        compiler_params=_cparams(("arbitrary",)),
        name="ffn",
    )(block_e, nb_used, xs2, wg, wu, wd)


def _combine_kernel(cnt_ref, seg_ref, gb_ref, lidx_ref, wts_ref, hf3_ref, x1_ref, gt2_ref, gpost_ref,
                    wsg_ref, wsu_ref, wsd_ref, ys_ref, y_ref, buf0, buf1, y3_ref, sems, *, tile0, ne):
    bb, ll, d = x1_ref.shape
    tt = bb * ll
    i = pl.program_id(0)
    n = pl.num_programs(0)

    def fetch(j, buf, sem):
        def seg_body(e, carry):
            c, s, g = cnt_ref[j * ne + e], seg_ref[j * ne + e], gb_ref[j * ne + e]
            _for_each_piece(c, lambda off, size: pltpu.make_async_copy(
                ys_ref.at[pl.ds(g + off, size)], buf.at[pl.ds(s + off, size)], sem).start(), SEG_BITS)
            return carry
        lax.fori_loop(0, ne, seg_body, 0)

    def run(buf, sem, nbuf, nsem):
        @pl.when(i == 0)
        def _():
            fetch(tile0, buf, sem)

        @pl.when(i + 1 < n)
        def _():
            fetch(tile0 + i + 1, nbuf, nsem)

        pltpu.make_async_copy(ys_ref.at[pl.ds(0, tt * TOP_K)], buf, sem).wait()

        def tok_body(tb, carry):
            for u in range(SORT_UNROLL):
                t = tb * SORT_UNROLL + u
                acc = wts_ref[t * TOP_K] * buf[lidx_ref[t * TOP_K]]
                for k in range(1, TOP_K):
                    acc = acc + wts_ref[t * TOP_K + k] * buf[lidx_ref[t * TOP_K + k]]
                y3_ref[pl.ds(pl.multiple_of(t * SUB, SUB), SUB), :] = acc
            return carry
        lax.fori_loop(0, tt // SORT_UNROLL, tok_body, 0)

    @pl.when(i % 2 == 0)
    def _():
        run(buf0, sems.at[0], buf1, sems.at[1])

    @pl.when(i % 2 == 1)
    def _():
        run(buf1, sems.at[1], buf0, sems.at[0])

    hf = _from_row_tiles(hf3_ref, tt)
    shared = _mm(_silu(_mm(hf, wsg_ref[...])) * _mm(hf, wsu_ref[...]), wsd_ref[...])
    yf = (_from_row_tiles(y3_ref, tt) + shared).reshape(bb, ll, d)
    y_ref[...] = x1_ref[...] + gt2_ref[...] * _rms(yf, gpost_ref[...])


def _combine(cnt, seg, gbase, lidx, wts, hf3, x1, gt2, gpost, wsg, wsu, wsd, ys, tile0, ne):
    b, l, d = x1.shape
    tt = ROUTE_TILE
    ll = min(l, tt)
    bb = tt // ll
    nl = l // ll
    nt = (b // bb) * nl
    fs = wsg.shape[1]
    if nl == 1:
        tmap = lambda i, *_: (i, 0, 0)
        bmap = lambda i, *_: (i, 0, 0)
    else:
        tmap = lambda i, *_: (i // nl, i % nl, 0)
        bmap = lambda i, *_: (i // nl, 0, 0)
    wmap = lambda i, *_: (0, 0)
    kspec = lambda: pl.BlockSpec((tt * TOP_K,), lambda i, *_: (i,), memory_space=pltpu.SMEM)
    return pl.pallas_call(
        functools.partial(_combine_kernel, tile0=tile0, ne=ne),
        grid_spec=pltpu.PrefetchScalarGridSpec(
            num_scalar_prefetch=3, grid=(nt,),
            in_specs=[kspec(), kspec(),
                      pl.BlockSpec((tt * SUB, LANES), lambda i, *_: (i, 0)),
                      pl.BlockSpec((bb, ll, d), tmap), pl.BlockSpec((bb, 1, d), bmap),
                      pl.BlockSpec((1, d), wmap), pl.BlockSpec((d, fs), wmap), pl.BlockSpec((d, fs), wmap),
                      pl.BlockSpec((fs, d), wmap),
                      pl.BlockSpec(memory_space=pl.ANY)],
            out_specs=pl.BlockSpec((bb, ll, d), tmap),
            scratch_shapes=[pltpu.VMEM((tt * TOP_K, SUB, LANES), F32), pltpu.VMEM((tt * TOP_K, SUB, LANES), F32),
                            pltpu.VMEM((tt * SUB, LANES), F32), pltpu.SemaphoreType.DMA((2,))]),
        out_shape=jax.ShapeDtypeStruct((b, l, d), F32),
        compiler_params=_cparams(("arbitrary",), VMEM_LIMIT_COMBINE),
        name="combine",
    )(cnt, seg, gbase, lidx, wts, hf3, x1, gt2, gpost, wsg, wsu, wsd, ys)


def _rope_tables(pos, rows):
    half = DH // 2
    freqs = ROPE_BASE ** (-jnp.arange(half, dtype=F32) / half)
    ang = pos.astype(F32)[:, None] * freqs[None, :]
    cos, sin = jnp.cos(ang), jnp.sin(ang)
    cos2 = jnp.concatenate([cos, cos], axis=1)
    sin2 = jnp.concatenate([-sin, sin], axis=1)
    padr = rows - pos.shape[0]
    return jnp.pad(cos2, ((0, padr), (0, 0))), jnp.pad(sin2, ((0, padr), (0, 0)))


def _global_plan(cnt_tiles, n_assign, ne):
    counts = jnp.sum(cnt_tiles, axis=0)
    padded = (counts + MOE_TILE - 1) // MOE_TILE * MOE_TILE
    pend = jnp.cumsum(padded)
    base = pend - padded
    gbase = base[None, :] + jnp.cumsum(cnt_tiles, axis=0) - cnt_tiles
    nb = n_assign // MOE_TILE + ne
    starts = jnp.arange(nb, dtype=jnp.int32) * MOE_TILE
    block_e = jnp.minimum(jnp.sum((pend[None, :] <= starts[:, None]).astype(jnp.int32), axis=1), ne - 1)
    nb_used = pend[-1:] // MOE_TILE
    tail = jnp.concatenate([pend[-1:], nb - nb_used])
    i32 = lambda v: v.astype(jnp.int32)
    return i32(gbase.reshape(-1)), i32(block_e), i32(nb_used), i32(base + counts), i32(padded - counts), i32(tail), nb


def kernel(x_prompt, x_sample, c_prompt, c_sample, state_mlstm_C, state_mlstm_n, state_mlstm_m, state_mlstm_conv, state_ret_S, w_ada, b_ada, g_mix_pre, g_mix_post, g_ffn_pre, g_ffn_post, w_in, b_gates, conv_w, conv_b, m_gn, r_gn, w_out, w_router, b_router, w_exp_gate, w_exp_up, w_exp_down, w_sh_gate, w_sh_up, w_sh_down):
    depth = w_ada.shape[0]
    bp, lp, d = x_prompt.shape
    bs, ls, _ = x_sample.shape
    mh, rh = state_mlstm_C.shape[2], state_ret_S.shape[2]
    mw, rw = mh * DH, rh * DH
    ne = w_router.shape[2]
    tp, ts = bp * lp, bs * ls
    ntp = tp // ROUTE_TILE

    cos_p, sin_p = _rope_tables(jnp.arange(lp, dtype=jnp.int32), lp)
    cos_s, sin_s = _rope_tables(PAST_LEN + jnp.arange(ls, dtype=jnp.int32), CHUNK if ls % CHUNK else ls)

    hp, hs = x_prompt, x_sample
    new_p, new_s = [], []
    for l in range(depth):
        wi = w_in[l]
        g0, nmain = 4 * mw, 4 * mw + 4 * rw
        w_in_r = jnp.zeros((d, nmain + GATE_LANES), MXU_DTYPE)
        w_in_r = w_in_r.at[:, :g0].set(wi[:, :g0].astype(MXU_DTYPE))
        w_in_r = w_in_r.at[:, g0:nmain].set(wi[:, g0 + 2 * mh:].astype(MXU_DTYPE))
        w_in_r = w_in_r.at[:, nmain:nmain + 2 * mh].set(wi[:, g0:g0 + 2 * mh].astype(MXU_DTYPE))
        bg = jnp.pad(b_gates[l], (0, GATE_LANES - 2 * mh)).reshape(1, GATE_LANES)
        wout = w_out[l].astype(MXU_DTYPE)
        wr = jnp.pad(w_router[l], ((0, 0), (0, GATE_LANES - ne)))
        wr_hi, wr_lo = _split_hi_lo(wr)
        wsg, wsu, wsd = (w_sh_gate[l].astype(MXU_DTYPE), w_sh_up[l].astype(MXU_DTYPE), w_sh_down[l].astype(MXU_DTYPE))
        row = lambda v: v.reshape(1, -1)

        mod = _ada(jnp.concatenate([c_prompt, c_sample], axis=0), w_ada[l], b_ada[l])
        mods_p = [m[:bp].reshape(bp, 1, d) for m in jnp.split(mod, 6, axis=1)]
        mods_s = [m[bp:].reshape(bs, 1, d) for m in jnp.split(mod, 6, axis=1)]

        def lane_m(m):
            return jnp.pad(m, ((0, 0), (mh, GATE_LANES - 2 * mh)))[:, None, :]

        init = (jnp.zeros((bp, mh, DH, DH), F32), jnp.zeros((bp, mh, DH), F32), lane_m(jnp.zeros((bp, mh), F32)),
                jnp.zeros((bp, CONV_W - 1, 2 * mw), F32), jnp.zeros((bp, rh, DH, DH), F32))
        past = (state_mlstm_C[l], state_mlstm_n[l], lane_m(state_mlstm_m[l]), state_mlstm_conv[l], state_ret_S[l])

        def mix(x, mods, state, cos2, sin2):
            sh1, sc1, gt1, sh2, sc2, gt2 = mods
            b, ll, _ = x.shape
            if ll % MIX_ROWS == 0:
                x1, hf3, lg, c_n, n_n, m_n, conv_n, s_n = _mixer(
                    x, mods, cos2, sin2, *state, row(g_mix_pre[l]), w_in_r, conv_w[l], row(conv_b[l]), bg,
                    row(m_gn[l]), row(r_gn[l]), row(g_mix_post[l]), row(g_ffn_pre[l]), wout, wr_hi, wr_lo)
            else:
                z = _inproj(x, sc1, sh1, row(g_mix_pre[l]), w_in_r).reshape(b, ll, -1)
                o_mix, c_n, n_n, m_n, conv_n, s_n = _mixcore_packed(
                    z, cos2, sin2, *state, conv_w[l], row(conv_b[l]), bg, row(m_gn[l]), row(r_gn[l]))
                x1, hf3, lg = _mixout(o_mix, x, gt1, sc2, sh2, row(g_mix_post[l]), row(g_ffn_pre[l]), wout, wr_hi,
                                      wr_lo)
            lidx, wts, cnt, seg = _router(lg, b_router[l], ne)
            flat = lambda v: v.T.reshape(-1)
            return (x1, hf3, flat(lidx), flat(wts), cnt[::SUB, :ne], seg[::SUB, :ne],
                    (c_n, n_n, m_n[:, 0, mh:2 * mh], conv_n, s_n))

        x1_p, hf3_p, lidx_p, wts_p, cnt_p, seg_p, st_p = mix(hp, mods_p, init, cos_p, sin_p)
        x1_s, hf3_s, lidx_s, wts_s, cnt_s, seg_s, st_s = mix(hs, mods_s, past, cos_s, sin_s)

        cnt_t = jnp.concatenate([cnt_p, cnt_s], axis=0)
        cnt = cnt_t.reshape(-1)
        seg = jnp.concatenate([seg_p, seg_s], axis=0).reshape(-1)
        gbase, block_e, nb_used, pad_start, pad_n, tail, nb = _global_plan(cnt_t, (tp + ts) * TOP_K, ne)
        xs = _dispatch(cnt, seg, gbase, pad_start, pad_n, tail, lidx_p, lidx_s, hf3_p, hf3_s, nb * MOE_TILE, ne)
        ys = _ffn(block_e, nb_used, xs.reshape(nb * MOE_TILE * SUB, LANES), w_exp_gate[l], w_exp_up[l], w_exp_down[l])
        ys = ys.reshape(nb * MOE_TILE, SUB, LANES)
        hp = _combine(cnt, seg, gbase, lidx_p, wts_p, hf3_p, x1_p, mods_p[5], row(g_ffn_post[l]), wsg, wsu, wsd, ys, 0, ne)
        hs = _combine(cnt, seg, gbase, lidx_s, wts_s, hf3_s, x1_s, mods_s[5], row(g_ffn_post[l]), wsg, wsu, wsd, ys, ntp, ne)
        new_p.append(st_p)
        new_s.append(st_s)

    p_state = [jnp.stack([st[i] for st in new_p]) for i in range(5)]
    s_state = [jnp.stack([st[i] for st in new_s]) for i in range(5)]
    return (hp, hs, *p_state, *s_state)
```

```python
import functools
import math

import jax
import jax.numpy as jnp
from jax import lax
from jax.experimental import pallas as pl
from jax.experimental.pallas import tpu as pltpu

F32 = jnp.float32
MXU_DTYPE = jnp.bfloat16

NORM_EPS = 1e-6
ROPE_BASE = 10000.0
PAST_LEN = 16384
CHUNK = 128
CONV_W = 4
DH = 128
TOP_K = 8
N_GROUPS = 8
TOPK_GROUPS = 4
ROUTED_SCALE = 2.5
GATE_LANES = 128

V7X_VMEM_BYTES = 64 * 1024 * 1024
VMEM_LIMIT = 48 * 1024 * 1024
VMEM_LIMIT_COMBINE = 56 * 1024 * 1024
assert VMEM_LIMIT < VMEM_LIMIT_COMBINE < V7X_VMEM_BYTES

SUB, LANES = 8, 128

TOK_TILE = 256
ROUTE_TILE = 512
MOE_TILE = 256


def _cparams(sem, vmem_limit=VMEM_LIMIT):
    return pltpu.CompilerParams(dimension_semantics=sem, vmem_limit_bytes=vmem_limit)


def _silu(x):
    return x * jax.nn.sigmoid(x)


def _rms(x, g):
    return x * lax.rsqrt(jnp.mean(x * x, axis=-1, keepdims=True) + NORM_EPS) * g


def _mm(a, b):
    return jnp.dot(a.astype(MXU_DTYPE), b.astype(MXU_DTYPE), preferred_element_type=F32)


def _mm_nt(a, b):
    return lax.dot_general(a.astype(MXU_DTYPE), b.astype(MXU_DTYPE), (((1,), (1,)), ((), ())),
                           preferred_element_type=F32)


def _mm_tn(a, b):
    return lax.dot_general(a.astype(MXU_DTYPE), b.astype(MXU_DTYPE), (((0,), (0,)), ((), ())),
                           preferred_element_type=F32)


def _ada_kernel(c_ref, w_ref, b_ref, o_ref):
    o_ref[...] = _mm(_silu(c_ref[...]), w_ref[...]) + b_ref[...]


def _ada(c, w_ada, b_ada):
    n, d = c.shape
    dout = w_ada.shape[1]
    tn = 1024
    return pl.pallas_call(
        _ada_kernel,
        grid=(dout // tn,),
        in_specs=[pl.BlockSpec((n, d), lambda j: (0, 0)),
                  pl.BlockSpec((d, tn), lambda j: (0, j)),
                  pl.BlockSpec((1, tn), lambda j: (0, j))],
        out_specs=pl.BlockSpec((n, tn), lambda j: (0, j)),
        out_shape=jax.ShapeDtypeStruct((n, dout), F32),
        compiler_params=_cparams(("arbitrary",)),
        name="ada",
    )(c, w_ada, b_ada.reshape(1, dout))


def _inproj_kernel(x_ref, sc_ref, sh_ref, g_ref, w_ref, z_ref):
    x = x_ref[...]
    h = _rms(x, g_ref[...]) * (1.0 + sc_ref[...]) + sh_ref[...]
    bb, ll, d = x.shape
    z_ref[...] = _mm(h.reshape(bb * ll, d), w_ref[...])


def _inproj(x, sc, sh, g, w):
    b, l, d = x.shape
    n = w.shape[1]
    ll = min(l, TOK_TILE)
    bb = TOK_TILE // ll
    mod_spec = pl.BlockSpec((bb, 1, d), lambda i, j: (i, 0, 0))
    return pl.pallas_call(
        _inproj_kernel,
        grid=(b // bb, l // ll),
        in_specs=[pl.BlockSpec((bb, ll, d), lambda i, j: (i, j, 0)), mod_spec, mod_spec,
                  pl.BlockSpec((1, d), lambda i, j: (0, 0)),
                  pl.BlockSpec((d, n), lambda i, j: (0, 0))],
        out_specs=pl.BlockSpec((bb * ll, n), lambda i, j: (i * (l // ll) + j, 0)),
        out_shape=jax.ShapeDtypeStruct((b * l, n), F32),
        compiler_params=_cparams(("arbitrary", "arbitrary")),
        name="inproj",
    )(x, sc, sh, g, w)


def _cumsum_rows(x):
    n = x.shape[0]
    row = lax.broadcasted_iota(jnp.int32, x.shape, 0)
    s = 1
    while s < n:
        x = x + jnp.where(row >= s, pltpu.roll(x, s, axis=0), 0.0)
        s *= 2
    return x


def _head_norm(y, g):
    mu = jnp.mean(y, axis=-1, keepdims=True)
    yc = y - mu
    var = jnp.mean(yc * yc, axis=-1, keepdims=True)
    return yc * lax.rsqrt(var + NORM_EPS) * g


def _log_sigmoid(x):
    return jnp.minimum(x, 0.0) - jnp.log1p(jnp.exp(-jnp.abs(x)))


def _chunk_step(rows, put, cos2, sin2, c_ref, n_ref, m_ref, conv_ref, s_ref, xcat_ref,
                convw_ref, convb_ref, bg_ref, mgn_ref, rgn_ref, mh, rh, after_head=None):
    cl = CHUNK
    mw = mh * DH
    rw = rh * DH
    o_qk, o_mv, o_mo = 0, 2 * mw, 3 * mw
    o_rq = 4 * mw
    o_rk, o_rv, o_rg, o_gt = o_rq + rw, o_rq + 2 * rw, o_rq + 3 * rw, o_rq + 4 * rw

    xcat_ref[8:8 + cl, :] = rows(o_qk, o_qk + 2 * mw)
    qk = convb_ref[...]
    for t in range(CONV_W):
        qk = qk + xcat_ref[8 - (CONV_W - 1) + t:8 - (CONV_W - 1) + t + cl, :] * convw_ref[t:t + 1, :]
    qk = _silu(qk)
    new_conv = xcat_ref[8 + cl - (CONV_W - 1):8 + cl, :]
    conv_ref[0] = new_conv
    xcat_ref[8 - (CONV_W - 1):8, :] = new_conv

    row_c = lax.broadcasted_iota(jnp.int32, (cl, 1), 0)
    ii = lax.broadcasted_iota(jnp.int32, (cl, cl), 0)
    jj = lax.broadcasted_iota(jnp.int32, (cl, cl), 1)
    causal = jj <= ii
    eye = jj == ii

    def to_row(col):
        return jnp.sum(jnp.where(eye, col, 0.0), axis=0, keepdims=True)

    g = rows(o_gt, o_gt + GATE_LANES) + bg_ref[...]
    bcum = _cumsum_rows(_log_sigmoid(g))
    m_prev = m_ref[0]
    a_all = bcum + m_prev
    lane = lax.broadcasted_iota(jnp.int32, (1, GATE_LANES), 1)
    m_new_row = m_prev

    for h in range(mh):
        q = qk[:, h * DH:(h + 1) * DH]
        k = qk[:, mw + h * DH:mw + (h + 1) * DH] * (DH ** -0.5)
        v = rows(o_mv + h * DH, o_mv + (h + 1) * DH)
        b_col = bcum[:, mh + h:mh + h + 1]
        logi_col = g[:, h:h + 1]
        a_col = a_all[:, mh + h:mh + h + 1]
        dmat = jnp.where(causal, b_col + to_row(logi_col - b_col), -jnp.inf)
        m_t = jnp.maximum(a_col, jnp.max(dmat, axis=1, keepdims=True))
        w_intra = jnp.exp(dmat - m_t)
        w_inter = jnp.exp(a_col - m_t)
        c_old = c_ref[0, h]
        n_old = n_ref[0, h:h + 1, :]
        s = _mm_nt(q, k) * w_intra
        num = _mm(s, v) + w_inter * _mm(q, c_old)
        den = jnp.sum(s, axis=1, keepdims=True) + w_inter * jnp.sum(q * n_old, axis=1, keepdims=True)
        hh = num * (1.0 / jnp.maximum(jnp.abs(den), jnp.exp(-m_t)))
        m_new = m_t[cl - 1:cl, :]
        b_last = b_col[cl - 1:cl, :]
        w_old = jnp.exp(b_last + m_prev[:, mh + h:mh + h + 1] - m_new)
        w_k = jnp.exp(b_last - b_col + logi_col - m_new)
        c_ref[0, h] = w_old * c_old + _mm_tn(k, w_k * v)
        n_ref[0, h:h + 1, :] = w_old * n_old + jnp.sum(w_k * k, axis=0, keepdims=True)
        m_new_row = jnp.where(lane == mh + h, m_new, m_new_row)
        om = _head_norm(hh, mgn_ref[:, h * DH:(h + 1) * DH]) * jax.nn.sigmoid(rows(o_mo + h * DH, o_mo + (h + 1) * DH))
        put(h * DH, (h + 1) * DH, om)
        if after_head is not None:
            after_head(h)
    m_ref[0] = m_new_row

    rel = (ii - jj).astype(F32)
    row_f = row_c.astype(F32)

    def rot(x):
        return x * cos2 + pltpu.roll(x, DH // 2, axis=1) * sin2

    for h in range(rh):
        log_g = math.log1p(-2.0 ** (-5.0 - h))
        q = rot(rows(o_rq + h * DH, o_rq + (h + 1) * DH))
        k = rot(rows(o_rk + h * DH, o_rk + (h + 1) * DH)) * (DH ** -0.5)
        v = rows(o_rv + h * DH, o_rv + (h + 1) * DH)
        decay = jnp.where(causal, jnp.exp(jnp.maximum(rel, 0.0) * log_g), 0.0)
        s_old = s_ref[0, h]
        y = _mm(_mm_nt(q, k) * decay, v) + jnp.exp((row_f + 1.0) * log_g) * _mm(q, s_old)
        k_w = jnp.exp((cl - 1.0 - row_f) * log_g)
        s_ref[0, h] = math.exp(cl * log_g) * s_old + _mm_tn(k, k_w * v)
        orr = _head_norm(y, rgn_ref[:, h * DH:(h + 1) * DH]) * _silu(rows(o_rg + h * DH, o_rg + (h + 1) * DH))
        put(mw + h * DH, mw + (h + 1) * DH, orr)
        if after_head is not None:
            after_head(mh + h)


MIX_ROWS = 256


def _mixer_kernel(x_ref, xn_ref, sc1_ref, sh1_ref, sc1n_ref, sh1n_ref, gt1_ref, sc2_ref, sh2_ref, cos_ref, sin_ref,
                  c0_ref, n0_ref, m0_ref, conv0_ref, s0_ref,
                  gpre1_ref, win_ref, convw_ref, convb_ref, bg_ref, mgn_ref, rgn_ref,
                  gpost_ref, gpre2_ref, wout_ref, wr_hi_ref, wr_lo_ref,
                  x1_ref, hf3_ref, lg_ref, c_ref, n_ref, m_ref, conv_ref, s_ref,
                  z_ref, o_ref, xcat_ref, *, mh, rh):
    i, j = pl.program_id(0), pl.program_id(1)
    step = i * pl.num_programs(1) + j
    cur = step % 2
    rs, d = x_ref.shape[1], x_ref.shape[2]
    nz = z_ref.shape[2]
    nchunks = rs // CHUNK
    npieces = nchunks * (mh + rh)
    cuts = [(nz // LANES * p // npieces) * LANES for p in range(npieces + 1)]

    def project(x3, sc_ref, sh_ref):
        return (_rms(x3, gpre1_ref[...]) * (1.0 + sc_ref[...]) + sh_ref[...]).reshape(rs, d)

    @pl.when(step == 0)
    def _():
        z_ref[0] = _mm(project(x_ref[...], sc1_ref, sh1_ref), win_ref[...])

    @pl.when(j == 0)
    def _():
        c_ref[...] = c0_ref[...]
        n_ref[...] = n0_ref[...]
        m_ref[...] = m0_ref[...]
        s_ref[...] = s0_ref[...]
        xcat_ref[8 - (CONV_W - 1):8, :] = conv0_ref[0]

    h_next = project(xn_ref[...], sc1n_ref, sh1n_ref).astype(MXU_DTYPE)
    for c in range(nchunks):
        r0 = c * CHUNK

        def project_piece(hd, c=c):
            lo, hi = cuts[c * (mh + rh) + hd], cuts[c * (mh + rh) + hd + 1]
            z_ref[1 - cur, :, lo:hi] = jnp.dot(h_next, win_ref[:, lo:hi], preferred_element_type=F32)

        def put(lo, hi, v, r0=r0):
            o_ref[r0:r0 + CHUNK, lo:hi] = v

        _chunk_step(lambda lo, hi, r0=r0: z_ref[cur, r0:r0 + CHUNK, lo:hi], put,
                    cos_ref[r0:r0 + CHUNK, :], sin_ref[r0:r0 + CHUNK, :],
                    c_ref, n_ref, m_ref, conv_ref, s_ref, xcat_ref,
                    convw_ref, convb_ref, bg_ref, mgn_ref, rgn_ref, mh, rh, after_head=project_piece)
    x = x_ref[...]
    _mixout_body(o_ref[...].reshape(1, rs, o_ref.shape[1]), x, gt1_ref, sc2_ref, sh2_ref, gpost_ref, gpre2_ref,
                 wout_ref, wr_hi_ref, wr_lo_ref, x1_ref, hf3_ref, lg_ref)


def _mixer(x, mods, cos2, sin2, c0, n0, m0, conv0, s0, gpre1, win, convw, convb, bg, mgn, rgn,
           gpost, gpre2, wout, wr_hi, wr_lo):
    sh1, sc1, gt1, sh2, sc2, gt2 = mods
    b, l, d = x.shape
    assert d == SUB * LANES and l % MIX_ROWS == 0
    mh, rh = c0.shape[1], s0.shape[1]
    mw, rw = mh * DH, rh * DH
    nz, ne = win.shape[1], wr_hi.shape[1]
    rs = MIX_ROWS
    assert nz % LANES == 0
    nj = l // rs
    bmap4 = lambda i, j: (i, 0, 0, 0)
    bmap3 = lambda i, j: (i, 0, 0)
    tmap = lambda i, j: (i, j, 0)
    fmap = lambda i, j: (i * nj + j, 0)
    once = lambda a: pl.BlockSpec(a.shape, lambda i, j: (0, 0), pipeline_mode=pl.Buffered(1))
    mod_spec = pl.BlockSpec((1, 1, d), bmap3)
    nxt = lambda i, j: jnp.minimum(i * nj + j + 1, b * nj - 1)
    x_next_spec = pl.BlockSpec((1, rs, d), lambda i, j: (nxt(i, j) // nj, nxt(i, j) % nj, 0))
    mod_next_spec = pl.BlockSpec((1, 1, d), lambda i, j: (nxt(i, j) // nj, 0, 0))
    rope_spec = pl.BlockSpec((rs, DH), lambda i, j: (j, 0))
    state_specs = [pl.BlockSpec((1, mh, DH, DH), bmap4), pl.BlockSpec((1, mh, DH), bmap3),
                   pl.BlockSpec((1, 1, GATE_LANES), bmap3), pl.BlockSpec((1, CONV_W - 1, 2 * mw), bmap3),
                   pl.BlockSpec((1, rh, DH, DH), bmap4)]
    state_shapes = [jax.ShapeDtypeStruct(c0.shape, F32), jax.ShapeDtypeStruct(n0.shape, F32),
                    jax.ShapeDtypeStruct(m0.shape, F32), jax.ShapeDtypeStruct(conv0.shape, F32),
                    jax.ShapeDtypeStruct(s0.shape, F32)]
    return pl.pallas_call(
        functools.partial(_mixer_kernel, mh=mh, rh=rh),
        grid=(b, nj),
        in_specs=[pl.BlockSpec((1, rs, d), tmap), x_next_spec, mod_spec, mod_spec, mod_next_spec, mod_next_spec]
                 + [mod_spec] * 3 + [rope_spec, rope_spec] + state_specs
                 + [once(a) for a in (gpre1, win, convw, convb, bg, mgn, rgn, gpost, gpre2, wout, wr_hi, wr_lo)],
        out_specs=[pl.BlockSpec((1, rs, d), tmap), pl.BlockSpec((rs * SUB, LANES), fmap),
                   pl.BlockSpec((rs, ne), fmap)] + state_specs,
        out_shape=[jax.ShapeDtypeStruct((b, l, d), F32), jax.ShapeDtypeStruct((b * l * SUB, LANES), F32),
                   jax.ShapeDtypeStruct((b * l, ne), F32)] + state_shapes,
        scratch_shapes=[pltpu.VMEM((2, rs, nz), F32), pltpu.VMEM((rs, mw + rw), F32),
                        pltpu.VMEM((CHUNK + 8, 2 * mw), F32)],
        compiler_params=_cparams(("arbitrary", "arbitrary")),
        name="mixer",
    )(x, x, sc1, sh1, sc1, sh1, gt1, sc2, sh2, cos2, sin2, c0, n0, m0, conv0, s0,
      gpre1, win, convw, convb, bg, mgn, rgn, gpost, gpre2, wout, wr_hi, wr_lo)


PACK_SEQS = CHUNK // SUB


def _mixcore_packed_kernel(z_ref, cos_ref, sin_ref, c0_ref, n0_ref, m0_ref, conv0_ref, s0_ref,
                           convw_ref, convb_ref, bg_ref, mgn_ref, rgn_ref,
                           out_ref, c_ref, n_ref, m_ref, conv_ref, s_ref, *, mh, rh):
    g_n, l_n, _ = z_ref.shape
    rr = g_n * l_n
    mw, rw = mh * DH, rh * DH
    o_qk, o_mv, o_mo = 0, 2 * mw, 3 * mw
    o_rq = 4 * mw
    o_rk, o_rv, o_rg, o_gt = o_rq + rw, o_rq + 2 * rw, o_rq + 3 * rw, o_rq + 4 * rw

    def rows(lo, hi):
        return z_ref[:, :, lo:hi].reshape(rr, hi - lo)

    def per_seq(x):
        return jnp.broadcast_to(x[:, None, :], (g_n, l_n, x.shape[-1])).reshape(rr, x.shape[-1])

    def last_of_seq(x):
        x3 = x.reshape(g_n, l_n, x.shape[-1])
        return jnp.broadcast_to(x3[:, l_n - 1:l_n, :], x3.shape).reshape(x.shape)

    def full(col):
        return jnp.broadcast_to(col, (rr, DH))

    pos_c = lax.broadcasted_iota(jnp.int32, (rr, 1), 0) % l_n
    ii = lax.broadcasted_iota(jnp.int32, (rr, rr), 0)
    jj = lax.broadcasted_iota(jnp.int32, (rr, rr), 1)
    mask = ((ii // l_n) == (jj // l_n)) & (jj <= ii)
    eye = jj == ii

    def to_row(col):
        return jnp.sum(jnp.where(eye, col, 0.0), axis=0, keepdims=True)

    x = rows(o_qk, o_qk + 2 * mw)
    e = conv0_ref[...].reshape(rr, 2 * mw)
    qk = convb_ref[...]
    for t in range(CONV_W):
        d = CONV_W - 1 - t
        xd = x if d == 0 else jnp.where(pos_c >= d, pltpu.roll(x, d, axis=0),
                                        pltpu.roll(e, (d - (CONV_W - 1)) % rr, axis=0))
        qk = qk + xd * convw_ref[t:t + 1, :]
    qk = _silu(qk)
    conv_ref[...] = x.reshape(g_n, l_n, 2 * mw)[:, l_n - (CONV_W - 1):, :]

    g = rows(o_gt, o_gt + GATE_LANES) + bg_ref[...]
    bcum = _log_sigmoid(g)
    s_ = 1
    while s_ < l_n:
        bcum = bcum + jnp.where(pos_c >= s_, pltpu.roll(bcum, s_, axis=0), 0.0)
        s_ *= 2
    m_prev = per_seq(m0_ref[:, 0, :])
    a_all = bcum + m_prev
    lane = lax.broadcasted_iota(jnp.int32, (1, GATE_LANES), 1)
    even_seq = (lax.broadcasted_iota(jnp.int32, (rr, 1), 0) // l_n) % 2 == 0
    m_new_rows = m_prev

    def state_dot(qb, st_ref, h):
        parts = []
        for s in range(g_n):
            pb, lo = s // 2, (s % 2) * l_n
            res = jnp.dot(qb[pb * 2 * l_n:(pb + 1) * 2 * l_n], st_ref[s, h].astype(MXU_DTYPE),
                          preferred_element_type=F32)
            parts.append(res[lo:lo + l_n])
        return jnp.concatenate(parts, axis=0)

    def state_update(k, wv, decay_full, st_in, st_out, h):
        k_even = jnp.where(even_seq, k, 0.0).astype(MXU_DTYPE)
        k_odd = jnp.where(even_seq, 0.0, k).astype(MXU_DTYPE)
        wvb = wv.astype(MXU_DTYPE)
        for s in range(g_n):
            pb = s // 2
            kp = (k_even if s % 2 == 0 else k_odd)[pb * 2 * l_n:(pb + 1) * 2 * l_n]
            upd = lax.dot_general(kp, wvb[pb * 2 * l_n:(pb + 1) * 2 * l_n], (((0,), (0,)), ((), ())),
                                  preferred_element_type=F32)
            st_out[s, h] = decay_full[s * l_n:s * l_n + 1, :] * st_in[s, h] + upd

    for h in range(mh):
        q = qk[:, h * DH:(h + 1) * DH]
        k = qk[:, mw + h * DH:mw + (h + 1) * DH] * (DH ** -0.5)
        v = rows(o_mv + h * DH, o_mv + (h + 1) * DH)
        b_col = bcum[:, mh + h:mh + h + 1]
        logi_col = g[:, h:h + 1]
        a_col = a_all[:, mh + h:mh + h + 1]
        dmat = jnp.where(mask, b_col + to_row(logi_col - b_col), -jnp.inf)
        m_t = jnp.maximum(a_col, jnp.max(dmat, axis=1, keepdims=True))
        w_intra = jnp.exp(dmat - m_t)
        w_inter = jnp.exp(a_col - m_t)
        n_rows = per_seq(n0_ref[:, h, :])
        s = _mm_nt(q, k) * w_intra
        num = _mm(s, v) + w_inter * state_dot(q.astype(MXU_DTYPE), c0_ref, h)
        den = jnp.sum(s, axis=1, keepdims=True) + w_inter * jnp.sum(q * n_rows, axis=1, keepdims=True)
        hh = num * (1.0 / jnp.maximum(jnp.abs(den), jnp.exp(-m_t)))
        m_new = last_of_seq(full(m_t))
        b_last = last_of_seq(full(b_col))
        w_old = jnp.exp(b_last + full(m_prev[:, mh + h:mh + h + 1]) - m_new)
        w_k = jnp.exp(b_last - full(b_col) + full(logi_col) - m_new)
        state_update(k, w_k * v, w_old, c0_ref, c_ref, h)
        n_ref[:, h, :] = (w_old.reshape(g_n, l_n, DH)[:, 0, :] * n0_ref[:, h, :]
                          + jnp.sum((w_k * k).reshape(g_n, l_n, DH), axis=1))
        m_new_rows = jnp.where(lane == mh + h, m_new, m_new_rows)
        om = _head_norm(hh, mgn_ref[:, h * DH:(h + 1) * DH]) * jax.nn.sigmoid(rows(o_mo + h * DH, o_mo + (h + 1) * DH))
        out_ref[:, :, h * DH:(h + 1) * DH] = om.reshape(g_n, l_n, DH)
    m_ref[...] = m_new_rows.reshape(g_n, l_n, GATE_LANES)[:, 0:1, :]

    cos2 = cos_ref[...]
    sin2 = sin_ref[...]
    rel = (ii - jj).astype(F32)
    pos_f = pos_c.astype(F32)

    def rot(x):
        return x * cos2 + pltpu.roll(x, DH // 2, axis=1) * sin2

    for h in range(rh):
        log_g = math.log1p(-2.0 ** (-5.0 - h))
        q = rot(rows(o_rq + h * DH, o_rq + (h + 1) * DH))
        k = rot(rows(o_rk + h * DH, o_rk + (h + 1) * DH)) * (DH ** -0.5)
        v = rows(o_rv + h * DH, o_rv + (h + 1) * DH)
        decay = jnp.where(mask, jnp.exp(jnp.maximum(rel, 0.0) * log_g), 0.0)
        y = (_mm(_mm_nt(q, k) * decay, v)
             + jnp.exp((pos_f + 1.0) * log_g) * state_dot(q.astype(MXU_DTYPE), s0_ref, h))
        k_w = jnp.exp((l_n - 1.0 - pos_f) * log_g)
        state_update(k, k_w * v, jnp.full((rr, DH), math.exp(l_n * log_g), F32), s0_ref, s_ref, h)
        orr = _head_norm(y, rgn_ref[:, h * DH:(h + 1) * DH]) * _silu(rows(o_rg + h * DH, o_rg + (h + 1) * DH))
        out_ref[:, :, mw + h * DH:mw + (h + 1) * DH] = orr.reshape(g_n, l_n, DH)


def _mixcore_packed(z, cos2, sin2, c0, n0, m0, conv0, s0, convw, convb, bg, mgn, rgn):
    b, l, nz = z.shape
    assert l == SUB and b % PACK_SEQS == 0
    mh, rh = c0.shape[1], s0.shape[1]
    mw, rw = mh * DH, rh * DH
    g_n = PACK_SEQS
    conv0p = jnp.pad(conv0, ((0, 0), (0, l - (CONV_W - 1)), (0, 0)))
    tile = lambda t: jnp.tile(t[:l], (g_n, 1))
    bmap4 = lambda i: (i, 0, 0, 0)
    bmap3 = lambda i: (i, 0, 0)
    wmap = lambda i: (0, 0)
    c_spec, n_spec = pl.BlockSpec((g_n, mh, DH, DH), bmap4), pl.BlockSpec((g_n, mh, DH), bmap3)
    m_spec, s_spec = pl.BlockSpec((g_n, 1, GATE_LANES), bmap3), pl.BlockSpec((g_n, rh, DH, DH), bmap4)
    return pl.pallas_call(
        functools.partial(_mixcore_packed_kernel, mh=mh, rh=rh),
        grid=(b // g_n,),
        in_specs=[pl.BlockSpec((g_n, l, nz), bmap3),
                  pl.BlockSpec((g_n * l, DH), wmap), pl.BlockSpec((g_n * l, DH), wmap),
                  c_spec, n_spec, m_spec, pl.BlockSpec((g_n, l, 2 * mw), bmap3), s_spec,
                  pl.BlockSpec(convw.shape, wmap), pl.BlockSpec(convb.shape, wmap), pl.BlockSpec(bg.shape, wmap),
                  pl.BlockSpec(mgn.shape, wmap), pl.BlockSpec(rgn.shape, wmap)],
        out_specs=[pl.BlockSpec((g_n, l, mw + rw), bmap3), c_spec, n_spec, m_spec,
                   pl.BlockSpec((g_n, CONV_W - 1, 2 * mw), bmap3), s_spec],
        out_shape=[jax.ShapeDtypeStruct((b, l, mw + rw), F32), jax.ShapeDtypeStruct(c0.shape, F32),
                   jax.ShapeDtypeStruct(n0.shape, F32), jax.ShapeDtypeStruct(m0.shape, F32),
                   jax.ShapeDtypeStruct(conv0.shape, F32), jax.ShapeDtypeStruct(s0.shape, F32)],
        compiler_params=_cparams(("arbitrary",)),
        name="mixcore_packed",
    )(z, tile(cos2), tile(sin2), c0, n0, m0, conv0p, s0, convw, convb, bg, mgn, rgn)


def _split_hi_lo(x):
    hi = x.astype(MXU_DTYPE)
    lo = (x - hi.astype(F32)).astype(MXU_DTYPE)
    return hi, lo


def _to_row_tiles(ref, x):
    n = x.shape[0]
    for j in range(SUB):
        ref[pl.ds(j, n, stride=SUB), :] = x[:, j * LANES:(j + 1) * LANES]


def _from_row_tiles(ref, n):
    return jnp.concatenate([ref[pl.ds(j, n, stride=SUB), :] for j in range(SUB)], axis=1)


def _mixout_kernel(o_ref, x_ref, gt1_ref, sc2_ref, sh2_ref, gpost_ref, gpre_ref, wout_ref, wr_hi_ref, wr_lo_ref,
                   x1_ref, hf3_ref, lg_ref):
    _mixout_body(o_ref[...], x_ref[...], gt1_ref, sc2_ref, sh2_ref, gpost_ref, gpre_ref, wout_ref, wr_hi_ref,
                 wr_lo_ref, x1_ref, hf3_ref, lg_ref)


def _mixout_body(o, x, gt1_ref, sc2_ref, sh2_ref, gpost_ref, gpre_ref, wout_ref, wr_hi_ref, wr_lo_ref,
                 x1_ref, hf3_ref, lg_ref):
    bb, ll, d = x.shape
    ym = _mm(o.reshape(bb * ll, o.shape[-1]), wout_ref[...]).reshape(bb, ll, d)
    x1 = x + gt1_ref[...] * _rms(ym, gpost_ref[...])
    x1_ref[...] = x1
    hf = (_rms(x1, gpre_ref[...]) * (1.0 + sc2_ref[...]) + sh2_ref[...]).reshape(bb * ll, d)
    _to_row_tiles(hf3_ref, hf)
    hi, lo = _split_hi_lo(hf)
    lg_ref[...] = _mm(hi, wr_hi_ref[...]) + (_mm(lo, wr_hi_ref[...]) + _mm(hi, wr_lo_ref[...]))


def _mixout(o, x, gt1, sc2, sh2, gpost, gpre, wout, wr_hi, wr_lo):
    b, l, d = x.shape
    assert d == SUB * LANES
    w = o.shape[-1]
    ne = wr_hi.shape[1]
    ll = min(l, TOK_TILE)
    bb = TOK_TILE // ll
    tmap = lambda i, j: (i, j, 0)
    fmap = lambda i, j: (i * (l // ll) + j, 0)
    mod_spec = pl.BlockSpec((bb, 1, d), lambda i, j: (i, 0, 0))
    wmap = lambda i, j: (0, 0)
    return pl.pallas_call(
        _mixout_kernel,
        grid=(b // bb, l // ll),
        in_specs=[pl.BlockSpec((bb, ll, w), tmap), pl.BlockSpec((bb, ll, d), tmap), mod_spec, mod_spec, mod_spec,
                  pl.BlockSpec((1, d), wmap), pl.BlockSpec((1, d), wmap), pl.BlockSpec((w, d), wmap),
                  pl.BlockSpec((d, ne), wmap), pl.BlockSpec((d, ne), wmap)],
        out_specs=[pl.BlockSpec((bb, ll, d), tmap), pl.BlockSpec((bb * ll * SUB, LANES), fmap),
                   pl.BlockSpec((bb * ll, ne), fmap)],
        out_shape=[jax.ShapeDtypeStruct((b, l, d), F32), jax.ShapeDtypeStruct((b * l * SUB, LANES), F32),
                   jax.ShapeDtypeStruct((b * l, ne), F32)],
        compiler_params=_cparams(("arbitrary", "arbitrary")),
        name="mixout",
    )(o, x, gt1, sc2, sh2, gpost, gpre, wout, wr_hi, wr_lo)


def _router_kernel(lg_ref, br_ref, lidx_ref, wts_ref, cnt_ref, seg_ref, *, ne):
    tm = lg_ref.shape[0]
    gsz = ne // N_GROUPS
    s = jax.nn.sigmoid(lg_ref[...].T[:ne, :])
    sb = s + br_ref[...]
    sb3 = sb.reshape(N_GROUPS, gsz, tm)
    e3 = lax.broadcasted_iota(jnp.int32, sb3.shape, 1)
    m1 = jnp.max(sb3, axis=1, keepdims=True)
    first = jnp.min(jnp.where(sb3 == m1, e3, gsz), axis=1, keepdims=True)
    m2 = jnp.max(jnp.where(e3 == first, -jnp.inf, sb3), axis=1, keepdims=True)
    gs = (m1 + m2).reshape(N_GROUPS, tm)
    gi = lax.broadcasted_iota(jnp.int32, gs.shape, 0)
    grank = jnp.zeros(gs.shape, F32)
    for g in range(N_GROUPS):
        r = gs[g:g + 1, :]
        grank = grank + jnp.where(r > gs, 1.0, jnp.where(r == gs, jnp.where(g < gi, 1.0, 0.0), 0.0))
    gsel = jnp.where(grank < TOPK_GROUPS, 1.0, 0.0)
    emask = jnp.broadcast_to(gsel.reshape(N_GROUPS, 1, tm), sb3.shape).reshape(ne, tm) > 0.0
    masked = jnp.where(emask, sb, -jnp.inf)
    ei = lax.broadcasted_iota(jnp.int32, masked.shape, 0)
    cur = masked
    sel_f = jnp.zeros(masked.shape, F32)
    for _ in range(TOP_K):
        best = jnp.max(cur, axis=0, keepdims=True)
        first_e = jnp.min(jnp.where(cur == best, ei, ne), axis=0, keepdims=True)
        pick = ei == first_e
        sel_f = jnp.where(pick, 1.0, sel_f)
        cur = jnp.where(pick, -jnp.inf, cur)
    sel = sel_f > 0.0
    w = jnp.where(sel, s, 0.0)
    w = w / jnp.sum(w, axis=0, keepdims=True) * ROUTED_SCALE

    self = jnp.where(sel, 1.0, 0.0)
    er = lax.broadcasted_iota(jnp.int32, (ne, ne), 0)
    ec = lax.broadcasted_iota(jnp.int32, (ne, ne), 1)
    slot = _mm(jnp.where(ec < er, 1.0, 0.0), self)
    tr = lax.broadcasted_iota(jnp.int32, (tm, tm), 0)
    tc = lax.broadcasted_iota(jnp.int32, (tm, tm), 1)
    pos = _mm(self, jnp.where(tr < tc, 1.0, 0.0))
    seg = jnp.sum(slot, axis=1, keepdims=True)
    lrow = seg + pos
    lidx_rows, w_rows = [], []
    for k in range(TOP_K):
        pick = sel & (slot == float(k))
        lidx_rows.append(jnp.sum(jnp.where(pick, lrow, 0.0), axis=0, keepdims=True))
        w_rows.append(jnp.sum(jnp.where(pick, w, 0.0), axis=0, keepdims=True))
    lidx_ref[...] = jnp.concatenate(lidx_rows, axis=0).astype(jnp.int32)
    wts_ref[...] = jnp.concatenate(w_rows, axis=0)
    ones = jnp.ones((SUB, tm), F32)
    padz = jnp.zeros((LANES - ne, tm), F32)
    cnt_ref[...] = _mm_nt(ones, jnp.concatenate([self, padz], axis=0)).astype(jnp.int32)
    seg_ref[...] = _mm_nt(ones, jnp.concatenate([slot, padz], axis=0)).astype(jnp.int32)


def _router(logits, b_router, ne):
    t, lanes = logits.shape
    tm = ROUTE_TILE
    nt = t // tm
    kspec = pl.BlockSpec((TOP_K, tm), lambda i: (0, i))
    cspec = pl.BlockSpec((SUB, LANES), lambda i: (i, 0))
    return pl.pallas_call(
        functools.partial(_router_kernel, ne=ne),
        grid=(nt,),
        in_specs=[pl.BlockSpec((tm, lanes), lambda i: (i, 0)), pl.BlockSpec((ne, 1), lambda i: (0, 0))],
        out_specs=[kspec, kspec, cspec, cspec],
        out_shape=[jax.ShapeDtypeStruct((TOP_K, t), jnp.int32), jax.ShapeDtypeStruct((TOP_K, t), F32),
                   jax.ShapeDtypeStruct((nt * SUB, LANES), jnp.int32),
                   jax.ShapeDtypeStruct((nt * SUB, LANES), jnp.int32)],
        compiler_params=_cparams(("arbitrary",)),
        name="router",
    )(logits, b_router.reshape(ne, 1))


SEG_BITS = ROUTE_TILE.bit_length()
PAD_BITS = (MOE_TILE - 1).bit_length()


RARE_BIT = 7


def _for_each_piece(n, fn, bits):
    def pieces(lo, hi):
        for bit in range(hi - 1, lo - 1, -1):
            @pl.when(((n >> bit) & 1) == 1)
            def _():
                fn((n >> (bit + 1)) << (bit + 1), 1 << bit)

    if bits > RARE_BIT:
        @pl.when(n >= (1 << RARE_BIT))
        def _():
            pieces(RARE_BIT, bits)
    pieces(0, min(bits, RARE_BIT))


SORT_UNROLL = 4


def _zero_rows(ps_ref, pn_ref, tail_ref, xs_ref, z_ref, sem, ne, act):
    def pad_body(e, carry):
        _for_each_piece(pn_ref[e], lambda off, size: act(
            pltpu.make_async_copy(z_ref.at[pl.ds(0, size)], xs_ref.at[pl.ds(ps_ref[e] + off, size)], sem)),
            PAD_BITS)
        return carry
    lax.fori_loop(0, ne, pad_body, 0)

    def tail_body(b, carry):
        act(pltpu.make_async_copy(z_ref, xs_ref.at[pl.ds(tail_ref[0] + b * MOE_TILE, MOE_TILE)], sem))
        return carry
    lax.fori_loop(0, tail_ref[1], tail_body, 0)


def _dispatch_kernel(cnt_ref, seg_ref, gb_ref, ps_ref, pn_ref, tail_ref, lidx_p_ref, lidx_s_ref, hf3_p_ref, hf3_s_ref,
                     xs_ref, buf0, buf1, z_ref, sems, *, ntp, ne):
    i = pl.program_id(0)
    n = pl.num_programs(0)
    tt = lidx_p_ref.shape[0] // TOP_K

    def whole(buf, sem):
        return pltpu.make_async_copy(buf, xs_ref.at[pl.ds(0, tt * TOP_K)], sem)

    @pl.when(i == 0)
    def _():
        z_ref[...] = jnp.zeros(z_ref.shape, F32)
        _zero_rows(ps_ref, pn_ref, tail_ref, xs_ref, z_ref, sems.at[2], ne, lambda c: c.start())

    def run(buf, sem):
        @pl.when(i >= 2)
        def _():
            whole(buf, sem).wait()

        def sort_from(hf3_ref, lidx_ref):
            def sort_body(tb, carry):
                for u in range(SORT_UNROLL):
                    t = tb * SORT_UNROLL + u
                    v = hf3_ref[pl.ds(pl.multiple_of(t * SUB, SUB), SUB), :]
                    for k in range(TOP_K):
                        buf[lidx_ref[t * TOP_K + k]] = v
                return carry
            lax.fori_loop(0, tt // SORT_UNROLL, sort_body, 0)

        @pl.when(i < ntp)
        def _():
            sort_from(hf3_p_ref, lidx_p_ref)

        @pl.when(i >= ntp)
        def _():
            sort_from(hf3_s_ref, lidx_s_ref)

        def seg_body(e, carry):
            c, s, g = cnt_ref[i * ne + e], seg_ref[i * ne + e], gb_ref[i * ne + e]
            _for_each_piece(c, lambda off, size: pltpu.make_async_copy(
                buf.at[pl.ds(s + off, size)], xs_ref.at[pl.ds(g + off, size)], sem).start(), SEG_BITS)
            return carry
        lax.fori_loop(0, ne, seg_body, 0)

    @pl.when(i % 2 == 0)
    def _():
        run(buf0, sems.at[0])

    @pl.when(i % 2 == 1)
    def _():
        run(buf1, sems.at[1])

    @pl.when(i == 0)
    def _():
        _zero_rows(ps_ref, pn_ref, tail_ref, xs_ref, z_ref, sems.at[2], ne, lambda c: c.wait())

    @pl.when(i == n - 1)
    def _():
        @pl.when(i % 2 == 0)
        def _():
            whole(buf0, sems.at[0]).wait()

            @pl.when(i >= 1)
            def _():
                whole(buf1, sems.at[1]).wait()

        @pl.when(i % 2 == 1)
        def _():
            whole(buf1, sems.at[1]).wait()
            whole(buf0, sems.at[0]).wait()


def _dispatch(cnt, seg, gbase, pad_start, pad_n, tail, lidx_p, lidx_s, hf3_p, hf3_s, rows, ne):
    tt = ROUTE_TILE
    ntp, nts = lidx_p.shape[0] // (tt * TOP_K), lidx_s.shape[0] // (tt * TOP_K)
    pmap = lambda i, *_: jnp.minimum(i, ntp - 1)
    smap = lambda i, *_: jnp.maximum(i - ntp, 0)
    return pl.pallas_call(
        functools.partial(_dispatch_kernel, ntp=ntp, ne=ne),
        grid_spec=pltpu.PrefetchScalarGridSpec(
            num_scalar_prefetch=6, grid=(ntp + nts,),
            in_specs=[pl.BlockSpec((tt * TOP_K,), lambda i, *_: (pmap(i),), memory_space=pltpu.SMEM),
                      pl.BlockSpec((tt * TOP_K,), lambda i, *_: (smap(i),), memory_space=pltpu.SMEM),
                      pl.BlockSpec((tt * SUB, LANES), lambda i, *_: (pmap(i), 0)),
                      pl.BlockSpec((tt * SUB, LANES), lambda i, *_: (smap(i), 0))],
            out_specs=pl.BlockSpec(memory_space=pl.ANY),
            scratch_shapes=[pltpu.VMEM((tt * TOP_K, SUB, LANES), F32), pltpu.VMEM((tt * TOP_K, SUB, LANES), F32),
                            pltpu.VMEM((MOE_TILE, SUB, LANES), F32), pltpu.SemaphoreType.DMA((3,))]),
        out_shape=jax.ShapeDtypeStruct((rows, SUB, LANES), F32),
        compiler_params=_cparams(("arbitrary",)),
        name="dispatch",
    )(cnt, seg, gbase, pad_start, pad_n, tail, lidx_p, lidx_s, hf3_p, hf3_s)


FFN_RING = 4


def _ffn_kernel(be_ref, nbu_ref, xs_ref, wg_ref, wu_ref, wd_ref, ys_ref, xbuf, wg_s, wu_s, wd_s, sems):
    b = pl.program_id(0)
    nbu = nbu_ref[0]
    tm = ys_ref.shape[0] // SUB
    blk = tm * SUB

    def fetch(k):
        slot = k % FFN_RING
        start = k * blk if isinstance(k, int) else pl.multiple_of(k * blk, blk)
        return pltpu.make_async_copy(xs_ref.at[pl.ds(start, blk)], xbuf.at[slot], sems.at[slot])

    @pl.when(b == 0)
    def _():
        for k in range(FFN_RING - 1):
            @pl.when(k < nbu)
            def _():
                fetch(k).start()

    @pl.when(b + FFN_RING - 1 < nbu)
    def _():
        fetch(b + FFN_RING - 1).start()

    @pl.when(b < nbu)
    def _():
        @pl.when((b == 0) | (be_ref[b] != be_ref[jnp.maximum(b - 1, 0)]))
        def _():
            wg_s[...] = wg_ref[0].astype(MXU_DTYPE)
            wu_s[...] = wu_ref[0].astype(MXU_DTYPE)
            wd_s[...] = wd_ref[0].astype(MXU_DTYPE)

        fetch(b).wait()
        x = _from_row_tiles(xbuf.at[b % FFN_RING], tm)
        hb = _silu(_mm(x, wg_s[...])) * _mm(x, wu_s[...])
        _to_row_tiles(ys_ref, _mm(hb, wd_s[...]))

    @pl.when(b >= nbu_ref[0])
    def _():
        ys_ref[...] = jnp.zeros(ys_ref.shape, F32)


def _ffn(block_e, nb_used, xs2, wg, wu, wd):
    rows = xs2.shape[0] // SUB
    d, f = wg.shape[1], wg.shape[2]
    tm = MOE_TILE
    nb = rows // tm
    wmap = lambda b, be, nbu: (be[b], 0, 0)
    return pl.pallas_call(
        _ffn_kernel,
        grid_spec=pltpu.PrefetchScalarGridSpec(
            num_scalar_prefetch=2,
            grid=(nb,),
            in_specs=[pl.BlockSpec(memory_space=pl.ANY), pl.BlockSpec((1, d, f), wmap),
                      pl.BlockSpec((1, d, f), wmap), pl.BlockSpec((1, f, d), wmap)],
            out_specs=pl.BlockSpec((tm * SUB, LANES), lambda b, be, nbu: (b, 0)),
            scratch_shapes=[pltpu.VMEM((FFN_RING, tm * SUB, LANES), F32),
                            pltpu.VMEM((d, f), MXU_DTYPE), pltpu.VMEM((d, f), MXU_DTYPE),
                            pltpu.VMEM((f, d), MXU_DTYPE), pltpu.SemaphoreType.DMA((FFN_RING,))]),
        out_shape=jax.ShapeDtypeStruct((rows * SUB, LANES), F32),
        compiler_params=_cparams(("arbitrary",)),
        name="ffn",
    )(block_e, nb_used, xs2, wg, wu, wd)


def _combine_kernel(cnt_ref, seg_ref, gb_ref, lidx_ref, wts_ref, hf3_ref, x1_ref, gt2_ref, gpost_ref,
                    wsg_ref, wsu_ref, wsd_ref, ys_ref, y_ref, buf0, buf1, y3_ref, sems, *, tile0, ne):
    bb, ll, d = x1_ref.shape
    tt = bb * ll
    i = pl.program_id(0)
    n = pl.num_programs(0)

    def fetch(j, buf, sem):
        def seg_body(e, carry):
            c, s, g = cnt_ref[j * ne + e], seg_ref[j * ne + e], gb_ref[j * ne + e]
            _for_each_piece(c, lambda off, size: pltpu.make_async_copy(
                ys_ref.at[pl.ds(g + off, size)], buf.at[pl.ds(s + off, size)], sem).start(), SEG_BITS)
            return carry
        lax.fori_loop(0, ne, seg_body, 0)

    def run(buf, sem, nbuf, nsem):
        @pl.when(i == 0)
        def _():
            fetch(tile0, buf, sem)

        @pl.when(i + 1 < n)
        def _():
            fetch(tile0 + i + 1, nbuf, nsem)

        pltpu.make_async_copy(ys_ref.at[pl.ds(0, tt * TOP_K)], buf, sem).wait()

        def tok_body(tb, carry):
            for u in range(SORT_UNROLL):
                t = tb * SORT_UNROLL + u
                acc = wts_ref[t * TOP_K] * buf[lidx_ref[t * TOP_K]]
                for k in range(1, TOP_K):
                    acc = acc + wts_ref[t * TOP_K + k] * buf[lidx_ref[t * TOP_K + k]]
                y3_ref[pl.ds(pl.multiple_of(t * SUB, SUB), SUB), :] = acc
            return carry
        lax.fori_loop(0, tt // SORT_UNROLL, tok_body, 0)

    @pl.when(i % 2 == 0)
    def _():
        run(buf0, sems.at[0], buf1, sems.at[1])

    @pl.when(i % 2 == 1)
    def _():
        run(buf1, sems.at[1], buf0, sems.at[0])

    hf = _from_row_tiles(hf3_ref, tt)
    shared = _mm(_silu(_mm(hf, wsg_ref[...])) * _mm(hf, wsu_ref[...]), wsd_ref[...])
    yf = (_from_row_tiles(y3_ref, tt) + shared).reshape(bb, ll, d)
    y_ref[...] = x1_ref[...] + gt2_ref[...] * _rms(yf, gpost_ref[...])


def _combine(cnt, seg, gbase, lidx, wts, hf3, x1, gt2, gpost, wsg, wsu, wsd, ys, tile0, ne):
    b, l, d = x1.shape
    tt = ROUTE_TILE
    ll = min(l, tt)
    bb = tt // ll
    nl = l // ll
    nt = (b // bb) * nl
    fs = wsg.shape[1]
    if nl == 1:
        tmap = lambda i, *_: (i, 0, 0)
        bmap = lambda i, *_: (i, 0, 0)
    else:
        tmap = lambda i, *_: (i // nl, i % nl, 0)
        bmap = lambda i, *_: (i // nl, 0, 0)
    wmap = lambda i, *_: (0, 0)
    kspec = lambda: pl.BlockSpec((tt * TOP_K,), lambda i, *_: (i,), memory_space=pltpu.SMEM)
    return pl.pallas_call(
        functools.partial(_combine_kernel, tile0=tile0, ne=ne),
        grid_spec=pltpu.PrefetchScalarGridSpec(
            num_scalar_prefetch=3, grid=(nt,),
            in_specs=[kspec(), kspec(),
                      pl.BlockSpec((tt * SUB, LANES), lambda i, *_: (i, 0)),
                      pl.BlockSpec((bb, ll, d), tmap), pl.BlockSpec((bb, 1, d), bmap),
                      pl.BlockSpec((1, d), wmap), pl.BlockSpec((d, fs), wmap), pl.BlockSpec((d, fs), wmap),
                      pl.BlockSpec((fs, d), wmap),
                      pl.BlockSpec(memory_space=pl.ANY)],
            out_specs=pl.BlockSpec((bb, ll, d), tmap),
            scratch_shapes=[pltpu.VMEM((tt * TOP_K, SUB, LANES), F32), pltpu.VMEM((tt * TOP_K, SUB, LANES), F32),
                            pltpu.VMEM((tt * SUB, LANES), F32), pltpu.SemaphoreType.DMA((2,))]),
        out_shape=jax.ShapeDtypeStruct((b, l, d), F32),
        compiler_params=_cparams(("arbitrary",), VMEM_LIMIT_COMBINE),
        name="combine",
    )(cnt, seg, gbase, lidx, wts, hf3, x1, gt2, gpost, wsg, wsu, wsd, ys)


def _rope_tables(pos, rows):
    half = DH // 2
    freqs = ROPE_BASE ** (-jnp.arange(half, dtype=F32) / half)
    ang = pos.astype(F32)[:, None] * freqs[None, :]
    cos, sin = jnp.cos(ang), jnp.sin(ang)
    cos2 = jnp.concatenate([cos, cos], axis=1)
    sin2 = jnp.concatenate([-sin, sin], axis=1)
    padr = rows - pos.shape[0]
    return jnp.pad(cos2, ((0, padr), (0, 0))), jnp.pad(sin2, ((0, padr), (0, 0)))


def _global_plan(cnt_tiles, n_assign, ne):
    counts = jnp.sum(cnt_tiles, axis=0)
    padded = (counts + MOE_TILE - 1) // MOE_TILE * MOE_TILE
    pend = jnp.cumsum(padded)
    base = pend - padded
    gbase = base[None, :] + jnp.cumsum(cnt_tiles, axis=0) - cnt_tiles
    nb = n_assign // MOE_TILE + ne
    starts = jnp.arange(nb, dtype=jnp.int32) * MOE_TILE
    block_e = jnp.minimum(jnp.sum((pend[None, :] <= starts[:, None]).astype(jnp.int32), axis=1), ne - 1)
    nb_used = pend[-1:] // MOE_TILE
    tail = jnp.concatenate([pend[-1:], nb - nb_used])
    i32 = lambda v: v.astype(jnp.int32)
    return i32(gbase.reshape(-1)), i32(block_e), i32(nb_used), i32(base + counts), i32(padded - counts), i32(tail), nb


def kernel(x_prompt, x_sample, c_prompt, c_sample, state_mlstm_C, state_mlstm_n, state_mlstm_m, state_mlstm_conv, state_ret_S, w_ada, b_ada, g_mix_pre, g_mix_post, g_ffn_pre, g_ffn_post, w_in, b_gates, conv_w, conv_b, m_gn, r_gn, w_out, w_router, b_router, w_exp_gate, w_exp_up, w_exp_down, w_sh_gate, w_sh_up, w_sh_down):
    depth = w_ada.shape[0]
    bp, lp, d = x_prompt.shape
    bs, ls, _ = x_sample.shape
    mh, rh = state_mlstm_C.shape[2], state_ret_S.shape[2]
    mw, rw = mh * DH, rh * DH
    ne = w_router.shape[2]
    tp, ts = bp * lp, bs * ls
    ntp = tp // ROUTE_TILE

    cos_p, sin_p = _rope_tables(jnp.arange(lp, dtype=jnp.int32), lp)
    cos_s, sin_s = _rope_tables(PAST_LEN + jnp.arange(ls, dtype=jnp.int32), CHUNK if ls % CHUNK else ls)

    hp, hs = x_prompt, x_sample
    new_p, new_s = [], []
    for l in range(depth):
        wi = w_in[l]
        g0, nmain = 4 * mw, 4 * mw + 4 * rw
        w_in_r = jnp.zeros((d, nmain + GATE_LANES), MXU_DTYPE)
        w_in_r = w_in_r.at[:, :g0].set(wi[:, :g0].astype(MXU_DTYPE))
        w_in_r = w_in_r.at[:, g0:nmain].set(wi[:, g0 + 2 * mh:].astype(MXU_DTYPE))
        w_in_r = w_in_r.at[:, nmain:nmain + 2 * mh].set(wi[:, g0:g0 + 2 * mh].astype(MXU_DTYPE))
        bg = jnp.pad(b_gates[l], (0, GATE_LANES - 2 * mh)).reshape(1, GATE_LANES)
        wout = w_out[l].astype(MXU_DTYPE)
        wr = jnp.pad(w_router[l], ((0, 0), (0, GATE_LANES - ne)))
        wr_hi, wr_lo = _split_hi_lo(wr)
        wsg, wsu, wsd = (w_sh_gate[l].astype(MXU_DTYPE), w_sh_up[l].astype(MXU_DTYPE), w_sh_down[l].astype(MXU_DTYPE))
        row = lambda v: v.reshape(1, -1)

        mod = _ada(jnp.concatenate([c_prompt, c_sample], axis=0), w_ada[l], b_ada[l])
        mods_p = [m[:bp].reshape(bp, 1, d) for m in jnp.split(mod, 6, axis=1)]
        mods_s = [m[bp:].reshape(bs, 1, d) for m in jnp.split(mod, 6, axis=1)]

        def lane_m(m):
            return jnp.pad(m, ((0, 0), (mh, GATE_LANES - 2 * mh)))[:, None, :]

        init = (jnp.zeros((bp, mh, DH, DH), F32), jnp.zeros((bp, mh, DH), F32), lane_m(jnp.zeros((bp, mh), F32)),
                jnp.zeros((bp, CONV_W - 1, 2 * mw), F32), jnp.zeros((bp, rh, DH, DH), F32))
        past = (state_mlstm_C[l], state_mlstm_n[l], lane_m(state_mlstm_m[l]), state_mlstm_conv[l], state_ret_S[l])

        def mix(x, mods, state, cos2, sin2):
            sh1, sc1, gt1, sh2, sc2, gt2 = mods
            b, ll, _ = x.shape
            if ll % MIX_ROWS == 0:
                x1, hf3, lg, c_n, n_n, m_n, conv_n, s_n = _mixer(
                    x, mods, cos2, sin2, *state, row(g_mix_pre[l]), w_in_r, conv_w[l], row(conv_b[l]), bg,
                    row(m_gn[l]), row(r_gn[l]), row(g_mix_post[l]), row(g_ffn_pre[l]), wout, wr_hi, wr_lo)
            else:
                z = _inproj(x, sc1, sh1, row(g_mix_pre[l]), w_in_r).reshape(b, ll, -1)
                o_mix, c_n, n_n, m_n, conv_n, s_n = _mixcore_packed(
                    z, cos2, sin2, *state, conv_w[l], row(conv_b[l]), bg, row(m_gn[l]), row(r_gn[l]))
                x1, hf3, lg = _mixout(o_mix, x, gt1, sc2, sh2, row(g_mix_post[l]), row(g_ffn_pre[l]), wout, wr_hi,
                                      wr_lo)
            lidx, wts, cnt, seg = _router(lg, b_router[l], ne)
            flat = lambda v: v.T.reshape(-1)
            return (x1, hf3, flat(lidx), flat(wts), cnt[::SUB, :ne], seg[::SUB, :ne],
                    (c_n, n_n, m_n[:, 0, mh:2 * mh], conv_n, s_n))

        x1_p, hf3_p, lidx_p, wts_p, cnt_p, seg_p, st_p = mix(hp, mods_p, init, cos_p, sin_p)
        x1_s, hf3_s, lidx_s, wts_s, cnt_s, seg_s, st_s = mix(hs, mods_s, past, cos_s, sin_s)

        cnt_t = jnp.concatenate([cnt_p, cnt_s], axis=0)
        cnt = cnt_t.reshape(-1)
        seg = jnp.concatenate([seg_p, seg_s], axis=0).reshape(-1)
        gbase, block_e, nb_used, pad_start, pad_n, tail, nb = _global_plan(cnt_t, (tp + ts) * TOP_K, ne)
        xs = _dispatch(cnt, seg, gbase, pad_start, pad_n, tail, lidx_p, lidx_s, hf3_p, hf3_s, nb * MOE_TILE, ne)
        ys = _ffn(block_e, nb_used, xs.reshape(nb * MOE_TILE * SUB, LANES), w_exp_gate[l], w_exp_up[l], w_exp_down[l])
        ys = ys.reshape(nb * MOE_TILE, SUB, LANES)
        hp = _combine(cnt, seg, gbase, lidx_p, wts_p, hf3_p, x1_p, mods_p[5], row(g_ffn_post[l]), wsg, wsu, wsd, ys, 0, ne)
        hs = _combine(cnt, seg, gbase, lidx_s, wts_s, hf3_s, x1_s, mods_s[5], row(g_ffn_post[l]), wsg, wsu, wsd, ys, ntp, ne)
        new_p.append(st_p)
        new_s.append(st_s)

    p_state = [jnp.stack([st[i] for st in new_p]) for i in range(5)]
    s_state = [jnp.stack([st[i] for st in new_s]) for i in range(5)]
    return (hp, hs, *p_state, *s_state)
```

```python
import functools
import math

import jax
import jax.numpy as jnp
from jax import lax
from jax.experimental import pallas as pl
from jax.experimental.pallas import tpu as pltpu

F32 = jnp.float32
MXU_DTYPE = jnp.bfloat16

NORM_EPS = 1e-6
ROPE_BASE = 10000.0
PAST_LEN = 16384
CHUNK = 128
CONV_W = 4
DH = 128
TOP_K = 8
N_GROUPS = 8
TOPK_GROUPS = 4
ROUTED_SCALE = 2.5
GATE_LANES = 128

V7X_VMEM_BYTES = 64 * 1024 * 1024
VMEM_LIMIT = 48 * 1024 * 1024
VMEM_LIMIT_COMBINE = 56 * 1024 * 1024
assert VMEM_LIMIT < VMEM_LIMIT_COMBINE < V7X_VMEM_BYTES

SUB, LANES = 8, 128

TOK_TILE = 256
ROUTE_TILE = 512
MOE_TILE = 512


def _cparams(sem, vmem_limit=VMEM_LIMIT):
    return pltpu.CompilerParams(dimension_semantics=sem, vmem_limit_bytes=vmem_limit)


def _silu(x):
    return x * jax.nn.sigmoid(x)


def _rms(x, g):
    return x * lax.rsqrt(jnp.mean(x * x, axis=-1, keepdims=True) + NORM_EPS) * g


def _mm(a, b):
    return jnp.dot(a.astype(MXU_DTYPE), b.astype(MXU_DTYPE), preferred_element_type=F32)


def _mm_nt(a, b):
    return lax.dot_general(a.astype(MXU_DTYPE), b.astype(MXU_DTYPE), (((1,), (1,)), ((), ())),
                           preferred_element_type=F32)


def _mm_tn(a, b):
    return lax.dot_general(a.astype(MXU_DTYPE), b.astype(MXU_DTYPE), (((0,), (0,)), ((), ())),
                           preferred_element_type=F32)


def _ada_kernel(c_ref, w_ref, b_ref, o_ref):
    o_ref[...] = _mm(_silu(c_ref[...]), w_ref[...]) + b_ref[...]


def _ada(c, w_ada, b_ada):
    n, d = c.shape
    dout = w_ada.shape[1]
    tn = 1024
    return pl.pallas_call(
        _ada_kernel,
        grid=(dout // tn,),
        in_specs=[pl.BlockSpec((n, d), lambda j: (0, 0)),
                  pl.BlockSpec((d, tn), lambda j: (0, j)),
                  pl.BlockSpec((1, tn), lambda j: (0, j))],
        out_specs=pl.BlockSpec((n, tn), lambda j: (0, j)),
        out_shape=jax.ShapeDtypeStruct((n, dout), F32),
        compiler_params=_cparams(("arbitrary",)),
        name="ada",
    )(c, w_ada, b_ada.reshape(1, dout))


def _inproj_kernel(x_ref, sc_ref, sh_ref, g_ref, w_ref, z_ref):
    x = x_ref[...]
    h = _rms(x, g_ref[...]) * (1.0 + sc_ref[...]) + sh_ref[...]
    bb, ll, d = x.shape
    z_ref[...] = _mm(h.reshape(bb * ll, d), w_ref[...])


def _inproj(x, sc, sh, g, w):
    b, l, d = x.shape
    n = w.shape[1]
    ll = min(l, TOK_TILE)
    bb = TOK_TILE // ll
    mod_spec = pl.BlockSpec((bb, 1, d), lambda i, j: (i, 0, 0))
    return pl.pallas_call(
        _inproj_kernel,
        grid=(b // bb, l // ll),
        in_specs=[pl.BlockSpec((bb, ll, d), lambda i, j: (i, j, 0)), mod_spec, mod_spec,
                  pl.BlockSpec((1, d), lambda i, j: (0, 0)),
                  pl.BlockSpec((d, n), lambda i, j: (0, 0))],
        out_specs=pl.BlockSpec((bb * ll, n), lambda i, j: (i * (l // ll) + j, 0)),
        out_shape=jax.ShapeDtypeStruct((b * l, n), F32),
        compiler_params=_cparams(("arbitrary", "arbitrary")),
        name="inproj",
    )(x, sc, sh, g, w)


def _cumsum_rows(x):
    n = x.shape[0]
    row = lax.broadcasted_iota(jnp.int32, x.shape, 0)
    s = 1
    while s < n:
        x = x + jnp.where(row >= s, pltpu.roll(x, s, axis=0), 0.0)
        s *= 2
    return x


def _head_norm(y, g):
    mu = jnp.mean(y, axis=-1, keepdims=True)
    yc = y - mu
    var = jnp.mean(yc * yc, axis=-1, keepdims=True)
    return yc * lax.rsqrt(var + NORM_EPS) * g


def _log_sigmoid(x):
    return jnp.minimum(x, 0.0) - jnp.log1p(jnp.exp(-jnp.abs(x)))


def _chunk_step(rows, put, cos2, sin2, c_ref, n_ref, m_ref, conv_ref, s_ref, xcat_ref,
                convw_ref, convb_ref, bg_ref, mgn_ref, rgn_ref, mh, rh, after_head=None):
    cl = CHUNK
    mw = mh * DH
    rw = rh * DH
    o_qk, o_mv, o_mo = 0, 2 * mw, 3 * mw
    o_rq = 4 * mw
    o_rk, o_rv, o_rg, o_gt = o_rq + rw, o_rq + 2 * rw, o_rq + 3 * rw, o_rq + 4 * rw

    xcat_ref[8:8 + cl, :] = rows(o_qk, o_qk + 2 * mw)
    qk = convb_ref[...]
    for t in range(CONV_W):
        qk = qk + xcat_ref[8 - (CONV_W - 1) + t:8 - (CONV_W - 1) + t + cl, :] * convw_ref[t:t + 1, :]
    qk = _silu(qk)
    new_conv = xcat_ref[8 + cl - (CONV_W - 1):8 + cl, :]
    conv_ref[0] = new_conv
    xcat_ref[8 - (CONV_W - 1):8, :] = new_conv

    row_c = lax.broadcasted_iota(jnp.int32, (cl, 1), 0)
    ii = lax.broadcasted_iota(jnp.int32, (cl, cl), 0)
    jj = lax.broadcasted_iota(jnp.int32, (cl, cl), 1)
    causal = jj <= ii
    eye = jj == ii

    def to_row(col):
        return jnp.sum(jnp.where(eye, col, 0.0), axis=0, keepdims=True)

    g = rows(o_gt, o_gt + GATE_LANES) + bg_ref[...]
    bcum = _cumsum_rows(_log_sigmoid(g))
    m_prev = m_ref[0]
    a_all = bcum + m_prev
    lane = lax.broadcasted_iota(jnp.int32, (1, GATE_LANES), 1)
    m_new_row = m_prev

    for h in range(mh):
        q = qk[:, h * DH:(h + 1) * DH]
        k = qk[:, mw + h * DH:mw + (h + 1) * DH] * (DH ** -0.5)
        v = rows(o_mv + h * DH, o_mv + (h + 1) * DH)
        b_col = bcum[:, mh + h:mh + h + 1]
        logi_col = g[:, h:h + 1]
        a_col = a_all[:, mh + h:mh + h + 1]
        dmat = jnp.where(causal, b_col + to_row(logi_col - b_col), -jnp.inf)
        m_t = jnp.maximum(a_col, jnp.max(dmat, axis=1, keepdims=True))
        w_intra = jnp.exp(dmat - m_t)
        w_inter = jnp.exp(a_col - m_t)
        c_old = c_ref[0, h]
        n_old = n_ref[0, h:h + 1, :]
        s = _mm_nt(q, k) * w_intra
        num = _mm(s, v) + w_inter * _mm(q, c_old)
        den = jnp.sum(s, axis=1, keepdims=True) + w_inter * jnp.sum(q * n_old, axis=1, keepdims=True)
        hh = num * (1.0 / jnp.maximum(jnp.abs(den), jnp.exp(-m_t)))
        m_new = m_t[cl - 1:cl, :]
        b_last = b_col[cl - 1:cl, :]
        w_old = jnp.exp(b_last + m_prev[:, mh + h:mh + h + 1] - m_new)
        w_k = jnp.exp(b_last - b_col + logi_col - m_new)
        c_ref[0, h] = w_old * c_old + _mm_tn(k, w_k * v)
        n_ref[0, h:h + 1, :] = w_old * n_old + jnp.sum(w_k * k, axis=0, keepdims=True)
        m_new_row = jnp.where(lane == mh + h, m_new, m_new_row)
        om = _head_norm(hh, mgn_ref[:, h * DH:(h + 1) * DH]) * jax.nn.sigmoid(rows(o_mo + h * DH, o_mo + (h + 1) * DH))
        put(h * DH, (h + 1) * DH, om)
        if after_head is not None:
            after_head(h)
    m_ref[0] = m_new_row

    rel = (ii - jj).astype(F32)
    row_f = row_c.astype(F32)

    def rot(x):
        return x * cos2 + pltpu.roll(x, DH // 2, axis=1) * sin2

    for h in range(rh):
        log_g = math.log1p(-2.0 ** (-5.0 - h))
        q = rot(rows(o_rq + h * DH, o_rq + (h + 1) * DH))
        k = rot(rows(o_rk + h * DH, o_rk + (h + 1) * DH)) * (DH ** -0.5)
        v = rows(o_rv + h * DH, o_rv + (h + 1) * DH)
        decay = jnp.where(causal, jnp.exp(jnp.maximum(rel, 0.0) * log_g), 0.0)
        s_old = s_ref[0, h]
        y = _mm(_mm_nt(q, k) * decay, v) + jnp.exp((row_f + 1.0) * log_g) * _mm(q, s_old)
        k_w = jnp.exp((cl - 1.0 - row_f) * log_g)
        s_ref[0, h] = math.exp(cl * log_g) * s_old + _mm_tn(k, k_w * v)
        orr = _head_norm(y, rgn_ref[:, h * DH:(h + 1) * DH]) * _silu(rows(o_rg + h * DH, o_rg + (h + 1) * DH))
        put(mw + h * DH, mw + (h + 1) * DH, orr)
        if after_head is not None:
            after_head(mh + h)


MIX_ROWS = 256


def _mixer_kernel(x_ref, xn_ref, sc1_ref, sh1_ref, sc1n_ref, sh1n_ref, gt1_ref, sc2_ref, sh2_ref, cos_ref, sin_ref,
                  c0_ref, n0_ref, m0_ref, conv0_ref, s0_ref,
                  gpre1_ref, win_ref, convw_ref, convb_ref, bg_ref, mgn_ref, rgn_ref,
                  gpost_ref, gpre2_ref, wout_ref, wr_hi_ref, wr_lo_ref,
                  x1_ref, hf3_ref, lg_ref, c_ref, n_ref, m_ref, conv_ref, s_ref,
                  z_ref, o_ref, xcat_ref, *, mh, rh):
    i, j = pl.program_id(0), pl.program_id(1)
    step = i * pl.num_programs(1) + j
    cur = step % 2
    rs, d = x_ref.shape[1], x_ref.shape[2]
    nz = z_ref.shape[2]
    nchunks = rs // CHUNK
    npieces = nchunks * (mh + rh)
    cuts = [(nz // LANES * p // npieces) * LANES for p in range(npieces + 1)]

    def project(x3, sc_ref, sh_ref):
        return (_rms(x3, gpre1_ref[...]) * (1.0 + sc_ref[...]) + sh_ref[...]).reshape(rs, d)

    @pl.when(step == 0)
    def _():
        z_ref[0] = _mm(project(x_ref[...], sc1_ref, sh1_ref), win_ref[...])

    @pl.when(j == 0)
    def _():
        c_ref[...] = c0_ref[...]
        n_ref[...] = n0_ref[...]
        m_ref[...] = m0_ref[...]
        s_ref[...] = s0_ref[...]
        xcat_ref[8 - (CONV_W - 1):8, :] = conv0_ref[0]

    h_next = project(xn_ref[...], sc1n_ref, sh1n_ref).astype(MXU_DTYPE)
    for c in range(nchunks):
        r0 = c * CHUNK

        def project_piece(hd, c=c):
            lo, hi = cuts[c * (mh + rh) + hd], cuts[c * (mh + rh) + hd + 1]
            z_ref[1 - cur, :, lo:hi] = jnp.dot(h_next, win_ref[:, lo:hi], preferred_element_type=F32)

        def put(lo, hi, v, r0=r0):
            o_ref[r0:r0 + CHUNK, lo:hi] = v

        _chunk_step(lambda lo, hi, r0=r0: z_ref[cur, r0:r0 + CHUNK, lo:hi], put,
                    cos_ref[r0:r0 + CHUNK, :], sin_ref[r0:r0 + CHUNK, :],
                    c_ref, n_ref, m_ref, conv_ref, s_ref, xcat_ref,
                    convw_ref, convb_ref, bg_ref, mgn_ref, rgn_ref, mh, rh, after_head=project_piece)
    x = x_ref[...]
    _mixout_body(o_ref[...].reshape(1, rs, o_ref.shape[1]), x, gt1_ref, sc2_ref, sh2_ref, gpost_ref, gpre2_ref,
                 wout_ref, wr_hi_ref, wr_lo_ref, x1_ref, hf3_ref, lg_ref)


def _mixer(x, mods, cos2, sin2, c0, n0, m0, conv0, s0, gpre1, win, convw, convb, bg, mgn, rgn,
           gpost, gpre2, wout, wr_hi, wr_lo):
    sh1, sc1, gt1, sh2, sc2, gt2 = mods
    b, l, d = x.shape
    assert d == SUB * LANES and l % MIX_ROWS == 0
    mh, rh = c0.shape[1], s0.shape[1]
    mw, rw = mh * DH, rh * DH
    nz, ne = win.shape[1], wr_hi.shape[1]
    rs = MIX_ROWS
    assert nz % LANES == 0
    nj = l // rs
    bmap4 = lambda i, j: (i, 0, 0, 0)
    bmap3 = lambda i, j: (i, 0, 0)
    tmap = lambda i, j: (i, j, 0)
    fmap = lambda i, j: (i * nj + j, 0)
    once = lambda a: pl.BlockSpec(a.shape, lambda i, j: (0, 0), pipeline_mode=pl.Buffered(1))
    mod_spec = pl.BlockSpec((1, 1, d), bmap3)
    nxt = lambda i, j: jnp.minimum(i * nj + j + 1, b * nj - 1)
    x_next_spec = pl.BlockSpec((1, rs, d), lambda i, j: (nxt(i, j) // nj, nxt(i, j) % nj, 0))
    mod_next_spec = pl.BlockSpec((1, 1, d), lambda i, j: (nxt(i, j) // nj, 0, 0))
    rope_spec = pl.BlockSpec((rs, DH), lambda i, j: (j, 0))
    state_specs = [pl.BlockSpec((1, mh, DH, DH), bmap4), pl.BlockSpec((1, mh, DH), bmap3),
                   pl.BlockSpec((1, 1, GATE_LANES), bmap3), pl.BlockSpec((1, CONV_W - 1, 2 * mw), bmap3),
                   pl.BlockSpec((1, rh, DH, DH), bmap4)]
    state_shapes = [jax.ShapeDtypeStruct(c0.shape, F32), jax.ShapeDtypeStruct(n0.shape, F32),
                    jax.ShapeDtypeStruct(m0.shape, F32), jax.ShapeDtypeStruct(conv0.shape, F32),
                    jax.ShapeDtypeStruct(s0.shape, F32)]
    return pl.pallas_call(
        functools.partial(_mixer_kernel, mh=mh, rh=rh),
        grid=(b, nj),
        in_specs=[pl.BlockSpec((1, rs, d), tmap), x_next_spec, mod_spec, mod_spec, mod_next_spec, mod_next_spec]
                 + [mod_spec] * 3 + [rope_spec, rope_spec] + state_specs
                 + [once(a) for a in (gpre1, win, convw, convb, bg, mgn, rgn, gpost, gpre2, wout, wr_hi, wr_lo)],
        out_specs=[pl.BlockSpec((1, rs, d), tmap), pl.BlockSpec((rs * SUB, LANES), fmap),
                   pl.BlockSpec((rs, ne), fmap)] + state_specs,
        out_shape=[jax.ShapeDtypeStruct((b, l, d), F32), jax.ShapeDtypeStruct((b * l * SUB, LANES), F32),
                   jax.ShapeDtypeStruct((b * l, ne), F32)] + state_shapes,
        scratch_shapes=[pltpu.VMEM((2, rs, nz), F32), pltpu.VMEM((rs, mw + rw), F32),
                        pltpu.VMEM((CHUNK + 8, 2 * mw), F32)],
        compiler_params=_cparams(("arbitrary", "arbitrary")),
        name="mixer",
    )(x, x, sc1, sh1, sc1, sh1, gt1, sc2, sh2, cos2, sin2, c0, n0, m0, conv0, s0,
      gpre1, win, convw, convb, bg, mgn, rgn, gpost, gpre2, wout, wr_hi, wr_lo)


PACK_SEQS = CHUNK // SUB


def _mixcore_packed_kernel(z_ref, cos_ref, sin_ref, c0_ref, n0_ref, m0_ref, conv0_ref, s0_ref,
                           convw_ref, convb_ref, bg_ref, mgn_ref, rgn_ref,
                           out_ref, c_ref, n_ref, m_ref, conv_ref, s_ref, *, mh, rh):
    g_n, l_n, _ = z_ref.shape
    rr = g_n * l_n
    mw, rw = mh * DH, rh * DH
    o_qk, o_mv, o_mo = 0, 2 * mw, 3 * mw
    o_rq = 4 * mw
    o_rk, o_rv, o_rg, o_gt = o_rq + rw, o_rq + 2 * rw, o_rq + 3 * rw, o_rq + 4 * rw

    def rows(lo, hi):
        return z_ref[:, :, lo:hi].reshape(rr, hi - lo)

    def per_seq(x):
        return jnp.broadcast_to(x[:, None, :], (g_n, l_n, x.shape[-1])).reshape(rr, x.shape[-1])

    def last_of_seq(x):
        x3 = x.reshape(g_n, l_n, x.shape[-1])
        return jnp.broadcast_to(x3[:, l_n - 1:l_n, :], x3.shape).reshape(x.shape)

    def full(col):
        return jnp.broadcast_to(col, (rr, DH))

    pos_c = lax.broadcasted_iota(jnp.int32, (rr, 1), 0) % l_n
    ii = lax.broadcasted_iota(jnp.int32, (rr, rr), 0)
    jj = lax.broadcasted_iota(jnp.int32, (rr, rr), 1)
    mask = ((ii // l_n) == (jj // l_n)) & (jj <= ii)
    eye = jj == ii

    def to_row(col):
        return jnp.sum(jnp.where(eye, col, 0.0), axis=0, keepdims=True)

    x = rows(o_qk, o_qk + 2 * mw)
    e = conv0_ref[...].reshape(rr, 2 * mw)
    qk = convb_ref[...]
    for t in range(CONV_W):
        d = CONV_W - 1 - t
        xd = x if d == 0 else jnp.where(pos_c >= d, pltpu.roll(x, d, axis=0),
                                        pltpu.roll(e, (d - (CONV_W - 1)) % rr, axis=0))
        qk = qk + xd * convw_ref[t:t + 1, :]
    qk = _silu(qk)
    conv_ref[...] = x.reshape(g_n, l_n, 2 * mw)[:, l_n - (CONV_W - 1):, :]

    g = rows(o_gt, o_gt + GATE_LANES) + bg_ref[...]
    bcum = _log_sigmoid(g)
    s_ = 1
    while s_ < l_n:
        bcum = bcum + jnp.where(pos_c >= s_, pltpu.roll(bcum, s_, axis=0), 0.0)
        s_ *= 2
    m_prev = per_seq(m0_ref[:, 0, :])
    a_all = bcum + m_prev
    lane = lax.broadcasted_iota(jnp.int32, (1, GATE_LANES), 1)
    even_seq = (lax.broadcasted_iota(jnp.int32, (rr, 1), 0) // l_n) % 2 == 0
    m_new_rows = m_prev

    def state_dot(qb, st_ref, h):
        parts = []
        for s in range(g_n):
            pb, lo = s // 2, (s % 2) * l_n
            res = jnp.dot(qb[pb * 2 * l_n:(pb + 1) * 2 * l_n], st_ref[s, h].astype(MXU_DTYPE),
                          preferred_element_type=F32)
            parts.append(res[lo:lo + l_n])
        return jnp.concatenate(parts, axis=0)

    def state_update(k, wv, decay_full, st_in, st_out, h):
        k_even = jnp.where(even_seq, k, 0.0).astype(MXU_DTYPE)
        k_odd = jnp.where(even_seq, 0.0, k).astype(MXU_DTYPE)
        wvb = wv.astype(MXU_DTYPE)
        for s in range(g_n):
            pb = s // 2
            kp = (k_even if s % 2 == 0 else k_odd)[pb * 2 * l_n:(pb + 1) * 2 * l_n]
            upd = lax.dot_general(kp, wvb[pb * 2 * l_n:(pb + 1) * 2 * l_n], (((0,), (0,)), ((), ())),
                                  preferred_element_type=F32)
            st_out[s, h] = decay_full[s * l_n:s * l_n + 1, :] * st_in[s, h] + upd

    for h in range(mh):
        q = qk[:, h * DH:(h + 1) * DH]
        k = qk[:, mw + h * DH:mw + (h + 1) * DH] * (DH ** -0.5)
        v = rows(o_mv + h * DH, o_mv + (h + 1) * DH)
        b_col = bcum[:, mh + h:mh + h + 1]
        logi_col = g[:, h:h + 1]
        a_col = a_all[:, mh + h:mh + h + 1]
        dmat = jnp.where(mask, b_col + to_row(logi_col - b_col), -jnp.inf)
        m_t = jnp.maximum(a_col, jnp.max(dmat, axis=1, keepdims=True))
        w_intra = jnp.exp(dmat - m_t)
        w_inter = jnp.exp(a_col - m_t)
        n_rows = per_seq(n0_ref[:, h, :])
        s = _mm_nt(q, k) * w_intra
        num = _mm(s, v) + w_inter * state_dot(q.astype(MXU_DTYPE), c0_ref, h)
        den = jnp.sum(s, axis=1, keepdims=True) + w_inter * jnp.sum(q * n_rows, axis=1, keepdims=True)
        hh = num * (1.0 / jnp.maximum(jnp.abs(den), jnp.exp(-m_t)))
        m_new = last_of_seq(full(m_t))
        b_last = last_of_seq(full(b_col))
        w_old = jnp.exp(b_last + full(m_prev[:, mh + h:mh + h + 1]) - m_new)
        w_k = jnp.exp(b_last - full(b_col) + full(logi_col) - m_new)
        state_update(k, w_k * v, w_old, c0_ref, c_ref, h)
        n_ref[:, h, :] = (w_old.reshape(g_n, l_n, DH)[:, 0, :] * n0_ref[:, h, :]
                          + jnp.sum((w_k * k).reshape(g_n, l_n, DH), axis=1))
        m_new_rows = jnp.where(lane == mh + h, m_new, m_new_rows)
        om = _head_norm(hh, mgn_ref[:, h * DH:(h + 1) * DH]) * jax.nn.sigmoid(rows(o_mo + h * DH, o_mo + (h + 1) * DH))
        out_ref[:, :, h * DH:(h + 1) * DH] = om.reshape(g_n, l_n, DH)
    m_ref[...] = m_new_rows.reshape(g_n, l_n, GATE_LANES)[:, 0:1, :]

    cos2 = cos_ref[...]
    sin2 = sin_ref[...]
    rel = (ii - jj).astype(F32)
    pos_f = pos_c.astype(F32)

    def rot(x):
        return x * cos2 + pltpu.roll(x, DH // 2, axis=1) * sin2

    for h in range(rh):
        log_g = math.log1p(-2.0 ** (-5.0 - h))
        q = rot(rows(o_rq + h * DH, o_rq + (h + 1) * DH))
        k = rot(rows(o_rk + h * DH, o_rk + (h + 1) * DH)) * (DH ** -0.5)
        v = rows(o_rv + h * DH, o_rv + (h + 1) * DH)
        decay = jnp.where(mask, jnp.exp(jnp.maximum(rel, 0.0) * log_g), 0.0)
        y = (_mm(_mm_nt(q, k) * decay, v)
             + jnp.exp((pos_f + 1.0) * log_g) * state_dot(q.astype(MXU_DTYPE), s0_ref, h))
        k_w = jnp.exp((l_n - 1.0 - pos_f) * log_g)
        state_update(k, k_w * v, jnp.full((rr, DH), math.exp(l_n * log_g), F32), s0_ref, s_ref, h)
        orr = _head_norm(y, rgn_ref[:, h * DH:(h + 1) * DH]) * _silu(rows(o_rg + h * DH, o_rg + (h + 1) * DH))
        out_ref[:, :, mw + h * DH:mw + (h + 1) * DH] = orr.reshape(g_n, l_n, DH)


def _mixcore_packed(z, cos2, sin2, c0, n0, m0, conv0, s0, convw, convb, bg, mgn, rgn):
    b, l, nz = z.shape
    assert l == SUB and b % PACK_SEQS == 0
    mh, rh = c0.shape[1], s0.shape[1]
    mw, rw = mh * DH, rh * DH
    g_n = PACK_SEQS
    conv0p = jnp.pad(conv0, ((0, 0), (0, l - (CONV_W - 1)), (0, 0)))
    tile = lambda t: jnp.tile(t[:l], (g_n, 1))
    bmap4 = lambda i: (i, 0, 0, 0)
    bmap3 = lambda i: (i, 0, 0)
    wmap = lambda i: (0, 0)
    c_spec, n_spec = pl.BlockSpec((g_n, mh, DH, DH), bmap4), pl.BlockSpec((g_n, mh, DH), bmap3)
    m_spec, s_spec = pl.BlockSpec((g_n, 1, GATE_LANES), bmap3), pl.BlockSpec((g_n, rh, DH, DH), bmap4)
    return pl.pallas_call(
        functools.partial(_mixcore_packed_kernel, mh=mh, rh=rh),
        grid=(b // g_n,),
        in_specs=[pl.BlockSpec((g_n, l, nz), bmap3),
                  pl.BlockSpec((g_n * l, DH), wmap), pl.BlockSpec((g_n * l, DH), wmap),
                  c_spec, n_spec, m_spec, pl.BlockSpec((g_n, l, 2 * mw), bmap3), s_spec,
                  pl.BlockSpec(convw.shape, wmap), pl.BlockSpec(convb.shape, wmap), pl.BlockSpec(bg.shape, wmap),
                  pl.BlockSpec(mgn.shape, wmap), pl.BlockSpec(rgn.shape, wmap)],
        out_specs=[pl.BlockSpec((g_n, l, mw + rw), bmap3), c_spec, n_spec, m_spec,
                   pl.BlockSpec((g_n, CONV_W - 1, 2 * mw), bmap3), s_spec],
        out_shape=[jax.ShapeDtypeStruct((b, l, mw + rw), F32), jax.ShapeDtypeStruct(c0.shape, F32),
                   jax.ShapeDtypeStruct(n0.shape, F32), jax.ShapeDtypeStruct(m0.shape, F32),
                   jax.ShapeDtypeStruct(conv0.shape, F32), jax.ShapeDtypeStruct(s0.shape, F32)],
        compiler_params=_cparams(("arbitrary",)),
        name="mixcore_packed",
    )(z, tile(cos2), tile(sin2), c0, n0, m0, conv0p, s0, convw, convb, bg, mgn, rgn)


def _split_hi_lo(x):
    hi = x.astype(MXU_DTYPE)
    lo = (x - hi.astype(F32)).astype(MXU_DTYPE)
    return hi, lo


def _to_row_tiles(ref, x):
    n = x.shape[0]
    for j in range(SUB):
        ref[pl.ds(j, n, stride=SUB), :] = x[:, j * LANES:(j + 1) * LANES]


def _from_row_tiles(ref, n):
    return jnp.concatenate([ref[pl.ds(j, n, stride=SUB), :] for j in range(SUB)], axis=1)


def _mixout_kernel(o_ref, x_ref, gt1_ref, sc2_ref, sh2_ref, gpost_ref, gpre_ref, wout_ref, wr_hi_ref, wr_lo_ref,
                   x1_ref, hf3_ref, lg_ref):
    _mixout_body(o_ref[...], x_ref[...], gt1_ref, sc2_ref, sh2_ref, gpost_ref, gpre_ref, wout_ref, wr_hi_ref,
                 wr_lo_ref, x1_ref, hf3_ref, lg_ref)


def _mixout_body(o, x, gt1_ref, sc2_ref, sh2_ref, gpost_ref, gpre_ref, wout_ref, wr_hi_ref, wr_lo_ref,
                 x1_ref, hf3_ref, lg_ref):
    bb, ll, d = x.shape
    ym = _mm(o.reshape(bb * ll, o.shape[-1]), wout_ref[...]).reshape(bb, ll, d)
    x1 = x + gt1_ref[...] * _rms(ym, gpost_ref[...])
    x1_ref[...] = x1
    hf = (_rms(x1, gpre_ref[...]) * (1.0 + sc2_ref[...]) + sh2_ref[...]).reshape(bb * ll, d)
    _to_row_tiles(hf3_ref, hf)
    hi, lo = _split_hi_lo(hf)
    lg_ref[...] = _mm(hi, wr_hi_ref[...]) + (_mm(lo, wr_hi_ref[...]) + _mm(hi, wr_lo_ref[...]))


def _mixout(o, x, gt1, sc2, sh2, gpost, gpre, wout, wr_hi, wr_lo):
    b, l, d = x.shape
    assert d == SUB * LANES
    w = o.shape[-1]
    ne = wr_hi.shape[1]
    ll = min(l, TOK_TILE)
    bb = TOK_TILE // ll
    tmap = lambda i, j: (i, j, 0)
    fmap = lambda i, j: (i * (l // ll) + j, 0)
    mod_spec = pl.BlockSpec((bb, 1, d), lambda i, j: (i, 0, 0))
    wmap = lambda i, j: (0, 0)
    return pl.pallas_call(
        _mixout_kernel,
        grid=(b // bb, l // ll),
        in_specs=[pl.BlockSpec((bb, ll, w), tmap), pl.BlockSpec((bb, ll, d), tmap), mod_spec, mod_spec, mod_spec,
                  pl.BlockSpec((1, d), wmap), pl.BlockSpec((1, d), wmap), pl.BlockSpec((w, d), wmap),
                  pl.BlockSpec((d, ne), wmap), pl.BlockSpec((d, ne), wmap)],
        out_specs=[pl.BlockSpec((bb, ll, d), tmap), pl.BlockSpec((bb * ll * SUB, LANES), fmap),
                   pl.BlockSpec((bb * ll, ne), fmap)],
        out_shape=[jax.ShapeDtypeStruct((b, l, d), F32), jax.ShapeDtypeStruct((b * l * SUB, LANES), F32),
                   jax.ShapeDtypeStruct((b * l, ne), F32)],
        compiler_params=_cparams(("arbitrary", "arbitrary")),
        name="mixout",
    )(o, x, gt1, sc2, sh2, gpost, gpre, wout, wr_hi, wr_lo)


def _router_kernel(lg_ref, br_ref, lidx_ref, wts_ref, cnt_ref, seg_ref, *, ne):
    tm = lg_ref.shape[0]
    gsz = ne // N_GROUPS
    s = jax.nn.sigmoid(lg_ref[...].T[:ne, :])
    sb = s + br_ref[...]
    sb3 = sb.reshape(N_GROUPS, gsz, tm)
    e3 = lax.broadcasted_iota(jnp.int32, sb3.shape, 1)
    m1 = jnp.max(sb3, axis=1, keepdims=True)
    first = jnp.min(jnp.where(sb3 == m1, e3, gsz), axis=1, keepdims=True)
    m2 = jnp.max(jnp.where(e3 == first, -jnp.inf, sb3), axis=1, keepdims=True)
    gs = (m1 + m2).reshape(N_GROUPS, tm)
    gi = lax.broadcasted_iota(jnp.int32, gs.shape, 0)
    grank = jnp.zeros(gs.shape, F32)
    for g in range(N_GROUPS):
        r = gs[g:g + 1, :]
        grank = grank + jnp.where(r > gs, 1.0, jnp.where(r == gs, jnp.where(g < gi, 1.0, 0.0), 0.0))
    gsel = jnp.where(grank < TOPK_GROUPS, 1.0, 0.0)
    emask = jnp.broadcast_to(gsel.reshape(N_GROUPS, 1, tm), sb3.shape).reshape(ne, tm) > 0.0
    masked = jnp.where(emask, sb, -jnp.inf)
    ei = lax.broadcasted_iota(jnp.int32, masked.shape, 0)
    cur = masked
    sel_f = jnp.zeros(masked.shape, F32)
    for _ in range(TOP_K):
        best = jnp.max(cur, axis=0, keepdims=True)
        first_e = jnp.min(jnp.where(cur == best, ei, ne), axis=0, keepdims=True)
        pick = ei == first_e
        sel_f = jnp.where(pick, 1.0, sel_f)
        cur = jnp.where(pick, -jnp.inf, cur)
    sel = sel_f > 0.0
    w = jnp.where(sel, s, 0.0)
    w = w / jnp.sum(w, axis=0, keepdims=True) * ROUTED_SCALE

    self = jnp.where(sel, 1.0, 0.0)
    er = lax.broadcasted_iota(jnp.int32, (ne, ne), 0)
    ec = lax.broadcasted_iota(jnp.int32, (ne, ne), 1)
    slot = _mm(jnp.where(ec < er, 1.0, 0.0), self)
    tr = lax.broadcasted_iota(jnp.int32, (tm, tm), 0)
    tc = lax.broadcasted_iota(jnp.int32, (tm, tm), 1)
    pos = _mm(self, jnp.where(tr < tc, 1.0, 0.0))
    seg = jnp.sum(slot, axis=1, keepdims=True)
    lrow = seg + pos
    lidx_rows, w_rows = [], []
    for k in range(TOP_K):
        pick = sel & (slot == float(k))
        lidx_rows.append(jnp.sum(jnp.where(pick, lrow, 0.0), axis=0, keepdims=True))
        w_rows.append(jnp.sum(jnp.where(pick, w, 0.0), axis=0, keepdims=True))
    lidx_ref[...] = jnp.concatenate(lidx_rows, axis=0).astype(jnp.int32)
    wts_ref[...] = jnp.concatenate(w_rows, axis=0)
    ones = jnp.ones((SUB, tm), F32)
    padz = jnp.zeros((LANES - ne, tm), F32)
    cnt_ref[...] = _mm_nt(ones, jnp.concatenate([self, padz], axis=0)).astype(jnp.int32)
    seg_ref[...] = _mm_nt(ones, jnp.concatenate([slot, padz], axis=0)).astype(jnp.int32)


def _router(logits, b_router, ne):
    t, lanes = logits.shape
    tm = ROUTE_TILE
    nt = t // tm
    kspec = pl.BlockSpec((TOP_K, tm), lambda i: (0, i))
    cspec = pl.BlockSpec((SUB, LANES), lambda i: (i, 0))
    return pl.pallas_call(
        functools.partial(_router_kernel, ne=ne),
        grid=(nt,),
        in_specs=[pl.BlockSpec((tm, lanes), lambda i: (i, 0)), pl.BlockSpec((ne, 1), lambda i: (0, 0))],
        out_specs=[kspec, kspec, cspec, cspec],
        out_shape=[jax.ShapeDtypeStruct((TOP_K, t), jnp.int32), jax.ShapeDtypeStruct((TOP_K, t), F32),
                   jax.ShapeDtypeStruct((nt * SUB, LANES), jnp.int32),
                   jax.ShapeDtypeStruct((nt * SUB, LANES), jnp.int32)],
        compiler_params=_cparams(("arbitrary",)),
        name="router",
    )(logits, b_router.reshape(ne, 1))


SEG_BITS = ROUTE_TILE.bit_length()
PAD_BITS = (MOE_TILE - 1).bit_length()


RARE_BIT = 7


def _for_each_piece(n, fn, bits):
    def pieces(lo, hi):
        for bit in range(hi - 1, lo - 1, -1):
            @pl.when(((n >> bit) & 1) == 1)
            def _():
                fn((n >> (bit + 1)) << (bit + 1), 1 << bit)

    if bits > RARE_BIT:
        @pl.when(n >= (1 << RARE_BIT))
        def _():
            pieces(RARE_BIT, bits)
    pieces(0, min(bits, RARE_BIT))


SORT_UNROLL = 4


def _zero_rows(ps_ref, pn_ref, tail_ref, xs_ref, z_ref, sem, ne, act):
    def pad_body(e, carry):
        _for_each_piece(pn_ref[e], lambda off, size: act(
            pltpu.make_async_copy(z_ref.at[pl.ds(0, size)], xs_ref.at[pl.ds(ps_ref[e] + off, size)], sem)),
            PAD_BITS)
        return carry
    lax.fori_loop(0, ne, pad_body, 0)

    def tail_body(b, carry):
        act(pltpu.make_async_copy(z_ref, xs_ref.at[pl.ds(tail_ref[0] + b * MOE_TILE, MOE_TILE)], sem))
        return carry
    lax.fori_loop(0, tail_ref[1], tail_body, 0)


def _dispatch_kernel(cnt_ref, seg_ref, gb_ref, ps_ref, pn_ref, tail_ref, lidx_p_ref, lidx_s_ref, hf3_p_ref, hf3_s_ref,
                     xs_ref, buf0, buf1, z_ref, sems, *, ntp, ne):
    i = pl.program_id(0)
    n = pl.num_programs(0)
    tt = lidx_p_ref.shape[0] // TOP_K

    def whole(buf, sem):
        return pltpu.make_async_copy(buf, xs_ref.at[pl.ds(0, tt * TOP_K)], sem)

    @pl.when(i == 0)
    def _():
        z_ref[...] = jnp.zeros(z_ref.shape, F32)
        _zero_rows(ps_ref, pn_ref, tail_ref, xs_ref, z_ref, sems.at[2], ne, lambda c: c.start())

    def run(buf, sem):
        @pl.when(i >= 2)
        def _():
            whole(buf, sem).wait()

        def sort_from(hf3_ref, lidx_ref):
            def sort_body(tb, carry):
                for u in range(SORT_UNROLL):
                    t = tb * SORT_UNROLL + u
                    v = hf3_ref[pl.ds(pl.multiple_of(t * SUB, SUB), SUB), :]
                    for k in range(TOP_K):
                        buf[lidx_ref[t * TOP_K + k]] = v
                return carry
            lax.fori_loop(0, tt // SORT_UNROLL, sort_body, 0)

        @pl.when(i < ntp)
        def _():
            sort_from(hf3_p_ref, lidx_p_ref)

        @pl.when(i >= ntp)
        def _():
            sort_from(hf3_s_ref, lidx_s_ref)

        def seg_body(e, carry):
            c, s, g = cnt_ref[i * ne + e], seg_ref[i * ne + e], gb_ref[i * ne + e]
            _for_each_piece(c, lambda off, size: pltpu.make_async_copy(
                buf.at[pl.ds(s + off, size)], xs_ref.at[pl.ds(g + off, size)], sem).start(), SEG_BITS)
            return carry
        lax.fori_loop(0, ne, seg_body, 0)

    @pl.when(i % 2 == 0)
    def _():
        run(buf0, sems.at[0])

    @pl.when(i % 2 == 1)
    def _():
        run(buf1, sems.at[1])

    @pl.when(i == n - 1)
    def _():
        _zero_rows(ps_ref, pn_ref, tail_ref, xs_ref, z_ref, sems.at[2], ne, lambda c: c.wait())

        @pl.when(i % 2 == 0)
        def _():
            whole(buf0, sems.at[0]).wait()

            @pl.when(i >= 1)
            def _():
                whole(buf1, sems.at[1]).wait()

        @pl.when(i % 2 == 1)
        def _():
            whole(buf1, sems.at[1]).wait()
            whole(buf0, sems.at[0]).wait()


def _dispatch(cnt, seg, gbase, pad_start, pad_n, tail, lidx_p, lidx_s, hf3_p, hf3_s, rows, ne):
    tt = ROUTE_TILE
    ntp, nts = lidx_p.shape[0] // (tt * TOP_K), lidx_s.shape[0] // (tt * TOP_K)
    pmap = lambda i, *_: jnp.minimum(i, ntp - 1)
    smap = lambda i, *_: jnp.maximum(i - ntp, 0)
    return pl.pallas_call(
        functools.partial(_dispatch_kernel, ntp=ntp, ne=ne),
        grid_spec=pltpu.PrefetchScalarGridSpec(
            num_scalar_prefetch=6, grid=(ntp + nts,),
            in_specs=[pl.BlockSpec((tt * TOP_K,), lambda i, *_: (pmap(i),), memory_space=pltpu.SMEM),
                      pl.BlockSpec((tt * TOP_K,), lambda i, *_: (smap(i),), memory_space=pltpu.SMEM),
                      pl.BlockSpec((tt * SUB, LANES), lambda i, *_: (pmap(i), 0)),
                      pl.BlockSpec((tt * SUB, LANES), lambda i, *_: (smap(i), 0))],
            out_specs=pl.BlockSpec(memory_space=pl.ANY),
            scratch_shapes=[pltpu.VMEM((tt * TOP_K, SUB, LANES), F32), pltpu.VMEM((tt * TOP_K, SUB, LANES), F32),
                            pltpu.VMEM((MOE_TILE, SUB, LANES), F32), pltpu.SemaphoreType.DMA((3,))]),
        out_shape=jax.ShapeDtypeStruct((rows, SUB, LANES), F32),
        compiler_params=_cparams(("arbitrary",)),
        name="dispatch",
    )(cnt, seg, gbase, pad_start, pad_n, tail, lidx_p, lidx_s, hf3_p, hf3_s)


FFN_RING = 3


def _ffn_kernel(be_ref, nbu_ref, xs_ref, wg_ref, wu_ref, wd_ref, ys_ref, xbuf, wg_s, wu_s, wd_s, sems):
    b = pl.program_id(0)
    nbu = nbu_ref[0]
    tm = ys_ref.shape[0] // SUB
    blk = tm * SUB

    def fetch(k):
        slot = k % FFN_RING
        start = k * blk if isinstance(k, int) else pl.multiple_of(k * blk, blk)
        return pltpu.make_async_copy(xs_ref.at[pl.ds(start, blk)], xbuf.at[slot], sems.at[slot])

    @pl.when(b == 0)
    def _():
        for k in range(FFN_RING - 1):
            @pl.when(k < nbu)
            def _():
                fetch(k).start()

    @pl.when(b + FFN_RING - 1 < nbu)
    def _():
        fetch(b + FFN_RING - 1).start()

    @pl.when(b < nbu)
    def _():
        @pl.when((b == 0) | (be_ref[b] != be_ref[jnp.maximum(b - 1, 0)]))
        def _():
            wg_s[...] = wg_ref[0].astype(MXU_DTYPE)
            wu_s[...] = wu_ref[0].astype(MXU_DTYPE)
            wd_s[...] = wd_ref[0].astype(MXU_DTYPE)

        fetch(b).wait()
        x = _from_row_tiles(xbuf.at[b % FFN_RING], tm)
        hb = _silu(_mm(x, wg_s[...])) * _mm(x, wu_s[...])
        _to_row_tiles(ys_ref, _mm(hb, wd_s[...]))

    @pl.when(b >= nbu_ref[0])
    def _():
        ys_ref[...] = jnp.zeros(ys_ref.shape, F32)


def _ffn(block_e, nb_used, xs2, wg, wu, wd):
    rows = xs2.shape[0] // SUB
    d, f = wg.shape[1], wg.shape[2]
    tm = MOE_TILE
    nb = rows // tm
    wmap = lambda b, be, nbu: (be[b], 0, 0)
    return pl.pallas_call(
        _ffn_kernel,
        grid_spec=pltpu.PrefetchScalarGridSpec(
            num_scalar_prefetch=2,
            grid=(nb,),
            in_specs=[pl.BlockSpec(memory_space=pl.ANY), pl.BlockSpec((1, d, f), wmap),
                      pl.BlockSpec((1, d, f), wmap), pl.BlockSpec((1, f, d), wmap)],
            out_specs=pl.BlockSpec((tm * SUB, LANES), lambda b, be, nbu: (b, 0)),
            scratch_shapes=[pltpu.VMEM((FFN_RING, tm * SUB, LANES), F32),
                            pltpu.VMEM((d, f), MXU_DTYPE), pltpu.VMEM((d, f), MXU_DTYPE),
                            pltpu.VMEM((f, d), MXU_DTYPE), pltpu.SemaphoreType.DMA((FFN_RING,))]),
        out_shape=jax.ShapeDtypeStruct((rows * SUB, LANES), F32),
        compiler_params=_cparams(("arbitrary",)),
        name="ffn",
    )(block_e, nb_used, xs2, wg, wu, wd)


def _combine_kernel(cnt_ref, seg_ref, gb_ref, lidx_ref, wts_ref, hf3_ref, x1_ref, gt2_ref, gpost_ref,
                    wsg_ref, wsu_ref, wsd_ref, ys_ref, y_ref, buf0, buf1, y3_ref, sems, *, tile0, ne):
    bb, ll, d = x1_ref.shape
    tt = bb * ll
    i = pl.program_id(0)
    n = pl.num_programs(0)

    def fetch(j, buf, sem):
        def seg_body(e, carry):
            c, s, g = cnt_ref[j * ne + e], seg_ref[j * ne + e], gb_ref[j * ne + e]
            _for_each_piece(c, lambda off, size: pltpu.make_async_copy(
                ys_ref.at[pl.ds(g + off, size)], buf.at[pl.ds(s + off, size)], sem).start(), SEG_BITS)
            return carry
        lax.fori_loop(0, ne, seg_body, 0)

    def run(buf, sem, nbuf, nsem):
        @pl.when(i == 0)
        def _():
            fetch(tile0, buf, sem)

        @pl.when(i + 1 < n)
        def _():
            fetch(tile0 + i + 1, nbuf, nsem)

        pltpu.make_async_copy(ys_ref.at[pl.ds(0, tt * TOP_K)], buf, sem).wait()

        def tok_body(tb, carry):
            for u in range(SORT_UNROLL):
                t = tb * SORT_UNROLL + u
                acc = wts_ref[t * TOP_K] * buf[lidx_ref[t * TOP_K]]
                for k in range(1, TOP_K):
                    acc = acc + wts_ref[t * TOP_K + k] * buf[lidx_ref[t * TOP_K + k]]
                y3_ref[pl.ds(pl.multiple_of(t * SUB, SUB), SUB), :] = acc
            return carry
        lax.fori_loop(0, tt // SORT_UNROLL, tok_body, 0)

    @pl.when(i % 2 == 0)
    def _():
        run(buf0, sems.at[0], buf1, sems.at[1])

    @pl.when(i % 2 == 1)
    def _():
        run(buf1, sems.at[1], buf0, sems.at[0])

    hf = _from_row_tiles(hf3_ref, tt)
    shared = _mm(_silu(_mm(hf, wsg_ref[...])) * _mm(hf, wsu_ref[...]), wsd_ref[...])
    yf = (_from_row_tiles(y3_ref, tt) + shared).reshape(bb, ll, d)
    y_ref[...] = x1_ref[...] + gt2_ref[...] * _rms(yf, gpost_ref[...])


def _combine(cnt, seg, gbase, lidx, wts, hf3, x1, gt2, gpost, wsg, wsu, wsd, ys, tile0, ne):
    b, l, d = x1.shape
    tt = ROUTE_TILE
    ll = min(l, tt)
    bb = tt // ll
    nl = l // ll
    nt = (b // bb) * nl
    fs = wsg.shape[1]
    if nl == 1:
        tmap = lambda i, *_: (i, 0, 0)
        bmap = lambda i, *_: (i, 0, 0)
    else:
        tmap = lambda i, *_: (i // nl, i % nl, 0)
        bmap = lambda i, *_: (i // nl, 0, 0)
    wmap = lambda i, *_: (0, 0)
    kspec = lambda: pl.BlockSpec((tt * TOP_K,), lambda i, *_: (i,), memory_space=pltpu.SMEM)
    return pl.pallas_call(
        functools.partial(_combine_kernel, tile0=tile0, ne=ne),
        grid_spec=pltpu.PrefetchScalarGridSpec(
            num_scalar_prefetch=3, grid=(nt,),
            in_specs=[kspec(), kspec(),
                      pl.BlockSpec((tt * SUB, LANES), lambda i, *_: (i, 0)),
                      pl.BlockSpec((bb, ll, d), tmap), pl.BlockSpec((bb, 1, d), bmap),
                      pl.BlockSpec((1, d), wmap), pl.BlockSpec((d, fs), wmap), pl.BlockSpec((d, fs), wmap),
                      pl.BlockSpec((fs, d), wmap),
                      pl.BlockSpec(memory_space=pl.ANY)],
            out_specs=pl.BlockSpec((bb, ll, d), tmap),
            scratch_shapes=[pltpu.VMEM((tt * TOP_K, SUB, LANES), F32), pltpu.VMEM((tt * TOP_K, SUB, LANES), F32),
                            pltpu.VMEM((tt * SUB, LANES), F32), pltpu.SemaphoreType.DMA((2,))]),
        out_shape=jax.ShapeDtypeStruct((b, l, d), F32),
        compiler_params=_cparams(("arbitrary",), VMEM_LIMIT_COMBINE),
        name="combine",
    )(cnt, seg, gbase, lidx, wts, hf3, x1, gt2, gpost, wsg, wsu, wsd, ys)


def _rope_tables(pos, rows):
    half = DH // 2
    freqs = ROPE_BASE ** (-jnp.arange(half, dtype=F32) / half)
    ang = pos.astype(F32)[:, None] * freqs[None, :]
    cos, sin = jnp.cos(ang), jnp.sin(ang)
    cos2 = jnp.concatenate([cos, cos], axis=1)
    sin2 = jnp.concatenate([-sin, sin], axis=1)
    padr = rows - pos.shape[0]
    return jnp.pad(cos2, ((0, padr), (0, 0))), jnp.pad(sin2, ((0, padr), (0, 0)))


def _global_plan(cnt_tiles, n_assign, ne):
    counts = jnp.sum(cnt_tiles, axis=0)
    padded = (counts + MOE_TILE - 1) // MOE_TILE * MOE_TILE
    pend = jnp.cumsum(padded)
    base = pend - padded
    gbase = base[None, :] + jnp.cumsum(cnt_tiles, axis=0) - cnt_tiles
    nb = n_assign // MOE_TILE + ne
    starts = jnp.arange(nb, dtype=jnp.int32) * MOE_TILE
    block_e = jnp.minimum(jnp.sum((pend[None, :] <= starts[:, None]).astype(jnp.int32), axis=1), ne - 1)
    nb_used = pend[-1:] // MOE_TILE
    tail = jnp.concatenate([pend[-1:], nb - nb_used])
    i32 = lambda v: v.astype(jnp.int32)
    return i32(gbase.reshape(-1)), i32(block_e), i32(nb_used), i32(base + counts), i32(padded - counts), i32(tail), nb


def kernel(x_prompt, x_sample, c_prompt, c_sample, state_mlstm_C, state_mlstm_n, state_mlstm_m, state_mlstm_conv, state_ret_S, w_ada, b_ada, g_mix_pre, g_mix_post, g_ffn_pre, g_ffn_post, w_in, b_gates, conv_w, conv_b, m_gn, r_gn, w_out, w_router, b_router, w_exp_gate, w_exp_up, w_exp_down, w_sh_gate, w_sh_up, w_sh_down):
    depth = w_ada.shape[0]
    bp, lp, d = x_prompt.shape
    bs, ls, _ = x_sample.shape
    mh, rh = state_mlstm_C.shape[2], state_ret_S.shape[2]
    mw, rw = mh * DH, rh * DH
    ne = w_router.shape[2]
    tp, ts = bp * lp, bs * ls
    ntp = tp // ROUTE_TILE

    cos_p, sin_p = _rope_tables(jnp.arange(lp, dtype=jnp.int32), lp)
    cos_s, sin_s = _rope_tables(PAST_LEN + jnp.arange(ls, dtype=jnp.int32), CHUNK if ls % CHUNK else ls)

    hp, hs = x_prompt, x_sample
    new_p, new_s = [], []
    for l in range(depth):
        wi = w_in[l]
        g0, nmain = 4 * mw, 4 * mw + 4 * rw
        w_in_r = jnp.zeros((d, nmain + GATE_LANES), MXU_DTYPE)
        w_in_r = w_in_r.at[:, :g0].set(wi[:, :g0].astype(MXU_DTYPE))
        w_in_r = w_in_r.at[:, g0:nmain].set(wi[:, g0 + 2 * mh:].astype(MXU_DTYPE))
        w_in_r = w_in_r.at[:, nmain:nmain + 2 * mh].set(wi[:, g0:g0 + 2 * mh].astype(MXU_DTYPE))
        bg = jnp.pad(b_gates[l], (0, GATE_LANES - 2 * mh)).reshape(1, GATE_LANES)
        wout = w_out[l].astype(MXU_DTYPE)
        wr = jnp.pad(w_router[l], ((0, 0), (0, GATE_LANES - ne)))
        wr_hi, wr_lo = _split_hi_lo(wr)
        wsg, wsu, wsd = (w_sh_gate[l].astype(MXU_DTYPE), w_sh_up[l].astype(MXU_DTYPE), w_sh_down[l].astype(MXU_DTYPE))
        row = lambda v: v.reshape(1, -1)

        mod = _ada(jnp.concatenate([c_prompt, c_sample], axis=0), w_ada[l], b_ada[l])
        mods_p = [m[:bp].reshape(bp, 1, d) for m in jnp.split(mod, 6, axis=1)]
        mods_s = [m[bp:].reshape(bs, 1, d) for m in jnp.split(mod, 6, axis=1)]

        def lane_m(m):
            return jnp.pad(m, ((0, 0), (mh, GATE_LANES - 2 * mh)))[:, None, :]

        init = (jnp.zeros((bp, mh, DH, DH), F32), jnp.zeros((bp, mh, DH), F32), lane_m(jnp.zeros((bp, mh), F32)),
                jnp.zeros((bp, CONV_W - 1, 2 * mw), F32), jnp.zeros((bp, rh, DH, DH), F32))
        past = (state_mlstm_C[l], state_mlstm_n[l], lane_m(state_mlstm_m[l]), state_mlstm_conv[l], state_ret_S[l])

        def mix(x, mods, state, cos2, sin2):
            sh1, sc1, gt1, sh2, sc2, gt2 = mods
            b, ll, _ = x.shape
            if ll % MIX_ROWS == 0:
                x1, hf3, lg, c_n, n_n, m_n, conv_n, s_n = _mixer(
                    x, mods, cos2, sin2, *state, row(g_mix_pre[l]), w_in_r, conv_w[l], row(conv_b[l]), bg,
                    row(m_gn[l]), row(r_gn[l]), row(g_mix_post[l]), row(g_ffn_pre[l]), wout, wr_hi, wr_lo)
            else:
                z = _inproj(x, sc1, sh1, row(g_mix_pre[l]), w_in_r).reshape(b, ll, -1)
                o_mix, c_n, n_n, m_n, conv_n, s_n = _mixcore_packed(
                    z, cos2, sin2, *state, conv_w[l], row(conv_b[l]), bg, row(m_gn[l]), row(r_gn[l]))
                x1, hf3, lg = _mixout(o_mix, x, gt1, sc2, sh2, row(g_mix_post[l]), row(g_ffn_pre[l]), wout, wr_hi,
                                      wr_lo)
            lidx, wts, cnt, seg = _router(lg, b_router[l], ne)
            flat = lambda v: v.T.reshape(-1)
            return (x1, hf3, flat(lidx), flat(wts), cnt[::SUB, :ne], seg[::SUB, :ne],
                    (c_n, n_n, m_n[:, 0, mh:2 * mh], conv_n, s_n))

        x1_p, hf3_p, lidx_p, wts_p, cnt_p, seg_p, st_p = mix(hp, mods_p, init, cos_p, sin_p)
        x1_s, hf3_s, lidx_s, wts_s, cnt_s, seg_s, st_s = mix(hs, mods_s, past, cos_s, sin_s)

        cnt_t = jnp.concatenate([cnt_p, cnt_s], axis=0)
        cnt = cnt_t.reshape(-1)
        seg = jnp.concatenate([seg_p, seg_s], axis=0).reshape(-1)
        gbase, block_e, nb_used, pad_start, pad_n, tail, nb = _global_plan(cnt_t, (tp + ts) * TOP_K, ne)
        xs = _dispatch(cnt, seg, gbase, pad_start, pad_n, tail, lidx_p, lidx_s, hf3_p, hf3_s, nb * MOE_TILE, ne)
        ys = _ffn(block_e, nb_used, xs.reshape(nb * MOE_TILE * SUB, LANES), w_exp_gate[l], w_exp_up[l], w_exp_down[l])
        ys = ys.reshape(nb * MOE_TILE, SUB, LANES)
        hp = _combine(cnt, seg, gbase, lidx_p, wts_p, hf3_p, x1_p, mods_p[5], row(g_ffn_post[l]), wsg, wsu, wsd, ys, 0, ne)
        hs = _combine(cnt, seg, gbase, lidx_s, wts_s, hf3_s, x1_s, mods_s[5], row(g_ffn_post[l]), wsg, wsu, wsd, ys, ntp, ne)
        new_p.append(st_p)
        new_s.append(st_s)

    p_state = [jnp.stack([st[i] for st in new_p]) for i in range(5)]
    s_state = [jnp.stack([st[i] for st in new_s]) for i in range(5)]
    return (hp, hs, *p_state, *s_state)
```
